```python
import math
import jax, jax.numpy as jnp
from jax import lax
import numpy as np

D_MODEL = 1024
BATCH = 4
SEQ = 8192
DEPTH = 2
DEC_BATCH = 16
DEC_SEQ = 64
PAST_LEN = 4096

CHUNK = 64
HEAD_DIM = 64
D_MIX = D_MODEL
H_A = 4
W_A = H_A * 2 * HEAD_DIM
H_B = 4
W_B = H_B * HEAD_DIM
H_IDX = 4
D_IDX = 64
TOPK_MAX = 256
W_C = D_MIX - W_A - W_B
N_GATE_BLOCKS = 4
GATE_BLOCK = W_C // N_GATE_BLOCKS
RG_C = 8.0
CONV_C = 4
D_FF = 3072
CONV_F = 3
Q_BLOCK = 128
EPS = 1e-6

PROJ_SIZES = (W_A, W_A, W_A, W_B, W_B, W_B, H_IDX * D_IDX, D_IDX, H_IDX, W_C, W_C)
PROJ_W = sum(PROJ_SIZES)
SPLIT_POINTS = [int(v) for v in np.cumsum(PROJ_SIZES)[:-1]]

kernel_name = "hybrid_streaming_encoder_step"


def rmsnorm(x, g):
    xf = x.astype(jnp.float32)
    y = xf * lax.rsqrt(jnp.mean(xf * xf, axis=-1, keepdims=True) + EPS)
    return (y * g.astype(jnp.float32)).astype(x.dtype)


def _blocking(lq):
    qb = lq if lq <= Q_BLOCK else Q_BLOCK
    return qb, lq // qb


def _to_blocks(t, nb, qb):
    return jnp.moveaxis(t.reshape(t.shape[0], nb, qb, *t.shape[2:]), 1, 0)


def _from_blocks(t):
    t = jnp.moveaxis(t, 0, 1)
    return t.reshape(t.shape[0], t.shape[1] * t.shape[2], *t.shape[3:])


def diff_attention(q, k, v, lam, gain, lam_init, q_pos, k_pos):
    B, Lq = q.shape[:2]
    qb, nb = _blocking(Lq)
    k1, k2 = k[..., :HEAD_DIM], k[..., HEAD_DIM:]
    kc = k_pos // CHUNK
    scale = HEAD_DIM ** -0.5

    def block(args):
        qblk, pblk = args
        mask = kc[None, :] <= (pblk // CHUNK)[:, None]

        def probs(qh, kh):
            s = jnp.einsum('bqhd,bkhd->bhqk', qh, kh).astype(jnp.float32) * scale
            s = jnp.where(mask[None, None], s, -jnp.inf)
            return jax.nn.softmax(s, axis=-1)

        a = probs(qblk[..., :HEAD_DIM], k1) - lam * probs(qblk[..., HEAD_DIM:], k2)
        return jnp.einsum('bhqk,bkhe->bqhe', a.astype(v.dtype), v)

    o = _from_blocks(lax.map(block, (_to_blocks(q, nb, qb), q_pos.reshape(nb, qb))))
    o = rmsnorm(o, gain) * (1.0 - lam_init)
    return o.reshape(B, Lq, H_A * 2 * HEAD_DIM)


def dsa_attention(q, k, v, q_idx, k_idx, w_idx, q_pos, k_pos):
    B, Lq = q.shape[:2]
    Lk = k.shape[1]
    n_sel = min(TOPK_MAX, Lk // 4)
    qb, nb = _blocking(Lq)
    kc = k_pos // CHUNK
    scale = HEAD_DIM ** -0.5

    def block(args):
        qblk, qiblk, wblk, pblk = args
        qcb = pblk // CHUNK
        adm = kc[None, :] <= qcb[:, None]
        rel = jax.nn.relu(jnp.einsum('bqhd,bkd->bqhk', qiblk, k_idx))
        score = jnp.einsum('bqh,bqhk->bqk', wblk, rel).astype(jnp.float32)
        score = jnp.where(adm[None], score, -jnp.inf)
        _, idx = lax.top_k(score, n_sel)
        valid = kc[idx] <= qcb[None, :, None]
        kg = jax.vmap(lambda kk, ii: kk[ii])(k, idx)
        vg = jax.vmap(lambda vv, ii: vv[ii])(v, idx)
        s = jnp.einsum('bqhd,bqkhd->bhqk', qblk, kg).astype(jnp.float32) * scale
        s = jnp.where(valid[:, None], s, -jnp.inf)
        p = jax.nn.softmax(s, axis=-1)
        return jnp.einsum('bhqk,bqkhd->bqhd', p.astype(v.dtype), vg)

    o = lax.map(block, (_to_blocks(q, nb, qb), _to_blocks(q_idx, nb, qb),
                        _to_blocks(w_idx, nb, qb), q_pos.reshape(nb, qb)))
    return _from_blocks(o).reshape(B, Lq, H_B * HEAD_DIM)


def causal_depthwise_conv(x_pad, w, b):
    y = lax.conv_general_dilated(
        x_pad, w[:, None, :].astype(x_pad.dtype), window_strides=(1,), padding='VALID',
        dimension_numbers=('NWC', 'WIO', 'NWC'), feature_group_count=x_pad.shape[-1])
    return y + b.astype(x_pad.dtype)


def rg_lru(xc, w_r, b_r, w_i, b_i, lam_log, h0, pos):
    B, L, C = xc.shape
    xb = xc.reshape(B, L, N_GATE_BLOCKS, GATE_BLOCK)
    r = jax.nn.sigmoid((jnp.einsum('blnc,ncd->blnd', xb, w_r) + b_r).reshape(B, L, C).astype(jnp.float32))
    i = jax.nn.sigmoid((jnp.einsum('blnc,ncd->blnd', xb, w_i) + b_i).reshape(B, L, C).astype(jnp.float32))
    log_a = -RG_C * r * jax.nn.softplus(-lam_log.astype(jnp.float32))
    a = jnp.exp(log_a)
    mult = jnp.where((pos == 0)[None, :, None], 1.0, jnp.sqrt(-jnp.expm1(2.0 * log_a)))
    bterm = mult * i * xc.astype(jnp.float32)

    def combine(lhs, rhs):
        a1, b1 = lhs
        a2, b2 = rhs
        return a1 * a2, a2 * b1 + b2

    a_cum, b_cum = lax.associative_scan(combine, (a, bterm), axis=1)
    h = a_cum * h0[:, None, :].astype(jnp.float32) + b_cum
    return h.astype(xc.dtype), h[:, -1].astype(xc.dtype)


def conv_ffn(h, w_up, conv_w, conv_b, w_down, buf):
    a = h @ w_up
    a_pad = jnp.concatenate([buf.astype(a.dtype), a], axis=1)
    a = causal_depthwise_conv(a_pad, conv_w, conv_b)
    u, g = jnp.split(a, 2, axis=-1)
    return (jax.nn.silu(g) * u) @ w_down, a_pad[:, a_pad.shape[1] - (CONV_F - 1):]


def trunk_layer(x, past, prm, lam_init):
    (a_k0, a_v0, b_k0, b_v0, b_ki0, c_h0, c_cv0, f_cv0) = past
    (norm_mix, w_in, lam_q1, lam_k1, lam_q2, lam_k2, diff_gain, rg_conv_w, rg_conv_b,
     rg_w_r, rg_b_r, rg_w_i, rg_b_i, rg_lambda, w_out, norm_ffn, ffn_w_up, ffn_conv_w,
     ffn_conv_b, ffn_w_down) = prm
    B, L = x.shape[:2]
    P = a_k0.shape[1]
    pos = P + jnp.arange(L)
    k_pos = jnp.arange(P + L)

    h = rmsnorm(x, norm_mix)
    z = h @ w_in
    qa, ka, va, qb, kb, vb, qi, ki, wi, xc, gc = jnp.split(z, SPLIT_POINTS, axis=-1)
    qa = qa.reshape(B, L, H_A, 2 * HEAD_DIM)
    ka = ka.reshape(B, L, H_A, 2 * HEAD_DIM)
    va = va.reshape(B, L, H_A, 2 * HEAD_DIM)
    qb = qb.reshape(B, L, H_B, HEAD_DIM)
    kb = kb.reshape(B, L, H_B, HEAD_DIM)
    vb = vb.reshape(B, L, H_B, HEAD_DIM)
    qi = qi.reshape(B, L, H_IDX, D_IDX)

    lam = (jnp.exp(jnp.sum(lam_q1.astype(jnp.float32) * lam_k1.astype(jnp.float32)))
           - jnp.exp(jnp.sum(lam_q2.astype(jnp.float32) * lam_k2.astype(jnp.float32))) + lam_init)
    ka_all = jnp.concatenate([a_k0.astype(x.dtype), ka], axis=1)
    va_all = jnp.concatenate([a_v0.astype(x.dtype), va], axis=1)
    o_a = diff_attention(qa, ka_all, va_all, lam, diff_gain, lam_init, pos, k_pos)

    kb_all = jnp.concatenate([b_k0.astype(x.dtype), kb], axis=1)
    vb_all = jnp.concatenate([b_v0.astype(x.dtype), vb], axis=1)
    ki_all = jnp.concatenate([b_ki0.astype(x.dtype), ki], axis=1)
    o_b = dsa_attention(qb, kb_all, vb_all, qi, ki_all, wi, pos, k_pos)

    xc_pad = jnp.concatenate([c_cv0.astype(x.dtype), xc], axis=1)
    xconv = causal_depthwise_conv(xc_pad, rg_conv_w, rg_conv_b)
    hc, hc_last = rg_lru(xconv, rg_w_r, rg_b_r, rg_w_i, rg_b_i, rg_lambda, c_h0, pos)
    o_c = hc * jax.nn.gelu(gc)

    x = x + jnp.concatenate([o_a, o_b, o_c], axis=-1) @ w_out
    f, f_buf = conv_ffn(rmsnorm(x, norm_ffn), ffn_w_up, ffn_conv_w, ffn_conv_b, ffn_w_down, f_cv0)
    x = x + f
    new = (ka, va, kb, vb, ki, hc_last, xc_pad[:, xc_pad.shape[1] - (CONV_C - 1):], f_buf)
    return x, new


def run_trunk(x, past, params):
    new_states = []
    for l in range(DEPTH):
        lam_init = 0.8 - 0.6 * math.exp(-0.3 * l)
        x, st = trunk_layer(x, tuple(c[l] for c in past), tuple(w[l] for w in params), lam_init)
        new_states.append(st)
    return x, new_states


def setup_inputs(seed: int = 0) -> dict:
    key = jax.random.key(seed)
    ks = iter(jax.random.split(key, 40))

    def nrm(shape, scale=1.0):
        return jax.random.normal(next(ks), shape, jnp.float32) * scale

    u = jax.random.uniform(next(ks), (DEPTH, W_C), jnp.float32, 0.9, 0.999)
    s = u ** (1.0 / RG_C)
    rg_lambda = jnp.log(s) - jnp.log1p(-s)
    return {
        "x_prompt": nrm((BATCH, SEQ, D_MODEL)),
        "x_sample": nrm((DEC_BATCH, DEC_SEQ, D_MODEL)),
        "cache_a_k": nrm((DEPTH, DEC_BATCH, PAST_LEN, H_A, 2 * HEAD_DIM)),
        "cache_a_v": nrm((DEPTH, DEC_BATCH, PAST_LEN, H_A, 2 * HEAD_DIM)),
        "cache_b_k": nrm((DEPTH, DEC_BATCH, PAST_LEN, H_B, HEAD_DIM)),
        "cache_b_v": nrm((DEPTH, DEC_BATCH, PAST_LEN, H_B, HEAD_DIM)),
        "cache_b_kidx": nrm((DEPTH, DEC_BATCH, PAST_LEN, D_IDX)),
        "state_c_h": nrm((DEPTH, DEC_BATCH, W_C), 0.5),
        "state_c_conv": nrm((DEPTH, DEC_BATCH, CONV_C - 1, W_C)),
        "state_ffn_conv": nrm((DEPTH, DEC_BATCH, CONV_F - 1, 2 * D_FF)),
        "norm_mix": 1.0 + nrm((DEPTH, D_MODEL), 0.01),
        "w_in": nrm((DEPTH, D_MODEL, PROJ_W), D_MODEL ** -0.5),
        "lam_q1": nrm((DEPTH, HEAD_DIM), 0.1),
        "lam_k1": nrm((DEPTH, HEAD_DIM), 0.1),
        "lam_q2": nrm((DEPTH, HEAD_DIM), 0.1),
        "lam_k2": nrm((DEPTH, HEAD_DIM), 0.1),
        "diff_gain": 1.0 + nrm((DEPTH, 2 * HEAD_DIM), 0.01),
        "rg_conv_w": nrm((DEPTH, CONV_C, W_C), CONV_C ** -0.5),
        "rg_conv_b": nrm((DEPTH, W_C), 0.01),
        "rg_w_r": nrm((DEPTH, N_GATE_BLOCKS, GATE_BLOCK, GATE_BLOCK), GATE_BLOCK ** -0.5),
        "rg_b_r": nrm((DEPTH, N_GATE_BLOCKS, GATE_BLOCK), 0.01),
        "rg_w_i": nrm((DEPTH, N_GATE_BLOCKS, GATE_BLOCK, GATE_BLOCK), GATE_BLOCK ** -0.5),
        "rg_b_i": nrm((DEPTH, N_GATE_BLOCKS, GATE_BLOCK), 0.01),
        "rg_lambda": rg_lambda,
        "w_out": nrm((DEPTH, D_MIX, D_MODEL), D_MIX ** -0.5),
        "norm_ffn": 1.0 + nrm((DEPTH, D_MODEL), 0.01),
        "ffn_w_up": nrm((DEPTH, D_MODEL, 2 * D_FF), D_MODEL ** -0.5),
        "ffn_conv_w": nrm((DEPTH, CONV_F, 2 * D_FF), CONV_F ** -0.5),
        "ffn_conv_b": nrm((DEPTH, 2 * D_FF), 0.01),
        "ffn_w_down": nrm((DEPTH, D_FF, D_MODEL), D_FF ** -0.5),
        "norm_final": 1.0 + nrm((D_MODEL,), 0.01),
    }


def reference(x_prompt, x_sample, cache_a_k, cache_a_v, cache_b_k, cache_b_v, cache_b_kidx,
              state_c_h, state_c_conv, state_ffn_conv, norm_mix, w_in, lam_q1, lam_k1, lam_q2,
              lam_k2, diff_gain, rg_conv_w, rg_conv_b, rg_w_r, rg_b_r, rg_w_i, rg_b_i, rg_lambda,
              w_out, norm_ffn, ffn_w_up, ffn_conv_w, ffn_conv_b, ffn_w_down, norm_final):
    params = (norm_mix, w_in, lam_q1, lam_k1, lam_q2, lam_k2, diff_gain, rg_conv_w, rg_conv_b,
              rg_w_r, rg_b_r, rg_w_i, rg_b_i, rg_lambda, w_out, norm_ffn, ffn_w_up, ffn_conv_w,
              ffn_conv_b, ffn_w_down)
    dt = x_prompt.dtype
    B = x_prompt.shape[0]
    past_prompt = (
        jnp.zeros((DEPTH, B, 0, H_A, 2 * HEAD_DIM), dt),
        jnp.zeros((DEPTH, B, 0, H_A, 2 * HEAD_DIM), dt),
        jnp.zeros((DEPTH, B, 0, H_B, HEAD_DIM), dt),
        jnp.zeros((DEPTH, B, 0, H_B, HEAD_DIM), dt),
        jnp.zeros((DEPTH, B, 0, D_IDX), dt),
        jnp.zeros((DEPTH, B, W_C), dt),
        jnp.zeros((DEPTH, B, CONV_C - 1, W_C), dt),
        jnp.zeros((DEPTH, B, CONV_F - 1, 2 * D_FF), dt),
    )
    past_sample = (cache_a_k, cache_a_v, cache_b_k, cache_b_v, cache_b_kidx,
                   state_c_h, state_c_conv, state_ffn_conv)

    yp, np_ = run_trunk(x_prompt, past_prompt, params)
    ys, ns_ = run_trunk(x_sample, past_sample, params)
    y_prompt = rmsnorm(yp, norm_final)
    y_sample = rmsnorm(ys, norm_final)

    def stk(states, j):
        return jnp.stack([st[j] for st in states], axis=0)

    return (y_prompt, y_sample,
            stk(np_, 0), stk(ns_, 0), stk(np_, 1), stk(ns_, 1),
            stk(np_, 2), stk(ns_, 2), stk(np_, 3), stk(ns_, 3),
            stk(np_, 4), stk(ns_, 4), stk(np_, 5), stk(ns_, 5),
            stk(np_, 6), stk(ns_, 6), stk(np_, 7), stk(ns_, 7))
```

```python
import functools
import math

import jax
import jax.numpy as jnp
from jax import lax
from jax.experimental import pallas as pl
from jax.experimental.pallas import tpu as pltpu

F32 = jnp.float32
BF16 = jnp.bfloat16

D_MODEL = 1024
N_LAYERS = 2
CHUNK = 64
CHUNK_SHIFT = 6
HEAD_DIM = 64
H_A = 4
W_A = H_A * 2 * HEAD_DIM
H_B = 4
W_B = H_B * HEAD_DIM
H_IDX = 4
D_IDX = 64
TOPK = 256
W_C = 256
N_GATE_BLOCKS = 4
GATE_BLOCK = W_C // N_GATE_BLOCKS
RG_C = 8.0
CONV_C = 4
D_FF = 3072
CONV_F = 3
Q_BLOCK = 128
EPS = 1e-6

LANES = 128
SUBLANES = 8
KEY_TILE = 512
VMEM_LIMIT = 56 * 2**20
MASKED = -1e30
N_BISECT = 20

PROJ_W_IN = 3140
PROJ_W_PAD = 3328
_C_QA, _C_KA, _C_VA = 0, 512, 1024
_C_QB, _C_KB, _C_VB = 1536, 1792, 2048
_C_QI = 2304
_C_KW = 2560
_C_KK = 2688
_C_XC = 2816
_C_GC = 3072


def _cparams(sem):
    return pltpu.CompilerParams(dimension_semantics=sem, vmem_limit_bytes=VMEM_LIMIT)


def _rms(x, g):
    return x * lax.rsqrt(jnp.mean(x * x, axis=-1, keepdims=True) + EPS) * g


def _dot_nt(a, b):
    return lax.dot_general(a, b, (((1,), (1,)), ((), ())), preferred_element_type=F32)


def _proj_kernel(x_ref, g_ref, w_ref, qa_ref, qb_ref, qi_ref, ka_ref, va_ref, kb_ref, vb_ref,
                 kw_ref, xc_ref, gc_ref, kab_ref, vab_ref, kbb_ref, vbb_ref, kib_ref):
    h = _rms(x_ref[...], g_ref[...])
    z = jnp.dot(h.astype(BF16), w_ref[...], preferred_element_type=F32)
    scale = HEAD_DIM ** -0.5
    qa_ref[...] = (z[:, _C_QA:_C_QA + W_A] * scale).astype(BF16)
    qb_ref[...] = (z[:, _C_QB:_C_QB + W_B] * scale).astype(BF16)
    qi_ref[...] = z[:, _C_QI:_C_QI + H_IDX * D_IDX].astype(BF16)
    ka = z[:, _C_KA:_C_KA + W_A]
    va = z[:, _C_VA:_C_VA + W_A]
    kb = z[:, _C_KB:_C_KB + W_B]
    vb = z[:, _C_VB:_C_VB + W_B]
    ka_ref[...] = ka
    va_ref[...] = va
    kb_ref[...] = kb
    vb_ref[...] = vb
    kab_ref[...] = ka.astype(BF16)
    vab_ref[...] = va.astype(BF16)
    kbb_ref[...] = kb.astype(BF16)
    vbb_ref[...] = vb.astype(BF16)
    kw_ref[...] = z[:, _C_KW:_C_KW + LANES]
    kib_ref[...] = z[:, _C_KK:_C_KK + LANES].astype(BF16)
    xc_ref[...] = z[:, _C_XC:_C_XC + W_C]
    gc_ref[...] = z[:, _C_GC:_C_GC + W_C]


def _proj(x2d, gain, w_pad):
    t = x2d.shape[0]
    tm = min(512, t)
    widths = [(W_A, BF16), (W_B, BF16), (H_IDX * D_IDX, BF16),
              (W_A, F32), (W_A, F32), (W_B, F32), (W_B, F32),
              (LANES, F32), (W_C, F32), (W_C, F32),
              (W_A, BF16), (W_A, BF16), (W_B, BF16), (W_B, BF16), (LANES, BF16)]
    return pl.pallas_call(
        _proj_kernel,
        grid=(t // tm,),
        in_specs=[pl.BlockSpec((tm, D_MODEL), lambda i: (i, 0)),
                  pl.BlockSpec((1, D_MODEL), lambda i: (0, 0)),
                  pl.BlockSpec((D_MODEL, PROJ_W_PAD), lambda i: (0, 0))],
        out_specs=[pl.BlockSpec((tm, w), lambda i: (i, 0)) for w, _ in widths],
        out_shape=[jax.ShapeDtypeStruct((t, w), d) for w, d in widths],
        compiler_params=_cparams(("parallel",)),
        name="proj",
    )(x2d, gain.reshape(1, D_MODEL), w_pad)


def _diff_attn_kernel(lam_ref, gain_ref, q_ref, k_ref, v_ref, o_ref, m_ref, l_ref, acc_ref,
                      *, past, lk, qb, tk, out_scale):
    j = pl.program_id(2)
    q = q_ref[...]
    lane = lax.broadcasted_iota(jnp.int32, (1, LANES), 1)
    zero = jnp.zeros_like(q)
    q_half = (jnp.where(lane < HEAD_DIM, q, zero), jnp.where(lane >= HEAD_DIM, q, zero))
    q_start = past + j * qb
    row = lax.broadcasted_iota(jnp.int32, (qb, 1), 0)
    q_chunk = lax.shift_right_logical(q_start + row, CHUNK_SHIFT)
    n_full = lax.div(jnp.minimum(q_start + CHUNK, lk), tk)
    n_tiles = lax.div(q_start + qb + tk - 1, tk)

    m_ref[...] = jnp.full(m_ref.shape, -jnp.inf, F32)
    l_ref[...] = jnp.zeros(l_ref.shape, F32)
    acc_ref[...] = jnp.zeros(acc_ref.shape, F32)

    def tile(t, masked):
        start = pl.multiple_of(t * tk, tk)
        k = k_ref[pl.ds(start, tk), :]
        v = v_ref[pl.ds(start, tk), :]
        if masked:
            col = start + lax.broadcasted_iota(jnp.int32, (1, tk), 1)
            ok = (lax.shift_right_logical(col, CHUNK_SHIFT) <= q_chunk) & (col < lk)
        for i in range(2):
            s = _dot_nt(q_half[i], k)
            if masked:
                s = jnp.where(ok, s, -jnp.inf)
            m_old = m_ref[i]
            m_new = jnp.maximum(m_old, jnp.max(s, axis=1, keepdims=True))
            alpha = jnp.exp(m_old - m_new)
            p = jnp.exp(s - m_new)
            l_ref[i] = alpha * l_ref[i] + jnp.sum(p, axis=1, keepdims=True)
            acc_ref[i] = alpha * acc_ref[i] + jnp.dot(p.astype(BF16), v, preferred_element_type=F32)
            m_ref[i] = m_new

    def full_body(t, c):
        tile(t, False)
        return c

    def masked_body(t, c):
        tile(t, True)
        return c

    lax.fori_loop(0, n_full, full_body, 0)
    lax.fori_loop(n_full, n_tiles, masked_body, 0)

    o = acc_ref[0] / l_ref[0] - lam_ref[...] * (acc_ref[1] / l_ref[1])
    o_ref[...] = (_rms(o, gain_ref[...]) * out_scale).astype(BF16)


def _diff_attn(q, k, v, lam, gain, *, past, lk, out_scale):
    b, l, _ = q.shape
    lkp = k.shape[1]
    qb = min(l, Q_BLOCK)
    kern = functools.partial(_diff_attn_kernel, past=past, lk=lk, qb=qb, tk=KEY_TILE,
                             out_scale=out_scale)
    return pl.pallas_call(
        kern,
        grid=(b, H_A, l // qb),
        in_specs=[pl.BlockSpec((1, LANES), lambda bi, h, j: (0, 0)),
                  pl.BlockSpec((1, LANES), lambda bi, h, j: (0, 0)),
                  pl.BlockSpec((None, qb, LANES), lambda bi, h, j: (bi, j, h)),
                  pl.BlockSpec((None, lkp, LANES), lambda bi, h, j: (bi, 0, h)),
                  pl.BlockSpec((None, lkp, LANES), lambda bi, h, j: (bi, 0, h))],
        out_specs=pl.BlockSpec((None, qb, LANES), lambda bi, h, j: (bi, j, h)),
        out_shape=jax.ShapeDtypeStruct((b, l, W_A), BF16),
        scratch_shapes=[pltpu.VMEM((2, qb, 1), F32), pltpu.VMEM((2, qb, 1), F32),
                        pltpu.VMEM((2, qb, LANES), F32)],
        compiler_params=_cparams(("parallel", "parallel", "parallel")),
        name="diff_attn",
    )(lam, gain, q, k, v)


def _dsa_kernel(qi_ref, kw_ref, q_ref, ki_ref, k_ref, v_ref, o_ref, s_ref, m_ref, l_ref, acc_ref,
                *, past, lk, qb, tk, topk, max_steps):
    j = pl.program_id(1)
    q_start = past + j * qb
    n_tiles = lax.div(q_start + qb + tk - 1, tk)
    n_groups = tk // LANES
    row = lax.broadcasted_iota(jnp.int32, (qb, 1), 0)
    q_chunk = lax.shift_right_logical(q_start + row, CHUNK_SHIFT)
    n_adm = jnp.minimum((q_chunk + 1) * CHUNK, lk)
    few = n_adm <= topk
    lane = lax.broadcasted_iota(jnp.int32, (1, LANES), 1)
    low_half = lane < HEAD_DIM
    k_sel = float(topk)

    def head_views(x):
        views = []
        for h in range(4):
            pair = x[:, (h // 2) * LANES:(h // 2 + 1) * LANES]
            keep = low_half if h % 2 == 0 else jnp.logical_not(low_half)
            views.append(jnp.where(keep, pair, jnp.zeros_like(pair)))
        return views

    qi_h = head_views(qi_ref[...])
    kw = kw_ref[...]
    w_h = [kw[:, D_IDX + h:D_IDX + h + 1] for h in range(H_IDX)]

    def score_tile(t, carry):
        rmin, rmax = carry
        start = pl.multiple_of(t * tk, tk)
        ki = ki_ref[pl.ds(start, tk), :]
        sc = jnp.zeros((qb, tk), F32)
        for h in range(H_IDX):
            sc = sc + w_h[h] * jnp.maximum(_dot_nt(qi_h[h], ki), 0.0)
        col = start + lax.broadcasted_iota(jnp.int32, (1, tk), 1)
        adm = (lax.shift_right_logical(col, CHUNK_SHIFT) <= q_chunk) & (col < lk)
        s_ref[t] = jnp.where(adm, sc, -jnp.inf)
        rmax = jnp.maximum(rmax, jnp.max(jnp.where(adm, sc, -jnp.inf), axis=1, keepdims=True))
        rmin = jnp.minimum(rmin, jnp.min(jnp.where(adm, sc, jnp.inf), axis=1, keepdims=True))
        return rmin, rmax

    rmin, rmax = lax.fori_loop(
        0, n_tiles, score_tile,
        (jnp.full((qb, 1), jnp.inf, F32), jnp.full((qb, 1), -jnp.inf, F32)))

    def wide(x):
        return jnp.broadcast_to(x, (qb, LANES))

    def count(pred):
        def body(t, acc):
            for g in range(n_groups):
                s = s_ref[t, :, g * LANES:(g + 1) * LANES]
                col = t * tk + g * LANES + lane
                acc = acc + jnp.where(pred(s, col), 1.0, 0.0)
            return acc
        acc = lax.fori_loop(0, n_tiles, body, jnp.zeros((qb, LANES), F32))
        return jnp.sum(acc, axis=1, keepdims=True)

    def count_ge(thr):
        thr_w = wide(thr)
        return count(lambda s, col: s >= thr_w)

    def max_below(bound):
        bound_w = wide(bound)

        def body(t, acc):
            for g in range(n_groups):
                s = s_ref[t, :, g * LANES:(g + 1) * LANES]
                acc = jnp.maximum(acc, jnp.where(s < bound_w, s, -jnp.inf))
            return acc
        acc = lax.fori_loop(0, n_tiles, body, jnp.full((qb, LANES), -jnp.inf, F32))
        return jnp.max(acc, axis=1, keepdims=True)

    top_ties = count_ge(rmax) >= k_sel

    def bisect(_, carry):
        lo, hi = carry
        mid = 0.5 * lo + 0.5 * hi
        ge = count_ge(mid) >= k_sel
        return jnp.where(ge, mid, lo), jnp.where(ge, hi, mid)

    _, hi = lax.fori_loop(0, N_BISECT, bisect, (rmin, rmax))

    def walk_cond(st):
        it, _, _, active = st
        return jnp.logical_and(jnp.max(active) > 0.0, it < max_steps)

    def walk_body(st):
        it, cand, thr, active = st
        ok = count_ge(cand) >= k_sel
        act = active > 0.0
        thr = jnp.where(jnp.logical_and(act, ok), cand, thr)
        active = jnp.where(jnp.logical_and(act, jnp.logical_not(ok)), 1.0, 0.0)
        return it + 1, max_below(cand), thr, active

    thr0 = jnp.where(jnp.logical_and(top_ties, jnp.logical_not(few)), rmax, -jnp.inf)
    active0 = jnp.where(jnp.logical_or(few, top_ties), 0.0, 1.0)
    _, _, thr, _ = lax.while_loop(walk_cond, walk_body, (jnp.int32(0), max_below(hi), thr0, active0))
    thr_w = wide(thr)

    c_gt = count(lambda s, col: s > thr_w)
    c_ge = count(lambda s, col: s >= thr_w)
    n_tie = k_sel - c_gt
    has_excess = jnp.max(jnp.where(jnp.logical_and(c_ge > k_sel, jnp.logical_not(few)), 1.0, 0.0)) > 0.0
    n_bits = max(1, (s_ref.shape[0] * tk - 1).bit_length())

    def tie_search():
        def body(i, jcut):
            cand = jcut + lax.shift_left(jnp.int32(1), jnp.int32(n_bits - 1) - i)
            cand_w = wide(cand)
            below = count(lambda s, col: jnp.logical_and(s == thr_w, col < cand_w))
            return jnp.where(below <= n_tie - 1.0, cand, jcut)
        return lax.fori_loop(0, n_bits, body, jnp.zeros((qb, 1), jnp.int32))

    jcut = lax.cond(has_excess, tie_search, lambda: jnp.full((qb, 1), 2**30, jnp.int32))
    jcut_w = wide(jcut)

    def bias_tile(t, c):
        for g in range(n_groups):
            s = s_ref[t, :, g * LANES:(g + 1) * LANES]
            col = t * tk + g * LANES + lane
            sel = jnp.logical_or(s > thr_w, jnp.logical_and(s == thr_w, col <= jcut_w))
            sel = jnp.logical_and(sel, s > -jnp.inf)
            s_ref[t, :, g * LANES:(g + 1) * LANES] = jnp.where(sel, 0.0, MASKED)
        return c

    lax.fori_loop(0, n_tiles, bias_tile, 0)

    q_h = head_views(q_ref[...])
    m_ref[...] = jnp.full(m_ref.shape, -jnp.inf, F32)
    l_ref[...] = jnp.zeros(l_ref.shape, F32)
    acc_ref[...] = jnp.zeros(acc_ref.shape, F32)

    def att_tile(t, c):
        start = pl.multiple_of(t * tk, tk)
        k = k_ref[pl.ds(start, tk), :]
        v = v_ref[pl.ds(start, tk), :]
        bias = s_ref[t]
        for h in range(H_B):
            g = h // 2
            s = _dot_nt(q_h[h], k[:, g * LANES:(g + 1) * LANES]) + bias
            m_old = m_ref[h]
            m_new = jnp.maximum(m_old, jnp.max(s, axis=1, keepdims=True))
            alpha = jnp.exp(m_old - m_new)
            p = jnp.exp(s - m_new)
            l_ref[h] = alpha * l_ref[h] + jnp.sum(p, axis=1, keepdims=True)
            acc_ref[h] = alpha * acc_ref[h] + jnp.dot(
                p.astype(BF16), v[:, g * LANES:(g + 1) * LANES], preferred_element_type=F32)
            m_ref[h] = m_new
        return c

    lax.fori_loop(0, n_tiles, att_tile, 0)

    for g in range(H_B // 2):
        even = acc_ref[2 * g] / l_ref[2 * g]
        odd = acc_ref[2 * g + 1] / l_ref[2 * g + 1]
        o_ref[:, g * LANES:(g + 1) * LANES] = jnp.where(low_half, even, odd).astype(BF16)


def _dsa(qi, kw, q, ki, k, v, *, past, lk):
    b, l, _ = q.shape
    lkp = k.shape[1]
    qb = min(l, Q_BLOCK)
    tk = KEY_TILE
    kern = functools.partial(_dsa_kernel, past=past, lk=lk, qb=qb, tk=tk,
                             topk=min(TOPK, lk // 4), max_steps=lkp)
    return pl.pallas_call(
        kern,
        grid=(b, l // qb),
        in_specs=[pl.BlockSpec((None, qb, H_IDX * D_IDX), lambda bi, j: (bi, j, 0)),
                  pl.BlockSpec((None, qb, LANES), lambda bi, j: (bi, j, 0)),
                  pl.BlockSpec((None, qb, W_B), lambda bi, j: (bi, j, 0)),
                  pl.BlockSpec((None, lkp, LANES), lambda bi, j: (bi, 0, 0)),
                  pl.BlockSpec((None, lkp, W_B), lambda bi, j: (bi, 0, 0)),
                  pl.BlockSpec((None, lkp, W_B), lambda bi, j: (bi, 0, 0))],
        out_specs=pl.BlockSpec((None, qb, W_B), lambda bi, j: (bi, j, 0)),
        out_shape=jax.ShapeDtypeStruct((b, l, W_B), BF16),
        scratch_shapes=[pltpu.VMEM((lkp // tk, qb, tk), F32),
                        pltpu.VMEM((H_B, qb, 1), F32), pltpu.VMEM((H_B, qb, 1), F32),
                        pltpu.VMEM((H_B, qb, LANES), F32)],
        compiler_params=_cparams(("parallel", "parallel")),
        name="dsa",
    )(qi, kw, q, ki, k, v)


def _shift_rows(x, prev, k):
    rolled = pltpu.roll(x, k, 0)
    row = lax.broadcasted_iota(jnp.int32, (SUBLANES, 1), 0)
    top = jnp.where(row < k, pltpu.roll(prev, k, 0), rolled[0:SUBLANES])
    if x.shape[0] == SUBLANES:
        return top
    return jnp.concatenate([top, rolled[SUBLANES:]], axis=0)


def _rglru_kernel(xc_ref, gc_ref, cst_ref, h0_ref, cw_ref, cb_ref, wg_ref, bg_ref, lam_ref,
                  oc_ref, hl_ref, cn_ref, prev_ref, h_ref, a_ref, b_ref, hs_ref, *, past, tl):
    i = pl.program_id(1)

    @pl.when(i == 0)
    def _():
        prev_ref[...] = jnp.zeros(prev_ref.shape, F32)
        prev_ref[SUBLANES - (CONV_C - 1):SUBLANES, :] = cst_ref[...]
        h_ref[...] = h0_ref[...]

    x = xc_ref[...]
    prev = prev_ref[...]
    cw = cw_ref[...]
    xconv = cw[CONV_C - 1:CONV_C] * x + cb_ref[...]
    for k in range(1, CONV_C):
        xconv = xconv + cw[CONV_C - 1 - k:CONV_C - k] * _shift_rows(x, prev, k)
    prev_ref[...] = x[tl - SUBLANES:tl]

    pre = jnp.dot(xconv.astype(BF16), wg_ref[...], preferred_element_type=F32) + bg_ref[...]
    r = jax.nn.sigmoid(pre[:, :W_C])
    gate_i = jax.nn.sigmoid(pre[:, W_C:])
    neg_lam = -lam_ref[...]
    softplus = jnp.maximum(neg_lam, 0.0) + jnp.log1p(jnp.exp(-jnp.abs(neg_lam)))
    log_a = -RG_C * r * softplus
    pos = past + i * tl + lax.broadcasted_iota(jnp.int32, (tl, 1), 0)
    th = jnp.tanh(log_a)
    mult = jnp.where(pos == 0, 1.0, jnp.sqrt(-2.0 * th / (1.0 - th)))
    a_ref[...] = jnp.exp(log_a)
    b_ref[...] = mult * gate_i * xconv

    def step(t, h):
        h = a_ref[pl.ds(t, 1), :] * h + b_ref[pl.ds(t, 1), :]
        hs_ref[pl.ds(t, 1), :] = h
        return h

    h_last = lax.fori_loop(0, tl, step, h_ref[...], unroll=8)
    h_ref[...] = h_last

    gc = gc_ref[...]
    gelu = 0.5 * gc * (1.0 + jnp.tanh(math.sqrt(2.0 / math.pi) * (gc + 0.044715 * (gc * gc * gc))))
    oc_ref[...] = (hs_ref[...] * gelu).astype(BF16)

    @pl.when(i == pl.num_programs(1) - 1)
    def _():
        hl_ref[...] = h_last
        cn_ref[...] = x[tl - (CONV_C - 1):tl]


def _rglru(xc, gc, conv_state, h0, conv_w, conv_b, w_gate, b_gate, lam, *, past):
    b, l, _ = xc.shape
    tl = min(l, 512)
    assert l >= SUBLANES and l % tl == 0
    kern = functools.partial(_rglru_kernel, past=past, tl=tl)
    const = lambda bi, i: (0, 0)
    return pl.pallas_call(
        kern,
        grid=(b, l // tl),
        in_specs=[pl.BlockSpec((None, tl, W_C), lambda bi, i: (bi, i, 0)),
                  pl.BlockSpec((None, tl, W_C), lambda bi, i: (bi, i, 0)),
                  pl.BlockSpec((None, CONV_C - 1, W_C), lambda bi, i: (bi, 0, 0)),
                  pl.BlockSpec((None, 1, W_C), lambda bi, i: (bi, 0, 0)),
                  pl.BlockSpec((CONV_C, W_C), const),
                  pl.BlockSpec((1, W_C), const),
                  pl.BlockSpec((W_C, 2 * W_C), const),
                  pl.BlockSpec((1, 2 * W_C), const),
                  pl.BlockSpec((1, W_C), const)],
        out_specs=[pl.BlockSpec((None, tl, W_C), lambda bi, i: (bi, i, 0)),
                   pl.BlockSpec((None, 1, W_C), lambda bi, i: (bi, 0, 0)),
                   pl.BlockSpec((None, CONV_C - 1, W_C), lambda bi, i: (bi, 0, 0))],
        out_shape=[jax.ShapeDtypeStruct((b, l, W_C), BF16),
                   jax.ShapeDtypeStruct((b, 1, W_C), F32),
                   jax.ShapeDtypeStruct((b, CONV_C - 1, W_C), F32)],
        scratch_shapes=[pltpu.VMEM((SUBLANES, W_C), F32), pltpu.VMEM((1, W_C), F32),
                        pltpu.VMEM((tl, W_C), F32), pltpu.VMEM((tl, W_C), F32),
                        pltpu.VMEM((tl, W_C), F32)],
        compiler_params=_cparams(("parallel", "arbitrary")),
        name="rglru",
    )(xc, gc, conv_state, h0.reshape(b, 1, W_C), conv_w, conv_b.reshape(1, W_C), w_gate,
      b_gate, lam.reshape(1, W_C))


def _outproj_kernel(x_ref, oa_ref, ob_ref, oc_ref, w_ref, g_ref, x1_ref, hn_ref):
    mix = jnp.dot(oa_ref[...], w_ref[0:W_A, :], preferred_element_type=F32)
    mix = mix + jnp.dot(ob_ref[...], w_ref[W_A:W_A + W_B, :], preferred_element_type=F32)
    mix = mix + jnp.dot(oc_ref[...], w_ref[W_A + W_B:, :], preferred_element_type=F32)
    x1 = x_ref[...] + mix
    x1_ref[...] = x1
    hn_ref[...] = _rms(x1, g_ref[...]).astype(BF16)


def _outproj(x2d, oa, ob, oc, w_out, gain):
    t = x2d.shape[0]
    tm = min(512, t)
    row = lambda w: pl.BlockSpec((tm, w), lambda i: (i, 0))
    return pl.pallas_call(
        _outproj_kernel,
        grid=(t // tm,),
        in_specs=[row(D_MODEL), row(W_A), row(W_B), row(W_C),
                  pl.BlockSpec((D_MODEL, D_MODEL), lambda i: (0, 0)),
                  pl.BlockSpec((1, D_MODEL), lambda i: (0, 0))],
        out_specs=[row(D_MODEL), row(D_MODEL)],
        out_shape=[jax.ShapeDtypeStruct((t, D_MODEL), F32), jax.ShapeDtypeStruct((t, D_MODEL), BF16)],
        compiler_params=_cparams(("parallel",)),
        name="outproj",
    )(x2d, oa, ob, oc, w_out, gain.reshape(1, D_MODEL))


def _ffn_kernel(hn_ref, x1_ref, wu_ref, wg_ref, cwu_ref, cwg_ref, cbu_ref, cbg_ref, wd_ref,
                su_ref, sg_ref, gfin_ref, y_ref, fu_ref, fg_ref, acc_ref, cu_ref, cg_ref,
                *, tm, final_norm):
    i = pl.program_id(1)
    f = pl.program_id(2)
    hn = hn_ref[...]

    def conv_branch(w_ref, cw_ref, cb_ref, st_ref, carry_ref, tail_ref):
        a = jnp.dot(hn, w_ref[...], preferred_element_type=F32)

        @pl.when(i == 0)
        def _():
            carry_ref[f] = jnp.zeros(carry_ref.shape[1:], F32)
            carry_ref[f, SUBLANES - (CONV_F - 1):SUBLANES, :] = st_ref[...]

        prev = carry_ref[f]
        cw = cw_ref[...]
        y = cw[CONV_F - 1:CONV_F] * a + cb_ref[...]
        for k in range(1, CONV_F):
            y = y + cw[CONV_F - 1 - k:CONV_F - k] * _shift_rows(a, prev, k)
        carry_ref[f] = a[tm - SUBLANES:tm]
        tail_ref[...] = a[tm - (CONV_F - 1):tm]
        return y

    u = conv_branch(wu_ref, cwu_ref, cbu_ref, su_ref, cu_ref, fu_ref)
    g = conv_branch(wg_ref, cwg_ref, cbg_ref, sg_ref, cg_ref, fg_ref)
    mid = (g * jax.nn.sigmoid(g) * u).astype(BF16)
    contrib = jnp.dot(mid, wd_ref[...], preferred_element_type=F32)

    @pl.when(f == 0)
    def _():
        acc_ref[...] = x1_ref[...] + contrib

    @pl.when(f > 0)
    def _():
        acc_ref[...] = acc_ref[...] + contrib

    @pl.when(f == pl.num_programs(2) - 1)
    def _():
        y = acc_ref[...]
        if final_norm:
            y = _rms(y, gfin_ref[...])
        y_ref[...] = y


def _ffn(hn, x1, w_up, conv_w, conv_b, w_down, state, final_gain, *, final_norm):
    b, l, _ = hn.shape
    tm = min(l, 1024)
    tf = 512
    nf = D_FF // tf
    assert l % tm == 0 and tm >= SUBLANES
    kern = functools.partial(_ffn_kernel, tm=tm, final_norm=final_norm)
    conv_b = conv_b.reshape(1, 2 * D_FF)
    u_col = lambda bi, i, f: (0, f)
    g_col = lambda bi, i, f: (0, nf + f)
    y, fu, fg = pl.pallas_call(
        kern,
        grid=(b, l // tm, nf),
        in_specs=[pl.BlockSpec((None, tm, D_MODEL), lambda bi, i, f: (bi, i, 0)),
                  pl.BlockSpec((None, tm, D_MODEL), lambda bi, i, f: (bi, i, 0)),
                  pl.BlockSpec((D_MODEL, tf), u_col),
                  pl.BlockSpec((D_MODEL, tf), g_col),
                  pl.BlockSpec((CONV_F, tf), u_col),
                  pl.BlockSpec((CONV_F, tf), g_col),
                  pl.BlockSpec((1, tf), u_col),
                  pl.BlockSpec((1, tf), g_col),
                  pl.BlockSpec((tf, D_MODEL), lambda bi, i, f: (f, 0)),
                  pl.BlockSpec((None, CONV_F - 1, tf), lambda bi, i, f: (bi, 0, f)),
                  pl.BlockSpec((None, CONV_F - 1, tf), lambda bi, i, f: (bi, 0, nf + f)),
                  pl.BlockSpec((1, D_MODEL), lambda bi, i, f: (0, 0))],
        out_specs=[pl.BlockSpec((None, tm, D_MODEL), lambda bi, i, f: (bi, i, 0)),
                   pl.BlockSpec((None, CONV_F - 1, tf), lambda bi, i, f: (bi, 0, f)),
                   pl.BlockSpec((None, CONV_F - 1, tf), lambda bi, i, f: (bi, 0, f))],
        out_shape=[jax.ShapeDtypeStruct((b, l, D_MODEL), F32),
                   jax.ShapeDtypeStruct((b, CONV_F - 1, D_FF), F32),
                   jax.ShapeDtypeStruct((b, CONV_F - 1, D_FF), F32)],
        scratch_shapes=[pltpu.VMEM((tm, D_MODEL), F32),
                        pltpu.VMEM((nf, SUBLANES, tf), F32), pltpu.VMEM((nf, SUBLANES, tf), F32)],
        compiler_params=_cparams(("parallel", "arbitrary", "arbitrary")),
        name="ffn",
    )(hn, x1, w_up, w_up, conv_w, conv_w, conv_b, conv_b, w_down, state, state,
      final_gain.reshape(1, D_MODEL))
    return y, jnp.concatenate([fu, fg], axis=-1)


def _round_up(n, m):
    return (n + m - 1) // m * m


def _prep_layer_weights(p, li):
    w_in = p["w_in"][li]
    zeros = lambda n: jnp.zeros((D_MODEL, n), w_in.dtype)
    k_idx = w_in[:, 2560:2624]
    w_pad = jnp.concatenate(
        [w_in[:, :2628], zeros(_C_KK - 2628), k_idx, k_idx, w_in[:, 2628:]], axis=1).astype(BF16)
    assert w_pad.shape[1] == PROJ_W_PAD

    def block_diag(w):
        out = jnp.zeros((W_C, W_C), w.dtype)
        for n in range(N_GATE_BLOCKS):
            sl = slice(n * GATE_BLOCK, (n + 1) * GATE_BLOCK)
            out = out.at[sl, sl].set(w[n])
        return out

    w_gate = jnp.concatenate([block_diag(p["rg_w_r"][li]), block_diag(p["rg_w_i"][li])], axis=1)
    b_gate = jnp.concatenate([p["rg_b_r"][li].reshape(1, W_C), p["rg_b_i"][li].reshape(1, W_C)], axis=1)
    lam_init = 0.8 - 0.6 * math.exp(-0.3 * li)
    f32 = lambda a: a.astype(F32)
    lam = (jnp.exp(jnp.sum(f32(p["lam_q1"][li]) * f32(p["lam_k1"][li])))
           - jnp.exp(jnp.sum(f32(p["lam_q2"][li]) * f32(p["lam_k2"][li]))) + lam_init)
    return dict(
        norm_mix=p["norm_mix"][li], w_pad=w_pad, lam=jnp.full((1, LANES), lam, F32),
        lam_init=lam_init, diff_gain=p["diff_gain"][li].reshape(1, LANES),
        rg_conv_w=p["rg_conv_w"][li], rg_conv_b=p["rg_conv_b"][li],
        w_gate=w_gate.astype(BF16), b_gate=b_gate, rg_lambda=p["rg_lambda"][li],
        w_out=p["w_out"][li].astype(BF16), norm_ffn=p["norm_ffn"][li],
        ffn_w_up=p["ffn_w_up"][li].astype(BF16), ffn_conv_w=p["ffn_conv_w"][li],
        ffn_conv_b=p["ffn_conv_b"][li], ffn_w_down=p["ffn_w_down"][li].astype(BF16))


def _layer(x, past, w, final_gain, final_norm):
    b, l, _ = x.shape
    a_k0, a_v0, b_k0, b_v0, b_ki0, c_h0, c_cv0, f_cv0 = past
    p_len = 0 if a_k0 is None else a_k0.shape[1]
    lk = p_len + l
    lkp = _round_up(lk, KEY_TILE)
    t = b * l

    (qa, qb, qi, ka, va, kb, vb, kw, xc, gc, kab, vab, kbb, vbb, kib) = _proj(
        x.reshape(t, D_MODEL), w["norm_mix"], w["w_pad"])

    def keys(cache, new, dup=False):
        new = new.reshape(b, l, -1)
        parts = []
        if cache is not None:
            c = cache.reshape(b, p_len, -1).astype(BF16)
            parts.append(jnp.concatenate([c, c], axis=-1) if dup else c)
        parts.append(new)
        if lkp > lk:
            parts.append(jnp.zeros((b, lkp - lk, new.shape[-1]), BF16))
        return parts[0] if len(parts) == 1 else jnp.concatenate(parts, axis=1)

    o_a = _diff_attn(qa.reshape(b, l, W_A), keys(a_k0, kab), keys(a_v0, vab), w["lam"],
                     w["diff_gain"], past=p_len, lk=lk, out_scale=1.0 - w["lam_init"])
    o_b = _dsa(qi.reshape(b, l, -1), kw.reshape(b, l, LANES), qb.reshape(b, l, W_B),
               keys(b_ki0, kib, dup=True), keys(b_k0, kbb), keys(b_v0, vbb), past=p_len, lk=lk)
    o_c, h_last, conv_new = _rglru(xc.reshape(b, l, W_C), gc.reshape(b, l, W_C), c_cv0, c_h0,
                                   w["rg_conv_w"], w["rg_conv_b"], w["w_gate"], w["b_gate"],
                                   w["rg_lambda"], past=p_len)
    x1, hn = _outproj(x.reshape(t, D_MODEL), o_a.reshape(t, W_A), o_b.reshape(t, W_B),
                      o_c.reshape(t, W_C), w["w_out"], w["norm_ffn"])
    y, f_buf = _ffn(hn.reshape(b, l, D_MODEL), x1.reshape(b, l, D_MODEL), w["ffn_w_up"],
                    w["ffn_conv_w"], w["ffn_conv_b"], w["ffn_w_down"], f_cv0, final_gain,
                    final_norm=final_norm)
    new = (ka.reshape(b, l, H_A, 2 * HEAD_DIM), va.reshape(b, l, H_A, 2 * HEAD_DIM),
           kb.reshape(b, l, H_B, HEAD_DIM), vb.reshape(b, l, H_B, HEAD_DIM),
           kw[:, :D_IDX].reshape(b, l, D_IDX), h_last.reshape(b, W_C), conv_new, f_buf)
    return y, new


def _trunk(x, past, weights, final_gain):
    states = []
    for li in range(N_LAYERS):
        layer_past = tuple(None if c is None else c[li] for c in past)
        x, st = _layer(x, layer_past, weights[li], final_gain, final_norm=(li == N_LAYERS - 1))
        states.append(st)
    return x, states


def _forward(x_prompt, x_sample, caches, params):
    weights = [_prep_layer_weights(params, li) for li in range(N_LAYERS)]
    bp = x_prompt.shape[0]
    dt = x_prompt.dtype
    past_prompt = (None, None, None, None, None,
                   jnp.zeros((N_LAYERS, bp, W_C), dt),
                   jnp.zeros((N_LAYERS, bp, CONV_C - 1, W_C), dt),
                   jnp.zeros((N_LAYERS, bp, CONV_F - 1, 2 * D_FF), dt))
    yp, sp = _trunk(x_prompt, past_prompt, weights, params["norm_final"])
    ys, ss = _trunk(x_sample, caches, weights, params["norm_final"])
    out = [yp, ys]
    for jdx in range(8):
        out.append(jnp.stack([st[jdx] for st in sp], axis=0))
        out.append(jnp.stack([st[jdx] for st in ss], axis=0))
    return tuple(out)


def kernel(x_prompt, x_sample, cache_a_k, cache_a_v, cache_b_k, cache_b_v, cache_b_kidx,
           state_c_h, state_c_conv, state_ffn_conv, norm_mix, w_in, lam_q1, lam_k1, lam_q2,
           lam_k2, diff_gain, rg_conv_w, rg_conv_b, rg_w_r, rg_b_r, rg_w_i, rg_b_i, rg_lambda,
           w_out, norm_ffn, ffn_w_up, ffn_conv_w, ffn_conv_b, ffn_w_down, norm_final):
    params = dict(norm_mix=norm_mix, w_in=w_in, lam_q1=lam_q1, lam_k1=lam_k1, lam_q2=lam_q2,
                  lam_k2=lam_k2, diff_gain=diff_gain, rg_conv_w=rg_conv_w, rg_conv_b=rg_conv_b,
                  rg_w_r=rg_w_r, rg_b_r=rg_b_r, rg_w_i=rg_w_i, rg_b_i=rg_b_i, rg_lambda=rg_lambda,
                  w_out=w_out, norm_ffn=norm_ffn, ffn_w_up=ffn_w_up, ffn_conv_w=ffn_conv_w,
                  ffn_conv_b=ffn_conv_b, ffn_w_down=ffn_w_down, norm_final=norm_final)
    caches = (cache_a_k, cache_a_v, cache_b_k, cache_b_v, cache_b_kidx,
              state_c_h, state_c_conv, state_ffn_conv)
    return _forward(x_prompt, x_sample, caches, params)
```

```python
import functools
import math

import jax
import jax.numpy as jnp
from jax import lax
from jax.experimental import pallas as pl
from jax.experimental.pallas import tpu as pltpu

F32 = jnp.float32
BF16 = jnp.bfloat16

D_MODEL = 1024
N_LAYERS = 2
CHUNK = 64
CHUNK_SHIFT = 6
HEAD_DIM = 64
H_A = 4
W_A = H_A * 2 * HEAD_DIM
H_B = 4
W_B = H_B * HEAD_DIM
H_IDX = 4
D_IDX = 64
TOPK = 256
W_C = 256
N_GATE_BLOCKS = 4
GATE_BLOCK = W_C // N_GATE_BLOCKS
RG_C = 8.0
CONV_C = 4
D_FF = 3072
CONV_F = 3
EPS = 1e-6

LANES = 128
SUBLANES = 8
MXU_DIM = 256
Q_BLOCK = 256
SCAN_ROWS = 128
PROMPT_KEY_TILE = 1024
VMEM_LIMIT = 58 * 2**20
MASKED = -1e30
N_BISECT = 20
Q_SCALE = HEAD_DIM ** -0.5 * math.log2(math.e)

PROJ_W_PAD = 3328
_C_QA, _C_KA, _C_VA = 0, 512, 1024
_C_QB, _C_KB, _C_VB = 1536, 1792, 2048
_C_QI = 2304
_C_KW = 2560
_C_KK = 2688
_C_XC = 2816
_C_GC = 3072


def _cparams(sem):
    return pltpu.CompilerParams(dimension_semantics=sem, vmem_limit_bytes=VMEM_LIMIT)


def _rms(x, g):
    return x * lax.rsqrt(jnp.mean(x * x, axis=-1, keepdims=True) + EPS) * g


def _dot_nt(a, b):
    return lax.dot_general(a, b, (((1,), (1,)), ((), ())), preferred_element_type=F32)


def _round_up(n, m):
    return (n + m - 1) // m * m


def _proj_kernel(x_ref, g_ref, w_ref, qa_ref, qb_ref, qi_ref, ka_ref, va_ref, kb_ref, vb_ref,
                 kw_ref, xc_ref, gc_ref, kab_ref, vab_ref, kbb_ref, vbb_ref, kib_ref):
    h = _rms(x_ref[...], g_ref[...])
    z = jnp.dot(h.astype(BF16), w_ref[...], preferred_element_type=F32)
    qa_ref[...] = (z[:, _C_QA:_C_QA + W_A] * Q_SCALE).astype(BF16)
    qb_ref[...] = (z[:, _C_QB:_C_QB + W_B] * Q_SCALE).astype(BF16)
    qi_ref[...] = z[:, _C_QI:_C_QI + H_IDX * D_IDX].astype(BF16)
    ka = z[:, _C_KA:_C_KA + W_A]
    va = z[:, _C_VA:_C_VA + W_A]
    kb = z[:, _C_KB:_C_KB + W_B]
    vb = z[:, _C_VB:_C_VB + W_B]
    ka_ref[...] = ka
    va_ref[...] = va
    kb_ref[...] = kb
    vb_ref[...] = vb
    kab_ref[...] = ka.astype(BF16)
    vab_ref[...] = va.astype(BF16)
    kbb_ref[...] = kb.astype(BF16)
    vbb_ref[...] = vb.astype(BF16)
    kw_ref[...] = z[:, _C_KW:_C_KW + LANES]
    kib_ref[...] = z[:, _C_KK:_C_KK + LANES].astype(BF16)
    xc_ref[...] = z[:, _C_XC:_C_XC + W_C]
    gc_ref[...] = z[:, _C_GC:_C_GC + W_C]


def _proj(x2d, gain, w_pad):
    t = x2d.shape[0]
    tm = min(512, t)
    widths = [(W_A, BF16), (W_B, BF16), (H_IDX * D_IDX, BF16),
              (W_A, F32), (W_A, F32), (W_B, F32), (W_B, F32),
              (LANES, F32), (W_C, F32), (W_C, F32),
              (W_A, BF16), (W_A, BF16), (W_B, BF16), (W_B, BF16), (LANES, BF16)]
    return pl.pallas_call(
        _proj_kernel,
        grid=(t // tm,),
        in_specs=[pl.BlockSpec((tm, D_MODEL), lambda i: (i, 0)),
                  pl.BlockSpec((1, D_MODEL), lambda i: (0, 0)),
                  pl.BlockSpec((D_MODEL, PROJ_W_PAD), lambda i: (0, 0))],
        out_specs=[pl.BlockSpec((tm, w), lambda i: (i, 0)) for w, _ in widths],
        out_shape=[jax.ShapeDtypeStruct((t, w), d) for w, d in widths],
        compiler_params=_cparams(("parallel",)),
        name="proj",
    )(x2d, gain.reshape(1, D_MODEL), w_pad)


def _tile_bounds(q_start, qb, lk, tk):
    n_full = lax.div(jnp.minimum(q_start + CHUNK, lk), tk)
    n_tiles = lax.div(q_start + qb + tk - 1, tk)
    return n_full, n_tiles


def _admissible(start, tk, q_chunk, lk):
    col = start + lax.broadcasted_iota(jnp.int32, (1, tk), 1)
    return (lax.shift_right_logical(col, CHUNK_SHIFT) <= q_chunk) & (col < lk)


def _lane_max(acc, x):
    for g in range(x.shape[1] // LANES):
        acc = jnp.maximum(acc, x[:, g * LANES:(g + 1) * LANES])
    return acc


def _softmax_value_tile(s_ref, idx, m, lsum, v):
    parts = []
    for g in range(s_ref.shape[-1] // LANES):
        p = jnp.exp2(s_ref[idx + (slice(None), slice(g * LANES, (g + 1) * LANES))] - m)
        lsum = lsum + p
        parts.append(p.astype(BF16))
    return lsum, jnp.dot(jnp.concatenate(parts, axis=1), v, preferred_element_type=F32)


def _diff_attn_kernel(lam_ref, gain_ref, q_ref, k_ref, v_ref, o_ref, s_ref, mx_ref, l_ref, acc_ref,
                      *, past, lk, qb, tk, out_scale):
    j = pl.program_id(2)
    q = q_ref[...]
    lane = lax.broadcasted_iota(jnp.int32, (1, LANES), 1)
    zero = jnp.zeros_like(q)
    q_half = (jnp.where(lane < HEAD_DIM, q, zero), jnp.where(lane >= HEAD_DIM, q, zero))
    q_start = past + j * qb
    row = lax.broadcasted_iota(jnp.int32, (qb, 1), 0)
    q_chunk = lax.shift_right_logical(q_start + row, CHUNK_SHIFT)
    n_full, n_tiles = _tile_bounds(q_start, qb, lk, tk)

    mx_ref[...] = jnp.full(mx_ref.shape, -jnp.inf, F32)
    l_ref[...] = jnp.zeros(l_ref.shape, F32)
    acc_ref[...] = jnp.zeros(acc_ref.shape, F32)

    def score_tile(t, masked):
        start = pl.multiple_of(t * tk, tk)
        k = k_ref[pl.ds(start, tk), :]
        if masked:
            ok = _admissible(start, tk, q_chunk, lk)
        for i in range(2):
            s = _dot_nt(q_half[i], k)
            if masked:
                s = jnp.where(ok, s, -jnp.inf)
            s_ref[i, t] = s
            mx_ref[i] = _lane_max(mx_ref[i], s)

    def full_body(t, c):
        score_tile(t, False)
        return c

    def masked_body(t, c):
        score_tile(t, True)
        return c

    lax.fori_loop(0, n_full, full_body, 0)
    lax.fori_loop(n_full, n_tiles, masked_body, 0)

    for i in range(2):
        mx_ref[i] = jnp.broadcast_to(jnp.max(mx_ref[i], axis=1, keepdims=True), (qb, LANES))

    def value_tile(t, c):
        start = pl.multiple_of(t * tk, tk)
        v = v_ref[pl.ds(start, tk), :]
        for i in range(2):
            lsum, pv = _softmax_value_tile(s_ref, (i, t), mx_ref[i], l_ref[i], v)
            l_ref[i] = lsum
            acc_ref[i] = acc_ref[i] + pv
        return c

    lax.fori_loop(0, n_tiles, value_tile, 0)

    l1 = jnp.sum(l_ref[0], axis=1, keepdims=True)
    l2 = jnp.sum(l_ref[1], axis=1, keepdims=True)
    o = acc_ref[0] / l1 - lam_ref[...] * (acc_ref[1] / l2)
    o_ref[...] = (_rms(o, gain_ref[...]) * out_scale).astype(BF16)


def _diff_attn(q, k, v, lam, gain, *, past, lk, tk, out_scale):
    b, l, _ = q.shape
    lkp = k.shape[1]
    qb = min(l, Q_BLOCK)
    assert l % qb == 0 and lkp % tk == 0 and qb % CHUNK == 0
    kern = functools.partial(_diff_attn_kernel, past=past, lk=lk, qb=qb, tk=tk, out_scale=out_scale)
    return pl.pallas_call(
        kern,
        grid=(b, H_A, l // qb),
        in_specs=[pl.BlockSpec((1, LANES), lambda bi, h, j: (0, 0)),
                  pl.BlockSpec((1, LANES), lambda bi, h, j: (0, 0)),
                  pl.BlockSpec((None, qb, LANES), lambda bi, h, j: (bi, j, h)),
                  pl.BlockSpec((None, lkp, LANES), lambda bi, h, j: (bi, 0, h)),
                  pl.BlockSpec((None, lkp, LANES), lambda bi, h, j: (bi, 0, h))],
        out_specs=pl.BlockSpec((None, qb, LANES), lambda bi, h, j: (bi, j, h)),
        out_shape=jax.ShapeDtypeStruct((b, l, W_A), BF16),
        scratch_shapes=[pltpu.VMEM((2, lkp // tk, qb, tk), F32),
                        pltpu.VMEM((2, qb, LANES), F32), pltpu.VMEM((2, qb, LANES), F32),
                        pltpu.VMEM((2, qb, LANES), F32)],
        compiler_params=_cparams(("parallel", "parallel", "parallel")),
        name="diff_attn",
    )(lam, gain, q, k, v)


def _dsa_kernel(qi_ref, kw_ref, q_ref, ki_ref, k_ref, v_ref, o_ref,
                s_ref, sc_ref, lo_ref, hi_ref, mx_ref, l_ref, acc_ref,
                *, past, lk, qb, tk, topk, max_steps):
    j = pl.program_id(1)
    q_start = past + j * qb
    n_full, n_tiles = _tile_bounds(q_start, qb, lk, tk)
    n_groups = tk // LANES
    row = lax.broadcasted_iota(jnp.int32, (qb, 1), 0)
    q_chunk = lax.shift_right_logical(q_start + row, CHUNK_SHIFT)
    n_adm = jnp.minimum((q_chunk + 1) * CHUNK, lk)
    few = n_adm <= topk
    lane = lax.broadcasted_iota(jnp.int32, (1, LANES), 1)
    low_half = lane < HEAD_DIM
    k_sel = float(topk)

    def wide(x):
        return jnp.broadcast_to(x, (qb, LANES))

    def head_views(x):
        views = []
        for h in range(4):
            pair = x[:, (h // 2) * LANES:(h // 2 + 1) * LANES]
            keep = low_half if h % 2 == 0 else jnp.logical_not(low_half)
            views.append(jnp.where(keep, pair, jnp.zeros_like(pair)))
        return views

    qi_h = head_views(qi_ref[...])
    kw = kw_ref[...]
    w_h = [wide(kw[:, D_IDX + h:D_IDX + h + 1]) for h in range(H_IDX)]
    lo_ref[...] = jnp.full(lo_ref.shape, jnp.inf, F32)
    hi_ref[...] = jnp.full(hi_ref.shape, -jnp.inf, F32)

    def index_tile(t, masked):
        start = pl.multiple_of(t * tk, tk)
        if masked:
            adm = _admissible(start, tk, q_chunk, lk)
        rmin = lo_ref[...]
        rmax = hi_ref[...]
        for c in range(tk // MXU_DIM):
            ki = ki_ref[pl.ds(start + c * MXU_DIM, MXU_DIM), :]
            rel = [jnp.maximum(_dot_nt(qi_h[h], ki), 0.0) for h in range(H_IDX)]
            for g in range(MXU_DIM // LANES):
                sl = slice(g * LANES, (g + 1) * LANES)
                sc = w_h[0] * rel[0][:, sl]
                for h in range(1, H_IDX):
                    sc = sc + w_h[h] * rel[h][:, sl]
                csl = slice(c * MXU_DIM + g * LANES, c * MXU_DIM + (g + 1) * LANES)
                if masked:
                    ok = adm[:, csl]
                    s_ref[t, :, csl] = jnp.where(ok, sc, -jnp.inf)
                    rmax = jnp.maximum(rmax, jnp.where(ok, sc, -jnp.inf))
                    rmin = jnp.minimum(rmin, jnp.where(ok, sc, jnp.inf))
                else:
                    s_ref[t, :, csl] = sc
                    rmax = jnp.maximum(rmax, sc)
                    rmin = jnp.minimum(rmin, sc)
        lo_ref[...] = rmin
        hi_ref[...] = rmax

    def index_full(t, c):
        index_tile(t, False)
        return c

    def index_masked(t, c):
        index_tile(t, True)
        return c

    lax.fori_loop(0, n_full, index_full, 0)
    lax.fori_loop(n_full, n_tiles, index_masked, 0)
    rmin = jnp.min(lo_ref[...], axis=1, keepdims=True)
    rmax = jnp.max(hi_ref[...], axis=1, keepdims=True)

    rb = min(qb, SCAN_ROWS)

    def scan(step, init, reduce, operands, store=None):
        outs = []
        for r in range(qb // rb):
            rows = slice(r * rb, (r + 1) * rb)
            ops = [jnp.broadcast_to(o[rows], (rb, LANES)) for o in operands]

            def body(t, acc, rows=rows, ops=ops):
                for g in range(n_groups):
                    lanes = slice(g * LANES, (g + 1) * LANES)
                    s = s_ref[t, rows, lanes]
                    col = t * tk + g * LANES + lane
                    if store is None:
                        acc = step(acc, s, col, *ops)
                    else:
                        s_ref[t, rows, lanes] = store(s, col, *ops)
                return acc

            acc = lax.fori_loop(0, n_tiles, body, jnp.full((rb, LANES), init, F32))
            if reduce is not None:
                outs.append(reduce(acc, axis=1, keepdims=True))
        if reduce is None:
            return None
        return outs[0] if len(outs) == 1 else jnp.concatenate(outs, axis=0)

    def count(pred, *operands):
        return scan(lambda acc, s, col, *ops: acc + jnp.where(pred(s, col, *ops), 1.0, 0.0),
                    0.0, jnp.sum, operands)

    def count_ge(thr):
        return count(lambda s, col, t: s >= t, thr)

    def max_below(bound):
        return scan(lambda acc, s, col, b: jnp.maximum(acc, jnp.where(s < b, s, -jnp.inf)),
                    -jnp.inf, jnp.max, (bound,))

    top_ties = count_ge(rmax) >= k_sel

    def bisect(_, carry):
        lo, hi = carry
        mid = 0.5 * lo + 0.5 * hi
        ge = count_ge(mid) >= k_sel
        return jnp.where(ge, mid, lo), jnp.where(ge, hi, mid)

    _, hi = lax.fori_loop(0, N_BISECT, bisect, (rmin, rmax))

    def walk_cond(st):
        it, _, _, _, active = st
        return jnp.logical_and(jnp.max(active) > 0.0, it < max_steps)

    def walk_body(st):
        it, cand, thr, c_thr, active = st
        c = count_ge(cand)
        ok = c >= k_sel
        act = active > 0.0
        hit = jnp.logical_and(act, ok)
        thr = jnp.where(hit, cand, thr)
        c_thr = jnp.where(hit, c, c_thr)
        active = jnp.where(jnp.logical_and(act, jnp.logical_not(ok)), 1.0, 0.0)
        return it + 1, max_below(cand), thr, c_thr, active

    settled = jnp.logical_or(few, top_ties)
    thr0 = jnp.where(jnp.logical_and(top_ties, jnp.logical_not(few)), rmax, -jnp.inf)
    _, _, thr, c_ge, _ = lax.while_loop(
        walk_cond, walk_body,
        (jnp.int32(0), max_below(hi), thr0, jnp.full((qb, 1), k_sel, F32) + jnp.where(top_ties, 1.0, 0.0),
         jnp.where(settled, 0.0, 1.0)))

    has_excess = jnp.max(jnp.where(jnp.logical_and(c_ge > k_sel, jnp.logical_not(few)), 1.0, 0.0)) > 0.0
    n_bits = max(1, (s_ref.shape[0] * tk - 1).bit_length())

    def tie_search():
        n_tie = k_sel - count(lambda s, col, t: s > t, thr)

        def body(i, jcut):
            cand = jcut + lax.shift_left(jnp.int32(1), jnp.int32(n_bits - 1) - i)
            below = count(lambda s, col, t, c: jnp.logical_and(s == t, col < c), thr, cand)
            return jnp.where(below <= n_tie - 1.0, cand, jcut)
        return lax.fori_loop(0, n_bits, body, jnp.zeros((qb, 1), jnp.int32))

    jcut = lax.cond(has_excess, tie_search, lambda: jnp.full((qb, 1), 2**30, jnp.int32))

    def selection_bias(s, col, t, jc):
        sel = jnp.logical_or(s > t, jnp.logical_and(s == t, col <= jc))
        return jnp.where(jnp.logical_and(sel, s > -jnp.inf), 0.0, MASKED)

    scan(None, 0.0, None, (thr, jcut), store=selection_bias)

    q_h = head_views(q_ref[...])
    for g in range(H_B // 2):
        lanes = slice(g * LANES, (g + 1) * LANES)
        mx_ref[...] = jnp.full(mx_ref.shape, -jnp.inf, F32)
        l_ref[...] = jnp.zeros(l_ref.shape, F32)
        acc_ref[...] = jnp.zeros(acc_ref.shape, F32)

        def score_tile(t, c, g=g, lanes=lanes):
            start = pl.multiple_of(t * tk, tk)
            k = k_ref[pl.ds(start, tk), lanes]
            bias = s_ref[t]
            for i in range(2):
                s = _dot_nt(q_h[2 * g + i], k) + bias
                sc_ref[i, t] = s
                mx_ref[i] = _lane_max(mx_ref[i], s)
            return c

        lax.fori_loop(0, n_tiles, score_tile, 0)
        for i in range(2):
            mx_ref[i] = jnp.broadcast_to(jnp.max(mx_ref[i], axis=1, keepdims=True), (qb, LANES))

        def value_tile(t, c, lanes=lanes):
            start = pl.multiple_of(t * tk, tk)
            v = v_ref[pl.ds(start, tk), lanes]
            for i in range(2):
                lsum, pv = _softmax_value_tile(sc_ref, (i, t), mx_ref[i], l_ref[i], v)
                l_ref[i] = lsum
                acc_ref[i] = acc_ref[i] + pv
            return c

        lax.fori_loop(0, n_tiles, value_tile, 0)
        even = acc_ref[0] / jnp.sum(l_ref[0], axis=1, keepdims=True)
        odd = acc_ref[1] / jnp.sum(l_ref[1], axis=1, keepdims=True)
        o_ref[:, lanes] = jnp.where(low_half, even, odd).astype(BF16)


def _dsa(qi, kw, q, ki, k, v, *, past, lk, tk):
    b, l, _ = q.shape
    lkp = k.shape[1]
    qb = min(l, Q_BLOCK)
    assert l % qb == 0 and lkp % tk == 0 and tk % MXU_DIM == 0 and qb % CHUNK == 0
    kern = functools.partial(_dsa_kernel, past=past, lk=lk, qb=qb, tk=tk,
                             topk=min(TOPK, lk // 4), max_steps=lkp)
    return pl.pallas_call(
        kern,
        grid=(b, l // qb),
        in_specs=[pl.BlockSpec((None, qb, H_IDX * D_IDX), lambda bi, j: (bi, j, 0)),
                  pl.BlockSpec((None, qb, LANES), lambda bi, j: (bi, j, 0)),
                  pl.BlockSpec((None, qb, W_B), lambda bi, j: (bi, j, 0)),
                  pl.BlockSpec((None, lkp, LANES), lambda bi, j: (bi, 0, 0)),
                  pl.BlockSpec((None, lkp, W_B), lambda bi, j: (bi, 0, 0)),
                  pl.BlockSpec((None, lkp, W_B), lambda bi, j: (bi, 0, 0))],
        out_specs=pl.BlockSpec((None, qb, W_B), lambda bi, j: (bi, j, 0)),
        out_shape=jax.ShapeDtypeStruct((b, l, W_B), BF16),
        scratch_shapes=[pltpu.VMEM((lkp // tk, qb, tk), F32),
                        pltpu.VMEM((2, lkp // tk, qb, tk), F32),
                        pltpu.VMEM((qb, LANES), F32), pltpu.VMEM((qb, LANES), F32),
                        pltpu.VMEM((2, qb, LANES), F32), pltpu.VMEM((2, qb, LANES), F32),
                        pltpu.VMEM((2, qb, LANES), F32)],
        compiler_params=_cparams(("parallel", "parallel")),
        name="dsa",
    )(qi, kw, q, ki, k, v)


def _shift_rows(x, prev, k):
    rolled = pltpu.roll(x, k, 0)
    row = lax.broadcasted_iota(jnp.int32, (SUBLANES, 1), 0)
    top = jnp.where(row < k, pltpu.roll(prev, k, 0), rolled[0:SUBLANES])
    if x.shape[0] == SUBLANES:
        return top
    return jnp.concatenate([top, rolled[SUBLANES:]], axis=0)


def _rglru_kernel(xc_ref, gc_ref, cst_ref, h0_ref, cw_ref, cb_ref, wg_ref, bg_ref, lam_ref,
                  oc_ref, hl_ref, cn_ref, prev_ref, h_ref, a_ref, b_ref, hs_ref, *, past, tl):
    i = pl.program_id(1)

    @pl.when(i == 0)
    def _():
        prev_ref[...] = jnp.zeros(prev_ref.shape, F32)
        prev_ref[SUBLANES - (CONV_C - 1):SUBLANES, :] = cst_ref[...]
        h_ref[...] = h0_ref[...]

    x = xc_ref[...]
    prev = prev_ref[...]
    cw = cw_ref[...]
    xconv = cw[CONV_C - 1:CONV_C] * x + cb_ref[...]
    for k in range(1, CONV_C):
        xconv = xconv + cw[CONV_C - 1 - k:CONV_C - k] * _shift_rows(x, prev, k)
    prev_ref[...] = x[tl - SUBLANES:tl]

    pre = jnp.dot(xconv.astype(BF16), wg_ref[...], preferred_element_type=F32) + bg_ref[...]
    r = jax.nn.sigmoid(pre[:, :W_C])
    gate_i = jax.nn.sigmoid(pre[:, W_C:])
    neg_lam = -lam_ref[...]
    softplus = jnp.maximum(neg_lam, 0.0) + jnp.log1p(jnp.exp(-jnp.abs(neg_lam)))
    log_a = -RG_C * r * softplus
    pos = past + i * tl + lax.broadcasted_iota(jnp.int32, (tl, 1), 0)
    th = jnp.tanh(log_a)
    mult = jnp.where(pos == 0, 1.0, jnp.sqrt(-2.0 * th / (1.0 - th)))
    a_ref[...] = jnp.exp(log_a)
    b_ref[...] = mult * gate_i * xconv

    def step(t, h):
        h = a_ref[pl.ds(t, 1), :] * h + b_ref[pl.ds(t, 1), :]
        hs_ref[pl.ds(t, 1), :] = h
        return h

    h_last = lax.fori_loop(0, tl, step, h_ref[...], unroll=8)
    h_ref[...] = h_last

    gc = gc_ref[...]
    gelu = 0.5 * gc * (1.0 + jnp.tanh(math.sqrt(2.0 / math.pi) * (gc + 0.044715 * (gc * gc * gc))))
    oc_ref[...] = (hs_ref[...] * gelu).astype(BF16)

    @pl.when(i == pl.num_programs(1) - 1)
    def _():
        hl_ref[...] = h_last
        cn_ref[...] = x[tl - (CONV_C - 1):tl]


def _rglru(xc, gc, conv_state, h0, conv_w, conv_b, w_gate, b_gate, lam, *, past):
    b, l, _ = xc.shape
    tl = min(l, 512)
    assert l >= SUBLANES and l % tl == 0
    kern = functools.partial(_rglru_kernel, past=past, tl=tl)
    const = lambda bi, i: (0, 0)
    return pl.pallas_call(
        kern,
        grid=(b, l // tl),
        in_specs=[pl.BlockSpec((None, tl, W_C), lambda bi, i: (bi, i, 0)),
                  pl.BlockSpec((None, tl, W_C), lambda bi, i: (bi, i, 0)),
                  pl.BlockSpec((None, CONV_C - 1, W_C), lambda bi, i: (bi, 0, 0)),
                  pl.BlockSpec((None, 1, W_C), lambda bi, i: (bi, 0, 0)),
                  pl.BlockSpec((CONV_C, W_C), const),
                  pl.BlockSpec((1, W_C), const),
                  pl.BlockSpec((W_C, 2 * W_C), const),
                  pl.BlockSpec((1, 2 * W_C), const),
                  pl.BlockSpec((1, W_C), const)],
        out_specs=[pl.BlockSpec((None, tl, W_C), lambda bi, i: (bi, i, 0)),
                   pl.BlockSpec((None, 1, W_C), lambda bi, i: (bi, 0, 0)),
                   pl.BlockSpec((None, CONV_C - 1, W_C), lambda bi, i: (bi, 0, 0))],
        out_shape=[jax.ShapeDtypeStruct((b, l, W_C), BF16),
                   jax.ShapeDtypeStruct((b, 1, W_C), F32),
                   jax.ShapeDtypeStruct((b, CONV_C - 1, W_C), F32)],
        scratch_shapes=[pltpu.VMEM((SUBLANES, W_C), F32), pltpu.VMEM((1, W_C), F32),
                        pltpu.VMEM((tl, W_C), F32), pltpu.VMEM((tl, W_C), F32),
                        pltpu.VMEM((tl, W_C), F32)],
        compiler_params=_cparams(("parallel", "arbitrary")),
        name="rglru",
    )(xc, gc, conv_state, h0.reshape(b, 1, W_C), conv_w, conv_b.reshape(1, W_C), w_gate,
      b_gate, lam.reshape(1, W_C))


def _outproj_kernel(x_ref, oa_ref, ob_ref, oc_ref, w_ref, g_ref, x1_ref, hn_ref):
    mix = jnp.dot(oa_ref[...], w_ref[0:W_A, :], preferred_element_type=F32)
    mix = mix + jnp.dot(ob_ref[...], w_ref[W_A:W_A + W_B, :], preferred_element_type=F32)
    mix = mix + jnp.dot(oc_ref[...], w_ref[W_A + W_B:, :], preferred_element_type=F32)
    x1 = x_ref[...] + mix
    x1_ref[...] = x1
    hn_ref[...] = _rms(x1, g_ref[...]).astype(BF16)


def _outproj(x2d, oa, ob, oc, w_out, gain):
    t = x2d.shape[0]
    tm = min(512, t)
    row = lambda w: pl.BlockSpec((tm, w), lambda i: (i, 0))
    return pl.pallas_call(
        _outproj_kernel,
        grid=(t // tm,),
        in_specs=[row(D_MODEL), row(W_A), row(W_B), row(W_C),
                  pl.BlockSpec((D_MODEL, D_MODEL), lambda i: (0, 0)),
                  pl.BlockSpec((1, D_MODEL), lambda i: (0, 0))],
        out_specs=[row(D_MODEL), row(D_MODEL)],
        out_shape=[jax.ShapeDtypeStruct((t, D_MODEL), F32), jax.ShapeDtypeStruct((t, D_MODEL), BF16)],
        compiler_params=_cparams(("parallel",)),
        name="outproj",
    )(x2d, oa, ob, oc, w_out, gain.reshape(1, D_MODEL))


def _ffn_kernel(hn_ref, x1_ref, wu_ref, wg_ref, cwu_ref, cwg_ref, cbu_ref, cbg_ref, wd_ref,
                su_ref, sg_ref, gfin_ref, y_ref, fu_ref, fg_ref, acc_ref, cu_ref, cg_ref,
                *, tm, final_norm):
    i = pl.program_id(1)
    f = pl.program_id(2)
    hn = hn_ref[...]

    def conv_branch(w_ref, cw_ref, cb_ref, st_ref, carry_ref, tail_ref):
        a = jnp.dot(hn, w_ref[...], preferred_element_type=F32)

        @pl.when(i == 0)
        def _():
            carry_ref[f] = jnp.zeros(carry_ref.shape[1:], F32)
            carry_ref[f, SUBLANES - (CONV_F - 1):SUBLANES, :] = st_ref[...]

        prev = carry_ref[f]
        cw = cw_ref[...]
        y = cw[CONV_F - 1:CONV_F] * a + cb_ref[...]
        for k in range(1, CONV_F):
            y = y + cw[CONV_F - 1 - k:CONV_F - k] * _shift_rows(a, prev, k)
        carry_ref[f] = a[tm - SUBLANES:tm]
        tail_ref[f] = a[tm - (CONV_F - 1):tm]
        return y

    u = conv_branch(wu_ref, cwu_ref, cbu_ref, su_ref, cu_ref, fu_ref)
    g = conv_branch(wg_ref, cwg_ref, cbg_ref, sg_ref, cg_ref, fg_ref)
    mid = (g * jax.nn.sigmoid(g) * u).astype(BF16)
    contrib = jnp.dot(mid, wd_ref[...], preferred_element_type=F32)

    @pl.when(f == 0)
    def _():
        acc_ref[...] = x1_ref[...] + contrib

    @pl.when(f > 0)
    def _():
        acc_ref[...] = acc_ref[...] + contrib

    @pl.when(f == pl.num_programs(2) - 1)
    def _():
        y = acc_ref[...]
        if final_norm:
            y = _rms(y, gfin_ref[...])
        y_ref[...] = y


def _ffn(hn, x1, w_up, conv_w, conv_b, w_down, state, final_gain, *, final_norm):
    b, l, _ = hn.shape
    tm = min(l, 1024)
    tf = 512
    nf = D_FF // tf
    assert l % tm == 0 and tm >= SUBLANES
    kern = functools.partial(_ffn_kernel, tm=tm, final_norm=final_norm)
    conv_b = conv_b.reshape(1, 2 * D_FF)
    u_col = lambda bi, i, f: (0, f)
    g_col = lambda bi, i, f: (0, nf + f)
    tail_spec = pl.BlockSpec((None, nf, CONV_F - 1, tf), lambda bi, i, f: (bi, 0, 0, 0))
    tail_shape = jax.ShapeDtypeStruct((b, nf, CONV_F - 1, tf), F32)
    y, fu, fg = pl.pallas_call(
        kern,
        grid=(b, l // tm, nf),
        in_specs=[pl.BlockSpec((None, tm, D_MODEL), lambda bi, i, f: (bi, i, 0)),
                  pl.BlockSpec((None, tm, D_MODEL), lambda bi, i, f: (bi, i, 0)),
                  pl.BlockSpec((D_MODEL, tf), u_col),
                  pl.BlockSpec((D_MODEL, tf), g_col),
                  pl.BlockSpec((CONV_F, tf), u_col),
                  pl.BlockSpec((CONV_F, tf), g_col),
                  pl.BlockSpec((1, tf), u_col),
                  pl.BlockSpec((1, tf), g_col),
                  pl.BlockSpec((tf, D_MODEL), lambda bi, i, f: (f, 0)),
                  pl.BlockSpec((None, CONV_F - 1, tf), lambda bi, i, f: (bi, 0, f)),
                  pl.BlockSpec((None, CONV_F - 1, tf), lambda bi, i, f: (bi, 0, nf + f)),
                  pl.BlockSpec((1, D_MODEL), lambda bi, i, f: (0, 0))],
        out_specs=[pl.BlockSpec((None, tm, D_MODEL), lambda bi, i, f: (bi, i, 0)),
                   tail_spec, tail_spec],
        out_shape=[jax.ShapeDtypeStruct((b, l, D_MODEL), F32), tail_shape, tail_shape],
        scratch_shapes=[pltpu.VMEM((tm, D_MODEL), F32),
                        pltpu.VMEM((nf, SUBLANES, tf), F32), pltpu.VMEM((nf, SUBLANES, tf), F32)],
        compiler_params=_cparams(("parallel", "arbitrary", "arbitrary")),
        name="ffn",
    )(hn, x1, w_up, w_up, conv_w, conv_w, conv_b, conv_b, w_down, state, state,
      final_gain.reshape(1, D_MODEL))
    flat = lambda a: jnp.swapaxes(a, 1, 2).reshape(b, CONV_F - 1, D_FF)
    return y, jnp.concatenate([flat(fu), flat(fg)], axis=-1)


def _prep_layer_weights(p, li):
    w_in = p["w_in"][li]
    zeros = lambda n: jnp.zeros((D_MODEL, n), w_in.dtype)
    k_idx = w_in[:, 2560:2624]
    w_pad = jnp.concatenate(
        [w_in[:, :2628], zeros(_C_KK - 2628), k_idx, k_idx, w_in[:, 2628:]], axis=1).astype(BF16)
    assert w_pad.shape[1] == PROJ_W_PAD

    def block_diag(w):
        out = jnp.zeros((W_C, W_C), w.dtype)
        for n in range(N_GATE_BLOCKS):
            sl = slice(n * GATE_BLOCK, (n + 1) * GATE_BLOCK)
            out = out.at[sl, sl].set(w[n])
        return out

    w_gate = jnp.concatenate([block_diag(p["rg_w_r"][li]), block_diag(p["rg_w_i"][li])], axis=1)
    b_gate = jnp.concatenate([p["rg_b_r"][li].reshape(1, W_C), p["rg_b_i"][li].reshape(1, W_C)], axis=1)
    lam_init = 0.8 - 0.6 * math.exp(-0.3 * li)
    f32 = lambda a: a.astype(F32)
    lam = (jnp.exp(jnp.sum(f32(p["lam_q1"][li]) * f32(p["lam_k1"][li])))
           - jnp.exp(jnp.sum(f32(p["lam_q2"][li]) * f32(p["lam_k2"][li]))) + lam_init)
    return dict(
        norm_mix=p["norm_mix"][li], w_pad=w_pad, lam=jnp.full((1, LANES), lam, F32),
        lam_init=lam_init, diff_gain=p["diff_gain"][li].reshape(1, LANES),
        rg_conv_w=p["rg_conv_w"][li], rg_conv_b=p["rg_conv_b"][li],
        w_gate=w_gate.astype(BF16), b_gate=b_gate, rg_lambda=p["rg_lambda"][li],
        w_out=p["w_out"][li].astype(BF16), norm_ffn=p["norm_ffn"][li],
        ffn_w_up=p["ffn_w_up"][li].astype(BF16), ffn_conv_w=p["ffn_conv_w"][li],
        ffn_conv_b=p["ffn_conv_b"][li], ffn_w_down=p["ffn_w_down"][li].astype(BF16))


def _layer(x, past, w, final_gain, final_norm):
    b, l, _ = x.shape
    a_k0, a_v0, b_k0, b_v0, b_ki0, c_h0, c_cv0, f_cv0 = past
    p_len = 0 if a_k0 is None else a_k0.shape[1]
    lk = p_len + l
    tk = PROMPT_KEY_TILE if p_len == 0 else _round_up(lk, MXU_DIM)
    lkp = _round_up(lk, tk)
    t = b * l

    (qa, qb, qi, ka, va, kb, vb, kw, xc, gc, kab, vab, kbb, vbb, kib) = _proj(
        x.reshape(t, D_MODEL), w["norm_mix"], w["w_pad"])

    def keys(cache, new, dup=False):
        new = new.reshape(b, l, -1)
        parts = []
        if cache is not None:
            c = cache.reshape(b, p_len, -1).astype(BF16)
            parts.append(jnp.concatenate([c, c], axis=-1) if dup else c)
        parts.append(new)
        if lkp > lk:
            parts.append(jnp.zeros((b, lkp - lk, new.shape[-1]), BF16))
        return parts[0] if len(parts) == 1 else jnp.concatenate(parts, axis=1)

    o_a = _diff_attn(qa.reshape(b, l, W_A), keys(a_k0, kab), keys(a_v0, vab), w["lam"],
                     w["diff_gain"], past=p_len, lk=lk, tk=tk, out_scale=1.0 - w["lam_init"])
    o_b = _dsa(qi.reshape(b, l, -1), kw.reshape(b, l, LANES), qb.reshape(b, l, W_B),
               keys(b_ki0, kib, dup=True), keys(b_k0, kbb), keys(b_v0, vbb),
               past=p_len, lk=lk, tk=tk)
    o_c, h_last, conv_new = _rglru(xc.reshape(b, l, W_C), gc.reshape(b, l, W_C), c_cv0, c_h0,
                                   w["rg_conv_w"], w["rg_conv_b"], w["w_gate"], w["b_gate"],
                                   w["rg_lambda"], past=p_len)
    x1, hn = _outproj(x.reshape(t, D_MODEL), o_a.reshape(t, W_A), o_b.reshape(t, W_B),
                      o_c.reshape(t, W_C), w["w_out"], w["norm_ffn"])
    y, f_buf = _ffn(hn.reshape(b, l, D_MODEL), x1.reshape(b, l, D_MODEL), w["ffn_w_up"],
                    w["ffn_conv_w"], w["ffn_conv_b"], w["ffn_w_down"], f_cv0, final_gain,
                    final_norm=final_norm)
    new = (ka.reshape(b, l, H_A, 2 * HEAD_DIM), va.reshape(b, l, H_A, 2 * HEAD_DIM),
           kb.reshape(b, l, H_B, HEAD_DIM), vb.reshape(b, l, H_B, HEAD_DIM),
           kw[:, :D_IDX].reshape(b, l, D_IDX), h_last.reshape(b, W_C), conv_new, f_buf)
    return y, new


def _trunk(x, past, weights, final_gain):
    states = []
    for li in range(N_LAYERS):
        layer_past = tuple(None if c is None else c[li] for c in past)
        x, st = _layer(x, layer_past, weights[li], final_gain, final_norm=(li == N_LAYERS - 1))
        states.append(st)
    return x, states


def _forward(x_prompt, x_sample, caches, params):
    weights = [_prep_layer_weights(params, li) for li in range(N_LAYERS)]
    bp = x_prompt.shape[0]
    dt = x_prompt.dtype
    past_prompt = (None, None, None, None, None,
                   jnp.zeros((N_LAYERS, bp, W_C), dt),
                   jnp.zeros((N_LAYERS, bp, CONV_C - 1, W_C), dt),
                   jnp.zeros((N_LAYERS, bp, CONV_F - 1, 2 * D_FF), dt))
    yp, sp = _trunk(x_prompt, past_prompt, weights, params["norm_final"])
    ys, ss = _trunk(x_sample, caches, weights, params["norm_final"])
    out = [yp, ys]
    for jdx in range(8):
        out.append(jnp.stack([st[jdx] for st in sp], axis=0))
        out.append(jnp.stack([st[jdx] for st in ss], axis=0))
    return tuple(out)


def kernel(x_prompt, x_sample, cache_a_k, cache_a_v, cache_b_k, cache_b_v, cache_b_kidx,
           state_c_h, state_c_conv, state_ffn_conv, norm_mix, w_in, lam_q1, lam_k1, lam_q2,
           lam_k2, diff_gain, rg_conv_w, rg_conv_b, rg_w_r, rg_b_r, rg_w_i, rg_b_i, rg_lambda,
           w_out, norm_ffn, ffn_w_up, ffn_conv_w, ffn_conv_b, ffn_w_down, norm_final):
    params = dict(norm_mix=norm_mix, w_in=w_in, lam_q1=lam_q1, lam_k1=lam_k1, lam_q2=lam_q2,
                  lam_k2=lam_k2, diff_gain=diff_gain, rg_conv_w=rg_conv_w, rg_conv_b=rg_conv_b,
                  rg_w_r=rg_w_r, rg_b_r=rg_b_r, rg_w_i=rg_w_i, rg_b_i=rg_b_i, rg_lambda=rg_lambda,
                  w_out=w_out, norm_ffn=norm_ffn, ffn_w_up=ffn_w_up, ffn_conv_w=ffn_conv_w,
                  ffn_conv_b=ffn_conv_b, ffn_w_down=ffn_w_down, norm_final=norm_final)
    caches = (cache_a_k, cache_a_v, cache_b_k, cache_b_v, cache_b_kidx,
              state_c_h, state_c_conv, state_ffn_conv)
    return _forward(x_prompt, x_sample, caches, params)
```

```python
import functools
import math

import jax
import jax.numpy as jnp
from jax import lax
from jax.experimental import pallas as pl
from jax.experimental.pallas import tpu as pltpu

F32 = jnp.float32
BF16 = jnp.bfloat16

D_MODEL = 1024
N_LAYERS = 2
CHUNK = 64
CHUNK_SHIFT = 6
HEAD_DIM = 64
H_A = 4
W_A = H_A * 2 * HEAD_DIM
H_B = 4
W_B = H_B * HEAD_DIM
H_IDX = 4
D_IDX = 64
TOPK = 256
W_C = 256
N_GATE_BLOCKS = 4
GATE_BLOCK = W_C // N_GATE_BLOCKS
RG_C = 8.0
CONV_C = 4
D_FF = 3072
CONV_F = 3
EPS = 1e-6

LANES = 128
SUBLANES = 8
MXU_DIM = 256
Q_BLOCK = 256
SCAN_ROWS = 128
PROMPT_KEY_TILE = 1024
VMEM_LIMIT = 58 * 2**20
MASKED = -1e30
N_BISECT = 18
Q_SCALE = HEAD_DIM ** -0.5 * math.log2(math.e)

PROJ_W_PAD = 3328
_C_QA, _C_KA, _C_VA = 0, 512, 1024
_C_QB, _C_KB, _C_VB = 1536, 1792, 2048
_C_QI = 2304
_C_KW = 2560
_C_KK = 2688
_C_XC = 2816
_C_GC = 3072


def _cparams(sem):
    return pltpu.CompilerParams(dimension_semantics=sem, vmem_limit_bytes=VMEM_LIMIT)


def _rms(x, g):
    return x * lax.rsqrt(jnp.mean(x * x, axis=-1, keepdims=True) + EPS) * g


def _dot_nt(a, b):
    return lax.dot_general(a, b, (((1,), (1,)), ((), ())), preferred_element_type=F32)


def _round_up(n, m):
    return (n + m - 1) // m * m


def _proj_kernel(x_ref, g_ref, w_ref, qa_ref, qb_ref, qi_ref, ka_ref, va_ref, kb_ref, vb_ref,
                 kw_ref, xc_ref, gc_ref, kab_ref, vab_ref, kbb_ref, vbb_ref, kib_ref):
    h = _rms(x_ref[...], g_ref[...])
    z = jnp.dot(h.astype(BF16), w_ref[...], preferred_element_type=F32)
    qa_ref[...] = (z[:, _C_QA:_C_QA + W_A] * Q_SCALE).astype(BF16)
    qb_ref[...] = (z[:, _C_QB:_C_QB + W_B] * Q_SCALE).astype(BF16)
    qi_ref[...] = z[:, _C_QI:_C_QI + H_IDX * D_IDX].astype(BF16)
    ka = z[:, _C_KA:_C_KA + W_A]
    va = z[:, _C_VA:_C_VA + W_A]
    kb = z[:, _C_KB:_C_KB + W_B]
    vb = z[:, _C_VB:_C_VB + W_B]
    ka_ref[...] = ka
    va_ref[...] = va
    kb_ref[...] = kb
    vb_ref[...] = vb
    kab_ref[...] = ka.astype(BF16)
    vab_ref[...] = va.astype(BF16)
    kbb_ref[...] = kb.astype(BF16)
    vbb_ref[...] = vb.astype(BF16)
    kw_ref[...] = z[:, _C_KW:_C_KW + LANES]
    kib_ref[...] = z[:, _C_KK:_C_KK + LANES].astype(BF16)
    xc_ref[...] = z[:, _C_XC:_C_XC + W_C]
    gc_ref[...] = z[:, _C_GC:_C_GC + W_C]


def _proj(x2d, gain, w_pad):
    t = x2d.shape[0]
    tm = min(512, t)
    widths = [(W_A, BF16), (W_B, BF16), (H_IDX * D_IDX, BF16),
              (W_A, F32), (W_A, F32), (W_B, F32), (W_B, F32),
              (LANES, F32), (W_C, F32), (W_C, F32),
              (W_A, BF16), (W_A, BF16), (W_B, BF16), (W_B, BF16), (LANES, BF16)]
    return pl.pallas_call(
        _proj_kernel,
        grid=(t // tm,),
        in_specs=[pl.BlockSpec((tm, D_MODEL), lambda i: (i, 0)),
                  pl.BlockSpec((1, D_MODEL), lambda i: (0, 0)),
                  pl.BlockSpec((D_MODEL, PROJ_W_PAD), lambda i: (0, 0))],
        out_specs=[pl.BlockSpec((tm, w), lambda i: (i, 0)) for w, _ in widths],
        out_shape=[jax.ShapeDtypeStruct((t, w), d) for w, d in widths],
        compiler_params=_cparams(("parallel",)),
        name="proj",
    )(x2d, gain.reshape(1, D_MODEL), w_pad)


def _tile_bounds(q_start, qb, lk, tk):
    n_full = lax.div(jnp.minimum(q_start + CHUNK, lk), tk)
    n_tiles = lax.div(q_start + qb + tk - 1, tk)
    return n_full, n_tiles


def _admissible(start, tk, q_chunk, lk):
    col = start + lax.broadcasted_iota(jnp.int32, (1, tk), 1)
    return (lax.shift_right_logical(col, CHUNK_SHIFT) <= q_chunk) & (col < lk)


def _lane_max(acc, x):
    for g in range(x.shape[1] // LANES):
        acc = jnp.maximum(acc, x[:, g * LANES:(g + 1) * LANES])
    return acc


def _softmax_value_tile(s_ref, idx, m, lsum, v):
    parts = []
    for g in range(s_ref.shape[-1] // LANES):
        p = jnp.exp2(s_ref[idx + (slice(None), slice(g * LANES, (g + 1) * LANES))] - m)
        lsum = lsum + p
        parts.append(p.astype(BF16))
    return lsum, jnp.dot(jnp.concatenate(parts, axis=1), v, preferred_element_type=F32)


def _diff_attn_kernel(lam_ref, gain_ref, q_ref, k_ref, v_ref, o_ref, s_ref, mx_ref, l_ref, acc_ref,
                      *, past, lk, qb, tk, out_scale):
    j = pl.program_id(2)
    q = q_ref[...]
    lane = lax.broadcasted_iota(jnp.int32, (1, LANES), 1)
    zero = jnp.zeros_like(q)
    q_half = (jnp.where(lane < HEAD_DIM, q, zero), jnp.where(lane >= HEAD_DIM, q, zero))
    q_start = past + j * qb
    row = lax.broadcasted_iota(jnp.int32, (qb, 1), 0)
    q_chunk = lax.shift_right_logical(q_start + row, CHUNK_SHIFT)
    n_full, n_tiles = _tile_bounds(q_start, qb, lk, tk)

    mx_ref[...] = jnp.full(mx_ref.shape, -jnp.inf, F32)
    l_ref[...] = jnp.zeros(l_ref.shape, F32)
    acc_ref[...] = jnp.zeros(acc_ref.shape, F32)

    def score_tile(t, masked):
        start = pl.multiple_of(t * tk, tk)
        k = k_ref[pl.ds(start, tk), :]
        if masked:
            ok = _admissible(start, tk, q_chunk, lk)
        for i in range(2):
            s = _dot_nt(q_half[i], k)
            if masked:
                s = jnp.where(ok, s, -jnp.inf)
            s_ref[i, t] = s
            mx_ref[i] = _lane_max(mx_ref[i], s)

    def full_body(t, c):
        score_tile(t, False)
        return c

    def masked_body(t, c):
        score_tile(t, True)
        return c

    lax.fori_loop(0, n_full, full_body, 0)
    lax.fori_loop(n_full, n_tiles, masked_body, 0)

    for i in range(2):
        mx_ref[i] = jnp.broadcast_to(jnp.max(mx_ref[i], axis=1, keepdims=True), (qb, LANES))

    def value_tile(t, c):
        start = pl.multiple_of(t * tk, tk)
        v = v_ref[pl.ds(start, tk), :]
        for i in range(2):
            lsum, pv = _softmax_value_tile(s_ref, (i, t), mx_ref[i], l_ref[i], v)
            l_ref[i] = lsum
            acc_ref[i] = acc_ref[i] + pv
        return c

    lax.fori_loop(0, n_tiles, value_tile, 0)

    l1 = jnp.sum(l_ref[0], axis=1, keepdims=True)
    l2 = jnp.sum(l_ref[1], axis=1, keepdims=True)
    o = acc_ref[0] / l1 - lam_ref[...] * (acc_ref[1] / l2)
    o_ref[...] = (_rms(o, gain_ref[...]) * out_scale).astype(BF16)


def _diff_attn(q, k, v, lam, gain, *, past, lk, tk, out_scale):
    b, l, _ = q.shape
    lkp = k.shape[1]
    qb = min(l, Q_BLOCK)
    assert l % qb == 0 and lkp % tk == 0 and qb % CHUNK == 0
    kern = functools.partial(_diff_attn_kernel, past=past, lk=lk, qb=qb, tk=tk, out_scale=out_scale)
    return pl.pallas_call(
        kern,
        grid=(b, H_A, l // qb),
        in_specs=[pl.BlockSpec((1, LANES), lambda bi, h, j: (0, 0)),
                  pl.BlockSpec((1, LANES), lambda bi, h, j: (0, 0)),
                  pl.BlockSpec((None, qb, LANES), lambda bi, h, j: (bi, j, h)),
                  pl.BlockSpec((None, lkp, LANES), lambda bi, h, j: (bi, 0, h)),
                  pl.BlockSpec((None, lkp, LANES), lambda bi, h, j: (bi, 0, h))],
        out_specs=pl.BlockSpec((None, qb, LANES), lambda bi, h, j: (bi, j, h)),
        out_shape=jax.ShapeDtypeStruct((b, l, W_A), BF16),
        scratch_shapes=[pltpu.VMEM((2, lkp // tk, qb, tk), F32),
                        pltpu.VMEM((2, qb, LANES), F32), pltpu.VMEM((2, qb, LANES), F32),
                        pltpu.VMEM((2, qb, LANES), F32)],
        compiler_params=_cparams(("parallel", "parallel", "parallel")),
        name="diff_attn",
    )(lam, gain, q, k, v)


def _dsa_kernel(qi_ref, kw_ref, q_ref, ki_ref, k_ref, v_ref, o_ref,
                s_ref, sc_ref, lo_ref, hi_ref, mx_ref, l_ref, acc_ref,
                *, past, lk, qb, tk, topk, max_steps):
    j = pl.program_id(1)
    q_start = past + j * qb
    n_full, n_tiles = _tile_bounds(q_start, qb, lk, tk)
    n_groups = tk // LANES
    row = lax.broadcasted_iota(jnp.int32, (qb, 1), 0)
    q_chunk = lax.shift_right_logical(q_start + row, CHUNK_SHIFT)
    lane = lax.broadcasted_iota(jnp.int32, (1, LANES), 1)
    low_half = lane < HEAD_DIM
    k_sel = float(topk)

    def wide(x):
        return jnp.broadcast_to(x, (qb, LANES))

    def head_views(x):
        views = []
        for h in range(4):
            pair = x[:, (h // 2) * LANES:(h // 2 + 1) * LANES]
            keep = low_half if h % 2 == 0 else jnp.logical_not(low_half)
            views.append(jnp.where(keep, pair, jnp.zeros_like(pair)))
        return views

    qi_h = head_views(qi_ref[...])
    kw = kw_ref[...]
    w_h = [wide(kw[:, D_IDX + h:D_IDX + h + 1]) for h in range(H_IDX)]
    lo_ref[...] = jnp.full(lo_ref.shape, jnp.inf, F32)
    hi_ref[...] = jnp.full(hi_ref.shape, -jnp.inf, F32)

    def index_tile(t, masked):
        start = pl.multiple_of(t * tk, tk)
        if masked:
            adm = _admissible(start, tk, q_chunk, lk)
        rmin = lo_ref[...]
        rmax = hi_ref[...]
        for c in range(tk // MXU_DIM):
            ki = ki_ref[pl.ds(start + c * MXU_DIM, MXU_DIM), :]
            rel = [jnp.maximum(_dot_nt(qi_h[h], ki), 0.0) for h in range(H_IDX)]
            for g in range(MXU_DIM // LANES):
                sl = slice(g * LANES, (g + 1) * LANES)
                sc = w_h[0] * rel[0][:, sl]
                for h in range(1, H_IDX):
                    sc = sc + w_h[h] * rel[h][:, sl]
                csl = slice(c * MXU_DIM + g * LANES, c * MXU_DIM + (g + 1) * LANES)
                if masked:
                    ok = adm[:, csl]
                    s_ref[t, :, csl] = jnp.where(ok, sc, -jnp.inf)
                    rmax = jnp.maximum(rmax, jnp.where(ok, sc, -jnp.inf))
                    rmin = jnp.minimum(rmin, jnp.where(ok, sc, jnp.inf))
                else:
                    s_ref[t, :, csl] = sc
                    rmax = jnp.maximum(rmax, sc)
                    rmin = jnp.minimum(rmin, sc)
        lo_ref[...] = rmin
        hi_ref[...] = rmax

    def index_full(t, c):
        index_tile(t, False)
        return c

    def index_masked(t, c):
        index_tile(t, True)
        return c

    lax.fori_loop(0, n_full, index_full, 0)
    lax.fori_loop(n_full, n_tiles, index_masked, 0)
    rb = min(qb, SCAN_ROWS)
    ones_mat = jnp.ones((LANES, LANES), BF16)
    tri_i = lax.broadcasted_iota(jnp.int32, (LANES, 2 * LANES), 0)
    tri_j = lax.broadcasted_iota(jnp.int32, (LANES, 2 * LANES), 1)
    prefix_mat = jnp.where(jnp.logical_or(tri_i <= tri_j, tri_j >= LANES), 1.0, 0.0).astype(BF16)
    assert s_ref.shape[0] * n_groups <= 256

    row_blocks = [slice(r * rb, (r + 1) * rb) for r in range(qb // rb)]
    pos = q_start + lax.broadcasted_iota(jnp.int32, (qb, LANES), 0)
    n_adm = jnp.minimum((lax.shift_right_logical(pos, CHUNK_SHIFT) + 1) * CHUNK, lk)
    few = n_adm <= topk

    def lanes_all(x, reduce):
        return jnp.broadcast_to(reduce(x, axis=1, keepdims=True), (qb, LANES))

    def row_sum(acc):
        return jnp.dot(acc.astype(BF16), ones_mat, preferred_element_type=F32)

    def scan(step, init, *operands):
        outs = []
        for rows in row_blocks:
            ops = [o[rows] for o in operands]

            def body(t, acc, rows=rows, ops=ops):
                for g in range(n_groups):
                    acc = step(acc, s_ref[t, rows, g * LANES:(g + 1) * LANES], *ops)
                return acc

            outs.append(lax.fori_loop(0, n_tiles, body, jnp.full((rb, LANES), init, F32)))
        return outs[0] if len(outs) == 1 else jnp.concatenate(outs, axis=0)

    def count_ge(thr):
        return row_sum(scan(lambda acc, s, t: acc + jnp.where(s >= t, 1.0, 0.0), 0.0, thr))

    def max_below(bound):
        acc = scan(lambda acc, s, b: jnp.maximum(acc, jnp.where(s < b, s, -jnp.inf)), -jnp.inf, bound)
        return lanes_all(acc, jnp.max)

    rmin = lanes_all(lo_ref[...], jnp.min)
    rmax = lanes_all(hi_ref[...], jnp.max)

    c_max = count_ge(rmax)
    top_ties = c_max >= k_sel

    def bisect(_, c):
        lo, hi = c
        mid = 0.5 * lo + 0.5 * hi
        ge = count_ge(mid) >= k_sel
        return jnp.where(ge, mid, lo), jnp.where(ge, hi, mid)

    _, hi = lax.fori_loop(0, N_BISECT, bisect, (rmin, rmax))

    def walk_cond(st):
        it, _, _, _, active = st
        return jnp.logical_and(jnp.max(active) > 0.0, it < max_steps)

    def walk_body(st):
        it, cand, thr, c_thr, active = st
        c = count_ge(cand)
        ok = c >= k_sel
        act = active > 0.0
        hit = jnp.logical_and(act, ok)
        thr = jnp.where(hit, cand, thr)
        c_thr = jnp.where(hit, c, c_thr)
        active = jnp.where(jnp.logical_and(act, jnp.logical_not(ok)), 1.0, 0.0)
        return it + 1, max_below(cand), thr, c_thr, active

    lowest = float(jnp.finfo(jnp.float32).min)
    thr0 = jnp.where(few, lowest, jnp.where(top_ties, rmax, lowest))
    c0 = jnp.where(few, k_sel, jnp.where(top_ties, c_max, k_sel))
    active0 = jnp.where(jnp.logical_or(few, top_ties), 0.0, 1.0)
    _, _, thr, c_thr, _ = lax.while_loop(
        walk_cond, walk_body, (jnp.int32(0), max_below(hi), thr0, c0, active0))

    has_excess = jnp.max(jnp.where(c_thr > k_sel, 1.0, 0.0)) > 0.0

    def ranked_bias():
        n_tie = k_sel - row_sum(scan(lambda acc, s, t: acc + jnp.where(s > t, 1.0, 0.0), 0.0, thr))
        for rows in row_blocks:
            thr_r = thr[rows]
            n_tie_r = n_tie[rows]

            def body(t, before, rows=rows, thr_r=thr_r, n_tie_r=n_tie_r):
                for g in range(n_groups):
                    lanes = slice(g * LANES, (g + 1) * LANES)
                    s = s_ref[t, rows, lanes]
                    tie = s == thr_r
                    pr = jnp.dot(jnp.where(tie, 1.0, 0.0).astype(BF16), prefix_mat,
                                 preferred_element_type=F32)
                    rank = before + pr[:, :LANES]
                    tie_bias = jnp.where(rank <= n_tie_r, 0.0, MASKED)
                    s_ref[t, rows, lanes] = jnp.where(s > thr_r, 0.0, jnp.where(tie, tie_bias, MASKED))
                    before = before + pr[:, LANES:]
                return before

            lax.fori_loop(0, n_tiles, body, jnp.zeros((rb, LANES), F32))
        return 0

    def plain_bias():
        for rows in row_blocks:
            thr_r = thr[rows]

            def body(t, c, rows=rows, thr_r=thr_r):
                for g in range(n_groups):
                    lanes = slice(g * LANES, (g + 1) * LANES)
                    s_ref[t, rows, lanes] = jnp.where(s_ref[t, rows, lanes] >= thr_r, 0.0, MASKED)
                return c

            lax.fori_loop(0, n_tiles, body, 0)
        return 0

    lax.cond(has_excess, ranked_bias, plain_bias)

    q_h = head_views(q_ref[...])
    for g in range(H_B // 2):
        lanes = slice(g * LANES, (g + 1) * LANES)
        mx_ref[...] = jnp.full(mx_ref.shape, -jnp.inf, F32)
        l_ref[...] = jnp.zeros(l_ref.shape, F32)
        acc_ref[...] = jnp.zeros(acc_ref.shape, F32)

        def score_tile(t, c, g=g, lanes=lanes):
            start = pl.multiple_of(t * tk, tk)
            k = k_ref[pl.ds(start, tk), lanes]
            bias = s_ref[t]
            for i in range(2):
                s = _dot_nt(q_h[2 * g + i], k) + bias
                sc_ref[i, t] = s
                mx_ref[i] = _lane_max(mx_ref[i], s)
            return c

        lax.fori_loop(0, n_tiles, score_tile, 0)
        for i in range(2):
            mx_ref[i] = jnp.broadcast_to(jnp.max(mx_ref[i], axis=1, keepdims=True), (qb, LANES))

        def value_tile(t, c, lanes=lanes):
            start = pl.multiple_of(t * tk, tk)
            v = v_ref[pl.ds(start, tk), lanes]
            for i in range(2):
                lsum, pv = _softmax_value_tile(sc_ref, (i, t), mx_ref[i], l_ref[i], v)
                l_ref[i] = lsum
                acc_ref[i] = acc_ref[i] + pv
            return c

        lax.fori_loop(0, n_tiles, value_tile, 0)
        even = acc_ref[0] / jnp.sum(l_ref[0], axis=1, keepdims=True)
        odd = acc_ref[1] / jnp.sum(l_ref[1], axis=1, keepdims=True)
        o_ref[:, lanes] = jnp.where(low_half, even, odd).astype(BF16)


def _dsa(qi, kw, q, ki, k, v, *, past, lk, tk):
    b, l, _ = q.shape
    lkp = k.shape[1]
    qb = min(l, Q_BLOCK)
    assert l % qb == 0 and lkp % tk == 0 and tk % MXU_DIM == 0 and qb % CHUNK == 0
    kern = functools.partial(_dsa_kernel, past=past, lk=lk, qb=qb, tk=tk,
                             topk=min(TOPK, lk // 4), max_steps=lkp)
    return pl.pallas_call(
        kern,
        grid=(b, l // qb),
        in_specs=[pl.BlockSpec((None, qb, H_IDX * D_IDX), lambda bi, j: (bi, j, 0)),
                  pl.BlockSpec((None, qb, LANES), lambda bi, j: (bi, j, 0)),
                  pl.BlockSpec((None, qb, W_B), lambda bi, j: (bi, j, 0)),
                  pl.BlockSpec((None, lkp, LANES), lambda bi, j: (bi, 0, 0)),
                  pl.BlockSpec((None, lkp, W_B), lambda bi, j: (bi, 0, 0)),
                  pl.BlockSpec((None, lkp, W_B), lambda bi, j: (bi, 0, 0))],
        out_specs=pl.BlockSpec((None, qb, W_B), lambda bi, j: (bi, j, 0)),
        out_shape=jax.ShapeDtypeStruct((b, l, W_B), BF16),
        scratch_shapes=[pltpu.VMEM((lkp // tk, qb, tk), F32),
                        pltpu.VMEM((2, lkp // tk, qb, tk), F32),
                        pltpu.VMEM((qb, LANES), F32), pltpu.VMEM((qb, LANES), F32),
                        pltpu.VMEM((2, qb, LANES), F32), pltpu.VMEM((2, qb, LANES), F32),
                        pltpu.VMEM((2, qb, LANES), F32)],
        compiler_params=_cparams(("parallel", "parallel")),
        name="dsa",
    )(qi, kw, q, ki, k, v)


def _shift_rows(x, prev, k):
    rolled = pltpu.roll(x, k, 0)
    row = lax.broadcasted_iota(jnp.int32, (SUBLANES, 1), 0)
    top = jnp.where(row < k, pltpu.roll(prev, k, 0), rolled[0:SUBLANES])
    if x.shape[0] == SUBLANES:
        return top
    return jnp.concatenate([top, rolled[SUBLANES:]], axis=0)


def _rglru_kernel(xc_ref, gc_ref, cst_ref, h0_ref, cw_ref, cb_ref, wg_ref, bg_ref, lam_ref,
                  oc_ref, hl_ref, cn_ref, prev_ref, h_ref, a_ref, b_ref, hs_ref, *, past, tl):
    i = pl.program_id(1)

    @pl.when(i == 0)
    def _():
        prev_ref[...] = jnp.zeros(prev_ref.shape, F32)
        prev_ref[SUBLANES - (CONV_C - 1):SUBLANES, :] = cst_ref[...]
        h_ref[...] = h0_ref[...]

    x = xc_ref[...]
    prev = prev_ref[...]
    cw = cw_ref[...]
    xconv = cw[CONV_C - 1:CONV_C] * x + cb_ref[...]
    for k in range(1, CONV_C):
        xconv = xconv + cw[CONV_C - 1 - k:CONV_C - k] * _shift_rows(x, prev, k)
    prev_ref[...] = x[tl - SUBLANES:tl]

    pre = jnp.dot(xconv.astype(BF16), wg_ref[...], preferred_element_type=F32) + bg_ref[...]
    r = jax.nn.sigmoid(pre[:, :W_C])
    gate_i = jax.nn.sigmoid(pre[:, W_C:])
    neg_lam = -lam_ref[...]
    softplus = jnp.maximum(neg_lam, 0.0) + jnp.log1p(jnp.exp(-jnp.abs(neg_lam)))
    log_a = -RG_C * r * softplus
    pos = past + i * tl + lax.broadcasted_iota(jnp.int32, (tl, 1), 0)
    th = jnp.tanh(log_a)
    mult = jnp.where(pos == 0, 1.0, jnp.sqrt(-2.0 * th / (1.0 - th)))
    a_ref[...] = jnp.exp(log_a)
    b_ref[...] = mult * gate_i * xconv

    def step(t, h):
        h = a_ref[pl.ds(t, 1), :] * h + b_ref[pl.ds(t, 1), :]
        hs_ref[pl.ds(t, 1), :] = h
        return h

    h_last = lax.fori_loop(0, tl, step, h_ref[...], unroll=8)
    h_ref[...] = h_last

    gc = gc_ref[...]
    gelu = 0.5 * gc * (1.0 + jnp.tanh(math.sqrt(2.0 / math.pi) * (gc + 0.044715 * (gc * gc * gc))))
    oc_ref[...] = (hs_ref[...] * gelu).astype(BF16)

    @pl.when(i == pl.num_programs(1) - 1)
    def _():
        hl_ref[...] = h_last
        cn_ref[...] = x[tl - (CONV_C - 1):tl]


def _rglru(xc, gc, conv_state, h0, conv_w, conv_b, w_gate, b_gate, lam, *, past):
    b, l, _ = xc.shape
    tl = min(l, 512)
    assert l >= SUBLANES and l % tl == 0
    kern = functools.partial(_rglru_kernel, past=past, tl=tl)
    const = lambda bi, i: (0, 0)
    return pl.pallas_call(
        kern,
        grid=(b, l // tl),
        in_specs=[pl.BlockSpec((None, tl, W_C), lambda bi, i: (bi, i, 0)),
                  pl.BlockSpec((None, tl, W_C), lambda bi, i: (bi, i, 0)),
                  pl.BlockSpec((None, CONV_C - 1, W_C), lambda bi, i: (bi, 0, 0)),
                  pl.BlockSpec((None, 1, W_C), lambda bi, i: (bi, 0, 0)),
                  pl.BlockSpec((CONV_C, W_C), const),
                  pl.BlockSpec((1, W_C), const),
                  pl.BlockSpec((W_C, 2 * W_C), const),
                  pl.BlockSpec((1, 2 * W_C), const),
                  pl.BlockSpec((1, W_C), const)],
        out_specs=[pl.BlockSpec((None, tl, W_C), lambda bi, i: (bi, i, 0)),
                   pl.BlockSpec((None, 1, W_C), lambda bi, i: (bi, 0, 0)),
                   pl.BlockSpec((None, CONV_C - 1, W_C), lambda bi, i: (bi, 0, 0))],
        out_shape=[jax.ShapeDtypeStruct((b, l, W_C), BF16),
                   jax.ShapeDtypeStruct((b, 1, W_C), F32),
                   jax.ShapeDtypeStruct((b, CONV_C - 1, W_C), F32)],
        scratch_shapes=[pltpu.VMEM((SUBLANES, W_C), F32), pltpu.VMEM((1, W_C), F32),
                        pltpu.VMEM((tl, W_C), F32), pltpu.VMEM((tl, W_C), F32),
                        pltpu.VMEM((tl, W_C), F32)],
        compiler_params=_cparams(("parallel", "arbitrary")),
        name="rglru",
    )(xc, gc, conv_state, h0.reshape(b, 1, W_C), conv_w, conv_b.reshape(1, W_C), w_gate,
      b_gate, lam.reshape(1, W_C))


def _outproj_kernel(x_ref, oa_ref, ob_ref, oc_ref, w_ref, g_ref, x1_ref, hn_ref):
    mix = jnp.dot(oa_ref[...], w_ref[0:W_A, :], preferred_element_type=F32)
    mix = mix + jnp.dot(ob_ref[...], w_ref[W_A:W_A + W_B, :], preferred_element_type=F32)
    mix = mix + jnp.dot(oc_ref[...], w_ref[W_A + W_B:, :], preferred_element_type=F32)
    x1 = x_ref[...] + mix
    x1_ref[...] = x1
    hn_ref[...] = _rms(x1, g_ref[...]).astype(BF16)


def _outproj(x2d, oa, ob, oc, w_out, gain):
    t = x2d.shape[0]
    tm = min(512, t)
    row = lambda w: pl.BlockSpec((tm, w), lambda i: (i, 0))
    return pl.pallas_call(
        _outproj_kernel,
        grid=(t // tm,),
        in_specs=[row(D_MODEL), row(W_A), row(W_B), row(W_C),
                  pl.BlockSpec((D_MODEL, D_MODEL), lambda i: (0, 0)),
                  pl.BlockSpec((1, D_MODEL), lambda i: (0, 0))],
        out_specs=[row(D_MODEL), row(D_MODEL)],
        out_shape=[jax.ShapeDtypeStruct((t, D_MODEL), F32), jax.ShapeDtypeStruct((t, D_MODEL), BF16)],
        compiler_params=_cparams(("parallel",)),
        name="outproj",
    )(x2d, oa, ob, oc, w_out, gain.reshape(1, D_MODEL))


def _ffn_kernel(hn_ref, x1_ref, wu_ref, wg_ref, cwu_ref, cwg_ref, cbu_ref, cbg_ref, wd_ref,
                su_ref, sg_ref, gfin_ref, y_ref, fu_ref, fg_ref, acc_ref, cu_ref, cg_ref,
                *, tm, final_norm):
    i = pl.program_id(1)
    f = pl.program_id(2)
    hn = hn_ref[...]

    def conv_branch(w_ref, cw_ref, cb_ref, st_ref, carry_ref, tail_ref):
        a = jnp.dot(hn, w_ref[...], preferred_element_type=F32)

        @pl.when(i == 0)
        def _():
            carry_ref[f] = jnp.zeros(carry_ref.shape[1:], F32)
            carry_ref[f, SUBLANES - (CONV_F - 1):SUBLANES, :] = st_ref[...]

        prev = carry_ref[f]
        cw = cw_ref[...]
        y = cw[CONV_F - 1:CONV_F] * a + cb_ref[...]
        for k in range(1, CONV_F):
            y = y + cw[CONV_F - 1 - k:CONV_F - k] * _shift_rows(a, prev, k)
        carry_ref[f] = a[tm - SUBLANES:tm]
        tail_ref[f] = a[tm - (CONV_F - 1):tm]
        return y

    u = conv_branch(wu_ref, cwu_ref, cbu_ref, su_ref, cu_ref, fu_ref)
    g = conv_branch(wg_ref, cwg_ref, cbg_ref, sg_ref, cg_ref, fg_ref)
    mid = (g * jax.nn.sigmoid(g) * u).astype(BF16)
    contrib = jnp.dot(mid, wd_ref[...], preferred_element_type=F32)

    @pl.when(f == 0)
    def _():
        acc_ref[...] = x1_ref[...] + contrib

    @pl.when(f > 0)
    def _():
        acc_ref[...] = acc_ref[...] + contrib

    @pl.when(f == pl.num_programs(2) - 1)
    def _():
        y = acc_ref[...]
        if final_norm:
            y = _rms(y, gfin_ref[...])
        y_ref[...] = y


def _ffn(hn, x1, w_up, conv_w, conv_b, w_down, state, final_gain, *, final_norm):
    b, l, _ = hn.shape
    tm = min(l, 1024)
    tf = 512
    nf = D_FF // tf
    assert l % tm == 0 and tm >= SUBLANES
    kern = functools.partial(_ffn_kernel, tm=tm, final_norm=final_norm)
    conv_b = conv_b.reshape(1, 2 * D_FF)
    u_col = lambda bi, i, f: (0, f)
    g_col = lambda bi, i, f: (0, nf + f)
    tail_spec = pl.BlockSpec((None, nf, CONV_F - 1, tf), lambda bi, i, f: (bi, 0, 0, 0))
    tail_shape = jax.ShapeDtypeStruct((b, nf, CONV_F - 1, tf), F32)
    y, fu, fg = pl.pallas_call(
        kern,
        grid=(b, l // tm, nf),
        in_specs=[pl.BlockSpec((None, tm, D_MODEL), lambda bi, i, f: (bi, i, 0)),
                  pl.BlockSpec((None, tm, D_MODEL), lambda bi, i, f: (bi, i, 0)),
                  pl.BlockSpec((D_MODEL, tf), u_col),
                  pl.BlockSpec((D_MODEL, tf), g_col),
                  pl.BlockSpec((CONV_F, tf), u_col),
                  pl.BlockSpec((CONV_F, tf), g_col),
                  pl.BlockSpec((1, tf), u_col),
                  pl.BlockSpec((1, tf), g_col),
                  pl.BlockSpec((tf, D_MODEL), lambda bi, i, f: (f, 0)),
                  pl.BlockSpec((None, CONV_F - 1, tf), lambda bi, i, f: (bi, 0, f)),
                  pl.BlockSpec((None, CONV_F - 1, tf), lambda bi, i, f: (bi, 0, nf + f)),
                  pl.BlockSpec((1, D_MODEL), lambda bi, i, f: (0, 0))],
        out_specs=[pl.BlockSpec((None, tm, D_MODEL), lambda bi, i, f: (bi, i, 0)),
                   tail_spec, tail_spec],
        out_shape=[jax.ShapeDtypeStruct((b, l, D_MODEL), F32), tail_shape, tail_shape],
        scratch_shapes=[pltpu.VMEM((tm, D_MODEL), F32),
                        pltpu.VMEM((nf, SUBLANES, tf), F32), pltpu.VMEM((nf, SUBLANES, tf), F32)],
        compiler_params=_cparams(("parallel", "arbitrary", "arbitrary")),
        name="ffn",
    )(hn, x1, w_up, w_up, conv_w, conv_w, conv_b, conv_b, w_down, state, state,
      final_gain.reshape(1, D_MODEL))
    flat = lambda a: jnp.swapaxes(a, 1, 2).reshape(b, CONV_F - 1, D_FF)
    return y, jnp.concatenate([flat(fu), flat(fg)], axis=-1)


def _prep_layer_weights(p, li):
    w_in = p["w_in"][li]
    zeros = lambda n: jnp.zeros((D_MODEL, n), w_in.dtype)
    k_idx = w_in[:, 2560:2624]
    w_pad = jnp.concatenate(
        [w_in[:, :2628], zeros(_C_KK - 2628), k_idx, k_idx, w_in[:, 2628:]], axis=1).astype(BF16)
    assert w_pad.shape[1] == PROJ_W_PAD

    def block_diag(w):
        out = jnp.zeros((W_C, W_C), w.dtype)
        for n in range(N_GATE_BLOCKS):
            sl = slice(n * GATE_BLOCK, (n + 1) * GATE_BLOCK)
            out = out.at[sl, sl].set(w[n])
        return out

    w_gate = jnp.concatenate([block_diag(p["rg_w_r"][li]), block_diag(p["rg_w_i"][li])], axis=1)
    b_gate = jnp.concatenate([p["rg_b_r"][li].reshape(1, W_C), p["rg_b_i"][li].reshape(1, W_C)], axis=1)
    lam_init = 0.8 - 0.6 * math.exp(-0.3 * li)
    f32 = lambda a: a.astype(F32)
    lam = (jnp.exp(jnp.sum(f32(p["lam_q1"][li]) * f32(p["lam_k1"][li])))
           - jnp.exp(jnp.sum(f32(p["lam_q2"][li]) * f32(p["lam_k2"][li]))) + lam_init)
    return dict(
        norm_mix=p["norm_mix"][li], w_pad=w_pad, lam=jnp.full((1, LANES), lam, F32),
        lam_init=lam_init, diff_gain=p["diff_gain"][li].reshape(1, LANES),
        rg_conv_w=p["rg_conv_w"][li], rg_conv_b=p["rg_conv_b"][li],
        w_gate=w_gate.astype(BF16), b_gate=b_gate, rg_lambda=p["rg_lambda"][li],
        w_out=p["w_out"][li].astype(BF16), norm_ffn=p["norm_ffn"][li],
        ffn_w_up=p["ffn_w_up"][li].astype(BF16), ffn_conv_w=p["ffn_conv_w"][li],
        ffn_conv_b=p["ffn_conv_b"][li], ffn_w_down=p["ffn_w_down"][li].astype(BF16))


def _layer(x, past, w, final_gain, final_norm):
    b, l, _ = x.shape
    a_k0, a_v0, b_k0, b_v0, b_ki0, c_h0, c_cv0, f_cv0 = past
    p_len = 0 if a_k0 is None else a_k0.shape[1]
    lk = p_len + l
    tk = PROMPT_KEY_TILE if p_len == 0 else _round_up(lk, MXU_DIM)
    lkp = _round_up(lk, tk)
    t = b * l

    (qa, qb, qi, ka, va, kb, vb, kw, xc, gc, kab, vab, kbb, vbb, kib) = _proj(
        x.reshape(t, D_MODEL), w["norm_mix"], w["w_pad"])

    def keys(cache, new, dup=False):
        new = new.reshape(b, l, -1)
        parts = []
        if cache is not None:
            c = cache.reshape(b, p_len, -1).astype(BF16)
            parts.append(jnp.concatenate([c, c], axis=-1) if dup else c)
        parts.append(new)
        if lkp > lk:
            parts.append(jnp.zeros((b, lkp - lk, new.shape[-1]), BF16))
        return parts[0] if len(parts) == 1 else jnp.concatenate(parts, axis=1)

    o_a = _diff_attn(qa.reshape(b, l, W_A), keys(a_k0, kab), keys(a_v0, vab), w["lam"],
                     w["diff_gain"], past=p_len, lk=lk, tk=tk, out_scale=1.0 - w["lam_init"])
    o_b = _dsa(qi.reshape(b, l, -1), kw.reshape(b, l, LANES), qb.reshape(b, l, W_B),
               keys(b_ki0, kib, dup=True), keys(b_k0, kbb), keys(b_v0, vbb),
               past=p_len, lk=lk, tk=tk)
    o_c, h_last, conv_new = _rglru(xc.reshape(b, l, W_C), gc.reshape(b, l, W_C), c_cv0, c_h0,
                                   w["rg_conv_w"], w["rg_conv_b"], w["w_gate"], w["b_gate"],
                                   w["rg_lambda"], past=p_len)
    x1, hn = _outproj(x.reshape(t, D_MODEL), o_a.reshape(t, W_A), o_b.reshape(t, W_B),
                      o_c.reshape(t, W_C), w["w_out"], w["norm_ffn"])
    y, f_buf = _ffn(hn.reshape(b, l, D_MODEL), x1.reshape(b, l, D_MODEL), w["ffn_w_up"],
                    w["ffn_conv_w"], w["ffn_conv_b"], w["ffn_w_down"], f_cv0, final_gain,
                    final_norm=final_norm)
    new = (ka.reshape(b, l, H_A, 2 * HEAD_DIM), va.reshape(b, l, H_A, 2 * HEAD_DIM),
           kb.reshape(b, l, H_B, HEAD_DIM), vb.reshape(b, l, H_B, HEAD_DIM),
           kw[:, :D_IDX].reshape(b, l, D_IDX), h_last.reshape(b, W_C), conv_new, f_buf)
    return y, new


def _trunk(x, past, weights, final_gain):
    states = []
    for li in range(N_LAYERS):
        layer_past = tuple(None if c is None else c[li] for c in past)
        x, st = _layer(x, layer_past, weights[li], final_gain, final_norm=(li == N_LAYERS - 1))
        states.append(st)
    return x, states


def _forward(x_prompt, x_sample, caches, params):
    weights = [_prep_layer_weights(params, li) for li in range(N_LAYERS)]
    bp = x_prompt.shape[0]
    dt = x_prompt.dtype
    past_prompt = (None, None, None, None, None,
                   jnp.zeros((N_LAYERS, bp, W_C), dt),
                   jnp.zeros((N_LAYERS, bp, CONV_C - 1, W_C), dt),
                   jnp.zeros((N_LAYERS, bp, CONV_F - 1, 2 * D_FF), dt))
    yp, sp = _trunk(x_prompt, past_prompt, weights, params["norm_final"])
    ys, ss = _trunk(x_sample, caches, weights, params["norm_final"])
    out = [yp, ys]
    for jdx in range(8):
        out.append(jnp.stack([st[jdx] for st in sp], axis=0))
        out.append(jnp.stack([st[jdx] for st in ss], axis=0))
    return tuple(out)


def kernel(x_prompt, x_sample, cache_a_k, cache_a_v, cache_b_k, cache_b_v, cache_b_kidx,
           state_c_h, state_c_conv, state_ffn_conv, norm_mix, w_in, lam_q1, lam_k1, lam_q2,
           lam_k2, diff_gain, rg_conv_w, rg_conv_b, rg_w_r, rg_b_r, rg_w_i, rg_b_i, rg_lambda,
           w_out, norm_ffn, ffn_w_up, ffn_conv_w, ffn_conv_b, ffn_w_down, norm_final):
    params = dict(norm_mix=norm_mix, w_in=w_in, lam_q1=lam_q1, lam_k1=lam_k1, lam_q2=lam_q2,
                  lam_k2=lam_k2, diff_gain=diff_gain, rg_conv_w=rg_conv_w, rg_conv_b=rg_conv_b,
                  rg_w_r=rg_w_r, rg_b_r=rg_b_r, rg_w_i=rg_w_i, rg_b_i=rg_b_i, rg_lambda=rg_lambda,
                  w_out=w_out, norm_ffn=norm_ffn, ffn_w_up=ffn_w_up, ffn_conv_w=ffn_conv_w,
                  ffn_conv_b=ffn_conv_b, ffn_w_down=ffn_w_down, norm_final=norm_final)
    caches = (cache_a_k, cache_a_v, cache_b_k, cache_b_v, cache_b_kidx,
              state_c_h, state_c_conv, state_ffn_conv)
    return _forward(x_prompt, x_sample, caches, params)
```

```python
import functools
import math

import jax
import jax.numpy as jnp
from jax import lax
from jax.experimental import pallas as pl
from jax.experimental.pallas import tpu as pltpu

F32 = jnp.float32
BF16 = jnp.bfloat16

D_MODEL = 1024
N_LAYERS = 2
CHUNK = 64
CHUNK_SHIFT = 6
HEAD_DIM = 64
H_A = 4
W_A = H_A * 2 * HEAD_DIM
H_B = 4
W_B = H_B * HEAD_DIM
H_IDX = 4
D_IDX = 64
TOPK = 256
W_C = 256
N_GATE_BLOCKS = 4
GATE_BLOCK = W_C // N_GATE_BLOCKS
RG_C = 8.0
CONV_C = 4
D_FF = 3072
CONV_F = 3
EPS = 1e-6

LANES = 128
SUBLANES = 8
MXU_DIM = 256
Q_BLOCK = 256
SCAN_ROWS = 128
PROMPT_KEY_TILE = 1024
VMEM_LIMIT = 58 * 2**20
MASKED = -1e30
N_BISECT = 18
FFN_ROW_CHUNKS = 4
Q_SCALE = HEAD_DIM ** -0.5 * math.log2(math.e)

PROJ_W_PAD = 3328
_C_QA, _C_KA, _C_VA = 0, 512, 1024
_C_QB, _C_KB, _C_VB = 1536, 1792, 2048
_C_QI = 2304
_C_KW = 2560
_C_KK = 2688
_C_XC = 2816
_C_GC = 3072


def _cparams(sem):
    return pltpu.CompilerParams(dimension_semantics=sem, vmem_limit_bytes=VMEM_LIMIT)


def _rms(x, g):
    return x * lax.rsqrt(jnp.mean(x * x, axis=-1, keepdims=True) + EPS) * g


def _dot_nt(a, b):
    return lax.dot_general(a, b, (((1,), (1,)), ((), ())), preferred_element_type=F32)


def _round_up(n, m):
    return (n + m - 1) // m * m


def _proj_kernel(x_ref, g_ref, w_ref, qa_ref, qb_ref, qi_ref, ka_ref, va_ref, kb_ref, vb_ref,
                 kw_ref, xc_ref, gc_ref, kab_ref, vab_ref, kbb_ref, vbb_ref, kib_ref):
    h = _rms(x_ref[...], g_ref[...])
    z = jnp.dot(h.astype(BF16), w_ref[...], preferred_element_type=F32)
    qa_ref[...] = (z[:, _C_QA:_C_QA + W_A] * Q_SCALE).astype(BF16)
    qb_ref[...] = (z[:, _C_QB:_C_QB + W_B] * Q_SCALE).astype(BF16)
    qi_ref[...] = z[:, _C_QI:_C_QI + H_IDX * D_IDX].astype(BF16)
    ka = z[:, _C_KA:_C_KA + W_A]
    va = z[:, _C_VA:_C_VA + W_A]
    kb = z[:, _C_KB:_C_KB + W_B]
    vb = z[:, _C_VB:_C_VB + W_B]
    ka_ref[...] = ka
    va_ref[...] = va
    kb_ref[...] = kb
    vb_ref[...] = vb
    kab_ref[...] = ka.astype(BF16)
    vab_ref[...] = va.astype(BF16)
    kbb_ref[...] = kb.astype(BF16)
    vbb_ref[...] = vb.astype(BF16)
    kw_ref[...] = z[:, _C_KW:_C_KW + LANES]
    kib_ref[...] = z[:, _C_KK:_C_KK + LANES].astype(BF16)
    xc_ref[...] = z[:, _C_XC:_C_XC + W_C]
    gc_ref[...] = z[:, _C_GC:_C_GC + W_C]


def _proj(x2d, gain, w_pad):
    t = x2d.shape[0]
    tm = min(512, t)
    widths = [(W_A, BF16), (W_B, BF16), (H_IDX * D_IDX, BF16),
              (W_A, F32), (W_A, F32), (W_B, F32), (W_B, F32),
              (LANES, F32), (W_C, F32), (W_C, F32),
              (W_A, BF16), (W_A, BF16), (W_B, BF16), (W_B, BF16), (LANES, BF16)]
    return pl.pallas_call(
        _proj_kernel,
        grid=(t // tm,),
        in_specs=[pl.BlockSpec((tm, D_MODEL), lambda i: (i, 0)),
                  pl.BlockSpec((1, D_MODEL), lambda i: (0, 0)),
                  pl.BlockSpec((D_MODEL, PROJ_W_PAD), lambda i: (0, 0))],
        out_specs=[pl.BlockSpec((tm, w), lambda i: (i, 0)) for w, _ in widths],
        out_shape=[jax.ShapeDtypeStruct((t, w), d) for w, d in widths],
        compiler_params=_cparams(("parallel",)),
        name="proj",
    )(x2d, gain.reshape(1, D_MODEL), w_pad)


def _tile_bounds(q_start, qb, lk, tk):
    n_full = lax.div(jnp.minimum(q_start + CHUNK, lk), tk)
    n_tiles = lax.div(q_start + qb + tk - 1, tk)
    return n_full, n_tiles


def _admissible(start, tk, q_chunk, lk):
    col = start + lax.broadcasted_iota(jnp.int32, (1, tk), 1)
    return (lax.shift_right_logical(col, CHUNK_SHIFT) <= q_chunk) & (col < lk)


def _lane_max(acc, x):
    for g in range(x.shape[1] // LANES):
        acc = jnp.maximum(acc, x[:, g * LANES:(g + 1) * LANES])
    return acc


def _softmax_value_tile(s_ref, idx, m, v):
    parts = []
    for g in range(s_ref.shape[-1] // LANES):
        s = s_ref[idx + (slice(None), slice(g * LANES, (g + 1) * LANES))]
        parts.append(jnp.exp2((s - m).astype(BF16)))
    v_ones = jnp.concatenate([v, jnp.ones_like(v)], axis=1)
    return jnp.dot(jnp.concatenate(parts, axis=1), v_ones, preferred_element_type=F32)


def _diff_attn_kernel(lam_ref, gain_ref, q_ref, k_ref, v_ref, o_ref, s_ref, mx_ref, acc_ref,
                      *, past, lk, qb, tk, out_scale):
    j = pl.program_id(2)
    q = q_ref[...]
    lane = lax.broadcasted_iota(jnp.int32, (1, LANES), 1)
    zero = jnp.zeros_like(q)
    q_half = (jnp.where(lane < HEAD_DIM, q, zero), jnp.where(lane >= HEAD_DIM, q, zero))
    q_start = past + j * qb
    row = lax.broadcasted_iota(jnp.int32, (qb, 1), 0)
    q_chunk = lax.shift_right_logical(q_start + row, CHUNK_SHIFT)
    n_full, n_tiles = _tile_bounds(q_start, qb, lk, tk)

    mx_ref[...] = jnp.full(mx_ref.shape, -jnp.inf, F32)
    acc_ref[...] = jnp.zeros(acc_ref.shape, F32)

    def score_tile(t, masked):
        start = pl.multiple_of(t * tk, tk)
        k = k_ref[pl.ds(start, tk), :]
        if masked:
            ok = _admissible(start, tk, q_chunk, lk)
        for i in range(2):
            s = _dot_nt(q_half[i], k)
            if masked:
                s = jnp.where(ok, s, -jnp.inf)
            s_ref[i, t] = s
            mx_ref[i] = _lane_max(mx_ref[i], s)

    def full_body(t, c):
        score_tile(t, False)
        return c

    def masked_body(t, c):
        score_tile(t, True)
        return c

    lax.fori_loop(0, n_full, full_body, 0)
    lax.fori_loop(n_full, n_tiles, masked_body, 0)

    for i in range(2):
        mx_ref[i] = jnp.broadcast_to(jnp.max(mx_ref[i], axis=1, keepdims=True), (qb, LANES))

    def value_tile(t, c):
        start = pl.multiple_of(t * tk, tk)
        v = v_ref[pl.ds(start, tk), :]
        for i in range(2):
            acc_ref[i] = acc_ref[i] + _softmax_value_tile(s_ref, (i, t), mx_ref[i], v)
        return c

    lax.fori_loop(0, n_tiles, value_tile, 0)

    o = (acc_ref[0, :, :LANES] / acc_ref[0, :, LANES:]
         - lam_ref[...] * (acc_ref[1, :, :LANES] / acc_ref[1, :, LANES:]))
    o_ref[...] = (_rms(o, gain_ref[...]) * out_scale).astype(BF16)


def _diff_attn(q, k, v, lam, gain, *, past, lk, tk, out_scale):
    b, l, _ = q.shape
    lkp = k.shape[1]
    qb = min(l, Q_BLOCK)
    assert l % qb == 0 and lkp % tk == 0 and qb % CHUNK == 0
    kern = functools.partial(_diff_attn_kernel, past=past, lk=lk, qb=qb, tk=tk, out_scale=out_scale)
    return pl.pallas_call(
        kern,
        grid=(b, H_A, l // qb),
        in_specs=[pl.BlockSpec((1, LANES), lambda bi, h, j: (0, 0)),
                  pl.BlockSpec((1, LANES), lambda bi, h, j: (0, 0)),
                  pl.BlockSpec((None, qb, LANES), lambda bi, h, j: (bi, j, h)),
                  pl.BlockSpec((None, lkp, LANES), lambda bi, h, j: (bi, 0, h)),
                  pl.BlockSpec((None, lkp, LANES), lambda bi, h, j: (bi, 0, h))],
        out_specs=pl.BlockSpec((None, qb, LANES), lambda bi, h, j: (bi, j, h)),
        out_shape=jax.ShapeDtypeStruct((b, l, W_A), BF16),
        scratch_shapes=[pltpu.VMEM((2, lkp // tk, qb, tk), F32),
                        pltpu.VMEM((2, qb, LANES), F32), pltpu.VMEM((2, qb, 2 * LANES), F32)],
        compiler_params=_cparams(("parallel", "parallel", "parallel")),
        name="diff_attn",
    )(lam, gain, q, k, v)


def _dsa_kernel(qi_ref, kw_ref, q_ref, ki_ref, k_ref, v_ref, o_ref,
                s_ref, sc_ref, lo_ref, hi_ref, mx_ref, acc_ref,
                *, past, lk, qb, tk, topk, max_steps):
    j = pl.program_id(1)
    q_start = past + j * qb
    n_full, n_tiles = _tile_bounds(q_start, qb, lk, tk)
    n_groups = tk // LANES
    row = lax.broadcasted_iota(jnp.int32, (qb, 1), 0)
    q_chunk = lax.shift_right_logical(q_start + row, CHUNK_SHIFT)
    lane = lax.broadcasted_iota(jnp.int32, (1, LANES), 1)
    low_half = lane < HEAD_DIM
    k_sel = float(topk)

    def wide(x):
        return jnp.broadcast_to(x, (qb, LANES))

    def head_views(x):
        views = []
        for h in range(4):
            pair = x[:, (h // 2) * LANES:(h // 2 + 1) * LANES]
            keep = low_half if h % 2 == 0 else jnp.logical_not(low_half)
            views.append(jnp.where(keep, pair, jnp.zeros_like(pair)))
        return views

    qi_h = head_views(qi_ref[...])
    kw = kw_ref[...]
    w_h = [wide(kw[:, D_IDX + h:D_IDX + h + 1]) for h in range(H_IDX)]
    lo_ref[...] = jnp.full(lo_ref.shape, jnp.inf, F32)
    hi_ref[...] = jnp.full(hi_ref.shape, -jnp.inf, F32)

    def index_tile(t, masked):
        start = pl.multiple_of(t * tk, tk)
        if masked:
            adm = _admissible(start, tk, q_chunk, lk)
        rmin = lo_ref[...]
        rmax = hi_ref[...]
        for c in range(tk // MXU_DIM):
            ki = ki_ref[pl.ds(start + c * MXU_DIM, MXU_DIM), :]
            rel = [jnp.maximum(_dot_nt(qi_h[h], ki), 0.0) for h in range(H_IDX)]
            for g in range(MXU_DIM // LANES):
                sl = slice(g * LANES, (g + 1) * LANES)
                sc = w_h[0] * rel[0][:, sl]
                for h in range(1, H_IDX):
                    sc = sc + w_h[h] * rel[h][:, sl]
                csl = slice(c * MXU_DIM + g * LANES, c * MXU_DIM + (g + 1) * LANES)
                if masked:
                    ok = adm[:, csl]
                    s_ref[t, :, csl] = jnp.where(ok, sc, -jnp.inf)
                    rmax = jnp.maximum(rmax, jnp.where(ok, sc, -jnp.inf))
                    rmin = jnp.minimum(rmin, jnp.where(ok, sc, jnp.inf))
                else:
                    s_ref[t, :, csl] = sc
                    rmax = jnp.maximum(rmax, sc)
                    rmin = jnp.minimum(rmin, sc)
        lo_ref[...] = rmin
        hi_ref[...] = rmax

    def index_full(t, c):
        index_tile(t, False)
        return c

    def index_masked(t, c):
        index_tile(t, True)
        return c

    lax.fori_loop(0, n_full, index_full, 0)
    lax.fori_loop(n_full, n_tiles, index_masked, 0)
    rb = min(qb, SCAN_ROWS)
    ones_mat = jnp.ones((LANES, LANES), BF16)
    tri_i = lax.broadcasted_iota(jnp.int32, (LANES, 2 * LANES), 0)
    tri_j = lax.broadcasted_iota(jnp.int32, (LANES, 2 * LANES), 1)
    prefix_mat = jnp.where(jnp.logical_or(tri_i <= tri_j, tri_j >= LANES), 1.0, 0.0).astype(BF16)
    assert s_ref.shape[0] * n_groups <= 256

    row_blocks = [slice(r * rb, (r + 1) * rb) for r in range(qb // rb)]
    pos = q_start + lax.broadcasted_iota(jnp.int32, (qb, LANES), 0)
    n_adm = jnp.minimum((lax.shift_right_logical(pos, CHUNK_SHIFT) + 1) * CHUNK, lk)
    few = n_adm <= topk

    def lanes_all(x, reduce):
        return jnp.broadcast_to(reduce(x, axis=1, keepdims=True), (qb, LANES))

    def row_sum(acc):
        return jnp.dot(acc.astype(BF16), ones_mat, preferred_element_type=F32)

    def scan(step, init, *operands):
        outs = []
        for rows in row_blocks:
            ops = [o[rows] for o in operands]

            def body(t, acc, rows=rows, ops=ops):
                for g in range(n_groups):
                    acc = step(acc, s_ref[t, rows, g * LANES:(g + 1) * LANES], *ops)
                return acc

            outs.append(lax.fori_loop(0, n_tiles, body, jnp.full((rb, LANES), init, F32)))
        return outs[0] if len(outs) == 1 else jnp.concatenate(outs, axis=0)

    def count_ge(thr):
        return row_sum(scan(lambda acc, s, t: acc + jnp.where(s >= t, 1.0, 0.0), 0.0, thr))

    def max_below(bound):
        acc = scan(lambda acc, s, b: jnp.maximum(acc, jnp.where(s < b, s, -jnp.inf)), -jnp.inf, bound)
        return lanes_all(acc, jnp.max)

    rmin = lanes_all(lo_ref[...], jnp.min)
    rmax = lanes_all(hi_ref[...], jnp.max)

    c_max = count_ge(rmax)
    top_ties = c_max >= k_sel

    def bisect(_, c):
        lo, hi = c
        mid = 0.5 * lo + 0.5 * hi
        ge = count_ge(mid) >= k_sel
        return jnp.where(ge, mid, lo), jnp.where(ge, hi, mid)

    _, hi = lax.fori_loop(0, N_BISECT, bisect, (rmin, rmax))

    def walk_cond(st):
        it, _, _, _, active = st
        return jnp.logical_and(jnp.max(active) > 0.0, it < max_steps)

    def walk_body(st):
        it, cand, thr, c_thr, active = st
        c = count_ge(cand)
        ok = c >= k_sel
        act = active > 0.0
        hit = jnp.logical_and(act, ok)
        thr = jnp.where(hit, cand, thr)
        c_thr = jnp.where(hit, c, c_thr)
        active = jnp.where(jnp.logical_and(act, jnp.logical_not(ok)), 1.0, 0.0)
        return it + 1, max_below(cand), thr, c_thr, active

    lowest = float(jnp.finfo(jnp.float32).min)
    thr0 = jnp.where(few, lowest, jnp.where(top_ties, rmax, lowest))
    c0 = jnp.where(few, k_sel, jnp.where(top_ties, c_max, k_sel))
    active0 = jnp.where(jnp.logical_or(few, top_ties), 0.0, 1.0)
    _, _, thr, c_thr, _ = lax.while_loop(
        walk_cond, walk_body, (jnp.int32(0), max_below(hi), thr0, c0, active0))

    has_excess = jnp.max(jnp.where(c_thr > k_sel, 1.0, 0.0)) > 0.0

    def ranked_bias():
        n_tie = k_sel - row_sum(scan(lambda acc, s, t: acc + jnp.where(s > t, 1.0, 0.0), 0.0, thr))
        for rows in row_blocks:
            thr_r = thr[rows]
            n_tie_r = n_tie[rows]

            def body(t, before, rows=rows, thr_r=thr_r, n_tie_r=n_tie_r):
                for g in range(n_groups):
                    lanes = slice(g * LANES, (g + 1) * LANES)
                    s = s_ref[t, rows, lanes]
                    tie = s == thr_r
                    pr = jnp.dot(jnp.where(tie, 1.0, 0.0).astype(BF16), prefix_mat,
                                 preferred_element_type=F32)
                    rank = before + pr[:, :LANES]
                    tie_bias = jnp.where(rank <= n_tie_r, 0.0, MASKED)
                    s_ref[t, rows, lanes] = jnp.where(s > thr_r, 0.0, jnp.where(tie, tie_bias, MASKED))
                    before = before + pr[:, LANES:]
                return before

            lax.fori_loop(0, n_tiles, body, jnp.zeros((rb, LANES), F32))
        return 0

    def plain_bias():
        for rows in row_blocks:
            thr_r = thr[rows]

            def body(t, c, rows=rows, thr_r=thr_r):
                for g in range(n_groups):
                    lanes = slice(g * LANES, (g + 1) * LANES)
                    s_ref[t, rows, lanes] = jnp.where(s_ref[t, rows, lanes] >= thr_r, 0.0, MASKED)
                return c

            lax.fori_loop(0, n_tiles, body, 0)
        return 0

    lax.cond(has_excess, ranked_bias, plain_bias)

    q_h = head_views(q_ref[...])
    for g in range(H_B // 2):
        lanes = slice(g * LANES, (g + 1) * LANES)
        mx_ref[...] = jnp.full(mx_ref.shape, -jnp.inf, F32)
        acc_ref[...] = jnp.zeros(acc_ref.shape, F32)

        def score_tile(t, c, g=g, lanes=lanes):
            start = pl.multiple_of(t * tk, tk)
            k = k_ref[pl.ds(start, tk), lanes]
            bias = s_ref[t]
            for i in range(2):
                s = _dot_nt(q_h[2 * g + i], k) + bias
                sc_ref[i, t] = s
                mx_ref[i] = _lane_max(mx_ref[i], s)
            return c

        lax.fori_loop(0, n_tiles, score_tile, 0)
        for i in range(2):
            mx_ref[i] = jnp.broadcast_to(jnp.max(mx_ref[i], axis=1, keepdims=True), (qb, LANES))

        def value_tile(t, c, lanes=lanes):
            start = pl.multiple_of(t * tk, tk)
            v = v_ref[pl.ds(start, tk), lanes]
            for i in range(2):
                acc_ref[i] = acc_ref[i] + _softmax_value_tile(sc_ref, (i, t), mx_ref[i], v)
            return c

        lax.fori_loop(0, n_tiles, value_tile, 0)
        even = acc_ref[0, :, :LANES] / acc_ref[0, :, LANES:]
        odd = acc_ref[1, :, :LANES] / acc_ref[1, :, LANES:]
        o_ref[:, lanes] = jnp.where(low_half, even, odd).astype(BF16)


def _dsa(qi, kw, q, ki, k, v, *, past, lk, tk):
    b, l, _ = q.shape
    lkp = k.shape[1]
    qb = min(l, Q_BLOCK)
    assert l % qb == 0 and lkp % tk == 0 and tk % MXU_DIM == 0 and qb % CHUNK == 0
    kern = functools.partial(_dsa_kernel, past=past, lk=lk, qb=qb, tk=tk,
                             topk=min(TOPK, lk // 4), max_steps=lkp)
    return pl.pallas_call(
        kern,
        grid=(b, l // qb),
        in_specs=[pl.BlockSpec((None, qb, H_IDX * D_IDX), lambda bi, j: (bi, j, 0)),
                  pl.BlockSpec((None, qb, LANES), lambda bi, j: (bi, j, 0)),
                  pl.BlockSpec((None, qb, W_B), lambda bi, j: (bi, j, 0)),
                  pl.BlockSpec((None, lkp, LANES), lambda bi, j: (bi, 0, 0)),
                  pl.BlockSpec((None, lkp, W_B), lambda bi, j: (bi, 0, 0)),
                  pl.BlockSpec((None, lkp, W_B), lambda bi, j: (bi, 0, 0))],
        out_specs=pl.BlockSpec((None, qb, W_B), lambda bi, j: (bi, j, 0)),
        out_shape=jax.ShapeDtypeStruct((b, l, W_B), BF16),
        scratch_shapes=[pltpu.VMEM((lkp // tk, qb, tk), F32),
                        pltpu.VMEM((2, lkp // tk, qb, tk), F32),
                        pltpu.VMEM((qb, LANES), F32), pltpu.VMEM((qb, LANES), F32),
                        pltpu.VMEM((2, qb, LANES), F32), pltpu.VMEM((2, qb, 2 * LANES), F32)],
        compiler_params=_cparams(("parallel", "parallel")),
        name="dsa",
    )(qi, kw, q, ki, k, v)


def _shift_rows(x, prev, k):
    rolled = pltpu.roll(x, k, 0)
    row = lax.broadcasted_iota(jnp.int32, (SUBLANES, 1), 0)
    top = jnp.where(row < k, pltpu.roll(prev, k, 0), rolled[0:SUBLANES])
    if x.shape[0] == SUBLANES:
        return top
    return jnp.concatenate([top, rolled[SUBLANES:]], axis=0)


def _rglru_kernel(xc_ref, gc_ref, cst_ref, h0_ref, cw_ref, cb_ref, wg_ref, bg_ref, lam_ref,
                  oc_ref, hl_ref, cn_ref, prev_ref, h_ref, a_ref, b_ref, hs_ref, *, past, tl):
    i = pl.program_id(1)

    @pl.when(i == 0)
    def _():
        prev_ref[...] = jnp.zeros(prev_ref.shape, F32)
        prev_ref[SUBLANES - (CONV_C - 1):SUBLANES, :] = cst_ref[...]
        h_ref[...] = h0_ref[...]

    x = xc_ref[...]
    prev = prev_ref[...]
    cw = cw_ref[...]
    xconv = cw[CONV_C - 1:CONV_C] * x + cb_ref[...]
    for k in range(1, CONV_C):
        xconv = xconv + cw[CONV_C - 1 - k:CONV_C - k] * _shift_rows(x, prev, k)
    prev_ref[...] = x[tl - SUBLANES:tl]

    pre = jnp.dot(xconv.astype(BF16), wg_ref[...], preferred_element_type=F32) + bg_ref[...]
    r = jax.nn.sigmoid(pre[:, :W_C])
    gate_i = jax.nn.sigmoid(pre[:, W_C:])
    neg_lam = -lam_ref[...]
    softplus = jnp.maximum(neg_lam, 0.0) + jnp.log1p(jnp.exp(-jnp.abs(neg_lam)))
    log_a = -RG_C * r * softplus
    pos = past + i * tl + lax.broadcasted_iota(jnp.int32, (tl, 1), 0)
    th = jnp.tanh(log_a)
    mult = jnp.where(pos == 0, 1.0, jnp.sqrt(-2.0 * th / (1.0 - th)))
    a_ref[...] = jnp.exp(log_a)
    b_ref[...] = mult * gate_i * xconv

    def step(t, h):
        h = a_ref[pl.ds(t, 1), :] * h + b_ref[pl.ds(t, 1), :]
        hs_ref[pl.ds(t, 1), :] = h
        return h

    h_last = lax.fori_loop(0, tl, step, h_ref[...], unroll=8)
    h_ref[...] = h_last

    gc = gc_ref[...]
    gelu = 0.5 * gc * (1.0 + jnp.tanh(math.sqrt(2.0 / math.pi) * (gc + 0.044715 * (gc * gc * gc))))
    oc_ref[...] = (hs_ref[...] * gelu).astype(BF16)

    @pl.when(i == pl.num_programs(1) - 1)
    def _():
        hl_ref[...] = h_last
        cn_ref[...] = x[tl - (CONV_C - 1):tl]


def _rglru(xc, gc, conv_state, h0, conv_w, conv_b, w_gate, b_gate, lam, *, past):
    b, l, _ = xc.shape
    tl = min(l, 512)
    assert l >= SUBLANES and l % tl == 0
    kern = functools.partial(_rglru_kernel, past=past, tl=tl)
    const = lambda bi, i: (0, 0)
    return pl.pallas_call(
        kern,
        grid=(b, l // tl),
        in_specs=[pl.BlockSpec((None, tl, W_C), lambda bi, i: (bi, i, 0)),
                  pl.BlockSpec((None, tl, W_C), lambda bi, i: (bi, i, 0)),
                  pl.BlockSpec((None, CONV_C - 1, W_C), lambda bi, i: (bi, 0, 0)),
                  pl.BlockSpec((None, 1, W_C), lambda bi, i: (bi, 0, 0)),
                  pl.BlockSpec((CONV_C, W_C), const),
                  pl.BlockSpec((1, W_C), const),
                  pl.BlockSpec((W_C, 2 * W_C), const),
                  pl.BlockSpec((1, 2 * W_C), const),
                  pl.BlockSpec((1, W_C), const)],
        out_specs=[pl.BlockSpec((None, tl, W_C), lambda bi, i: (bi, i, 0)),
                   pl.BlockSpec((None, 1, W_C), lambda bi, i: (bi, 0, 0)),
                   pl.BlockSpec((None, CONV_C - 1, W_C), lambda bi, i: (bi, 0, 0))],
        out_shape=[jax.ShapeDtypeStruct((b, l, W_C), BF16),
                   jax.ShapeDtypeStruct((b, 1, W_C), F32),
                   jax.ShapeDtypeStruct((b, CONV_C - 1, W_C), F32)],
        scratch_shapes=[pltpu.VMEM((SUBLANES, W_C), F32), pltpu.VMEM((1, W_C), F32),
                        pltpu.VMEM((tl, W_C), F32), pltpu.VMEM((tl, W_C), F32),
                        pltpu.VMEM((tl, W_C), F32)],
        compiler_params=_cparams(("parallel", "arbitrary")),
        name="rglru",
    )(xc, gc, conv_state, h0.reshape(b, 1, W_C), conv_w, conv_b.reshape(1, W_C), w_gate,
      b_gate, lam.reshape(1, W_C))


def _outproj_kernel(x_ref, oa_ref, ob_ref, oc_ref, w_ref, g_ref, x1_ref, hn_ref):
    mix = jnp.dot(oa_ref[...], w_ref[0:W_A, :], preferred_element_type=F32)
    mix = mix + jnp.dot(ob_ref[...], w_ref[W_A:W_A + W_B, :], preferred_element_type=F32)
    mix = mix + jnp.dot(oc_ref[...], w_ref[W_A + W_B:, :], preferred_element_type=F32)
    x1 = x_ref[...] + mix
    x1_ref[...] = x1
    hn_ref[...] = _rms(x1, g_ref[...]).astype(BF16)


def _outproj(x2d, oa, ob, oc, w_out, gain):
    t = x2d.shape[0]
    tm = min(512, t)
    row = lambda w: pl.BlockSpec((tm, w), lambda i: (i, 0))
    return pl.pallas_call(
        _outproj_kernel,
        grid=(t // tm,),
        in_specs=[row(D_MODEL), row(W_A), row(W_B), row(W_C),
                  pl.BlockSpec((D_MODEL, D_MODEL), lambda i: (0, 0)),
                  pl.BlockSpec((1, D_MODEL), lambda i: (0, 0))],
        out_specs=[row(D_MODEL), row(D_MODEL)],
        out_shape=[jax.ShapeDtypeStruct((t, D_MODEL), F32), jax.ShapeDtypeStruct((t, D_MODEL), BF16)],
        compiler_params=_cparams(("parallel",)),
        name="outproj",
    )(x2d, oa, ob, oc, w_out, gain.reshape(1, D_MODEL))


def _ffn_kernel(hn_ref, x1_ref, wu_ref, wg_ref, cwu_ref, cwg_ref, cbu_ref, cbg_ref, wd_ref,
                su_ref, sg_ref, gfin_ref, y_ref, fu_ref, fg_ref,
                acc_ref, au_ref, ag_ref, cu_ref, cg_ref, *, tm, final_norm):
    i = pl.program_id(1)
    s = pl.program_id(2)
    nf = pl.num_programs(2) - 1
    live = s > 0
    fb = jnp.maximum(s - 1, 0)
    row = jnp.where(live, fb, nf)

    @pl.when(s == 0)
    def _():
        au_ref[1] = jnp.zeros(au_ref.shape[1:], F32)
        ag_ref[1] = jnp.zeros(ag_ref.shape[1:], F32)
        cu_ref[nf] = jnp.zeros(cu_ref.shape[1:], F32)
        cg_ref[nf] = jnp.zeros(cg_ref.shape[1:], F32)
        acc_ref[...] = x1_ref[...]

    @pl.when(jnp.logical_and(i == 0, live))
    def _():
        for carry_ref, st_ref in ((cu_ref, su_ref), (cg_ref, sg_ref)):
            carry_ref[fb] = jnp.zeros(carry_ref.shape[1:], F32)
            carry_ref[fb, SUBLANES - (CONV_F - 1):SUBLANES, :] = st_ref[...]

    def step(wslot, rslot):
        au_ref[rslot, 0:SUBLANES, :] = cu_ref[row]
        ag_ref[rslot, 0:SUBLANES, :] = cg_ref[row]
        rc = tm // FFN_ROW_CHUNKS

        def conv(a_ref, cw_ref, cb_ref, r0):
            cw = cw_ref[...]
            y = cw[CONV_F - 1:CONV_F] * a_ref[rslot, SUBLANES + r0:SUBLANES + r0 + rc, :] + cb_ref[...]
            for k in range(1, CONV_F):
                y = y + (cw[CONV_F - 1 - k:CONV_F - k]
                         * a_ref[rslot, SUBLANES - k + r0:SUBLANES - k + r0 + rc, :])
            return y

        for c in range(FFN_ROW_CHUNKS):
            r0 = c * rc
            hn = hn_ref[r0:r0 + rc, :]
            au_ref[wslot, SUBLANES + r0:SUBLANES + r0 + rc, :] = jnp.dot(
                hn, wu_ref[...], preferred_element_type=F32)
            ag_ref[wslot, SUBLANES + r0:SUBLANES + r0 + rc, :] = jnp.dot(
                hn, wg_ref[...], preferred_element_type=F32)
            u = conv(au_ref, cwu_ref, cbu_ref, r0)
            g = conv(ag_ref, cwg_ref, cbg_ref, r0)
            mid = (g * jax.nn.sigmoid(g) * u).astype(BF16)
            contrib = jnp.dot(mid, wd_ref[...], preferred_element_type=F32)
            acc_ref[r0:r0 + rc, :] = acc_ref[r0:r0 + rc, :] + jnp.where(live, contrib, 0.0)

        for a_ref, carry_ref, tail_ref in ((au_ref, cu_ref, fu_ref), (ag_ref, cg_ref, fg_ref)):
            carry_ref[row] = a_ref[rslot, tm:tm + SUBLANES, :]
            tail_ref[row] = a_ref[rslot, tm + SUBLANES - (CONV_F - 1):tm + SUBLANES, :]

    parity = lax.rem(s, 2)

    @pl.when(parity == 0)
    def _():
        step(0, 1)

    @pl.when(parity == 1)
    def _():
        step(1, 0)

    @pl.when(s == nf)
    def _():
        y = acc_ref[...]
        if final_norm:
            y = _rms(y, gfin_ref[...])
        y_ref[...] = y


def _ffn(hn, x1, w_up, conv_w, conv_b, w_down, state, final_gain, *, final_norm):
    b, l, _ = hn.shape
    tm = min(l, 1024)
    tf = 512
    nf = D_FF // tf
    assert l % tm == 0 and tm >= SUBLANES
    kern = functools.partial(_ffn_kernel, tm=tm, final_norm=final_norm)
    conv_b = conv_b.reshape(1, 2 * D_FF)
    up = lambda s: jnp.minimum(s, nf - 1)
    fin = lambda s: jnp.maximum(s - 1, 0)
    tail_spec = pl.BlockSpec((None, nf + 1, CONV_F - 1, tf), lambda bi, i, s: (bi, 0, 0, 0))
    tail_shape = jax.ShapeDtypeStruct((b, nf + 1, CONV_F - 1, tf), F32)
    y, fu, fg = pl.pallas_call(
        kern,
        grid=(b, l // tm, nf + 1),
        in_specs=[pl.BlockSpec((None, tm, D_MODEL), lambda bi, i, s: (bi, i, 0)),
                  pl.BlockSpec((None, tm, D_MODEL), lambda bi, i, s: (bi, i, 0)),
                  pl.BlockSpec((D_MODEL, tf), lambda bi, i, s: (0, up(s))),
                  pl.BlockSpec((D_MODEL, tf), lambda bi, i, s: (0, nf + up(s))),
                  pl.BlockSpec((CONV_F, tf), lambda bi, i, s: (0, fin(s))),
                  pl.BlockSpec((CONV_F, tf), lambda bi, i, s: (0, nf + fin(s))),
                  pl.BlockSpec((1, tf), lambda bi, i, s: (0, fin(s))),
                  pl.BlockSpec((1, tf), lambda bi, i, s: (0, nf + fin(s))),
                  pl.BlockSpec((tf, D_MODEL), lambda bi, i, s: (fin(s), 0)),
                  pl.BlockSpec((None, CONV_F - 1, tf), lambda bi, i, s: (bi, 0, fin(s))),
                  pl.BlockSpec((None, CONV_F - 1, tf), lambda bi, i, s: (bi, 0, nf + fin(s))),
                  pl.BlockSpec((1, D_MODEL), lambda bi, i, s: (0, 0))],
        out_specs=[pl.BlockSpec((None, tm, D_MODEL), lambda bi, i, s: (bi, i, 0)),
                   tail_spec, tail_spec],
        out_shape=[jax.ShapeDtypeStruct((b, l, D_MODEL), F32), tail_shape, tail_shape],
        scratch_shapes=[pltpu.VMEM((tm, D_MODEL), F32),
                        pltpu.VMEM((2, tm + SUBLANES, tf), F32),
                        pltpu.VMEM((2, tm + SUBLANES, tf), F32),
                        pltpu.VMEM((nf + 1, SUBLANES, tf), F32),
                        pltpu.VMEM((nf + 1, SUBLANES, tf), F32)],
        compiler_params=_cparams(("parallel", "arbitrary", "arbitrary")),
        name="ffn",
    )(hn, x1, w_up, w_up, conv_w, conv_w, conv_b, conv_b, w_down, state, state,
      final_gain.reshape(1, D_MODEL))
    flat = lambda a: jnp.swapaxes(a[:, :nf], 1, 2).reshape(b, CONV_F - 1, D_FF)
    return y, jnp.concatenate([flat(fu), flat(fg)], axis=-1)


def _cast_kernel(x_ref, o_ref):
    o_ref[...] = x_ref[...].astype(o_ref.dtype)


def _layer_bf16(w, li):
    _, r, c = w.shape
    tr = 256 if r % 256 == 0 else r
    return pl.pallas_call(
        _cast_kernel,
        grid=(r // tr,),
        in_specs=[pl.BlockSpec((None, tr, c), lambda i: (li, i, 0))],
        out_specs=pl.BlockSpec((tr, c), lambda i: (i, 0)),
        out_shape=jax.ShapeDtypeStruct((r, c), BF16),
        compiler_params=_cparams(("parallel",)),
        name="cast",
    )(w)


def _prep_layer_weights(p, li):
    w_in = p["w_in"][li]
    zeros = lambda n: jnp.zeros((D_MODEL, n), w_in.dtype)
    k_idx = w_in[:, 2560:2624]
    w_pad = jnp.concatenate(
        [w_in[:, :2628], zeros(_C_KK - 2628), k_idx, k_idx, w_in[:, 2628:]], axis=1).astype(BF16)
    assert w_pad.shape[1] == PROJ_W_PAD

    def block_diag(w):
        out = jnp.zeros((W_C, W_C), w.dtype)
        for n in range(N_GATE_BLOCKS):
            sl = slice(n * GATE_BLOCK, (n + 1) * GATE_BLOCK)
            out = out.at[sl, sl].set(w[n])
        return out

    w_gate = jnp.concatenate([block_diag(p["rg_w_r"][li]), block_diag(p["rg_w_i"][li])], axis=1)
    b_gate = jnp.concatenate([p["rg_b_r"][li].reshape(1, W_C), p["rg_b_i"][li].reshape(1, W_C)], axis=1)
    lam_init = 0.8 - 0.6 * math.exp(-0.3 * li)
    f32 = lambda a: a.astype(F32)
    lam = (jnp.exp(jnp.sum(f32(p["lam_q1"][li]) * f32(p["lam_k1"][li])))
           - jnp.exp(jnp.sum(f32(p["lam_q2"][li]) * f32(p["lam_k2"][li]))) + lam_init)
    return dict(
        norm_mix=p["norm_mix"][li], w_pad=w_pad, lam=jnp.full((1, LANES), lam, F32),
        lam_init=lam_init, diff_gain=p["diff_gain"][li].reshape(1, LANES),
        rg_conv_w=p["rg_conv_w"][li], rg_conv_b=p["rg_conv_b"][li],
        w_gate=w_gate.astype(BF16), b_gate=b_gate, rg_lambda=p["rg_lambda"][li],
        w_out=_layer_bf16(p["w_out"], li), norm_ffn=p["norm_ffn"][li],
        ffn_w_up=_layer_bf16(p["ffn_w_up"], li), ffn_conv_w=p["ffn_conv_w"][li],
        ffn_conv_b=p["ffn_conv_b"][li], ffn_w_down=_layer_bf16(p["ffn_w_down"], li))


def _layer(x, past, w, final_gain, final_norm):
    b, l, _ = x.shape
    a_k0, a_v0, b_k0, b_v0, b_ki0, c_h0, c_cv0, f_cv0 = past
    p_len = 0 if a_k0 is None else a_k0.shape[1]
    lk = p_len + l
    tk = PROMPT_KEY_TILE if p_len == 0 else _round_up(lk, MXU_DIM)
    lkp = _round_up(lk, tk)
    t = b * l

    (qa, qb, qi, ka, va, kb, vb, kw, xc, gc, kab, vab, kbb, vbb, kib) = _proj(
        x.reshape(t, D_MODEL), w["norm_mix"], w["w_pad"])

    def keys(cache, new, dup=False):
        new = new.reshape(b, l, -1)
        parts = []
        if cache is not None:
            c = cache.reshape(b, p_len, -1).astype(BF16)
            parts.append(jnp.concatenate([c, c], axis=-1) if dup else c)
        parts.append(new)
        if lkp > lk:
            parts.append(jnp.zeros((b, lkp - lk, new.shape[-1]), BF16))
        return parts[0] if len(parts) == 1 else jnp.concatenate(parts, axis=1)

    o_a = _diff_attn(qa.reshape(b, l, W_A), keys(a_k0, kab), keys(a_v0, vab), w["lam"],
                     w["diff_gain"], past=p_len, lk=lk, tk=tk, out_scale=1.0 - w["lam_init"])
    o_b = _dsa(qi.reshape(b, l, -1), kw.reshape(b, l, LANES), qb.reshape(b, l, W_B),
               keys(b_ki0, kib, dup=True), keys(b_k0, kbb), keys(b_v0, vbb),
               past=p_len, lk=lk, tk=tk)
    o_c, h_last, conv_new = _rglru(xc.reshape(b, l, W_C), gc.reshape(b, l, W_C), c_cv0, c_h0,
                                   w["rg_conv_w"], w["rg_conv_b"], w["w_gate"], w["b_gate"],
                                   w["rg_lambda"], past=p_len)
    x1, hn = _outproj(x.reshape(t, D_MODEL), o_a.reshape(t, W_A), o_b.reshape(t, W_B),
                      o_c.reshape(t, W_C), w["w_out"], w["norm_ffn"])
    y, f_buf = _ffn(hn.reshape(b, l, D_MODEL), x1.reshape(b, l, D_MODEL), w["ffn_w_up"],
                    w["ffn_conv_w"], w["ffn_conv_b"], w["ffn_w_down"], f_cv0, final_gain,
                    final_norm=final_norm)
    new = (ka.reshape(b, l, H_A, 2 * HEAD_DIM), va.reshape(b, l, H_A, 2 * HEAD_DIM),
           kb.reshape(b, l, H_B, HEAD_DIM), vb.reshape(b, l, H_B, HEAD_DIM),
           kw[:, :D_IDX].reshape(b, l, D_IDX), h_last.reshape(b, W_C), conv_new, f_buf)
    return y, new


def _trunk(x, past, weights, final_gain):
    states = []
    for li in range(N_LAYERS):
        layer_past = tuple(None if c is None else c[li] for c in past)
        x, st = _layer(x, layer_past, weights[li], final_gain, final_norm=(li == N_LAYERS - 1))
        states.append(st)
    return x, states


def _forward(x_prompt, x_sample, caches, params):
    weights = [_prep_layer_weights(params, li) for li in range(N_LAYERS)]
    bp = x_prompt.shape[0]
    dt = x_prompt.dtype
    past_prompt = (None, None, None, None, None,
                   jnp.zeros((N_LAYERS, bp, W_C), dt),
                   jnp.zeros((N_LAYERS, bp, CONV_C - 1, W_C), dt),
                   jnp.zeros((N_LAYERS, bp, CONV_F - 1, 2 * D_FF), dt))
    yp, sp = _trunk(x_prompt, past_prompt, weights, params["norm_final"])
    ys, ss = _trunk(x_sample, caches, weights, params["norm_final"])
    out = [yp, ys]
    for jdx in range(8):
        out.append(jnp.stack([st[jdx] for st in sp], axis=0))
        out.append(jnp.stack([st[jdx] for st in ss], axis=0))
    return tuple(out)


def kernel(x_prompt, x_sample, cache_a_k, cache_a_v, cache_b_k, cache_b_v, cache_b_kidx,
           state_c_h, state_c_conv, state_ffn_conv, norm_mix, w_in, lam_q1, lam_k1, lam_q2,
           lam_k2, diff_gain, rg_conv_w, rg_conv_b, rg_w_r, rg_b_r, rg_w_i, rg_b_i, rg_lambda,
           w_out, norm_ffn, ffn_w_up, ffn_conv_w, ffn_conv_b, ffn_w_down, norm_final):
    params = dict(norm_mix=norm_mix, w_in=w_in, lam_q1=lam_q1, lam_k1=lam_k1, lam_q2=lam_q2,
                  lam_k2=lam_k2, diff_gain=diff_gain, rg_conv_w=rg_conv_w, rg_conv_b=rg_conv_b,
                  rg_w_r=rg_w_r, rg_b_r=rg_b_r, rg_w_i=rg_w_i, rg_b_i=rg_b_i, rg_lambda=rg_lambda,
                  w_out=w_out, norm_ffn=norm_ffn, ffn_w_up=ffn_w_up, ffn_conv_w=ffn_conv_w,
                  ffn_conv_b=ffn_conv_b, ffn_w_down=ffn_w_down, norm_final=norm_final)
    caches = (cache_a_k, cache_a_v, cache_b_k, cache_b_v, cache_b_kidx,
              state_c_h, state_c_conv, state_ffn_conv)
    return _forward(x_prompt, x_sample, caches, params)
```

```python
import functools
import math

import jax
import jax.numpy as jnp
from jax import lax
from jax.experimental import pallas as pl
from jax.experimental.pallas import tpu as pltpu

F32 = jnp.float32
BF16 = jnp.bfloat16

D_MODEL = 1024
N_LAYERS = 2
CHUNK = 64
CHUNK_SHIFT = 6
HEAD_DIM = 64
H_A = 4
W_A = H_A * 2 * HEAD_DIM
H_B = 4
W_B = H_B * HEAD_DIM
H_IDX = 4
D_IDX = 64
TOPK = 256
W_C = 256
N_GATE_BLOCKS = 4
GATE_BLOCK = W_C // N_GATE_BLOCKS
RG_C = 8.0
CONV_C = 4
D_FF = 3072
CONV_F = 3
EPS = 1e-6

LANES = 128
SUBLANES = 8
MXU_DIM = 256
Q_BLOCK = 256
SCAN_ROWS = 128
PROMPT_KEY_TILE = 1024
VMEM_LIMIT = 58 * 2**20
MASKED = -1e30
N_BISECT = 18
FFN_ROW_CHUNKS = 4
Q_SCALE = HEAD_DIM ** -0.5 * math.log2(math.e)

PROJ_W_PAD = 3328
_C_QA, _C_KA, _C_VA = 0, 512, 1024
_C_QB, _C_KB, _C_VB = 1536, 1792, 2048
_C_QI = 2304
_C_KW = 2560
_C_KK = 2688
_C_XC = 2816
_C_GC = 3072


def _cparams(sem):
    return pltpu.CompilerParams(dimension_semantics=sem, vmem_limit_bytes=VMEM_LIMIT)


def _rms(x, g):
    return x * lax.rsqrt(jnp.mean(x * x, axis=-1, keepdims=True) + EPS) * g


def _dot_nt(a, b):
    return lax.dot_general(a, b, (((1,), (1,)), ((), ())), preferred_element_type=F32)


def _round_up(n, m):
    return (n + m - 1) // m * m


def _proj_kernel(x_ref, g_ref, w_ref, qa_ref, qb_ref, qi_ref, ka_ref, va_ref, kb_ref, vb_ref,
                 kw_ref, xc_ref, gc_ref, kab_ref, vab_ref, kbb_ref, vbb_ref, kib_ref):
    h = _rms(x_ref[...], g_ref[...])
    z = jnp.dot(h.astype(BF16), w_ref[...], preferred_element_type=F32)
    qa_ref[...] = (z[:, _C_QA:_C_QA + W_A] * Q_SCALE).astype(BF16)
    qb_ref[...] = (z[:, _C_QB:_C_QB + W_B] * Q_SCALE).astype(BF16)
    qi_ref[...] = z[:, _C_QI:_C_QI + H_IDX * D_IDX].astype(BF16)
    ka = z[:, _C_KA:_C_KA + W_A]
    va = z[:, _C_VA:_C_VA + W_A]
    kb = z[:, _C_KB:_C_KB + W_B]
    vb = z[:, _C_VB:_C_VB + W_B]
    ka_ref[...] = ka
    va_ref[...] = va
    kb_ref[...] = kb
    vb_ref[...] = vb
    kab_ref[...] = ka.astype(BF16)
    vab_ref[...] = va.astype(BF16)
    kbb_ref[...] = kb.astype(BF16)
    vbb_ref[...] = vb.astype(BF16)
    kw_ref[...] = z[:, _C_KW:_C_KW + LANES]
    kib_ref[...] = z[:, _C_KK:_C_KK + LANES].astype(BF16)
    xc_ref[...] = z[:, _C_XC:_C_XC + W_C]
    gc_ref[...] = z[:, _C_GC:_C_GC + W_C]


def _proj(x2d, gain, w_pad):
    t = x2d.shape[0]
    tm = min(512, t)
    widths = [(W_A, BF16), (W_B, BF16), (H_IDX * D_IDX, BF16),
              (W_A, F32), (W_A, F32), (W_B, F32), (W_B, F32),
              (LANES, F32), (W_C, F32), (W_C, F32),
              (W_A, BF16), (W_A, BF16), (W_B, BF16), (W_B, BF16), (LANES, BF16)]
    return pl.pallas_call(
        _proj_kernel,
        grid=(t // tm,),
        in_specs=[pl.BlockSpec((tm, D_MODEL), lambda i: (i, 0)),
                  pl.BlockSpec((1, D_MODEL), lambda i: (0, 0)),
                  pl.BlockSpec((D_MODEL, PROJ_W_PAD), lambda i: (0, 0))],
        out_specs=[pl.BlockSpec((tm, w), lambda i: (i, 0)) for w, _ in widths],
        out_shape=[jax.ShapeDtypeStruct((t, w), d) for w, d in widths],
        compiler_params=_cparams(("parallel",)),
        name="proj",
    )(x2d, gain.reshape(1, D_MODEL), w_pad)


def _tile_bounds(q_start, qb, lk, tk):
    n_full = lax.div(jnp.minimum(q_start + CHUNK, lk), tk)
    n_tiles = lax.div(q_start + qb + tk - 1, tk)
    return n_full, n_tiles


def _admissible(start, tk, q_chunk, lk):
    col = start + lax.broadcasted_iota(jnp.int32, (1, tk), 1)
    return (lax.shift_right_logical(col, CHUNK_SHIFT) <= q_chunk) & (col < lk)


def _lane_max(acc, x):
    for g in range(x.shape[1] // LANES):
        acc = jnp.maximum(acc, x[:, g * LANES:(g + 1) * LANES])
    return acc


def _softmax_value_tile(s_ref, idx, m, v):
    parts = []
    for g in range(s_ref.shape[-1] // LANES):
        s = s_ref[idx + (slice(None), slice(g * LANES, (g + 1) * LANES))]
        parts.append(jnp.exp2((s - m).astype(BF16)))
    v_ones = jnp.concatenate([v, jnp.ones_like(v)], axis=1)
    return jnp.dot(jnp.concatenate(parts, axis=1), v_ones, preferred_element_type=F32)


def _diff_attn_kernel(*refs, past, lk, qb, tk, out_scale, cached):
    if cached:
        (lam_ref, gain_ref, q_ref, kn_ref, vn_ref, kc_ref, vc_ref, o_ref,
         s_ref, mx_ref, acc_ref, k_ref, v_ref) = refs
        n_cache, n_new = kc_ref.shape[0], kn_ref.shape[0]
        for cache_ref, new_ref, dst_ref in ((kc_ref, kn_ref, k_ref), (vc_ref, vn_ref, v_ref)):
            dst_ref[0:n_cache, :] = cache_ref[...].astype(BF16)
            dst_ref[n_cache:n_cache + n_new, :] = new_ref[...]
            if tk > n_cache + n_new:
                dst_ref[n_cache + n_new:, :] = jnp.zeros((tk - n_cache - n_new, LANES), BF16)
    else:
        lam_ref, gain_ref, q_ref, k_ref, v_ref, o_ref, s_ref, mx_ref, acc_ref = refs
    j = pl.program_id(2)
    q = q_ref[...]
    lane = lax.broadcasted_iota(jnp.int32, (1, LANES), 1)
    zero = jnp.zeros_like(q)
    q_half = (jnp.where(lane < HEAD_DIM, q, zero), jnp.where(lane >= HEAD_DIM, q, zero))
    q_start = past + j * qb
    row = lax.broadcasted_iota(jnp.int32, (qb, 1), 0)
    q_chunk = lax.shift_right_logical(q_start + row, CHUNK_SHIFT)
    n_full, n_tiles = _tile_bounds(q_start, qb, lk, tk)

    mx_ref[...] = jnp.full(mx_ref.shape, -jnp.inf, F32)
    acc_ref[...] = jnp.zeros(acc_ref.shape, F32)

    def score_tile(t, masked):
        start = pl.multiple_of(t * tk, tk)
        k = k_ref[pl.ds(start, tk), :]
        if masked:
            ok = _admissible(start, tk, q_chunk, lk)
        for i in range(2):
            s = _dot_nt(q_half[i], k)
            if masked:
                s = jnp.where(ok, s, -jnp.inf)
            s_ref[i, t] = s
            mx_ref[i] = _lane_max(mx_ref[i], s)

    def full_body(t, c):
        score_tile(t, False)
        return c

    def masked_body(t, c):
        score_tile(t, True)
        return c

    lax.fori_loop(0, n_full, full_body, 0)
    lax.fori_loop(n_full, n_tiles, masked_body, 0)

    for i in range(2):
        mx_ref[i] = jnp.broadcast_to(jnp.max(mx_ref[i], axis=1, keepdims=True), (qb, LANES))

    def value_tile(t, c):
        start = pl.multiple_of(t * tk, tk)
        v = v_ref[pl.ds(start, tk), :]
        for i in range(2):
            acc_ref[i] = acc_ref[i] + _softmax_value_tile(s_ref, (i, t), mx_ref[i], v)
        return c

    lax.fori_loop(0, n_tiles, value_tile, 0)

    o = (acc_ref[0, :, :LANES] / acc_ref[0, :, LANES:]
         - lam_ref[...] * (acc_ref[1, :, :LANES] / acc_ref[1, :, LANES:]))
    o_ref[...] = (_rms(o, gain_ref[...]) * out_scale).astype(BF16)


def _diff_attn(q, k, v, lam, gain, *, past, lk, tk, out_scale, cache=None):
    b, l, _ = q.shape
    qb = min(l, Q_BLOCK)
    cached = cache is not None
    lkp = tk if cached else k.shape[1]
    assert l % qb == 0 and lkp % tk == 0 and qb % CHUNK == 0
    kern = functools.partial(_diff_attn_kernel, past=past, lk=lk, qb=qb, tk=tk,
                             out_scale=out_scale, cached=cached)
    in_specs = [pl.BlockSpec((1, LANES), lambda bi, h, j: (0, 0)),
                pl.BlockSpec((1, LANES), lambda bi, h, j: (0, 0)),
                pl.BlockSpec((None, qb, LANES), lambda bi, h, j: (bi, j, h)),
                pl.BlockSpec((None, k.shape[1], LANES), lambda bi, h, j: (bi, 0, h)),
                pl.BlockSpec((None, k.shape[1], LANES), lambda bi, h, j: (bi, 0, h))]
    scratch = [pltpu.VMEM((2, lkp // tk, qb, tk), F32),
               pltpu.VMEM((2, qb, LANES), F32), pltpu.VMEM((2, qb, 2 * LANES), F32)]
    operands = [lam, gain, q, k, v]
    if cached:
        cache_k, cache_v, li = cache
        assert l == qb and past + l <= tk and cache_k.shape[2:] == (past, W_A)
        head_rows = pl.BlockSpec((None, None, past, LANES), lambda bi, h, j: (li, bi, 0, h))
        in_specs += [head_rows, head_rows]
        scratch += [pltpu.VMEM((tk, LANES), BF16), pltpu.VMEM((tk, LANES), BF16)]
        operands += [cache_k, cache_v]
    return pl.pallas_call(
        kern,
        grid=(b, H_A, l // qb),
        in_specs=in_specs,
        out_specs=pl.BlockSpec((None, qb, LANES), lambda bi, h, j: (bi, j, h)),
        out_shape=jax.ShapeDtypeStruct((b, l, W_A), BF16),
        scratch_shapes=scratch,
        compiler_params=_cparams(("parallel", "parallel", "parallel")),
        name="diff_attn",
    )(*operands)


def _dsa_kernel(qi_ref, kw_ref, q_ref, ki_ref, k_ref, v_ref, o_ref,
                s_ref, sc_ref, lo_ref, hi_ref, mx_ref, acc_ref,
                *, past, lk, qb, tk, topk, max_steps):
    j = pl.program_id(1)
    q_start = past + j * qb
    n_full, n_tiles = _tile_bounds(q_start, qb, lk, tk)
    n_groups = tk // LANES
    row = lax.broadcasted_iota(jnp.int32, (qb, 1), 0)
    q_chunk = lax.shift_right_logical(q_start + row, CHUNK_SHIFT)
    lane = lax.broadcasted_iota(jnp.int32, (1, LANES), 1)
    low_half = lane < HEAD_DIM
    k_sel = float(topk)

    def wide(x):
        return jnp.broadcast_to(x, (qb, LANES))

    def head_views(x):
        views = []
        for h in range(4):
            pair = x[:, (h // 2) * LANES:(h // 2 + 1) * LANES]
            keep = low_half if h % 2 == 0 else jnp.logical_not(low_half)
            views.append(jnp.where(keep, pair, jnp.zeros_like(pair)))
        return views

    qi_h = head_views(qi_ref[...])
    kw = kw_ref[...]
    w_h = [wide(kw[:, D_IDX + h:D_IDX + h + 1]) for h in range(H_IDX)]
    lo_ref[...] = jnp.full(lo_ref.shape, jnp.inf, F32)
    hi_ref[...] = jnp.full(hi_ref.shape, -jnp.inf, F32)

    def index_tile(t, masked):
        start = pl.multiple_of(t * tk, tk)
        if masked:
            adm = _admissible(start, tk, q_chunk, lk)
        rmin = lo_ref[...]
        rmax = hi_ref[...]
        for c in range(tk // MXU_DIM):
            ki = ki_ref[pl.ds(start + c * MXU_DIM, MXU_DIM), :]
            rel = [jnp.maximum(_dot_nt(qi_h[h], ki), 0.0) for h in range(H_IDX)]
            for g in range(MXU_DIM // LANES):
                sl = slice(g * LANES, (g + 1) * LANES)
                sc = w_h[0] * rel[0][:, sl]
                for h in range(1, H_IDX):
                    sc = sc + w_h[h] * rel[h][:, sl]
                csl = slice(c * MXU_DIM + g * LANES, c * MXU_DIM + (g + 1) * LANES)
                if masked:
                    ok = adm[:, csl]
                    s_ref[t, :, csl] = jnp.where(ok, sc, -jnp.inf)
                    rmax = jnp.maximum(rmax, jnp.where(ok, sc, -jnp.inf))
                    rmin = jnp.minimum(rmin, jnp.where(ok, sc, jnp.inf))
                else:
                    s_ref[t, :, csl] = sc
                    rmax = jnp.maximum(rmax, sc)
                    rmin = jnp.minimum(rmin, sc)
        lo_ref[...] = rmin
        hi_ref[...] = rmax

    def index_full(t, c):
        index_tile(t, False)
        return c

    def index_masked(t, c):
        index_tile(t, True)
        return c

    lax.fori_loop(0, n_full, index_full, 0)
    lax.fori_loop(n_full, n_tiles, index_masked, 0)
    rb = min(qb, SCAN_ROWS)
    ones_mat = jnp.ones((LANES, LANES), BF16)
    tri_i = lax.broadcasted_iota(jnp.int32, (LANES, 2 * LANES), 0)
    tri_j = lax.broadcasted_iota(jnp.int32, (LANES, 2 * LANES), 1)
    prefix_mat = jnp.where(jnp.logical_or(tri_i <= tri_j, tri_j >= LANES), 1.0, 0.0).astype(BF16)
    assert s_ref.shape[0] * n_groups <= 256

    row_blocks = [slice(r * rb, (r + 1) * rb) for r in range(qb // rb)]
    pos = q_start + lax.broadcasted_iota(jnp.int32, (qb, LANES), 0)
    n_adm = jnp.minimum((lax.shift_right_logical(pos, CHUNK_SHIFT) + 1) * CHUNK, lk)
    few = n_adm <= topk

    def lanes_all(x, reduce):
        return jnp.broadcast_to(reduce(x, axis=1, keepdims=True), (qb, LANES))

    def row_sum(acc):
        return jnp.dot(acc.astype(BF16), ones_mat, preferred_element_type=F32)

    def scan(step, init, *operands):
        outs = []
        for rows in row_blocks:
            ops = [o[rows] for o in operands]

            def body(t, acc, rows=rows, ops=ops):
                for g in range(n_groups):
                    acc = step(acc, s_ref[t, rows, g * LANES:(g + 1) * LANES], *ops)
                return acc

            outs.append(lax.fori_loop(0, n_tiles, body, jnp.full((rb, LANES), init, F32)))
        return outs[0] if len(outs) == 1 else jnp.concatenate(outs, axis=0)

    def count_ge(thr):
        return row_sum(scan(lambda acc, s, t: acc + jnp.where(s >= t, 1.0, 0.0), 0.0, thr))

    def max_below(bound):
        acc = scan(lambda acc, s, b: jnp.maximum(acc, jnp.where(s < b, s, -jnp.inf)), -jnp.inf, bound)
        return lanes_all(acc, jnp.max)

    rmin = lanes_all(lo_ref[...], jnp.min)
    rmax = lanes_all(hi_ref[...], jnp.max)

    c_max = count_ge(rmax)
    top_ties = c_max >= k_sel

    def bisect(_, c):
        lo, hi = c
        mid = 0.5 * lo + 0.5 * hi
        ge = count_ge(mid) >= k_sel
        return jnp.where(ge, mid, lo), jnp.where(ge, hi, mid)

    _, hi = lax.fori_loop(0, N_BISECT, bisect, (rmin, rmax))

    def walk_cond(st):
        it, _, _, _, active = st
        return jnp.logical_and(jnp.max(active) > 0.0, it < max_steps)

    def walk_body(st):
        it, cand, thr, c_thr, active = st
        c = count_ge(cand)
        ok = c >= k_sel
        act = active > 0.0
        hit = jnp.logical_and(act, ok)
        thr = jnp.where(hit, cand, thr)
        c_thr = jnp.where(hit, c, c_thr)
        active = jnp.where(jnp.logical_and(act, jnp.logical_not(ok)), 1.0, 0.0)
        return it + 1, max_below(cand), thr, c_thr, active

    lowest = float(jnp.finfo(jnp.float32).min)
    thr0 = jnp.where(few, lowest, jnp.where(top_ties, rmax, lowest))
    c0 = jnp.where(few, k_sel, jnp.where(top_ties, c_max, k_sel))
    active0 = jnp.where(jnp.logical_or(few, top_ties), 0.0, 1.0)
    _, _, thr, c_thr, _ = lax.while_loop(
        walk_cond, walk_body, (jnp.int32(0), max_below(hi), thr0, c0, active0))

    has_excess = jnp.max(jnp.where(c_thr > k_sel, 1.0, 0.0)) > 0.0

    def ranked_bias():
        n_tie = k_sel - row_sum(scan(lambda acc, s, t: acc + jnp.where(s > t, 1.0, 0.0), 0.0, thr))
        for rows in row_blocks:
            thr_r = thr[rows]
            n_tie_r = n_tie[rows]

            def body(t, before, rows=rows, thr_r=thr_r, n_tie_r=n_tie_r):
                for g in range(n_groups):
                    lanes = slice(g * LANES, (g + 1) * LANES)
                    s = s_ref[t, rows, lanes]
                    tie = s == thr_r
                    pr = jnp.dot(jnp.where(tie, 1.0, 0.0).astype(BF16), prefix_mat,
                                 preferred_element_type=F32)
                    rank = before + pr[:, :LANES]
                    tie_bias = jnp.where(rank <= n_tie_r, 0.0, MASKED)
                    s_ref[t, rows, lanes] = jnp.where(s > thr_r, 0.0, jnp.where(tie, tie_bias, MASKED))
                    before = before + pr[:, LANES:]
                return before

            lax.fori_loop(0, n_tiles, body, jnp.zeros((rb, LANES), F32))
        return 0

    def plain_bias():
        for rows in row_blocks:
            thr_r = thr[rows]

            def body(t, c, rows=rows, thr_r=thr_r):
                for g in range(n_groups):
                    lanes = slice(g * LANES, (g + 1) * LANES)
                    s_ref[t, rows, lanes] = jnp.where(s_ref[t, rows, lanes] >= thr_r, 0.0, MASKED)
                return c

            lax.fori_loop(0, n_tiles, body, 0)
        return 0

    lax.cond(has_excess, ranked_bias, plain_bias)

    q_h = head_views(q_ref[...])
    for g in range(H_B // 2):
        lanes = slice(g * LANES, (g + 1) * LANES)
        mx_ref[...] = jnp.full(mx_ref.shape, -jnp.inf, F32)
        acc_ref[...] = jnp.zeros(acc_ref.shape, F32)

        def score_tile(t, c, g=g, lanes=lanes):
            start = pl.multiple_of(t * tk, tk)
            k = k_ref[pl.ds(start, tk), lanes]
            bias = s_ref[t]
            for i in range(2):
                s = _dot_nt(q_h[2 * g + i], k) + bias
                sc_ref[i, t] = s
                mx_ref[i] = _lane_max(mx_ref[i], s)
            return c

        lax.fori_loop(0, n_tiles, score_tile, 0)
        for i in range(2):
            mx_ref[i] = jnp.broadcast_to(jnp.max(mx_ref[i], axis=1, keepdims=True), (qb, LANES))

        def value_tile(t, c, lanes=lanes):
            start = pl.multiple_of(t * tk, tk)
            v = v_ref[pl.ds(start, tk), lanes]
            for i in range(2):
                acc_ref[i] = acc_ref[i] + _softmax_value_tile(sc_ref, (i, t), mx_ref[i], v)
            return c

        lax.fori_loop(0, n_tiles, value_tile, 0)
        even = acc_ref[0, :, :LANES] / acc_ref[0, :, LANES:]
        odd = acc_ref[1, :, :LANES] / acc_ref[1, :, LANES:]
        o_ref[:, lanes] = jnp.where(low_half, even, odd).astype(BF16)


def _dsa(qi, kw, q, ki, k, v, *, past, lk, tk):
    b, l, _ = q.shape
    lkp = k.shape[1]
    qb = min(l, Q_BLOCK)
    assert l % qb == 0 and lkp % tk == 0 and tk % MXU_DIM == 0 and qb % CHUNK == 0
    kern = functools.partial(_dsa_kernel, past=past, lk=lk, qb=qb, tk=tk,
                             topk=min(TOPK, lk // 4), max_steps=lkp)
    return pl.pallas_call(
        kern,
        grid=(b, l // qb),
        in_specs=[pl.BlockSpec((None, qb, H_IDX * D_IDX), lambda bi, j: (bi, j, 0)),
                  pl.BlockSpec((None, qb, LANES), lambda bi, j: (bi, j, 0)),
                  pl.BlockSpec((None, qb, W_B), lambda bi, j: (bi, j, 0)),
                  pl.BlockSpec((None, lkp, LANES), lambda bi, j: (bi, 0, 0)),
                  pl.BlockSpec((None, lkp, W_B), lambda bi, j: (bi, 0, 0)),
                  pl.BlockSpec((None, lkp, W_B), lambda bi, j: (bi, 0, 0))],
        out_specs=pl.BlockSpec((None, qb, W_B), lambda bi, j: (bi, j, 0)),
        out_shape=jax.ShapeDtypeStruct((b, l, W_B), BF16),
        scratch_shapes=[pltpu.VMEM((lkp // tk, qb, tk), F32),
                        pltpu.VMEM((2, lkp // tk, qb, tk), F32),
                        pltpu.VMEM((qb, LANES), F32), pltpu.VMEM((qb, LANES), F32),
                        pltpu.VMEM((2, qb, LANES), F32), pltpu.VMEM((2, qb, 2 * LANES), F32)],
        compiler_params=_cparams(("parallel", "parallel")),
        name="dsa",
    )(qi, kw, q, ki, k, v)


def _shift_rows(x, prev, k):
    rolled = pltpu.roll(x, k, 0)
    row = lax.broadcasted_iota(jnp.int32, (SUBLANES, 1), 0)
    top = jnp.where(row < k, pltpu.roll(prev, k, 0), rolled[0:SUBLANES])
    if x.shape[0] == SUBLANES:
        return top
    return jnp.concatenate([top, rolled[SUBLANES:]], axis=0)


def _rglru_kernel(xc_ref, gc_ref, cst_ref, h0_ref, cw_ref, cb_ref, wg_ref, bg_ref, lam_ref,
                  oc_ref, hl_ref, cn_ref, prev_ref, h_ref, a_ref, b_ref, hs_ref, *, past, tl):
    i = pl.program_id(1)

    @pl.when(i == 0)
    def _():
        prev_ref[...] = jnp.zeros(prev_ref.shape, F32)
        prev_ref[SUBLANES - (CONV_C - 1):SUBLANES, :] = cst_ref[...]
        h_ref[...] = h0_ref[...]

    x = xc_ref[...]
    prev = prev_ref[...]
    cw = cw_ref[...]
    xconv = cw[CONV_C - 1:CONV_C] * x + cb_ref[...]
    for k in range(1, CONV_C):
        xconv = xconv + cw[CONV_C - 1 - k:CONV_C - k] * _shift_rows(x, prev, k)
    prev_ref[...] = x[tl - SUBLANES:tl]

    pre = jnp.dot(xconv.astype(BF16), wg_ref[...], preferred_element_type=F32) + bg_ref[...]
    r = jax.nn.sigmoid(pre[:, :W_C])
    gate_i = jax.nn.sigmoid(pre[:, W_C:])
    neg_lam = -lam_ref[...]
    softplus = jnp.maximum(neg_lam, 0.0) + jnp.log1p(jnp.exp(-jnp.abs(neg_lam)))
    log_a = -RG_C * r * softplus
    pos = past + i * tl + lax.broadcasted_iota(jnp.int32, (tl, 1), 0)
    th = jnp.tanh(log_a)
    mult = jnp.where(pos == 0, 1.0, jnp.sqrt(-2.0 * th / (1.0 - th)))
    a_ref[...] = jnp.exp(log_a)
    b_ref[...] = mult * gate_i * xconv

    def step(t, h):
        h = a_ref[pl.ds(t, 1), :] * h + b_ref[pl.ds(t, 1), :]
        hs_ref[pl.ds(t, 1), :] = h
        return h

    h_last = lax.fori_loop(0, tl, step, h_ref[...], unroll=8)
    h_ref[...] = h_last

    gc = gc_ref[...]
    gelu = 0.5 * gc * (1.0 + jnp.tanh(math.sqrt(2.0 / math.pi) * (gc + 0.044715 * (gc * gc * gc))))
    oc_ref[...] = (hs_ref[...] * gelu).astype(BF16)

    @pl.when(i == pl.num_programs(1) - 1)
    def _():
        hl_ref[...] = h_last
        cn_ref[...] = x[tl - (CONV_C - 1):tl]


def _rglru(xc, gc, conv_state, h0, conv_w, conv_b, w_gate, b_gate, lam, *, past):
    b, l, _ = xc.shape
    tl = min(l, 512)
    assert l >= SUBLANES and l % tl == 0
    kern = functools.partial(_rglru_kernel, past=past, tl=tl)
    const = lambda bi, i: (0, 0)
    return pl.pallas_call(
        kern,
        grid=(b, l // tl),
        in_specs=[pl.BlockSpec((None, tl, W_C), lambda bi, i: (bi, i, 0)),
                  pl.BlockSpec((None, tl, W_C), lambda bi, i: (bi, i, 0)),
                  pl.BlockSpec((None, CONV_C - 1, W_C), lambda bi, i: (bi, 0, 0)),
                  pl.BlockSpec((None, 1, W_C), lambda bi, i: (bi, 0, 0)),
                  pl.BlockSpec((CONV_C, W_C), const),
                  pl.BlockSpec((1, W_C), const),
                  pl.BlockSpec((W_C, 2 * W_C), const),
                  pl.BlockSpec((1, 2 * W_C), const),
                  pl.BlockSpec((1, W_C), const)],
        out_specs=[pl.BlockSpec((None, tl, W_C), lambda bi, i: (bi, i, 0)),
                   pl.BlockSpec((None, 1, W_C), lambda bi, i: (bi, 0, 0)),
                   pl.BlockSpec((None, CONV_C - 1, W_C), lambda bi, i: (bi, 0, 0))],
        out_shape=[jax.ShapeDtypeStruct((b, l, W_C), BF16),
                   jax.ShapeDtypeStruct((b, 1, W_C), F32),
                   jax.ShapeDtypeStruct((b, CONV_C - 1, W_C), F32)],
        scratch_shapes=[pltpu.VMEM((SUBLANES, W_C), F32), pltpu.VMEM((1, W_C), F32),
                        pltpu.VMEM((tl, W_C), F32), pltpu.VMEM((tl, W_C), F32),
                        pltpu.VMEM((tl, W_C), F32)],
        compiler_params=_cparams(("parallel", "arbitrary")),
        name="rglru",
    )(xc, gc, conv_state, h0.reshape(b, 1, W_C), conv_w, conv_b.reshape(1, W_C), w_gate,
      b_gate, lam.reshape(1, W_C))


def _outproj_kernel(x_ref, oa_ref, ob_ref, oc_ref, w_ref, g_ref, x1_ref, hn_ref):
    mix = jnp.dot(oa_ref[...], w_ref[0:W_A, :], preferred_element_type=F32)
    mix = mix + jnp.dot(ob_ref[...], w_ref[W_A:W_A + W_B, :], preferred_element_type=F32)
    mix = mix + jnp.dot(oc_ref[...], w_ref[W_A + W_B:, :], preferred_element_type=F32)
    x1 = x_ref[...] + mix
    x1_ref[...] = x1
    hn_ref[...] = _rms(x1, g_ref[...]).astype(BF16)


def _outproj(x2d, oa, ob, oc, w_out, gain):
    t = x2d.shape[0]
    tm = min(512, t)
    row = lambda w: pl.BlockSpec((tm, w), lambda i: (i, 0))
    return pl.pallas_call(
        _outproj_kernel,
        grid=(t // tm,),
        in_specs=[row(D_MODEL), row(W_A), row(W_B), row(W_C),
                  pl.BlockSpec((D_MODEL, D_MODEL), lambda i: (0, 0)),
                  pl.BlockSpec((1, D_MODEL), lambda i: (0, 0))],
        out_specs=[row(D_MODEL), row(D_MODEL)],
        out_shape=[jax.ShapeDtypeStruct((t, D_MODEL), F32), jax.ShapeDtypeStruct((t, D_MODEL), BF16)],
        compiler_params=_cparams(("parallel",)),
        name="outproj",
    )(x2d, oa, ob, oc, w_out, gain.reshape(1, D_MODEL))


def _ffn_kernel(hn_ref, x1_ref, wu_ref, wg_ref, cwu_ref, cwg_ref, cbu_ref, cbg_ref, wd_ref,
                su_ref, sg_ref, gfin_ref, y_ref, fu_ref, fg_ref,
                acc_ref, au_ref, ag_ref, cu_ref, cg_ref, *, tm, final_norm):
    i = pl.program_id(1)
    s = pl.program_id(2)
    nf = pl.num_programs(2) - 1
    live = s > 0
    fb = jnp.maximum(s - 1, 0)
    row = jnp.where(live, fb, nf)

    @pl.when(s == 0)
    def _():
        au_ref[1] = jnp.zeros(au_ref.shape[1:], F32)
        ag_ref[1] = jnp.zeros(ag_ref.shape[1:], F32)
        cu_ref[nf] = jnp.zeros(cu_ref.shape[1:], F32)
        cg_ref[nf] = jnp.zeros(cg_ref.shape[1:], F32)
        acc_ref[...] = x1_ref[...]

    @pl.when(jnp.logical_and(i == 0, live))
    def _():
        for carry_ref, st_ref in ((cu_ref, su_ref), (cg_ref, sg_ref)):
            carry_ref[fb] = jnp.zeros(carry_ref.shape[1:], F32)
            carry_ref[fb, SUBLANES - (CONV_F - 1):SUBLANES, :] = st_ref[...]

    def step(wslot, rslot):
        au_ref[rslot, 0:SUBLANES, :] = cu_ref[row]
        ag_ref[rslot, 0:SUBLANES, :] = cg_ref[row]
        rc = tm // FFN_ROW_CHUNKS

        def conv(a_ref, cw_ref, cb_ref, r0):
            cw = cw_ref[...]
            y = cw[CONV_F - 1:CONV_F] * a_ref[rslot, SUBLANES + r0:SUBLANES + r0 + rc, :] + cb_ref[...]
            for k in range(1, CONV_F):
                y = y + (cw[CONV_F - 1 - k:CONV_F - k]
                         * a_ref[rslot, SUBLANES - k + r0:SUBLANES - k + r0 + rc, :])
            return y

        for c in range(FFN_ROW_CHUNKS):
            r0 = c * rc
            hn = hn_ref[r0:r0 + rc, :]
            au_ref[wslot, SUBLANES + r0:SUBLANES + r0 + rc, :] = jnp.dot(
                hn, wu_ref[...], preferred_element_type=F32)
            ag_ref[wslot, SUBLANES + r0:SUBLANES + r0 + rc, :] = jnp.dot(
                hn, wg_ref[...], preferred_element_type=F32)
            u = conv(au_ref, cwu_ref, cbu_ref, r0)
            g = conv(ag_ref, cwg_ref, cbg_ref, r0)
            mid = (g * jax.nn.sigmoid(g) * u).astype(BF16)
            contrib = jnp.dot(mid, wd_ref[...], preferred_element_type=F32)
            acc_ref[r0:r0 + rc, :] = acc_ref[r0:r0 + rc, :] + jnp.where(live, contrib, 0.0)

        for a_ref, carry_ref, tail_ref in ((au_ref, cu_ref, fu_ref), (ag_ref, cg_ref, fg_ref)):
            carry_ref[row] = a_ref[rslot, tm:tm + SUBLANES, :]
            tail_ref[row] = a_ref[rslot, tm + SUBLANES - (CONV_F - 1):tm + SUBLANES, :]

    parity = lax.rem(s, 2)

    @pl.when(parity == 0)
    def _():
        step(0, 1)

    @pl.when(parity == 1)
    def _():
        step(1, 0)

    @pl.when(s == nf)
    def _():
        y = acc_ref[...]
        if final_norm:
            y = _rms(y, gfin_ref[...])
        y_ref[...] = y


def _ffn(hn, x1, w_up, conv_w, conv_b, w_down, state, final_gain, *, final_norm):
    b, l, _ = hn.shape
    tm = min(l, 1024)
    tf = 512
    nf = D_FF // tf
    assert l % tm == 0 and tm >= SUBLANES
    kern = functools.partial(_ffn_kernel, tm=tm, final_norm=final_norm)
    conv_b = conv_b.reshape(1, 2 * D_FF)
    up = lambda s: jnp.minimum(s, nf - 1)
    fin = lambda s: jnp.maximum(s - 1, 0)
    tail_spec = pl.BlockSpec((None, nf + 1, CONV_F - 1, tf), lambda bi, i, s: (bi, 0, 0, 0))
    tail_shape = jax.ShapeDtypeStruct((b, nf + 1, CONV_F - 1, tf), F32)
    y, fu, fg = pl.pallas_call(
        kern,
        grid=(b, l // tm, nf + 1),
        in_specs=[pl.BlockSpec((None, tm, D_MODEL), lambda bi, i, s: (bi, i, 0)),
                  pl.BlockSpec((None, tm, D_MODEL), lambda bi, i, s: (bi, i, 0)),
                  pl.BlockSpec((D_MODEL, tf), lambda bi, i, s: (0, up(s))),
                  pl.BlockSpec((D_MODEL, tf), lambda bi, i, s: (0, nf + up(s))),
                  pl.BlockSpec((CONV_F, tf), lambda bi, i, s: (0, fin(s))),
                  pl.BlockSpec((CONV_F, tf), lambda bi, i, s: (0, nf + fin(s))),
                  pl.BlockSpec((1, tf), lambda bi, i, s: (0, fin(s))),
                  pl.BlockSpec((1, tf), lambda bi, i, s: (0, nf + fin(s))),
                  pl.BlockSpec((tf, D_MODEL), lambda bi, i, s: (fin(s), 0)),
                  pl.BlockSpec((None, CONV_F - 1, tf), lambda bi, i, s: (bi, 0, fin(s))),
                  pl.BlockSpec((None, CONV_F - 1, tf), lambda bi, i, s: (bi, 0, nf + fin(s))),
                  pl.BlockSpec((1, D_MODEL), lambda bi, i, s: (0, 0))],
        out_specs=[pl.BlockSpec((None, tm, D_MODEL), lambda bi, i, s: (bi, i, 0)),
                   tail_spec, tail_spec],
        out_shape=[jax.ShapeDtypeStruct((b, l, D_MODEL), F32), tail_shape, tail_shape],
        scratch_shapes=[pltpu.VMEM((tm, D_MODEL), F32),
                        pltpu.VMEM((2, tm + SUBLANES, tf), F32),
                        pltpu.VMEM((2, tm + SUBLANES, tf), F32),
                        pltpu.VMEM((nf + 1, SUBLANES, tf), F32),
                        pltpu.VMEM((nf + 1, SUBLANES, tf), F32)],
        compiler_params=_cparams(("parallel", "arbitrary", "arbitrary")),
        name="ffn",
    )(hn, x1, w_up, w_up, conv_w, conv_w, conv_b, conv_b, w_down, state, state,
      final_gain.reshape(1, D_MODEL))
    flat = lambda a: jnp.swapaxes(a[:, :nf], 1, 2).reshape(b, CONV_F - 1, D_FF)
    return y, jnp.concatenate([flat(fu), flat(fg)], axis=-1)


def _cast_kernel(x_ref, o_ref):
    o_ref[...] = x_ref[...].astype(o_ref.dtype)


def _layer_bf16(w, li):
    _, r, c = w.shape
    tr = 256 if r % 256 == 0 else r
    return pl.pallas_call(
        _cast_kernel,
        grid=(r // tr,),
        in_specs=[pl.BlockSpec((None, tr, c), lambda i: (li, i, 0))],
        out_specs=pl.BlockSpec((tr, c), lambda i: (i, 0)),
        out_shape=jax.ShapeDtypeStruct((r, c), BF16),
        compiler_params=_cparams(("parallel",)),
        name="cast",
    )(w)


def _prep_layer_weights(p, li):
    w_in = _layer_bf16(p["w_in"], li)
    zeros = lambda n: jnp.zeros((D_MODEL, n), w_in.dtype)
    k_idx = w_in[:, 2560:2624]
    w_pad = jnp.concatenate(
        [w_in[:, :2628], zeros(_C_KK - 2628), k_idx, k_idx, w_in[:, 2628:]], axis=1)
    assert w_pad.shape[1] == PROJ_W_PAD

    def block_diag(w):
        out = jnp.zeros((W_C, W_C), w.dtype)
        for n in range(N_GATE_BLOCKS):
            sl = slice(n * GATE_BLOCK, (n + 1) * GATE_BLOCK)
            out = out.at[sl, sl].set(w[n])
        return out

    w_gate = jnp.concatenate([block_diag(p["rg_w_r"][li]), block_diag(p["rg_w_i"][li])], axis=1)
    b_gate = jnp.concatenate([p["rg_b_r"][li].reshape(1, W_C), p["rg_b_i"][li].reshape(1, W_C)], axis=1)
    lam_init = 0.8 - 0.6 * math.exp(-0.3 * li)
    f32 = lambda a: a.astype(F32)
    lam = (jnp.exp(jnp.sum(f32(p["lam_q1"][li]) * f32(p["lam_k1"][li])))
           - jnp.exp(jnp.sum(f32(p["lam_q2"][li]) * f32(p["lam_k2"][li]))) + lam_init)
    return dict(
        norm_mix=p["norm_mix"][li], w_pad=w_pad, lam=jnp.full((1, LANES), lam, F32),
        lam_init=lam_init, diff_gain=p["diff_gain"][li].reshape(1, LANES),
        rg_conv_w=p["rg_conv_w"][li], rg_conv_b=p["rg_conv_b"][li],
        w_gate=w_gate.astype(BF16), b_gate=b_gate, rg_lambda=p["rg_lambda"][li],
        w_out=_layer_bf16(p["w_out"], li), norm_ffn=p["norm_ffn"][li],
        ffn_w_up=_layer_bf16(p["ffn_w_up"], li), ffn_conv_w=p["ffn_conv_w"][li],
        ffn_conv_b=p["ffn_conv_b"][li], ffn_w_down=_layer_bf16(p["ffn_w_down"], li))


def _layer(x, past, attn_cache, w, final_gain, final_norm):
    b, l, _ = x.shape
    _, _, b_k0, b_v0, b_ki0, c_h0, c_cv0, f_cv0 = past
    p_len = 0 if b_k0 is None else b_k0.shape[1]
    lk = p_len + l
    tk = PROMPT_KEY_TILE if p_len == 0 else _round_up(lk, MXU_DIM)
    lkp = _round_up(lk, tk)
    t = b * l

    (qa, qb, qi, ka, va, kb, vb, kw, xc, gc, kab, vab, kbb, vbb, kib) = _proj(
        x.reshape(t, D_MODEL), w["norm_mix"], w["w_pad"])

    def keys(cache, new, dup=False):
        new = new.reshape(b, l, -1)
        parts = []
        if cache is not None:
            c = cache.reshape(b, p_len, -1).astype(BF16)
            parts.append(jnp.concatenate([c, c], axis=-1) if dup else c)
        parts.append(new)
        if lkp > lk:
            parts.append(jnp.zeros((b, lkp - lk, new.shape[-1]), BF16))
        return parts[0] if len(parts) == 1 else jnp.concatenate(parts, axis=1)

    if attn_cache is None:
        k_a, v_a = keys(None, kab), keys(None, vab)
    else:
        k_a, v_a = kab.reshape(b, l, W_A), vab.reshape(b, l, W_A)
    o_a = _diff_attn(qa.reshape(b, l, W_A), k_a, v_a, w["lam"], w["diff_gain"], past=p_len, lk=lk,
                     tk=tk, out_scale=1.0 - w["lam_init"], cache=attn_cache)
    o_b = _dsa(qi.reshape(b, l, -1), kw.reshape(b, l, LANES), qb.reshape(b, l, W_B),
               keys(b_ki0, kib, dup=True), keys(b_k0, kbb), keys(b_v0, vbb),
               past=p_len, lk=lk, tk=tk)
    o_c, h_last, conv_new = _rglru(xc.reshape(b, l, W_C), gc.reshape(b, l, W_C), c_cv0, c_h0,
                                   w["rg_conv_w"], w["rg_conv_b"], w["w_gate"], w["b_gate"],
                                   w["rg_lambda"], past=p_len)
    x1, hn = _outproj(x.reshape(t, D_MODEL), o_a.reshape(t, W_A), o_b.reshape(t, W_B),
                      o_c.reshape(t, W_C), w["w_out"], w["norm_ffn"])
    y, f_buf = _ffn(hn.reshape(b, l, D_MODEL), x1.reshape(b, l, D_MODEL), w["ffn_w_up"],
                    w["ffn_conv_w"], w["ffn_conv_b"], w["ffn_w_down"], f_cv0, final_gain,
                    final_norm=final_norm)
    new = (ka.reshape(b, l, H_A, 2 * HEAD_DIM), va.reshape(b, l, H_A, 2 * HEAD_DIM),
           kb.reshape(b, l, H_B, HEAD_DIM), vb.reshape(b, l, H_B, HEAD_DIM),
           kw[:, :D_IDX].reshape(b, l, D_IDX), h_last.reshape(b, W_C), conv_new, f_buf)
    return y, new


def _trunk(x, past, weights, final_gain):
    states = []
    if past[0] is not None:
        flat_heads = lambda c: c.reshape(c.shape[:3] + (W_A,))
        cache_a = (flat_heads(past[0]), flat_heads(past[1]))
    for li in range(N_LAYERS):
        layer_past = tuple(None if c is None else c[li] for c in past)
        attn_cache = None if past[0] is None else cache_a + (li,)
        x, st = _layer(x, layer_past, attn_cache, weights[li], final_gain,
                       final_norm=(li == N_LAYERS - 1))
        states.append(st)
    return x, states


def _forward(x_prompt, x_sample, caches, params):
    weights = [_prep_layer_weights(params, li) for li in range(N_LAYERS)]
    bp = x_prompt.shape[0]
    dt = x_prompt.dtype
    past_prompt = (None, None, None, None, None,
                   jnp.zeros((N_LAYERS, bp, W_C), dt),
                   jnp.zeros((N_LAYERS, bp, CONV_C - 1, W_C), dt),
                   jnp.zeros((N_LAYERS, bp, CONV_F - 1, 2 * D_FF), dt))
    yp, sp = _trunk(x_prompt, past_prompt, weights, params["norm_final"])
    ys, ss = _trunk(x_sample, caches, weights, params["norm_final"])
    out = [yp, ys]
    for jdx in range(8):
        out.append(jnp.stack([st[jdx] for st in sp], axis=0))
        out.append(jnp.stack([st[jdx] for st in ss], axis=0))
    return tuple(out)


def kernel(x_prompt, x_sample, cache_a_k, cache_a_v, cache_b_k, cache_b_v, cache_b_kidx,
           state_c_h, state_c_conv, state_ffn_conv, norm_mix, w_in, lam_q1, lam_k1, lam_q2,
           lam_k2, diff_gain, rg_conv_w, rg_conv_b, rg_w_r, rg_b_r, rg_w_i, rg_b_i, rg_lambda,
           w_out, norm_ffn, ffn_w_up, ffn_conv_w, ffn_conv_b, ffn_w_down, norm_final):
    params = dict(norm_mix=norm_mix, w_in=w_in, lam_q1=lam_q1, lam_k1=lam_k1, lam_q2=lam_q2,
                  lam_k2=lam_k2, diff_gain=diff_gain, rg_conv_w=rg_conv_w, rg_conv_b=rg_conv_b,
                  rg_w_r=rg_w_r, rg_b_r=rg_b_r, rg_w_i=rg_w_i, rg_b_i=rg_b_i, rg_lambda=rg_lambda,
                  w_out=w_out, norm_ffn=norm_ffn, ffn_w_up=ffn_w_up, ffn_conv_w=ffn_conv_w,
                  ffn_conv_b=ffn_conv_b, ffn_w_down=ffn_w_down, norm_final=norm_final)
    caches = (cache_a_k, cache_a_v, cache_b_k, cache_b_v, cache_b_kidx,
              state_c_h, state_c_conv, state_ffn_conv)
    return _forward(x_prompt, x_sample, caches, params)
```

```python
import functools
import math

import jax
import jax.numpy as jnp
from jax import lax
from jax.experimental import pallas as pl
from jax.experimental.pallas import tpu as pltpu

F32 = jnp.float32
BF16 = jnp.bfloat16

D_MODEL = 1024
N_LAYERS = 2
CHUNK = 64
CHUNK_SHIFT = 6
HEAD_DIM = 64
H_A = 4
W_A = H_A * 2 * HEAD_DIM
H_B = 4
W_B = H_B * HEAD_DIM
H_IDX = 4
D_IDX = 64
TOPK = 256
W_C = 256
N_GATE_BLOCKS = 4
GATE_BLOCK = W_C // N_GATE_BLOCKS
RG_C = 8.0
CONV_C = 4
D_FF = 3072
CONV_F = 3
EPS = 1e-6

LANES = 128
SUBLANES = 8
MXU_DIM = 256
Q_BLOCK = 256
SCAN_ROWS = 128
PROMPT_KEY_TILE = 1024
VMEM_LIMIT = 58 * 2**20
MASKED = -1e30
N_BISECT = 16
FFN_ROW_CHUNKS = 4
Q_SCALE = HEAD_DIM ** -0.5 * math.log2(math.e)

PROJ_W_PAD = 3328
_C_QA, _C_KA, _C_VA = 0, 512, 1024
_C_QB, _C_KB, _C_VB = 1536, 1792, 2048
_C_QI = 2304
_C_KW = 2560
_C_KK = 2688
_C_XC = 2816
_C_GC = 3072


def _cparams(sem):
    return pltpu.CompilerParams(dimension_semantics=sem, vmem_limit_bytes=VMEM_LIMIT)


def _rms(x, g):
    return x * lax.rsqrt(jnp.mean(x * x, axis=-1, keepdims=True) + EPS) * g


def _dot_nt(a, b):
    return lax.dot_general(a, b, (((1,), (1,)), ((), ())), preferred_element_type=F32)


def _round_up(n, m):
    return (n + m - 1) // m * m


def _proj_kernel(x_ref, g_ref, w_ref, qa_ref, qb_ref, qi_ref, ka_ref, va_ref, kb_ref, vb_ref,
                 kw_ref, xc_ref, gc_ref, kab_ref, vab_ref, kbb_ref, vbb_ref, kib_ref):
    h = _rms(x_ref[...], g_ref[...])
    z = jnp.dot(h.astype(BF16), w_ref[...], preferred_element_type=F32)
    qa_ref[...] = (z[:, _C_QA:_C_QA + W_A] * Q_SCALE).astype(BF16)
    qb_ref[...] = (z[:, _C_QB:_C_QB + W_B] * Q_SCALE).astype(BF16)
    qi_ref[...] = z[:, _C_QI:_C_QI + H_IDX * D_IDX].astype(BF16)
    ka = z[:, _C_KA:_C_KA + W_A]
    va = z[:, _C_VA:_C_VA + W_A]
    kb = z[:, _C_KB:_C_KB + W_B]
    vb = z[:, _C_VB:_C_VB + W_B]
    ka_ref[...] = ka
    va_ref[...] = va
    kb_ref[...] = kb
    vb_ref[...] = vb
    kab_ref[...] = ka.astype(BF16)
    vab_ref[...] = va.astype(BF16)
    kbb_ref[...] = kb.astype(BF16)
    vbb_ref[...] = vb.astype(BF16)
    kw_ref[...] = z[:, _C_KW:_C_KW + LANES]
    kib_ref[...] = z[:, _C_KK:_C_KK + LANES].astype(BF16)
    xc_ref[...] = z[:, _C_XC:_C_XC + W_C]
    gc_ref[...] = z[:, _C_GC:_C_GC + W_C]


def _proj(x2d, gain, w_pad):
    t = x2d.shape[0]
    tm = min(512, t)
    widths = [(W_A, BF16), (W_B, BF16), (H_IDX * D_IDX, BF16),
              (W_A, F32), (W_A, F32), (W_B, F32), (W_B, F32),
              (LANES, F32), (W_C, F32), (W_C, F32),
              (W_A, BF16), (W_A, BF16), (W_B, BF16), (W_B, BF16), (LANES, BF16)]
    return pl.pallas_call(
        _proj_kernel,
        grid=(t // tm,),
        in_specs=[pl.BlockSpec((tm, D_MODEL), lambda i: (i, 0)),
                  pl.BlockSpec((1, D_MODEL), lambda i: (0, 0)),
                  pl.BlockSpec((D_MODEL, PROJ_W_PAD), lambda i: (0, 0))],
        out_specs=[pl.BlockSpec((tm, w), lambda i: (i, 0)) for w, _ in widths],
        out_shape=[jax.ShapeDtypeStruct((t, w), d) for w, d in widths],
        compiler_params=_cparams(("parallel",)),
        name="proj",
    )(x2d, gain.reshape(1, D_MODEL), w_pad)


def _tile_bounds(q_start, qb, lk, tk):
    n_full = lax.div(jnp.minimum(q_start + CHUNK, lk), tk)
    n_tiles = lax.div(q_start + qb + tk - 1, tk)
    return n_full, n_tiles


def _admissible(start, tk, q_chunk, lk):
    col = start + lax.broadcasted_iota(jnp.int32, (1, tk), 1)
    return (lax.shift_right_logical(col, CHUNK_SHIFT) <= q_chunk) & (col < lk)


def _lane_max(acc, x):
    for g in range(x.shape[1] // LANES):
        acc = jnp.maximum(acc, x[:, g * LANES:(g + 1) * LANES])
    return acc


def _softmax_value_tile(s_ref, idx, m, v):
    parts = []
    for g in range(s_ref.shape[-1] // LANES):
        s = s_ref[idx + (slice(None), slice(g * LANES, (g + 1) * LANES))]
        parts.append(jnp.exp2((s - m).astype(BF16)))
    v_ones = jnp.concatenate([v, jnp.ones_like(v)], axis=1)
    return jnp.dot(jnp.concatenate(parts, axis=1), v_ones, preferred_element_type=F32)


def _diff_attn_kernel(*refs, past, lk, qb, tk, out_scale, cached):
    if cached:
        (lam_ref, gain_ref, q_ref, kn_ref, vn_ref, kc_ref, vc_ref, o_ref,
         s_ref, mx_ref, acc_ref, k_ref, v_ref) = refs
        n_cache, n_new = kc_ref.shape[0], kn_ref.shape[0]
        for cache_ref, new_ref, dst_ref in ((kc_ref, kn_ref, k_ref), (vc_ref, vn_ref, v_ref)):
            dst_ref[0:n_cache, :] = cache_ref[...].astype(BF16)
            dst_ref[n_cache:n_cache + n_new, :] = new_ref[...]
            if tk > n_cache + n_new:
                dst_ref[n_cache + n_new:, :] = jnp.zeros((tk - n_cache - n_new, LANES), BF16)
    else:
        lam_ref, gain_ref, q_ref, k_ref, v_ref, o_ref, s_ref, mx_ref, acc_ref = refs
    j = pl.program_id(2)
    q = q_ref[...]
    lane = lax.broadcasted_iota(jnp.int32, (1, LANES), 1)
    zero = jnp.zeros_like(q)
    q_half = (jnp.where(lane < HEAD_DIM, q, zero), jnp.where(lane >= HEAD_DIM, q, zero))
    q_start = past + j * qb
    row = lax.broadcasted_iota(jnp.int32, (qb, 1), 0)
    q_chunk = lax.shift_right_logical(q_start + row, CHUNK_SHIFT)
    n_full, n_tiles = _tile_bounds(q_start, qb, lk, tk)

    mx_ref[...] = jnp.full(mx_ref.shape, -jnp.inf, F32)
    acc_ref[...] = jnp.zeros(acc_ref.shape, F32)

    def score_tile(t, masked):
        start = pl.multiple_of(t * tk, tk)
        k = k_ref[pl.ds(start, tk), :]
        if masked:
            ok = _admissible(start, tk, q_chunk, lk)
        for i in range(2):
            s = _dot_nt(q_half[i], k)
            if masked:
                s = jnp.where(ok, s, -jnp.inf)
            s_ref[i, t] = s
            mx_ref[i] = _lane_max(mx_ref[i], s)

    def full_body(t, c):
        score_tile(t, False)
        return c

    def masked_body(t, c):
        score_tile(t, True)
        return c

    lax.fori_loop(0, n_full, full_body, 0)
    lax.fori_loop(n_full, n_tiles, masked_body, 0)

    for i in range(2):
        mx_ref[i] = jnp.broadcast_to(jnp.max(mx_ref[i], axis=1, keepdims=True), (qb, LANES))

    def value_tile(t, c):
        start = pl.multiple_of(t * tk, tk)
        v = v_ref[pl.ds(start, tk), :]
        for i in range(2):
            acc_ref[i] = acc_ref[i] + _softmax_value_tile(s_ref, (i, t), mx_ref[i], v)
        return c

    lax.fori_loop(0, n_tiles, value_tile, 0)

    o = (acc_ref[0, :, :LANES] / acc_ref[0, :, LANES:]
         - lam_ref[...] * (acc_ref[1, :, :LANES] / acc_ref[1, :, LANES:]))
    o_ref[...] = (_rms(o, gain_ref[...]) * out_scale).astype(BF16)


def _diff_attn(q, k, v, lam, gain, *, past, lk, tk, out_scale, cache=None):
    b, l, _ = q.shape
    qb = min(l, Q_BLOCK)
    cached = cache is not None
    lkp = tk if cached else k.shape[1]
    assert l % qb == 0 and lkp % tk == 0 and qb % CHUNK == 0
    kern = functools.partial(_diff_attn_kernel, past=past, lk=lk, qb=qb, tk=tk,
                             out_scale=out_scale, cached=cached)
    in_specs = [pl.BlockSpec((1, LANES), lambda bi, h, j: (0, 0)),
                pl.BlockSpec((1, LANES), lambda bi, h, j: (0, 0)),
                pl.BlockSpec((None, qb, LANES), lambda bi, h, j: (bi, j, h)),
                pl.BlockSpec((None, k.shape[1], LANES), lambda bi, h, j: (bi, 0, h)),
                pl.BlockSpec((None, k.shape[1], LANES), lambda bi, h, j: (bi, 0, h))]
    scratch = [pltpu.VMEM((2, lkp // tk, qb, tk), F32),
               pltpu.VMEM((2, qb, LANES), F32), pltpu.VMEM((2, qb, 2 * LANES), F32)]
    operands = [lam, gain, q, k, v]
    if cached:
        cache_k, cache_v, li = cache
        assert l == qb and past + l <= tk and cache_k.shape[2:] == (past, W_A)
        head_rows = pl.BlockSpec((None, None, past, LANES), lambda bi, h, j: (li, bi, 0, h))
        in_specs += [head_rows, head_rows]
        scratch += [pltpu.VMEM((tk, LANES), BF16), pltpu.VMEM((tk, LANES), BF16)]
        operands += [cache_k, cache_v]
    return pl.pallas_call(
        kern,
        grid=(b, H_A, l // qb),
        in_specs=in_specs,
        out_specs=pl.BlockSpec((None, qb, LANES), lambda bi, h, j: (bi, j, h)),
        out_shape=jax.ShapeDtypeStruct((b, l, W_A), BF16),
        scratch_shapes=scratch,
        compiler_params=_cparams(("parallel", "parallel", "parallel")),
        name="diff_attn",
    )(*operands)


def _dsa_kernel(qi_ref, kw_ref, q_ref, ki_ref, k_ref, v_ref, o_ref,
                s_ref, sc_ref, lo_ref, hi_ref, mx_ref, acc_ref,
                *, past, lk, qb, tk, topk, max_steps):
    j = pl.program_id(1)
    q_start = past + j * qb
    n_full, n_tiles = _tile_bounds(q_start, qb, lk, tk)
    n_groups = tk // LANES
    row = lax.broadcasted_iota(jnp.int32, (qb, 1), 0)
    q_chunk = lax.shift_right_logical(q_start + row, CHUNK_SHIFT)
    lane = lax.broadcasted_iota(jnp.int32, (1, LANES), 1)
    low_half = lane < HEAD_DIM
    k_sel = float(topk)

    def wide(x):
        return jnp.broadcast_to(x, (qb, LANES))

    def head_views(x):
        views = []
        for h in range(4):
            pair = x[:, (h // 2) * LANES:(h // 2 + 1) * LANES]
            keep = low_half if h % 2 == 0 else jnp.logical_not(low_half)
            views.append(jnp.where(keep, pair, jnp.zeros_like(pair)))
        return views

    qi_h = head_views(qi_ref[...])
    kw = kw_ref[...]
    w_h = [wide(kw[:, D_IDX + h:D_IDX + h + 1]) for h in range(H_IDX)]
    lo_ref[...] = jnp.full(lo_ref.shape, -jnp.inf, F32)
    hi_ref[...] = jnp.full(hi_ref.shape, -jnp.inf, F32)

    def index_tile(t, masked):
        start = pl.multiple_of(t * tk, tk)
        if masked:
            adm = _admissible(start, tk, q_chunk, lk)
        top2 = lo_ref[...]
        top1 = hi_ref[...]
        for c in range(tk // MXU_DIM):
            ki = ki_ref[pl.ds(start + c * MXU_DIM, MXU_DIM), :]
            rel = [jnp.maximum(_dot_nt(qi_h[h], ki), 0.0) for h in range(H_IDX)]
            for g in range(MXU_DIM // LANES):
                sl = slice(g * LANES, (g + 1) * LANES)
                sc = w_h[0] * rel[0][:, sl]
                for h in range(1, H_IDX):
                    sc = sc + w_h[h] * rel[h][:, sl]
                csl = slice(c * MXU_DIM + g * LANES, c * MXU_DIM + (g + 1) * LANES)
                if masked:
                    sc = jnp.where(adm[:, csl], sc, -jnp.inf)
                s_ref[t, :, csl] = sc
                top2 = jnp.maximum(top2, jnp.minimum(top1, sc))
                top1 = jnp.maximum(top1, sc)
        lo_ref[...] = top2
        hi_ref[...] = top1

    def index_full(t, c):
        index_tile(t, False)
        return c

    def index_masked(t, c):
        index_tile(t, True)
        return c

    lax.fori_loop(0, n_full, index_full, 0)
    lax.fori_loop(n_full, n_tiles, index_masked, 0)
    rb = min(qb, SCAN_ROWS)
    ones_mat = jnp.ones((LANES, LANES), BF16)
    tri_i = lax.broadcasted_iota(jnp.int32, (LANES, 2 * LANES), 0)
    tri_j = lax.broadcasted_iota(jnp.int32, (LANES, 2 * LANES), 1)
    prefix_mat = jnp.where(jnp.logical_or(tri_i <= tri_j, tri_j >= LANES), 1.0, 0.0).astype(BF16)
    assert s_ref.shape[0] * n_groups <= 256

    row_blocks = [slice(r * rb, (r + 1) * rb) for r in range(qb // rb)]
    pos = q_start + lax.broadcasted_iota(jnp.int32, (qb, LANES), 0)
    n_adm = jnp.minimum((lax.shift_right_logical(pos, CHUNK_SHIFT) + 1) * CHUNK, lk)
    few = n_adm <= topk

    def lanes_all(x, reduce):
        return jnp.broadcast_to(reduce(x, axis=1, keepdims=True), (qb, LANES))

    def row_sum(acc):
        return jnp.dot(acc.astype(BF16), ones_mat, preferred_element_type=F32)

    def scan(step, init, *operands):
        outs = []
        for rows in row_blocks:
            ops = [o[rows] for o in operands]

            def body(t, acc, rows=rows, ops=ops):
                for g in range(n_groups):
                    acc = step(acc, s_ref[t, rows, g * LANES:(g + 1) * LANES], *ops)
                return acc

            outs.append(lax.fori_loop(0, n_tiles, body, jnp.full((rb, LANES), init, F32)))
        return outs[0] if len(outs) == 1 else jnp.concatenate(outs, axis=0)

    def count_ge(thr):
        return row_sum(scan(lambda acc, s, t: acc + jnp.where(s >= t, 1.0, 0.0), 0.0, thr))

    def max_below(bound):
        acc = scan(lambda acc, s, b: jnp.maximum(acc, jnp.where(s < b, s, -jnp.inf)), -jnp.inf, bound)
        return lanes_all(acc, jnp.max)

    assert topk <= 2 * LANES
    lane_best = hi_ref[...] if topk <= LANES else lo_ref[...]
    lowest = float(jnp.finfo(jnp.float32).min)
    rmin = jnp.maximum(lanes_all(lane_best, jnp.min), lowest)
    rmax = jnp.maximum(lanes_all(lane_best, jnp.max), lowest)

    c_max = count_ge(rmax)
    top_ties = c_max >= k_sel

    def bisect(_, c):
        lo, hi = c
        mid = 0.5 * lo + 0.5 * hi
        ge = count_ge(mid) >= k_sel
        return jnp.where(ge, mid, lo), jnp.where(ge, hi, mid)

    _, hi = lax.fori_loop(0, N_BISECT, bisect, (rmin, rmax))

    def walk_cond(st):
        it, _, _, _, active = st
        return jnp.logical_and(jnp.max(active) > 0.0, it < max_steps)

    def walk_body(st):
        it, cand, thr, c_thr, active = st
        c = count_ge(cand)
        ok = c >= k_sel
        act = active > 0.0
        hit = jnp.logical_and(act, ok)
        thr = jnp.where(hit, cand, thr)
        c_thr = jnp.where(hit, c, c_thr)
        active = jnp.where(jnp.logical_and(act, jnp.logical_not(ok)), 1.0, 0.0)
        return it + 1, max_below(cand), thr, c_thr, active

    thr0 = jnp.where(few, lowest, jnp.where(top_ties, rmax, lowest))
    c0 = jnp.where(few, k_sel, jnp.where(top_ties, c_max, k_sel))
    active0 = jnp.where(jnp.logical_or(few, top_ties), 0.0, 1.0)
    _, _, thr, c_thr, _ = lax.while_loop(
        walk_cond, walk_body, (jnp.int32(0), max_below(hi), thr0, c0, active0))

    has_excess = jnp.max(jnp.where(c_thr > k_sel, 1.0, 0.0)) > 0.0

    def ranked_bias():
        n_tie = k_sel - row_sum(scan(lambda acc, s, t: acc + jnp.where(s > t, 1.0, 0.0), 0.0, thr))
        for rows in row_blocks:
            thr_r = thr[rows]
            n_tie_r = n_tie[rows]

            def body(t, before, rows=rows, thr_r=thr_r, n_tie_r=n_tie_r):
                for g in range(n_groups):
                    lanes = slice(g * LANES, (g + 1) * LANES)
                    s = s_ref[t, rows, lanes]
                    tie = s == thr_r
                    pr = jnp.dot(jnp.where(tie, 1.0, 0.0).astype(BF16), prefix_mat,
                                 preferred_element_type=F32)
                    rank = before + pr[:, :LANES]
                    tie_bias = jnp.where(rank <= n_tie_r, 0.0, MASKED)
                    s_ref[t, rows, lanes] = jnp.where(s > thr_r, 0.0, jnp.where(tie, tie_bias, MASKED))
                    before = before + pr[:, LANES:]
                return before

            lax.fori_loop(0, n_tiles, body, jnp.zeros((rb, LANES), F32))
        return 0

    def plain_bias():
        for rows in row_blocks:
            thr_r = thr[rows]

            def body(t, c, rows=rows, thr_r=thr_r):
                for g in range(n_groups):
                    lanes = slice(g * LANES, (g + 1) * LANES)
                    s_ref[t, rows, lanes] = jnp.where(s_ref[t, rows, lanes] >= thr_r, 0.0, MASKED)
                return c

            lax.fori_loop(0, n_tiles, body, 0)
        return 0

    lax.cond(has_excess, ranked_bias, plain_bias)

    q_h = head_views(q_ref[...])
    for g in range(H_B // 2):
        lanes = slice(g * LANES, (g + 1) * LANES)
        mx_ref[...] = jnp.full(mx_ref.shape, -jnp.inf, F32)
        acc_ref[...] = jnp.zeros(acc_ref.shape, F32)

        def score_tile(t, c, g=g, lanes=lanes):
            start = pl.multiple_of(t * tk, tk)
            k = k_ref[pl.ds(start, tk), lanes]
            bias = s_ref[t]
            for i in range(2):
                s = _dot_nt(q_h[2 * g + i], k) + bias
                sc_ref[i, t] = s
                mx_ref[i] = _lane_max(mx_ref[i], s)
            return c

        lax.fori_loop(0, n_tiles, score_tile, 0)
        for i in range(2):
            mx_ref[i] = jnp.broadcast_to(jnp.max(mx_ref[i], axis=1, keepdims=True), (qb, LANES))

        def value_tile(t, c, lanes=lanes):
            start = pl.multiple_of(t * tk, tk)
            v = v_ref[pl.ds(start, tk), lanes]
            for i in range(2):
                acc_ref[i] = acc_ref[i] + _softmax_value_tile(sc_ref, (i, t), mx_ref[i], v)
            return c

        lax.fori_loop(0, n_tiles, value_tile, 0)
        even = acc_ref[0, :, :LANES] / acc_ref[0, :, LANES:]
        odd = acc_ref[1, :, :LANES] / acc_ref[1, :, LANES:]
        o_ref[:, lanes] = jnp.where(low_half, even, odd).astype(BF16)


def _dsa(qi, kw, q, ki, k, v, *, past, lk, tk):
    b, l, _ = q.shape
    lkp = k.shape[1]
    qb = min(l, Q_BLOCK)
    assert l % qb == 0 and lkp % tk == 0 and tk % MXU_DIM == 0 and qb % CHUNK == 0
    kern = functools.partial(_dsa_kernel, past=past, lk=lk, qb=qb, tk=tk,
                             topk=min(TOPK, lk // 4), max_steps=lkp)
    return pl.pallas_call(
        kern,
        grid=(b, l // qb),
        in_specs=[pl.BlockSpec((None, qb, H_IDX * D_IDX), lambda bi, j: (bi, j, 0)),
                  pl.BlockSpec((None, qb, LANES), lambda bi, j: (bi, j, 0)),
                  pl.BlockSpec((None, qb, W_B), lambda bi, j: (bi, j, 0)),
                  pl.BlockSpec((None, lkp, LANES), lambda bi, j: (bi, 0, 0)),
                  pl.BlockSpec((None, lkp, W_B), lambda bi, j: (bi, 0, 0)),
                  pl.BlockSpec((None, lkp, W_B), lambda bi, j: (bi, 0, 0))],
        out_specs=pl.BlockSpec((None, qb, W_B), lambda bi, j: (bi, j, 0)),
        out_shape=jax.ShapeDtypeStruct((b, l, W_B), BF16),
        scratch_shapes=[pltpu.VMEM((lkp // tk, qb, tk), F32),
                        pltpu.VMEM((2, lkp // tk, qb, tk), F32),
                        pltpu.VMEM((qb, LANES), F32), pltpu.VMEM((qb, LANES), F32),
                        pltpu.VMEM((2, qb, LANES), F32), pltpu.VMEM((2, qb, 2 * LANES), F32)],
        compiler_params=_cparams(("parallel", "parallel")),
        name="dsa",
    )(qi, kw, q, ki, k, v)


def _shift_rows(x, prev, k):
    rolled = pltpu.roll(x, k, 0)
    row = lax.broadcasted_iota(jnp.int32, (SUBLANES, 1), 0)
    top = jnp.where(row < k, pltpu.roll(prev, k, 0), rolled[0:SUBLANES])
    if x.shape[0] == SUBLANES:
        return top
    return jnp.concatenate([top, rolled[SUBLANES:]], axis=0)


def _rglru_kernel(xc_ref, gc_ref, cst_ref, h0_ref, cw_ref, cb_ref, wg_ref, bg_ref, lam_ref,
                  oc_ref, hl_ref, cn_ref, prev_ref, h_ref, a_ref, b_ref, hs_ref, *, past, tl):
    i = pl.program_id(1)

    @pl.when(i == 0)
    def _():
        prev_ref[...] = jnp.zeros(prev_ref.shape, F32)
        prev_ref[SUBLANES - (CONV_C - 1):SUBLANES, :] = cst_ref[...]
        h_ref[...] = h0_ref[...]

    x = xc_ref[...]
    prev = prev_ref[...]
    cw = cw_ref[...]
    xconv = cw[CONV_C - 1:CONV_C] * x + cb_ref[...]
    for k in range(1, CONV_C):
        xconv = xconv + cw[CONV_C - 1 - k:CONV_C - k] * _shift_rows(x, prev, k)
    prev_ref[...] = x[tl - SUBLANES:tl]

    pre = jnp.dot(xconv.astype(BF16), wg_ref[...], preferred_element_type=F32) + bg_ref[...]
    r = jax.nn.sigmoid(pre[:, :W_C])
    gate_i = jax.nn.sigmoid(pre[:, W_C:])
    neg_lam = -lam_ref[...]
    softplus = jnp.maximum(neg_lam, 0.0) + jnp.log1p(jnp.exp(-jnp.abs(neg_lam)))
    log_a = -RG_C * r * softplus
    pos = past + i * tl + lax.broadcasted_iota(jnp.int32, (tl, 1), 0)
    th = jnp.tanh(log_a)
    mult = jnp.where(pos == 0, 1.0, jnp.sqrt(-2.0 * th / (1.0 - th)))
    a_ref[...] = jnp.exp(log_a)
    b_ref[...] = mult * gate_i * xconv

    def step(t, h):
        h = a_ref[pl.ds(t, 1), :] * h + b_ref[pl.ds(t, 1), :]
        hs_ref[pl.ds(t, 1), :] = h
        return h

    h_last = lax.fori_loop(0, tl, step, h_ref[...], unroll=8)
    h_ref[...] = h_last

    gc = gc_ref[...]
    gelu = 0.5 * gc * (1.0 + jnp.tanh(math.sqrt(2.0 / math.pi) * (gc + 0.044715 * (gc * gc * gc))))
    oc_ref[...] = (hs_ref[...] * gelu).astype(BF16)

    @pl.when(i == pl.num_programs(1) - 1)
    def _():
        hl_ref[...] = h_last
        cn_ref[...] = x[tl - (CONV_C - 1):tl]


def _rglru(xc, gc, conv_state, h0, conv_w, conv_b, w_gate, b_gate, lam, *, past):
    b, l, _ = xc.shape
    tl = min(l, 512)
    assert l >= SUBLANES and l % tl == 0
    kern = functools.partial(_rglru_kernel, past=past, tl=tl)
    const = lambda bi, i: (0, 0)
    return pl.pallas_call(
        kern,
        grid=(b, l // tl),
        in_specs=[pl.BlockSpec((None, tl, W_C), lambda bi, i: (bi, i, 0)),
                  pl.BlockSpec((None, tl, W_C), lambda bi, i: (bi, i, 0)),
                  pl.BlockSpec((None, CONV_C - 1, W_C), lambda bi, i: (bi, 0, 0)),
                  pl.BlockSpec((None, 1, W_C), lambda bi, i: (bi, 0, 0)),
                  pl.BlockSpec((CONV_C, W_C), const),
                  pl.BlockSpec((1, W_C), const),
                  pl.BlockSpec((W_C, 2 * W_C), const),
                  pl.BlockSpec((1, 2 * W_C), const),
                  pl.BlockSpec((1, W_C), const)],
        out_specs=[pl.BlockSpec((None, tl, W_C), lambda bi, i: (bi, i, 0)),
                   pl.BlockSpec((None, 1, W_C), lambda bi, i: (bi, 0, 0)),
                   pl.BlockSpec((None, CONV_C - 1, W_C), lambda bi, i: (bi, 0, 0))],
        out_shape=[jax.ShapeDtypeStruct((b, l, W_C), BF16),
                   jax.ShapeDtypeStruct((b, 1, W_C), F32),
                   jax.ShapeDtypeStruct((b, CONV_C - 1, W_C), F32)],
        scratch_shapes=[pltpu.VMEM((SUBLANES, W_C), F32), pltpu.VMEM((1, W_C), F32),
                        pltpu.VMEM((tl, W_C), F32), pltpu.VMEM((tl, W_C), F32),
                        pltpu.VMEM((tl, W_C), F32)],
        compiler_params=_cparams(("parallel", "arbitrary")),
        name="rglru",
    )(xc, gc, conv_state, h0.reshape(b, 1, W_C), conv_w, conv_b.reshape(1, W_C), w_gate,
      b_gate, lam.reshape(1, W_C))


def _outproj_kernel(x_ref, oa_ref, ob_ref, oc_ref, w_ref, g_ref, x1_ref, hn_ref):
    mix = jnp.dot(oa_ref[...], w_ref[0:W_A, :], preferred_element_type=F32)
    mix = mix + jnp.dot(ob_ref[...], w_ref[W_A:W_A + W_B, :], preferred_element_type=F32)
    mix = mix + jnp.dot(oc_ref[...], w_ref[W_A + W_B:, :], preferred_element_type=F32)
    x1 = x_ref[...] + mix
    x1_ref[...] = x1
    hn_ref[...] = _rms(x1, g_ref[...]).astype(BF16)


def _outproj(x2d, oa, ob, oc, w_out, gain):
    t = x2d.shape[0]
    tm = min(512, t)
    row = lambda w: pl.BlockSpec((tm, w), lambda i: (i, 0))
    return pl.pallas_call(
        _outproj_kernel,
        grid=(t // tm,),
        in_specs=[row(D_MODEL), row(W_A), row(W_B), row(W_C),
                  pl.BlockSpec((D_MODEL, D_MODEL), lambda i: (0, 0)),
                  pl.BlockSpec((1, D_MODEL), lambda i: (0, 0))],
        out_specs=[row(D_MODEL), row(D_MODEL)],
        out_shape=[jax.ShapeDtypeStruct((t, D_MODEL), F32), jax.ShapeDtypeStruct((t, D_MODEL), BF16)],
        compiler_params=_cparams(("parallel",)),
        name="outproj",
    )(x2d, oa, ob, oc, w_out, gain.reshape(1, D_MODEL))


def _ffn_kernel(hn_ref, x1_ref, wu_ref, wg_ref, cwu_ref, cwg_ref, cbu_ref, cbg_ref, wd_ref,
                su_ref, sg_ref, gfin_ref, y_ref, fu_ref, fg_ref,
                acc_ref, au_ref, ag_ref, cu_ref, cg_ref, *, tm, final_norm):
    i = pl.program_id(1)
    s = pl.program_id(2)
    nf = pl.num_programs(2) - 1
    live = s > 0
    fb = jnp.maximum(s - 1, 0)
    row = jnp.where(live, fb, nf)

    @pl.when(s == 0)
    def _():
        au_ref[1] = jnp.zeros(au_ref.shape[1:], F32)
        ag_ref[1] = jnp.zeros(ag_ref.shape[1:], F32)
        cu_ref[nf] = jnp.zeros(cu_ref.shape[1:], F32)
        cg_ref[nf] = jnp.zeros(cg_ref.shape[1:], F32)
        acc_ref[...] = x1_ref[...]

    @pl.when(jnp.logical_and(i == 0, live))
    def _():
        for carry_ref, st_ref in ((cu_ref, su_ref), (cg_ref, sg_ref)):
            carry_ref[fb] = jnp.zeros(carry_ref.shape[1:], F32)
            carry_ref[fb, SUBLANES - (CONV_F - 1):SUBLANES, :] = st_ref[...]

    def step(wslot, rslot):
        au_ref[rslot, 0:SUBLANES, :] = cu_ref[row]
        ag_ref[rslot, 0:SUBLANES, :] = cg_ref[row]
        rc = tm // FFN_ROW_CHUNKS

        def conv(a_ref, cw_ref, cb_ref, r0):
            cw = cw_ref[...]
            y = cw[CONV_F - 1:CONV_F] * a_ref[rslot, SUBLANES + r0:SUBLANES + r0 + rc, :] + cb_ref[...]
            for k in range(1, CONV_F):
                y = y + (cw[CONV_F - 1 - k:CONV_F - k]
                         * a_ref[rslot, SUBLANES - k + r0:SUBLANES - k + r0 + rc, :])
            return y

        for c in range(FFN_ROW_CHUNKS):
            r0 = c * rc
            hn = hn_ref[r0:r0 + rc, :]
            au_ref[wslot, SUBLANES + r0:SUBLANES + r0 + rc, :] = jnp.dot(
                hn, wu_ref[...], preferred_element_type=F32)
            ag_ref[wslot, SUBLANES + r0:SUBLANES + r0 + rc, :] = jnp.dot(
                hn, wg_ref[...], preferred_element_type=F32)
            u = conv(au_ref, cwu_ref, cbu_ref, r0)
            g = conv(ag_ref, cwg_ref, cbg_ref, r0)
            mid = (g * jax.nn.sigmoid(g) * u).astype(BF16)
            contrib = jnp.dot(mid, wd_ref[...], preferred_element_type=F32)
            acc_ref[r0:r0 + rc, :] = acc_ref[r0:r0 + rc, :] + jnp.where(live, contrib, 0.0)

        for a_ref, carry_ref, tail_ref in ((au_ref, cu_ref, fu_ref), (ag_ref, cg_ref, fg_ref)):
            carry_ref[row] = a_ref[rslot, tm:tm + SUBLANES, :]
            tail_ref[row] = a_ref[rslot, tm + SUBLANES - (CONV_F - 1):tm + SUBLANES, :]

    parity = lax.rem(s, 2)

    @pl.when(parity == 0)
    def _():
        step(0, 1)

    @pl.when(parity == 1)
    def _():
        step(1, 0)

    @pl.when(s == nf)
    def _():
        y = acc_ref[...]
        if final_norm:
            y = _rms(y, gfin_ref[...])
        y_ref[...] = y


def _ffn(hn, x1, w_up, conv_w, conv_b, w_down, state, final_gain, *, final_norm):
    b, l, _ = hn.shape
    tm = min(l, 1024)
    tf = 512
    nf = D_FF // tf
    assert l % tm == 0 and tm >= SUBLANES
    kern = functools.partial(_ffn_kernel, tm=tm, final_norm=final_norm)
    conv_b = conv_b.reshape(1, 2 * D_FF)
    up = lambda s: jnp.minimum(s, nf - 1)
    fin = lambda s: jnp.maximum(s - 1, 0)
    tail_spec = pl.BlockSpec((None, nf + 1, CONV_F - 1, tf), lambda bi, i, s: (bi, 0, 0, 0))
    tail_shape = jax.ShapeDtypeStruct((b, nf + 1, CONV_F - 1, tf), F32)
    y, fu, fg = pl.pallas_call(
        kern,
        grid=(b, l // tm, nf + 1),
        in_specs=[pl.BlockSpec((None, tm, D_MODEL), lambda bi, i, s: (bi, i, 0)),
                  pl.BlockSpec((None, tm, D_MODEL), lambda bi, i, s: (bi, i, 0)),
                  pl.BlockSpec((D_MODEL, tf), lambda bi, i, s: (0, up(s))),
                  pl.BlockSpec((D_MODEL, tf), lambda bi, i, s: (0, nf + up(s))),
                  pl.BlockSpec((CONV_F, tf), lambda bi, i, s: (0, fin(s))),
                  pl.BlockSpec((CONV_F, tf), lambda bi, i, s: (0, nf + fin(s))),
                  pl.BlockSpec((1, tf), lambda bi, i, s: (0, fin(s))),
                  pl.BlockSpec((1, tf), lambda bi, i, s: (0, nf + fin(s))),
                  pl.BlockSpec((tf, D_MODEL), lambda bi, i, s: (fin(s), 0)),
                  pl.BlockSpec((None, CONV_F - 1, tf), lambda bi, i, s: (bi, 0, fin(s))),
                  pl.BlockSpec((None, CONV_F - 1, tf), lambda bi, i, s: (bi, 0, nf + fin(s))),
                  pl.BlockSpec((1, D_MODEL), lambda bi, i, s: (0, 0))],
        out_specs=[pl.BlockSpec((None, tm, D_MODEL), lambda bi, i, s: (bi, i, 0)),
                   tail_spec, tail_spec],
        out_shape=[jax.ShapeDtypeStruct((b, l, D_MODEL), F32), tail_shape, tail_shape],
        scratch_shapes=[pltpu.VMEM((tm, D_MODEL), F32),
                        pltpu.VMEM((2, tm + SUBLANES, tf), F32),
                        pltpu.VMEM((2, tm + SUBLANES, tf), F32),
                        pltpu.VMEM((nf + 1, SUBLANES, tf), F32),
                        pltpu.VMEM((nf + 1, SUBLANES, tf), F32)],
        compiler_params=_cparams(("parallel", "arbitrary", "arbitrary")),
        name="ffn",
    )(hn, x1, w_up, w_up, conv_w, conv_w, conv_b, conv_b, w_down, state, state,
      final_gain.reshape(1, D_MODEL))
    flat = lambda a: jnp.swapaxes(a[:, :nf], 1, 2).reshape(b, CONV_F - 1, D_FF)
    return y, jnp.concatenate([flat(fu), flat(fg)], axis=-1)


def _cast_kernel(x_ref, o_ref):
    o_ref[...] = x_ref[...].astype(o_ref.dtype)


def _layer_bf16(w, li):
    _, r, c = w.shape
    tr = 256 if r % 256 == 0 else r
    return pl.pallas_call(
        _cast_kernel,
        grid=(r // tr,),
        in_specs=[pl.BlockSpec((None, tr, c), lambda i: (li, i, 0))],
        out_specs=pl.BlockSpec((tr, c), lambda i: (i, 0)),
        out_shape=jax.ShapeDtypeStruct((r, c), BF16),
        compiler_params=_cparams(("parallel",)),
        name="cast",
    )(w)


def _prep_layer_weights(p, li):
    w_in = _layer_bf16(p["w_in"], li)
    zeros = lambda n: jnp.zeros((D_MODEL, n), w_in.dtype)
    k_idx = w_in[:, 2560:2624]
    w_pad = jnp.concatenate(
        [w_in[:, :2628], zeros(_C_KK - 2628), k_idx, k_idx, w_in[:, 2628:]], axis=1)
    assert w_pad.shape[1] == PROJ_W_PAD

    def block_diag(w):
        out = jnp.zeros((W_C, W_C), w.dtype)
        for n in range(N_GATE_BLOCKS):
            sl = slice(n * GATE_BLOCK, (n + 1) * GATE_BLOCK)
            out = out.at[sl, sl].set(w[n])
        return out

    w_gate = jnp.concatenate([block_diag(p["rg_w_r"][li]), block_diag(p["rg_w_i"][li])], axis=1)
    b_gate = jnp.concatenate([p["rg_b_r"][li].reshape(1, W_C), p["rg_b_i"][li].reshape(1, W_C)], axis=1)
    lam_init = 0.8 - 0.6 * math.exp(-0.3 * li)
    f32 = lambda a: a.astype(F32)
    lam = (jnp.exp(jnp.sum(f32(p["lam_q1"][li]) * f32(p["lam_k1"][li])))
           - jnp.exp(jnp.sum(f32(p["lam_q2"][li]) * f32(p["lam_k2"][li]))) + lam_init)
    return dict(
        norm_mix=p["norm_mix"][li], w_pad=w_pad, lam=jnp.full((1, LANES), lam, F32),
        lam_init=lam_init, diff_gain=p["diff_gain"][li].reshape(1, LANES),
        rg_conv_w=p["rg_conv_w"][li], rg_conv_b=p["rg_conv_b"][li],
        w_gate=w_gate.astype(BF16), b_gate=b_gate, rg_lambda=p["rg_lambda"][li],
        w_out=_layer_bf16(p["w_out"], li), norm_ffn=p["norm_ffn"][li],
        ffn_w_up=_layer_bf16(p["ffn_w_up"], li), ffn_conv_w=p["ffn_conv_w"][li],
        ffn_conv_b=p["ffn_conv_b"][li], ffn_w_down=_layer_bf16(p["ffn_w_down"], li))


def _layer(x, past, attn_cache, w, final_gain, final_norm):
    b, l, _ = x.shape
    _, _, b_k0, b_v0, b_ki0, c_h0, c_cv0, f_cv0 = past
    p_len = 0 if b_k0 is None else b_k0.shape[1]
    lk = p_len + l
    tk = PROMPT_KEY_TILE if p_len == 0 else _round_up(lk, MXU_DIM)
    lkp = _round_up(lk, tk)
    t = b * l

    (qa, qb, qi, ka, va, kb, vb, kw, xc, gc, kab, vab, kbb, vbb, kib) = _proj(
        x.reshape(t, D_MODEL), w["norm_mix"], w["w_pad"])

    def keys(cache, new, dup=False):
        new = new.reshape(b, l, -1)
        parts = []
        if cache is not None:
            c = cache.reshape(b, p_len, -1).astype(BF16)
            parts.append(jnp.concatenate([c, c], axis=-1) if dup else c)
        parts.append(new)
        if lkp > lk:
            parts.append(jnp.zeros((b, lkp - lk, new.shape[-1]), BF16))
        return parts[0] if len(parts) == 1 else jnp.concatenate(parts, axis=1)

    if attn_cache is None:
        k_a, v_a = keys(None, kab), keys(None, vab)
    else:
        k_a, v_a = kab.reshape(b, l, W_A), vab.reshape(b, l, W_A)
    o_a = _diff_attn(qa.reshape(b, l, W_A), k_a, v_a, w["lam"], w["diff_gain"], past=p_len, lk=lk,
                     tk=tk, out_scale=1.0 - w["lam_init"], cache=attn_cache)
    o_b = _dsa(qi.reshape(b, l, -1), kw.reshape(b, l, LANES), qb.reshape(b, l, W_B),
               keys(b_ki0, kib, dup=True), keys(b_k0, kbb), keys(b_v0, vbb),
               past=p_len, lk=lk, tk=tk)
    o_c, h_last, conv_new = _rglru(xc.reshape(b, l, W_C), gc.reshape(b, l, W_C), c_cv0, c_h0,
                                   w["rg_conv_w"], w["rg_conv_b"], w["w_gate"], w["b_gate"],
                                   w["rg_lambda"], past=p_len)
    x1, hn = _outproj(x.reshape(t, D_MODEL), o_a.reshape(t, W_A), o_b.reshape(t, W_B),
                      o_c.reshape(t, W_C), w["w_out"], w["norm_ffn"])
    y, f_buf = _ffn(hn.reshape(b, l, D_MODEL), x1.reshape(b, l, D_MODEL), w["ffn_w_up"],
                    w["ffn_conv_w"], w["ffn_conv_b"], w["ffn_w_down"], f_cv0, final_gain,
                    final_norm=final_norm)
    new = (ka.reshape(b, l, H_A, 2 * HEAD_DIM), va.reshape(b, l, H_A, 2 * HEAD_DIM),
           kb.reshape(b, l, H_B, HEAD_DIM), vb.reshape(b, l, H_B, HEAD_DIM),
           kw[:, :D_IDX].reshape(b, l, D_IDX), h_last.reshape(b, W_C), conv_new, f_buf)
    return y, new


def _trunk(x, past, weights, final_gain):
    states = []
    if past[0] is not None:
        flat_heads = lambda c: c.reshape(c.shape[:3] + (W_A,))
        cache_a = (flat_heads(past[0]), flat_heads(past[1]))
    for li in range(N_LAYERS):
        layer_past = tuple(None if c is None else c[li] for c in past)
        attn_cache = None if past[0] is None else cache_a + (li,)
        x, st = _layer(x, layer_past, attn_cache, weights[li], final_gain,
                       final_norm=(li == N_LAYERS - 1))
        states.append(st)
    return x, states


def _forward(x_prompt, x_sample, caches, params):
    weights = [_prep_layer_weights(params, li) for li in range(N_LAYERS)]
    bp = x_prompt.shape[0]
    dt = x_prompt.dtype
    past_prompt = (None, None, None, None, None,
                   jnp.zeros((N_LAYERS, bp, W_C), dt),
                   jnp.zeros((N_LAYERS, bp, CONV_C - 1, W_C), dt),
                   jnp.zeros((N_LAYERS, bp, CONV_F - 1, 2 * D_FF), dt))
    yp, sp = _trunk(x_prompt, past_prompt, weights, params["norm_final"])
    ys, ss = _trunk(x_sample, caches, weights, params["norm_final"])
    out = [yp, ys]
    for jdx in range(8):
        out.append(jnp.stack([st[jdx] for st in sp], axis=0))
        out.append(jnp.stack([st[jdx] for st in ss], axis=0))
    return tuple(out)


def kernel(x_prompt, x_sample, cache_a_k, cache_a_v, cache_b_k, cache_b_v, cache_b_kidx,
           state_c_h, state_c_conv, state_ffn_conv, norm_mix, w_in, lam_q1, lam_k1, lam_q2,
           lam_k2, diff_gain, rg_conv_w, rg_conv_b, rg_w_r, rg_b_r, rg_w_i, rg_b_i, rg_lambda,
           w_out, norm_ffn, ffn_w_up, ffn_conv_w, ffn_conv_b, ffn_w_down, norm_final):
    params = dict(norm_mix=norm_mix, w_in=w_in, lam_q1=lam_q1, lam_k1=lam_k1, lam_q2=lam_q2,
                  lam_k2=lam_k2, diff_gain=diff_gain, rg_conv_w=rg_conv_w, rg_conv_b=rg_conv_b,
                  rg_w_r=rg_w_r, rg_b_r=rg_b_r, rg_w_i=rg_w_i, rg_b_i=rg_b_i, rg_lambda=rg_lambda,
                  w_out=w_out, norm_ffn=norm_ffn, ffn_w_up=ffn_w_up, ffn_conv_w=ffn_conv_w,
                  ffn_conv_b=ffn_conv_b, ffn_w_down=ffn_w_down, norm_final=norm_final)
    caches = (cache_a_k, cache_a_v, cache_b_k, cache_b_v, cache_b_kidx,
              state_c_h, state_c_conv, state_ffn_conv)
    return _forward(x_prompt, x_sample, caches, params)
```

```python
import functools
import math

import jax
import jax.numpy as jnp
from jax import lax
from jax.experimental import pallas as pl
from jax.experimental.pallas import tpu as pltpu

F32 = jnp.float32
BF16 = jnp.bfloat16

D_MODEL = 1024
N_LAYERS = 2
CHUNK = 64
CHUNK_SHIFT = 6
HEAD_DIM = 64
H_A = 4
W_A = H_A * 2 * HEAD_DIM
H_B = 4
W_B = H_B * HEAD_DIM
H_IDX = 4
D_IDX = 64
TOPK = 256
W_C = 256
N_GATE_BLOCKS = 4
GATE_BLOCK = W_C // N_GATE_BLOCKS
RG_C = 8.0
CONV_C = 4
D_FF = 3072
CONV_F = 3
EPS = 1e-6

LANES = 128
SUBLANES = 8
MXU_DIM = 256
Q_BLOCK = 256
DIFF_Q_BLOCK = 512
SCAN_ROWS = 128
PROMPT_KEY_TILE = 1024
VMEM_LIMIT = 58 * 2**20
MASKED = -1e30
N_BISECT = 16
FFN_ROW_CHUNKS = 4
Q_SCALE = HEAD_DIM ** -0.5 * math.log2(math.e)

PROJ_W_PAD = 3328
_C_QA, _C_KA, _C_VA = 0, 512, 1024
_C_QB, _C_KB, _C_VB = 1536, 1792, 2048
_C_QI = 2304
_C_KW = 2560
_C_KK = 2688
_C_XC = 2816
_C_GC = 3072


def _cparams(sem):
    return pltpu.CompilerParams(dimension_semantics=sem, vmem_limit_bytes=VMEM_LIMIT)


def _rms(x, g):
    return x * lax.rsqrt(jnp.mean(x * x, axis=-1, keepdims=True) + EPS) * g


def _dot_nt(a, b):
    return lax.dot_general(a, b, (((1,), (1,)), ((), ())), preferred_element_type=F32)


def _round_up(n, m):
    return (n + m - 1) // m * m


def _proj_kernel(x_ref, g_ref, w_ref, qa_ref, qb_ref, qi_ref, ka_ref, va_ref, kb_ref, vb_ref,
                 kw_ref, xc_ref, gc_ref, kab_ref, vab_ref, kbb_ref, vbb_ref, kib_ref):
    h = _rms(x_ref[...], g_ref[...])
    z = jnp.dot(h.astype(BF16), w_ref[...], preferred_element_type=F32)
    qa_ref[...] = (z[:, _C_QA:_C_QA + W_A] * Q_SCALE).astype(BF16)
    qb_ref[...] = (z[:, _C_QB:_C_QB + W_B] * Q_SCALE).astype(BF16)
    qi_ref[...] = z[:, _C_QI:_C_QI + H_IDX * D_IDX].astype(BF16)
    ka = z[:, _C_KA:_C_KA + W_A]
    va = z[:, _C_VA:_C_VA + W_A]
    kb = z[:, _C_KB:_C_KB + W_B]
    vb = z[:, _C_VB:_C_VB + W_B]
    ka_ref[...] = ka
    va_ref[...] = va
    kb_ref[...] = kb
    vb_ref[...] = vb
    kab_ref[...] = ka.astype(BF16)
    vab_ref[...] = va.astype(BF16)
    kbb_ref[...] = kb.astype(BF16)
    vbb_ref[...] = vb.astype(BF16)
    kw_ref[...] = z[:, _C_KW:_C_KW + LANES]
    kib_ref[...] = z[:, _C_KK:_C_KK + LANES].astype(BF16)
    xc_ref[...] = z[:, _C_XC:_C_XC + W_C]
    gc_ref[...] = z[:, _C_GC:_C_GC + W_C]


def _proj(x2d, gain, w_pad):
    t = x2d.shape[0]
    tm = min(512, t)
    widths = [(W_A, BF16), (W_B, BF16), (H_IDX * D_IDX, BF16),
              (W_A, F32), (W_A, F32), (W_B, F32), (W_B, F32),
              (LANES, F32), (W_C, F32), (W_C, F32),
              (W_A, BF16), (W_A, BF16), (W_B, BF16), (W_B, BF16), (LANES, BF16)]
    return pl.pallas_call(
        _proj_kernel,
        grid=(t // tm,),
        in_specs=[pl.BlockSpec((tm, D_MODEL), lambda i: (i, 0)),
                  pl.BlockSpec((1, D_MODEL), lambda i: (0, 0)),
                  pl.BlockSpec((D_MODEL, PROJ_W_PAD), lambda i: (0, 0))],
        out_specs=[pl.BlockSpec((tm, w), lambda i: (i, 0)) for w, _ in widths],
        out_shape=[jax.ShapeDtypeStruct((t, w), d) for w, d in widths],
        compiler_params=_cparams(("parallel",)),
        name="proj",
    )(x2d, gain.reshape(1, D_MODEL), w_pad)


def _tile_bounds(q_start, qb, lk, tk):
    n_full = lax.div(jnp.minimum(q_start + CHUNK, lk), tk)
    n_tiles = lax.div(q_start + qb + tk - 1, tk)
    return n_full, n_tiles


def _admissible(start, tk, q_chunk, lk):
    col = start + lax.broadcasted_iota(jnp.int32, (1, tk), 1)
    return (lax.shift_right_logical(col, CHUNK_SHIFT) <= q_chunk) & (col < lk)


def _fori_pairs(n, body):
    pairs = lax.div(n, 2)

    def two(i, c):
        return body(2 * i + 1, body(2 * i, c))

    lax.fori_loop(0, pairs, two, 0)
    lax.fori_loop(2 * pairs, n, body, 0)


def _lane_max(acc, x):
    for g in range(x.shape[1] // LANES):
        acc = jnp.maximum(acc, x[:, g * LANES:(g + 1) * LANES])
    return acc


def _softmax_value_tile(s_ref, idx, m, v):
    parts = []
    for g in range(s_ref.shape[-1] // LANES):
        s = s_ref[idx + (slice(None), slice(g * LANES, (g + 1) * LANES))]
        parts.append(jnp.exp2((s - m).astype(BF16)))
    v_ones = jnp.concatenate([v, jnp.ones_like(v)], axis=1)
    return jnp.dot(jnp.concatenate(parts, axis=1), v_ones, preferred_element_type=F32)


def _diff_attn_kernel(*refs, past, lk, qb, tk, out_scale, cached):
    if cached:
        (lam_ref, gain_ref, q_ref, kn_ref, vn_ref, kc_ref, vc_ref, o_ref,
         s_ref, mx_ref, acc_ref, k_ref, v_ref) = refs
        n_cache, n_new = kc_ref.shape[0], kn_ref.shape[0]
        for cache_ref, new_ref, dst_ref in ((kc_ref, kn_ref, k_ref), (vc_ref, vn_ref, v_ref)):
            dst_ref[0:n_cache, :] = cache_ref[...].astype(BF16)
            dst_ref[n_cache:n_cache + n_new, :] = new_ref[...]
            if tk > n_cache + n_new:
                dst_ref[n_cache + n_new:, :] = jnp.zeros((tk - n_cache - n_new, LANES), BF16)
    else:
        lam_ref, gain_ref, q_ref, k_ref, v_ref, o_ref, s_ref, mx_ref, acc_ref = refs
    j = pl.program_id(2)
    q = q_ref[...]
    lane = lax.broadcasted_iota(jnp.int32, (1, LANES), 1)
    zero = jnp.zeros_like(q)
    q_half = (jnp.where(lane < HEAD_DIM, q, zero), jnp.where(lane >= HEAD_DIM, q, zero))
    q_start = past + j * qb
    row = lax.broadcasted_iota(jnp.int32, (qb, 1), 0)
    q_chunk = lax.shift_right_logical(q_start + row, CHUNK_SHIFT)
    n_full, n_tiles = _tile_bounds(q_start, qb, lk, tk)

    mx_ref[...] = jnp.full(mx_ref.shape, -jnp.inf, F32)
    acc_ref[...] = jnp.zeros(acc_ref.shape, F32)

    def score_tile(t, masked):
        start = pl.multiple_of(t * tk, tk)
        k = k_ref[pl.ds(start, tk), :]
        if masked:
            ok = _admissible(start, tk, q_chunk, lk)
        for i in range(2):
            s = _dot_nt(q_half[i], k)
            if masked:
                s = jnp.where(ok, s, -jnp.inf)
            s_ref[i, t] = s
            mx_ref[i] = _lane_max(mx_ref[i], s)

    def full_body(t, c):
        score_tile(t, False)
        return c

    def masked_body(t, c):
        score_tile(t, True)
        return c

    lax.fori_loop(0, n_full, full_body, 0)
    lax.fori_loop(n_full, n_tiles, masked_body, 0)

    for i in range(2):
        mx_ref[i] = jnp.broadcast_to(jnp.max(mx_ref[i], axis=1, keepdims=True), (qb, LANES))

    def value_tile(t, c):
        start = pl.multiple_of(t * tk, tk)
        v = v_ref[pl.ds(start, tk), :]
        for i in range(2):
            acc_ref[i] = acc_ref[i] + _softmax_value_tile(s_ref, (i, t), mx_ref[i], v)
        return c

    lax.fori_loop(0, n_tiles, value_tile, 0)

    o = (acc_ref[0, :, :LANES] / acc_ref[0, :, LANES:]
         - lam_ref[...] * (acc_ref[1, :, :LANES] / acc_ref[1, :, LANES:]))
    o_ref[...] = (_rms(o, gain_ref[...]) * out_scale).astype(BF16)


def _diff_attn(q, k, v, lam, gain, *, past, lk, tk, out_scale, cache=None):
    b, l, _ = q.shape
    qb = min(l, DIFF_Q_BLOCK)
    cached = cache is not None
    lkp = tk if cached else k.shape[1]
    assert l % qb == 0 and lkp % tk == 0 and qb % CHUNK == 0
    kern = functools.partial(_diff_attn_kernel, past=past, lk=lk, qb=qb, tk=tk,
                             out_scale=out_scale, cached=cached)
    in_specs = [pl.BlockSpec((1, LANES), lambda bi, h, j: (0, 0)),
                pl.BlockSpec((1, LANES), lambda bi, h, j: (0, 0)),
                pl.BlockSpec((None, qb, LANES), lambda bi, h, j: (bi, j, h)),
                pl.BlockSpec((None, k.shape[1], LANES), lambda bi, h, j: (bi, 0, h)),
                pl.BlockSpec((None, k.shape[1], LANES), lambda bi, h, j: (bi, 0, h))]
    scratch = [pltpu.VMEM((2, lkp // tk, qb, tk), F32),
               pltpu.VMEM((2, qb, LANES), F32), pltpu.VMEM((2, qb, 2 * LANES), F32)]
    operands = [lam, gain, q, k, v]
    if cached:
        cache_k, cache_v, li = cache
        assert l == qb and past + l <= tk and cache_k.shape[2:] == (past, W_A)
        head_rows = pl.BlockSpec((None, None, past, LANES), lambda bi, h, j: (li, bi, 0, h))
        in_specs += [head_rows, head_rows]
        scratch += [pltpu.VMEM((tk, LANES), BF16), pltpu.VMEM((tk, LANES), BF16)]
        operands += [cache_k, cache_v]
    return pl.pallas_call(
        kern,
        grid=(b, H_A, l // qb),
        in_specs=in_specs,
        out_specs=pl.BlockSpec((None, qb, LANES), lambda bi, h, j: (bi, j, h)),
        out_shape=jax.ShapeDtypeStruct((b, l, W_A), BF16),
        scratch_shapes=scratch,
        compiler_params=_cparams(("parallel", "parallel", "parallel")),
        name="diff_attn",
    )(*operands)


def _dsa_kernel(qi_ref, kw_ref, q_ref, ki_ref, k_ref, v_ref, o_ref,
                s_ref, sc_ref, lo_ref, hi_ref, mx_ref, acc_ref,
                *, past, lk, qb, tk, topk, max_steps):
    j = pl.program_id(1)
    q_start = past + j * qb
    n_full, n_tiles = _tile_bounds(q_start, qb, lk, tk)
    n_groups = tk // LANES
    row = lax.broadcasted_iota(jnp.int32, (qb, 1), 0)
    q_chunk = lax.shift_right_logical(q_start + row, CHUNK_SHIFT)
    lane = lax.broadcasted_iota(jnp.int32, (1, LANES), 1)
    low_half = lane < HEAD_DIM
    k_sel = float(topk)

    def wide(x):
        return jnp.broadcast_to(x, (qb, LANES))

    def head_views(x):
        views = []
        for h in range(4):
            pair = x[:, (h // 2) * LANES:(h // 2 + 1) * LANES]
            keep = low_half if h % 2 == 0 else jnp.logical_not(low_half)
            views.append(jnp.where(keep, pair, jnp.zeros_like(pair)))
        return views

    qi_h = head_views(qi_ref[...])
    kw = kw_ref[...]
    w_h = [wide(kw[:, D_IDX + h:D_IDX + h + 1]) for h in range(H_IDX)]
    lo_ref[...] = jnp.full(lo_ref.shape, -jnp.inf, F32)
    hi_ref[...] = jnp.full(hi_ref.shape, -jnp.inf, F32)

    def index_tile(t, masked):
        start = pl.multiple_of(t * tk, tk)
        if masked:
            adm = _admissible(start, tk, q_chunk, lk)
        top2 = lo_ref[...]
        top1 = hi_ref[...]
        for c in range(tk // MXU_DIM):
            ki = ki_ref[pl.ds(start + c * MXU_DIM, MXU_DIM), :]
            rel = [jnp.maximum(_dot_nt(qi_h[h], ki), 0.0) for h in range(H_IDX)]
            for g in range(MXU_DIM // LANES):
                sl = slice(g * LANES, (g + 1) * LANES)
                sc = w_h[0] * rel[0][:, sl]
                for h in range(1, H_IDX):
                    sc = sc + w_h[h] * rel[h][:, sl]
                csl = slice(c * MXU_DIM + g * LANES, c * MXU_DIM + (g + 1) * LANES)
                if masked:
                    sc = jnp.where(adm[:, csl], sc, -jnp.inf)
                s_ref[t, :, csl] = sc
                top2 = jnp.maximum(top2, jnp.minimum(top1, sc))
                top1 = jnp.maximum(top1, sc)
        lo_ref[...] = top2
        hi_ref[...] = top1

    def index_full(t, c):
        index_tile(t, False)
        return c

    def index_masked(t, c):
        index_tile(t, True)
        return c

    lax.fori_loop(0, n_full, index_full, 0)
    lax.fori_loop(n_full, n_tiles, index_masked, 0)
    rb = min(qb, SCAN_ROWS)
    ones_mat = jnp.ones((LANES, LANES), BF16)
    tri_i = lax.broadcasted_iota(jnp.int32, (LANES, 2 * LANES), 0)
    tri_j = lax.broadcasted_iota(jnp.int32, (LANES, 2 * LANES), 1)
    prefix_mat = jnp.where(jnp.logical_or(tri_i <= tri_j, tri_j >= LANES), 1.0, 0.0).astype(BF16)
    assert s_ref.shape[0] * n_groups <= 256

    row_blocks = [slice(r * rb, (r + 1) * rb) for r in range(qb // rb)]
    pos = q_start + lax.broadcasted_iota(jnp.int32, (qb, LANES), 0)
    n_adm = jnp.minimum((lax.shift_right_logical(pos, CHUNK_SHIFT) + 1) * CHUNK, lk)
    few = n_adm <= topk

    def lanes_all(x, reduce):
        return jnp.broadcast_to(reduce(x, axis=1, keepdims=True), (qb, LANES))

    def row_sum(acc):
        return jnp.dot(acc.astype(BF16), ones_mat, preferred_element_type=F32)

    def scan(step, init, *operands):
        outs = []
        for rows in row_blocks:
            ops = [o[rows] for o in operands]

            def body(t, acc, rows=rows, ops=ops):
                for g in range(n_groups):
                    acc = step(acc, s_ref[t, rows, g * LANES:(g + 1) * LANES], *ops)
                return acc

            outs.append(lax.fori_loop(0, n_tiles, body, jnp.full((rb, LANES), init, F32)))
        return outs[0] if len(outs) == 1 else jnp.concatenate(outs, axis=0)

    def count_ge(thr):
        return row_sum(scan(lambda acc, s, t: acc + jnp.where(s >= t, 1.0, 0.0), 0.0, thr))

    def max_below(bound):
        acc = scan(lambda acc, s, b: jnp.maximum(acc, jnp.where(s < b, s, -jnp.inf)), -jnp.inf, bound)
        return lanes_all(acc, jnp.max)

    assert topk <= 2 * LANES
    lane_best = hi_ref[...] if topk <= LANES else lo_ref[...]
    lowest = float(jnp.finfo(jnp.float32).min)
    rmin = jnp.maximum(lanes_all(lane_best, jnp.min), lowest)
    rmax = jnp.maximum(lanes_all(lane_best, jnp.max), lowest)

    c_max = count_ge(rmax)
    top_ties = c_max >= k_sel

    def bisect(_, c):
        lo, hi = c
        mid = 0.5 * lo + 0.5 * hi
        ge = count_ge(mid) >= k_sel
        return jnp.where(ge, mid, lo), jnp.where(ge, hi, mid)

    _, hi = lax.fori_loop(0, N_BISECT, bisect, (rmin, rmax))

    def walk_cond(st):
        it, _, _, _, active = st
        return jnp.logical_and(jnp.max(active) > 0.0, it < max_steps)

    def walk_body(st):
        it, cand, thr, c_thr, active = st
        c = count_ge(cand)
        ok = c >= k_sel
        act = active > 0.0
        hit = jnp.logical_and(act, ok)
        thr = jnp.where(hit, cand, thr)
        c_thr = jnp.where(hit, c, c_thr)
        active = jnp.where(jnp.logical_and(act, jnp.logical_not(ok)), 1.0, 0.0)
        return it + 1, max_below(cand), thr, c_thr, active

    thr0 = jnp.where(few, lowest, jnp.where(top_ties, rmax, lowest))
    c0 = jnp.where(few, k_sel, jnp.where(top_ties, c_max, k_sel))
    active0 = jnp.where(jnp.logical_or(few, top_ties), 0.0, 1.0)
    _, _, thr, c_thr, _ = lax.while_loop(
        walk_cond, walk_body, (jnp.int32(0), max_below(hi), thr0, c0, active0))

    has_excess = jnp.max(jnp.where(c_thr > k_sel, 1.0, 0.0)) > 0.0

    def ranked_bias():
        n_tie = k_sel - row_sum(scan(lambda acc, s, t: acc + jnp.where(s > t, 1.0, 0.0), 0.0, thr))
        for rows in row_blocks:
            thr_r = thr[rows]
            n_tie_r = n_tie[rows]

            def body(t, before, rows=rows, thr_r=thr_r, n_tie_r=n_tie_r):
                for g in range(n_groups):
                    lanes = slice(g * LANES, (g + 1) * LANES)
                    s = s_ref[t, rows, lanes]
                    tie = s == thr_r
                    pr = jnp.dot(jnp.where(tie, 1.0, 0.0).astype(BF16), prefix_mat,
                                 preferred_element_type=F32)
                    rank = before + pr[:, :LANES]
                    tie_bias = jnp.where(rank <= n_tie_r, 0.0, MASKED)
                    s_ref[t, rows, lanes] = jnp.where(s > thr_r, 0.0, jnp.where(tie, tie_bias, MASKED))
                    before = before + pr[:, LANES:]
                return before

            lax.fori_loop(0, n_tiles, body, jnp.zeros((rb, LANES), F32))
        return 0

    def plain_bias():
        for rows in row_blocks:
            thr_r = thr[rows]

            def body(t, c, rows=rows, thr_r=thr_r):
                for g in range(n_groups):
                    lanes = slice(g * LANES, (g + 1) * LANES)
                    s_ref[t, rows, lanes] = jnp.where(s_ref[t, rows, lanes] >= thr_r, 0.0, MASKED)
                return c

            lax.fori_loop(0, n_tiles, body, 0)
        return 0

    lax.cond(has_excess, ranked_bias, plain_bias)

    q_h = head_views(q_ref[...])
    for g in range(H_B // 2):
        lanes = slice(g * LANES, (g + 1) * LANES)
        mx_ref[...] = jnp.full(mx_ref.shape, -jnp.inf, F32)
        acc_ref[...] = jnp.zeros(acc_ref.shape, F32)

        def score_tile(t, c, g=g, lanes=lanes):
            start = pl.multiple_of(t * tk, tk)
            k = k_ref[pl.ds(start, tk), lanes]
            bias = s_ref[t]
            for i in range(2):
                s = _dot_nt(q_h[2 * g + i], k) + bias
                sc_ref[i, t] = s
                mx_ref[i] = _lane_max(mx_ref[i], s)
            return c

        _fori_pairs(n_tiles, score_tile)
        for i in range(2):
            mx_ref[i] = jnp.broadcast_to(jnp.max(mx_ref[i], axis=1, keepdims=True), (qb, LANES))

        def value_tile(t, c, lanes=lanes):
            start = pl.multiple_of(t * tk, tk)
            v = v_ref[pl.ds(start, tk), lanes]
            for i in range(2):
                acc_ref[i] = acc_ref[i] + _softmax_value_tile(sc_ref, (i, t), mx_ref[i], v)
            return c

        _fori_pairs(n_tiles, value_tile)
        even = acc_ref[0, :, :LANES] / acc_ref[0, :, LANES:]
        odd = acc_ref[1, :, :LANES] / acc_ref[1, :, LANES:]
        o_ref[:, lanes] = jnp.where(low_half, even, odd).astype(BF16)


def _dsa(qi, kw, q, ki, k, v, *, past, lk, tk):
    b, l, _ = q.shape
    lkp = k.shape[1]
    qb = min(l, Q_BLOCK)
    assert l % qb == 0 and lkp % tk == 0 and tk % MXU_DIM == 0 and qb % CHUNK == 0
    kern = functools.partial(_dsa_kernel, past=past, lk=lk, qb=qb, tk=tk,
                             topk=min(TOPK, lk // 4), max_steps=lkp)
    return pl.pallas_call(
        kern,
        grid=(b, l // qb),
        in_specs=[pl.BlockSpec((None, qb, H_IDX * D_IDX), lambda bi, j: (bi, j, 0)),
                  pl.BlockSpec((None, qb, LANES), lambda bi, j: (bi, j, 0)),
                  pl.BlockSpec((None, qb, W_B), lambda bi, j: (bi, j, 0)),
                  pl.BlockSpec((None, lkp, LANES), lambda bi, j: (bi, 0, 0)),
                  pl.BlockSpec((None, lkp, W_B), lambda bi, j: (bi, 0, 0)),
                  pl.BlockSpec((None, lkp, W_B), lambda bi, j: (bi, 0, 0))],
        out_specs=pl.BlockSpec((None, qb, W_B), lambda bi, j: (bi, j, 0)),
        out_shape=jax.ShapeDtypeStruct((b, l, W_B), BF16),
        scratch_shapes=[pltpu.VMEM((lkp // tk, qb, tk), F32),
                        pltpu.VMEM((2, lkp // tk, qb, tk), F32),
                        pltpu.VMEM((qb, LANES), F32), pltpu.VMEM((qb, LANES), F32),
                        pltpu.VMEM((2, qb, LANES), F32), pltpu.VMEM((2, qb, 2 * LANES), F32)],
        compiler_params=_cparams(("parallel", "parallel")),
        name="dsa",
    )(qi, kw, q, ki, k, v)


def _shift_rows(x, prev, k):
    rolled = pltpu.roll(x, k, 0)
    row = lax.broadcasted_iota(jnp.int32, (SUBLANES, 1), 0)
    top = jnp.where(row < k, pltpu.roll(prev, k, 0), rolled[0:SUBLANES])
    if x.shape[0] == SUBLANES:
        return top
    return jnp.concatenate([top, rolled[SUBLANES:]], axis=0)


def _rglru_kernel(xc_ref, gc_ref, cst_ref, h0_ref, cw_ref, cb_ref, wg_ref, bg_ref, lam_ref,
                  oc_ref, hl_ref, cn_ref, prev_ref, h_ref, a_ref, b_ref, hs_ref, *, past, tl):
    i = pl.program_id(1)

    @pl.when(i == 0)
    def _():
        prev_ref[...] = jnp.zeros(prev_ref.shape, F32)
        prev_ref[SUBLANES - (CONV_C - 1):SUBLANES, :] = cst_ref[...]
        h_ref[...] = h0_ref[...]

    x = xc_ref[...]
    prev = prev_ref[...]
    cw = cw_ref[...]
    xconv = cw[CONV_C - 1:CONV_C] * x + cb_ref[...]
    for k in range(1, CONV_C):
        xconv = xconv + cw[CONV_C - 1 - k:CONV_C - k] * _shift_rows(x, prev, k)
    prev_ref[...] = x[tl - SUBLANES:tl]

    pre = jnp.dot(xconv.astype(BF16), wg_ref[...], preferred_element_type=F32) + bg_ref[...]
    r = jax.nn.sigmoid(pre[:, :W_C])
    gate_i = jax.nn.sigmoid(pre[:, W_C:])
    neg_lam = -lam_ref[...]
    softplus = jnp.maximum(neg_lam, 0.0) + jnp.log1p(jnp.exp(-jnp.abs(neg_lam)))
    log_a = -RG_C * r * softplus
    pos = past + i * tl + lax.broadcasted_iota(jnp.int32, (tl, 1), 0)
    th = jnp.tanh(log_a)
    mult = jnp.where(pos == 0, 1.0, jnp.sqrt(-2.0 * th / (1.0 - th)))
    a_ref[...] = jnp.exp(log_a)
    b_ref[...] = mult * gate_i * xconv

    def step(t, h):
        h = a_ref[pl.ds(t, 1), :] * h + b_ref[pl.ds(t, 1), :]
        hs_ref[pl.ds(t, 1), :] = h
        return h

    h_last = lax.fori_loop(0, tl, step, h_ref[...], unroll=8)
    h_ref[...] = h_last

    gc = gc_ref[...]
    gelu = 0.5 * gc * (1.0 + jnp.tanh(math.sqrt(2.0 / math.pi) * (gc + 0.044715 * (gc * gc * gc))))
    oc_ref[...] = (hs_ref[...] * gelu).astype(BF16)

    @pl.when(i == pl.num_programs(1) - 1)
    def _():
        hl_ref[...] = h_last
        cn_ref[...] = x[tl - (CONV_C - 1):tl]


def _rglru(xc, gc, conv_state, h0, conv_w, conv_b, w_gate, b_gate, lam, *, past):
    b, l, _ = xc.shape
    tl = min(l, 512)
    assert l >= SUBLANES and l % tl == 0
    kern = functools.partial(_rglru_kernel, past=past, tl=tl)
    const = lambda bi, i: (0, 0)
    return pl.pallas_call(
        kern,
        grid=(b, l // tl),
        in_specs=[pl.BlockSpec((None, tl, W_C), lambda bi, i: (bi, i, 0)),
                  pl.BlockSpec((None, tl, W_C), lambda bi, i: (bi, i, 0)),
                  pl.BlockSpec((None, CONV_C - 1, W_C), lambda bi, i: (bi, 0, 0)),
                  pl.BlockSpec((None, 1, W_C), lambda bi, i: (bi, 0, 0)),
                  pl.BlockSpec((CONV_C, W_C), const),
                  pl.BlockSpec((1, W_C), const),
                  pl.BlockSpec((W_C, 2 * W_C), const),
                  pl.BlockSpec((1, 2 * W_C), const),
                  pl.BlockSpec((1, W_C), const)],
        out_specs=[pl.BlockSpec((None, tl, W_C), lambda bi, i: (bi, i, 0)),
                   pl.BlockSpec((None, 1, W_C), lambda bi, i: (bi, 0, 0)),
                   pl.BlockSpec((None, CONV_C - 1, W_C), lambda bi, i: (bi, 0, 0))],
        out_shape=[jax.ShapeDtypeStruct((b, l, W_C), BF16),
                   jax.ShapeDtypeStruct((b, 1, W_C), F32),
                   jax.ShapeDtypeStruct((b, CONV_C - 1, W_C), F32)],
        scratch_shapes=[pltpu.VMEM((SUBLANES, W_C), F32), pltpu.VMEM((1, W_C), F32),
                        pltpu.VMEM((tl, W_C), F32), pltpu.VMEM((tl, W_C), F32),
                        pltpu.VMEM((tl, W_C), F32)],
        compiler_params=_cparams(("parallel", "arbitrary")),
        name="rglru",
    )(xc, gc, conv_state, h0.reshape(b, 1, W_C), conv_w, conv_b.reshape(1, W_C), w_gate,
      b_gate, lam.reshape(1, W_C))


def _outproj_kernel(x_ref, oa_ref, ob_ref, oc_ref, w_ref, g_ref, x1_ref, hn_ref):
    mix = jnp.dot(oa_ref[...], w_ref[0:W_A, :], preferred_element_type=F32)
    mix = mix + jnp.dot(ob_ref[...], w_ref[W_A:W_A + W_B, :], preferred_element_type=F32)
    mix = mix + jnp.dot(oc_ref[...], w_ref[W_A + W_B:, :], preferred_element_type=F32)
    x1 = x_ref[...] + mix
    x1_ref[...] = x1
    hn_ref[...] = _rms(x1, g_ref[...]).astype(BF16)


def _outproj(x2d, oa, ob, oc, w_out, gain):
    t = x2d.shape[0]
    tm = min(512, t)
    row = lambda w: pl.BlockSpec((tm, w), lambda i: (i, 0))
    return pl.pallas_call(
        _outproj_kernel,
        grid=(t // tm,),
        in_specs=[row(D_MODEL), row(W_A), row(W_B), row(W_C),
                  pl.BlockSpec((D_MODEL, D_MODEL), lambda i: (0, 0)),
                  pl.BlockSpec((1, D_MODEL), lambda i: (0, 0))],
        out_specs=[row(D_MODEL), row(D_MODEL)],
        out_shape=[jax.ShapeDtypeStruct((t, D_MODEL), F32), jax.ShapeDtypeStruct((t, D_MODEL), BF16)],
        compiler_params=_cparams(("parallel",)),
        name="outproj",
    )(x2d, oa, ob, oc, w_out, gain.reshape(1, D_MODEL))


def _ffn_kernel(hn_ref, x1_ref, wu_ref, wg_ref, cwu_ref, cwg_ref, cbu_ref, cbg_ref, wd_ref,
                su_ref, sg_ref, gfin_ref, y_ref, fu_ref, fg_ref,
                acc_ref, au_ref, ag_ref, cu_ref, cg_ref, *, tm, final_norm):
    i = pl.program_id(1)
    s = pl.program_id(2)
    nf = pl.num_programs(2) - 1
    live = s > 0
    fb = jnp.maximum(s - 1, 0)
    row = jnp.where(live, fb, nf)

    @pl.when(s == 0)
    def _():
        au_ref[1] = jnp.zeros(au_ref.shape[1:], F32)
        ag_ref[1] = jnp.zeros(ag_ref.shape[1:], F32)
        cu_ref[nf] = jnp.zeros(cu_ref.shape[1:], F32)
        cg_ref[nf] = jnp.zeros(cg_ref.shape[1:], F32)
        acc_ref[...] = x1_ref[...]

    @pl.when(jnp.logical_and(i == 0, live))
    def _():
        for carry_ref, st_ref in ((cu_ref, su_ref), (cg_ref, sg_ref)):
            carry_ref[fb] = jnp.zeros(carry_ref.shape[1:], F32)
            carry_ref[fb, SUBLANES - (CONV_F - 1):SUBLANES, :] = st_ref[...]

    def step(wslot, rslot):
        au_ref[rslot, 0:SUBLANES, :] = cu_ref[row]
        ag_ref[rslot, 0:SUBLANES, :] = cg_ref[row]
        rc = tm // FFN_ROW_CHUNKS

        def conv(a_ref, cw_ref, cb_ref, r0):
            cw = cw_ref[...]
            y = cw[CONV_F - 1:CONV_F] * a_ref[rslot, SUBLANES + r0:SUBLANES + r0 + rc, :] + cb_ref[...]
            for k in range(1, CONV_F):
                y = y + (cw[CONV_F - 1 - k:CONV_F - k]
                         * a_ref[rslot, SUBLANES - k + r0:SUBLANES - k + r0 + rc, :])
            return y

        for c in range(FFN_ROW_CHUNKS):
            r0 = c * rc
            hn = hn_ref[r0:r0 + rc, :]
            au_ref[wslot, SUBLANES + r0:SUBLANES + r0 + rc, :] = jnp.dot(
                hn, wu_ref[...], preferred_element_type=F32)
            ag_ref[wslot, SUBLANES + r0:SUBLANES + r0 + rc, :] = jnp.dot(
                hn, wg_ref[...], preferred_element_type=F32)
            u = conv(au_ref, cwu_ref, cbu_ref, r0)
            g = conv(ag_ref, cwg_ref, cbg_ref, r0)
            mid = (g * jax.nn.sigmoid(g) * u).astype(BF16)
            contrib = jnp.dot(mid, wd_ref[...], preferred_element_type=F32)
            acc_ref[r0:r0 + rc, :] = acc_ref[r0:r0 + rc, :] + jnp.where(live, contrib, 0.0)

        for a_ref, carry_ref, tail_ref in ((au_ref, cu_ref, fu_ref), (ag_ref, cg_ref, fg_ref)):
            carry_ref[row] = a_ref[rslot, tm:tm + SUBLANES, :]
            tail_ref[row] = a_ref[rslot, tm + SUBLANES - (CONV_F - 1):tm + SUBLANES, :]

    parity = lax.rem(s, 2)

    @pl.when(parity == 0)
    def _():
        step(0, 1)

    @pl.when(parity == 1)
    def _():
        step(1, 0)

    @pl.when(s == nf)
    def _():
        y = acc_ref[...]
        if final_norm:
            y = _rms(y, gfin_ref[...])
        y_ref[...] = y


def _ffn(hn, x1, w_up, conv_w, conv_b, w_down, state, final_gain, *, final_norm):
    b, l, _ = hn.shape
    tm = min(l, 1024)
    tf = 512
    nf = D_FF // tf
    assert l % tm == 0 and tm >= SUBLANES
    kern = functools.partial(_ffn_kernel, tm=tm, final_norm=final_norm)
    conv_b = conv_b.reshape(1, 2 * D_FF)
    up = lambda s: jnp.minimum(s, nf - 1)
    fin = lambda s: jnp.maximum(s - 1, 0)
    tail_spec = pl.BlockSpec((None, nf + 1, CONV_F - 1, tf), lambda bi, i, s: (bi, 0, 0, 0))
    tail_shape = jax.ShapeDtypeStruct((b, nf + 1, CONV_F - 1, tf), F32)
    y, fu, fg = pl.pallas_call(
        kern,
        grid=(b, l // tm, nf + 1),
        in_specs=[pl.BlockSpec((None, tm, D_MODEL), lambda bi, i, s: (bi, i, 0)),
                  pl.BlockSpec((None, tm, D_MODEL), lambda bi, i, s: (bi, i, 0)),
                  pl.BlockSpec((D_MODEL, tf), lambda bi, i, s: (0, up(s))),
                  pl.BlockSpec((D_MODEL, tf), lambda bi, i, s: (0, nf + up(s))),
                  pl.BlockSpec((CONV_F, tf), lambda bi, i, s: (0, fin(s))),
                  pl.BlockSpec((CONV_F, tf), lambda bi, i, s: (0, nf + fin(s))),
                  pl.BlockSpec((1, tf), lambda bi, i, s: (0, fin(s))),
                  pl.BlockSpec((1, tf), lambda bi, i, s: (0, nf + fin(s))),
                  pl.BlockSpec((tf, D_MODEL), lambda bi, i, s: (fin(s), 0)),
                  pl.BlockSpec((None, CONV_F - 1, tf), lambda bi, i, s: (bi, 0, fin(s))),
                  pl.BlockSpec((None, CONV_F - 1, tf), lambda bi, i, s: (bi, 0, nf + fin(s))),
                  pl.BlockSpec((1, D_MODEL), lambda bi, i, s: (0, 0))],
        out_specs=[pl.BlockSpec((None, tm, D_MODEL), lambda bi, i, s: (bi, i, 0)),
                   tail_spec, tail_spec],
        out_shape=[jax.ShapeDtypeStruct((b, l, D_MODEL), F32), tail_shape, tail_shape],
        scratch_shapes=[pltpu.VMEM((tm, D_MODEL), F32),
                        pltpu.VMEM((2, tm + SUBLANES, tf), F32),
                        pltpu.VMEM((2, tm + SUBLANES, tf), F32),
                        pltpu.VMEM((nf + 1, SUBLANES, tf), F32),
                        pltpu.VMEM((nf + 1, SUBLANES, tf), F32)],
        compiler_params=_cparams(("parallel", "arbitrary", "arbitrary")),
        name="ffn",
    )(hn, x1, w_up, w_up, conv_w, conv_w, conv_b, conv_b, w_down, state, state,
      final_gain.reshape(1, D_MODEL))
    flat = lambda a: jnp.swapaxes(a[:, :nf], 1, 2).reshape(b, CONV_F - 1, D_FF)
    return y, jnp.concatenate([flat(fu), flat(fg)], axis=-1)


def _cast_kernel(x_ref, o_ref):
    o_ref[...] = x_ref[...].astype(o_ref.dtype)


def _layer_bf16(w, li):
    _, r, c = w.shape
    tr = 256 if r % 256 == 0 else r
    return pl.pallas_call(
        _cast_kernel,
        grid=(r // tr,),
        in_specs=[pl.BlockSpec((None, tr, c), lambda i: (li, i, 0))],
        out_specs=pl.BlockSpec((tr, c), lambda i: (i, 0)),
        out_shape=jax.ShapeDtypeStruct((r, c), BF16),
        compiler_params=_cparams(("parallel",)),
        name="cast",
    )(w)


def _prep_layer_weights(p, li):
    w_in = _layer_bf16(p["w_in"], li)
    zeros = lambda n: jnp.zeros((D_MODEL, n), w_in.dtype)
    k_idx = w_in[:, 2560:2624]
    w_pad = jnp.concatenate(
        [w_in[:, :2628], zeros(_C_KK - 2628), k_idx, k_idx, w_in[:, 2628:]], axis=1)
    assert w_pad.shape[1] == PROJ_W_PAD

    def block_diag(w):
        out = jnp.zeros((W_C, W_C), w.dtype)
        for n in range(N_GATE_BLOCKS):
            sl = slice(n * GATE_BLOCK, (n + 1) * GATE_BLOCK)
            out = out.at[sl, sl].set(w[n])
        return out

    w_gate = jnp.concatenate([block_diag(p["rg_w_r"][li]), block_diag(p["rg_w_i"][li])], axis=1)
    b_gate = jnp.concatenate([p["rg_b_r"][li].reshape(1, W_C), p["rg_b_i"][li].reshape(1, W_C)], axis=1)
    lam_init = 0.8 - 0.6 * math.exp(-0.3 * li)
    f32 = lambda a: a.astype(F32)
    lam = (jnp.exp(jnp.sum(f32(p["lam_q1"][li]) * f32(p["lam_k1"][li])))
           - jnp.exp(jnp.sum(f32(p["lam_q2"][li]) * f32(p["lam_k2"][li]))) + lam_init)
    return dict(
        norm_mix=p["norm_mix"][li], w_pad=w_pad, lam=jnp.full((1, LANES), lam, F32),
        lam_init=lam_init, diff_gain=p["diff_gain"][li].reshape(1, LANES),
        rg_conv_w=p["rg_conv_w"][li], rg_conv_b=p["rg_conv_b"][li],
        w_gate=w_gate.astype(BF16), b_gate=b_gate, rg_lambda=p["rg_lambda"][li],
        w_out=_layer_bf16(p["w_out"], li), norm_ffn=p["norm_ffn"][li],
        ffn_w_up=_layer_bf16(p["ffn_w_up"], li), ffn_conv_w=p["ffn_conv_w"][li],
        ffn_conv_b=p["ffn_conv_b"][li], ffn_w_down=_layer_bf16(p["ffn_w_down"], li))


def _layer(x, past, attn_cache, w, final_gain, final_norm):
    b, l, _ = x.shape
    _, _, b_k0, b_v0, b_ki0, c_h0, c_cv0, f_cv0 = past
    p_len = 0 if b_k0 is None else b_k0.shape[1]
    lk = p_len + l
    tk = PROMPT_KEY_TILE if p_len == 0 else _round_up(lk, MXU_DIM)
    lkp = _round_up(lk, tk)
    t = b * l

    (qa, qb, qi, ka, va, kb, vb, kw, xc, gc, kab, vab, kbb, vbb, kib) = _proj(
        x.reshape(t, D_MODEL), w["norm_mix"], w["w_pad"])

    def keys(cache, new, dup=False):
        new = new.reshape(b, l, -1)
        parts = []
        if cache is not None:
            c = cache.reshape(b, p_len, -1).astype(BF16)
            parts.append(jnp.concatenate([c, c], axis=-1) if dup else c)
        parts.append(new)
        if lkp > lk:
            parts.append(jnp.zeros((b, lkp - lk, new.shape[-1]), BF16))
        return parts[0] if len(parts) == 1 else jnp.concatenate(parts, axis=1)

    if attn_cache is None:
        k_a, v_a = keys(None, kab), keys(None, vab)
    else:
        k_a, v_a = kab.reshape(b, l, W_A), vab.reshape(b, l, W_A)
    o_a = _diff_attn(qa.reshape(b, l, W_A), k_a, v_a, w["lam"], w["diff_gain"], past=p_len, lk=lk,
                     tk=tk, out_scale=1.0 - w["lam_init"], cache=attn_cache)
    o_b = _dsa(qi.reshape(b, l, -1), kw.reshape(b, l, LANES), qb.reshape(b, l, W_B),
               keys(b_ki0, kib, dup=True), keys(b_k0, kbb), keys(b_v0, vbb),
               past=p_len, lk=lk, tk=tk)
    o_c, h_last, conv_new = _rglru(xc.reshape(b, l, W_C), gc.reshape(b, l, W_C), c_cv0, c_h0,
                                   w["rg_conv_w"], w["rg_conv_b"], w["w_gate"], w["b_gate"],
                                   w["rg_lambda"], past=p_len)
    x1, hn = _outproj(x.reshape(t, D_MODEL), o_a.reshape(t, W_A), o_b.reshape(t, W_B),
                      o_c.reshape(t, W_C), w["w_out"], w["norm_ffn"])
    y, f_buf = _ffn(hn.reshape(b, l, D_MODEL), x1.reshape(b, l, D_MODEL), w["ffn_w_up"],
                    w["ffn_conv_w"], w["ffn_conv_b"], w["ffn_w_down"], f_cv0, final_gain,
                    final_norm=final_norm)
    new = (ka.reshape(b, l, H_A, 2 * HEAD_DIM), va.reshape(b, l, H_A, 2 * HEAD_DIM),
           kb.reshape(b, l, H_B, HEAD_DIM), vb.reshape(b, l, H_B, HEAD_DIM),
           kw[:, :D_IDX].reshape(b, l, D_IDX), h_last.reshape(b, W_C), conv_new, f_buf)
    return y, new


def _trunk(x, past, weights, final_gain):
    states = []
    if past[0] is not None:
        flat_heads = lambda c: c.reshape(c.shape[:3] + (W_A,))
        cache_a = (flat_heads(past[0]), flat_heads(past[1]))
    for li in range(N_LAYERS):
        layer_past = tuple(None if c is None else c[li] for c in past)
        attn_cache = None if past[0] is None else cache_a + (li,)
        x, st = _layer(x, layer_past, attn_cache, weights[li], final_gain,
                       final_norm=(li == N_LAYERS - 1))
        states.append(st)
    return x, states


def _forward(x_prompt, x_sample, caches, params):
    weights = [_prep_layer_weights(params, li) for li in range(N_LAYERS)]
    bp = x_prompt.shape[0]
    dt = x_prompt.dtype
    past_prompt = (None, None, None, None, None,
                   jnp.zeros((N_LAYERS, bp, W_C), dt),
                   jnp.zeros((N_LAYERS, bp, CONV_C - 1, W_C), dt),
                   jnp.zeros((N_LAYERS, bp, CONV_F - 1, 2 * D_FF), dt))
    yp, sp = _trunk(x_prompt, past_prompt, weights, params["norm_final"])
    ys, ss = _trunk(x_sample, caches, weights, params["norm_final"])
    out = [yp, ys]
    for jdx in range(8):
        out.append(jnp.stack([st[jdx] for st in sp], axis=0))
        out.append(jnp.stack([st[jdx] for st in ss], axis=0))
    return tuple(out)


def kernel(x_prompt, x_sample, cache_a_k, cache_a_v, cache_b_k, cache_b_v, cache_b_kidx,
           state_c_h, state_c_conv, state_ffn_conv, norm_mix, w_in, lam_q1, lam_k1, lam_q2,
           lam_k2, diff_gain, rg_conv_w, rg_conv_b, rg_w_r, rg_b_r, rg_w_i, rg_b_i, rg_lambda,
           w_out, norm_ffn, ffn_w_up, ffn_conv_w, ffn_conv_b, ffn_w_down, norm_final):
    params = dict(norm_mix=norm_mix, w_in=w_in, lam_q1=lam_q1, lam_k1=lam_k1, lam_q2=lam_q2,
                  lam_k2=lam_k2, diff_gain=diff_gain, rg_conv_w=rg_conv_w, rg_conv_b=rg_conv_b,
                  rg_w_r=rg_w_r, rg_b_r=rg_b_r, rg_w_i=rg_w_i, rg_b_i=rg_b_i, rg_lambda=rg_lambda,
                  w_out=w_out, norm_ffn=norm_ffn, ffn_w_up=ffn_w_up, ffn_conv_w=ffn_conv_w,
                  ffn_conv_b=ffn_conv_b, ffn_w_down=ffn_w_down, norm_final=norm_final)
    caches = (cache_a_k, cache_a_v, cache_b_k, cache_b_v, cache_b_kidx,
              state_c_h, state_c_conv, state_ffn_conv)
    return _forward(x_prompt, x_sample, caches, params)
```

```python
import functools
import math

import jax
import jax.numpy as jnp
from jax import lax
from jax.experimental import pallas as pl
from jax.experimental.pallas import tpu as pltpu

F32 = jnp.float32
BF16 = jnp.bfloat16

D_MODEL = 1024
N_LAYERS = 2
CHUNK = 64
CHUNK_SHIFT = 6
HEAD_DIM = 64
H_A = 4
W_A = H_A * 2 * HEAD_DIM
H_B = 4
W_B = H_B * HEAD_DIM
H_IDX = 4
D_IDX = 64
TOPK = 256
W_C = 256
N_GATE_BLOCKS = 4
GATE_BLOCK = W_C // N_GATE_BLOCKS
RG_C = 8.0
CONV_C = 4
D_FF = 3072
CONV_F = 3
EPS = 1e-6

LANES = 128
SUBLANES = 8
MXU_DIM = 256
Q_BLOCK = 256
DIFF_Q_BLOCK = 512
SCAN_ROWS = 128
PROMPT_KEY_TILE = 1024
VMEM_LIMIT = 58 * 2**20
MASKED = -1e30
N_BISECT = 16
FFN_ROW_CHUNKS = 4
Q_SCALE = HEAD_DIM ** -0.5 * math.log2(math.e)

PROJ_W_PAD = 3328
_C_QA, _C_KA, _C_VA = 0, 512, 1024
_C_QB, _C_KB, _C_VB = 1536, 1792, 2048
_C_QI = 2304
_C_KW = 2560
_C_KK = 2688
_C_XC = 2816
_C_GC = 3072


def _cparams(sem):
    return pltpu.CompilerParams(dimension_semantics=sem, vmem_limit_bytes=VMEM_LIMIT)


def _rms(x, g):
    return x * lax.rsqrt(jnp.mean(x * x, axis=-1, keepdims=True) + EPS) * g


def _dot_nt(a, b):
    return lax.dot_general(a, b, (((1,), (1,)), ((), ())), preferred_element_type=F32)


def _round_up(n, m):
    return (n + m - 1) // m * m


def _proj_kernel(x_ref, g_ref, w_ref, qa_ref, qb_ref, qi_ref, ka_ref, va_ref, kb_ref, vb_ref,
                 kw_ref, xc_ref, gc_ref, kab_ref, vab_ref, kbb_ref, vbb_ref, kib_ref):
    h = _rms(x_ref[...], g_ref[...])
    z = jnp.dot(h.astype(BF16), w_ref[...], preferred_element_type=F32)
    qa_ref[...] = (z[:, _C_QA:_C_QA + W_A] * Q_SCALE).astype(BF16)
    qb_ref[...] = (z[:, _C_QB:_C_QB + W_B] * Q_SCALE).astype(BF16)
    qi_ref[...] = z[:, _C_QI:_C_QI + H_IDX * D_IDX].astype(BF16)
    ka = z[:, _C_KA:_C_KA + W_A]
    va = z[:, _C_VA:_C_VA + W_A]
    kb = z[:, _C_KB:_C_KB + W_B]
    vb = z[:, _C_VB:_C_VB + W_B]
    ka_ref[...] = ka
    va_ref[...] = va
    kb_ref[...] = kb
    vb_ref[...] = vb
    kab_ref[...] = ka.astype(BF16)
    vab_ref[...] = va.astype(BF16)
    kbb_ref[...] = kb.astype(BF16)
    vbb_ref[...] = vb.astype(BF16)
    kw_ref[...] = z[:, _C_KW:_C_KW + LANES]
    kib_ref[...] = z[:, _C_KK:_C_KK + LANES].astype(BF16)
    xc_ref[...] = z[:, _C_XC:_C_XC + W_C]
    gc_ref[...] = z[:, _C_GC:_C_GC + W_C]


def _proj(x2d, gain, w_pad):
    t = x2d.shape[0]
    tm = min(512, t)
    widths = [(W_A, BF16), (W_B, BF16), (H_IDX * D_IDX, BF16),
              (W_A, F32), (W_A, F32), (W_B, F32), (W_B, F32),
              (LANES, F32), (W_C, F32), (W_C, F32),
              (W_A, BF16), (W_A, BF16), (W_B, BF16), (W_B, BF16), (LANES, BF16)]
    return pl.pallas_call(
        _proj_kernel,
        grid=(t // tm,),
        in_specs=[pl.BlockSpec((tm, D_MODEL), lambda i: (i, 0)),
                  pl.BlockSpec((1, D_MODEL), lambda i: (0, 0)),
                  pl.BlockSpec((D_MODEL, PROJ_W_PAD), lambda i: (0, 0))],
        out_specs=[pl.BlockSpec((tm, w), lambda i: (i, 0)) for w, _ in widths],
        out_shape=[jax.ShapeDtypeStruct((t, w), d) for w, d in widths],
        compiler_params=_cparams(("parallel",)),
        name="proj",
    )(x2d, gain.reshape(1, D_MODEL), w_pad)


def _tile_bounds(q_start, qb, lk, tk):
    n_full = lax.div(jnp.minimum(q_start + CHUNK, lk), tk)
    n_tiles = lax.div(q_start + qb + tk - 1, tk)
    return n_full, n_tiles


def _admissible(start, tk, q_chunk, lk):
    col = start + lax.broadcasted_iota(jnp.int32, (1, tk), 1)
    return (lax.shift_right_logical(col, CHUNK_SHIFT) <= q_chunk) & (col < lk)


def _fori_pairs(n, body):
    pairs = lax.div(n, 2)

    def two(i, c):
        return body(2 * i + 1, body(2 * i, c))

    lax.fori_loop(0, pairs, two, 0)
    lax.fori_loop(2 * pairs, n, body, 0)


def _lane_max(acc, x):
    for g in range(x.shape[1] // LANES):
        acc = jnp.maximum(acc, x[:, g * LANES:(g + 1) * LANES])
    return acc


def _softmax_value_tile(s_ref, idx, m, v):
    parts = []
    for g in range(s_ref.shape[-1] // LANES):
        s = s_ref[idx + (slice(None), slice(g * LANES, (g + 1) * LANES))]
        parts.append(jnp.exp2((s - m).astype(BF16)))
    v_ones = jnp.concatenate([v, jnp.ones_like(v)], axis=1)
    return jnp.dot(jnp.concatenate(parts, axis=1), v_ones, preferred_element_type=F32)


def _diff_attn_kernel(*refs, past, lk, qb, tk, out_scale, cached):
    if cached:
        (lam_ref, gain_ref, q_ref, kn_ref, vn_ref, kc_ref, vc_ref, o_ref,
         s_ref, mx_ref, acc_ref, k_ref, v_ref) = refs
        n_cache, n_new = kc_ref.shape[0], kn_ref.shape[0]
        for cache_ref, new_ref, dst_ref in ((kc_ref, kn_ref, k_ref), (vc_ref, vn_ref, v_ref)):
            dst_ref[0:n_cache, :] = cache_ref[...].astype(BF16)
            dst_ref[n_cache:n_cache + n_new, :] = new_ref[...]
            if tk > n_cache + n_new:
                dst_ref[n_cache + n_new:, :] = jnp.zeros((tk - n_cache - n_new, LANES), BF16)
    else:
        lam_ref, gain_ref, q_ref, k_ref, v_ref, o_ref, s_ref, mx_ref, acc_ref = refs
    j = pl.program_id(2)
    q = q_ref[...]
    lane = lax.broadcasted_iota(jnp.int32, (1, LANES), 1)
    zero = jnp.zeros_like(q)
    q_half = (jnp.where(lane < HEAD_DIM, q, zero), jnp.where(lane >= HEAD_DIM, q, zero))
    q_start = past + j * qb
    row = lax.broadcasted_iota(jnp.int32, (qb, 1), 0)
    q_chunk = lax.shift_right_logical(q_start + row, CHUNK_SHIFT)
    n_full, n_tiles = _tile_bounds(q_start, qb, lk, tk)

    mx_ref[...] = jnp.full(mx_ref.shape, -jnp.inf, F32)
    acc_ref[...] = jnp.zeros(acc_ref.shape, F32)

    def score_tile(t, masked):
        start = pl.multiple_of(t * tk, tk)
        k = k_ref[pl.ds(start, tk), :]
        if masked:
            ok = _admissible(start, tk, q_chunk, lk)
        for i in range(2):
            s = _dot_nt(q_half[i], k)
            if masked:
                s = jnp.where(ok, s, -jnp.inf)
            s_ref[i, t] = s
            mx_ref[i] = _lane_max(mx_ref[i], s)

    def full_body(t, c):
        score_tile(t, False)
        return c

    def masked_body(t, c):
        score_tile(t, True)
        return c

    _fori_pairs(n_full, full_body)
    lax.fori_loop(n_full, n_tiles, masked_body, 0)

    for i in range(2):
        mx_ref[i] = jnp.broadcast_to(jnp.max(mx_ref[i], axis=1, keepdims=True), (qb, LANES))

    def value_tile(t, c):
        start = pl.multiple_of(t * tk, tk)
        v = v_ref[pl.ds(start, tk), :]
        for i in range(2):
            acc_ref[i] = acc_ref[i] + _softmax_value_tile(s_ref, (i, t), mx_ref[i], v)
        return c

    _fori_pairs(n_tiles, value_tile)

    o = (acc_ref[0, :, :LANES] / acc_ref[0, :, LANES:]
         - lam_ref[...] * (acc_ref[1, :, :LANES] / acc_ref[1, :, LANES:]))
    o_ref[...] = (_rms(o, gain_ref[...]) * out_scale).astype(BF16)


def _diff_attn(q, k, v, lam, gain, *, past, lk, tk, out_scale, cache=None):
    b, l, _ = q.shape
    qb = min(l, DIFF_Q_BLOCK)
    cached = cache is not None
    lkp = tk if cached else k.shape[1]
    assert l % qb == 0 and lkp % tk == 0 and qb % CHUNK == 0
    kern = functools.partial(_diff_attn_kernel, past=past, lk=lk, qb=qb, tk=tk,
                             out_scale=out_scale, cached=cached)
    in_specs = [pl.BlockSpec((1, LANES), lambda bi, h, j: (0, 0)),
                pl.BlockSpec((1, LANES), lambda bi, h, j: (0, 0)),
                pl.BlockSpec((None, qb, LANES), lambda bi, h, j: (bi, j, h)),
                pl.BlockSpec((None, k.shape[1], LANES), lambda bi, h, j: (bi, 0, h)),
                pl.BlockSpec((None, k.shape[1], LANES), lambda bi, h, j: (bi, 0, h))]
    scratch = [pltpu.VMEM((2, lkp // tk, qb, tk), F32),
               pltpu.VMEM((2, qb, LANES), F32), pltpu.VMEM((2, qb, 2 * LANES), F32)]
    operands = [lam, gain, q, k, v]
    if cached:
        cache_k, cache_v, li = cache
        assert l == qb and past + l <= tk and cache_k.shape[2:] == (past, W_A)
        head_rows = pl.BlockSpec((None, None, past, LANES), lambda bi, h, j: (li, bi, 0, h))
        in_specs += [head_rows, head_rows]
        scratch += [pltpu.VMEM((tk, LANES), BF16), pltpu.VMEM((tk, LANES), BF16)]
        operands += [cache_k, cache_v]
    return pl.pallas_call(
        kern,
        grid=(b, H_A, l // qb),
        in_specs=in_specs,
        out_specs=pl.BlockSpec((None, qb, LANES), lambda bi, h, j: (bi, j, h)),
        out_shape=jax.ShapeDtypeStruct((b, l, W_A), BF16),
        scratch_shapes=scratch,
        compiler_params=_cparams(("parallel", "parallel", "parallel")),
        name="diff_attn",
    )(*operands)


def _dsa_kernel(qi_ref, kw_ref, q_ref, ki_ref, k_ref, v_ref, o_ref,
                s_ref, sc_ref, lo_ref, hi_ref, mx_ref, acc_ref,
                *, past, lk, qb, tk, topk, max_steps):
    j = pl.program_id(1)
    q_start = past + j * qb
    n_full, n_tiles = _tile_bounds(q_start, qb, lk, tk)
    n_groups = tk // LANES
    row = lax.broadcasted_iota(jnp.int32, (qb, 1), 0)
    q_chunk = lax.shift_right_logical(q_start + row, CHUNK_SHIFT)
    lane = lax.broadcasted_iota(jnp.int32, (1, LANES), 1)
    low_half = lane < HEAD_DIM
    k_sel = float(topk)

    def wide(x):
        return jnp.broadcast_to(x, (qb, LANES))

    def head_views(x):
        views = []
        for h in range(4):
            pair = x[:, (h // 2) * LANES:(h // 2 + 1) * LANES]
            keep = low_half if h % 2 == 0 else jnp.logical_not(low_half)
            views.append(jnp.where(keep, pair, jnp.zeros_like(pair)))
        return views

    qi_h = head_views(qi_ref[...])
    kw = kw_ref[...]
    w_h = [wide(kw[:, D_IDX + h:D_IDX + h + 1]) for h in range(H_IDX)]
    lo_ref[...] = jnp.full(lo_ref.shape, -jnp.inf, F32)
    hi_ref[...] = jnp.full(hi_ref.shape, -jnp.inf, F32)

    def index_tile(t, masked):
        start = pl.multiple_of(t * tk, tk)
        if masked:
            adm = _admissible(start, tk, q_chunk, lk)
        top2 = lo_ref[...]
        top1 = hi_ref[...]
        for c in range(tk // MXU_DIM):
            ki = ki_ref[pl.ds(start + c * MXU_DIM, MXU_DIM), :]
            rel = [jnp.maximum(_dot_nt(qi_h[h], ki), 0.0) for h in range(H_IDX)]
            for g in range(MXU_DIM // LANES):
                sl = slice(g * LANES, (g + 1) * LANES)
                sc = w_h[0] * rel[0][:, sl]
                for h in range(1, H_IDX):
                    sc = sc + w_h[h] * rel[h][:, sl]
                csl = slice(c * MXU_DIM + g * LANES, c * MXU_DIM + (g + 1) * LANES)
                if masked:
                    sc = jnp.where(adm[:, csl], sc, -jnp.inf)
                s_ref[t, :, csl] = sc
                top2 = jnp.maximum(top2, jnp.minimum(top1, sc))
                top1 = jnp.maximum(top1, sc)
        lo_ref[...] = top2
        hi_ref[...] = top1

    def index_full(t, c):
        index_tile(t, False)
        return c

    def index_masked(t, c):
        index_tile(t, True)
        return c

    _fori_pairs(n_full, index_full)
    lax.fori_loop(n_full, n_tiles, index_masked, 0)
    rb = min(qb, SCAN_ROWS)
    ones_mat = jnp.ones((LANES, LANES), BF16)
    tri_i = lax.broadcasted_iota(jnp.int32, (LANES, 2 * LANES), 0)
    tri_j = lax.broadcasted_iota(jnp.int32, (LANES, 2 * LANES), 1)
    prefix_mat = jnp.where(jnp.logical_or(tri_i <= tri_j, tri_j >= LANES), 1.0, 0.0).astype(BF16)
    assert s_ref.shape[0] * n_groups <= 256

    row_blocks = [slice(r * rb, (r + 1) * rb) for r in range(qb // rb)]
    pos = q_start + lax.broadcasted_iota(jnp.int32, (qb, LANES), 0)
    n_adm = jnp.minimum((lax.shift_right_logical(pos, CHUNK_SHIFT) + 1) * CHUNK, lk)
    few = n_adm <= topk

    def lanes_all(x, reduce):
        return jnp.broadcast_to(reduce(x, axis=1, keepdims=True), (qb, LANES))

    def row_sum(acc):
        return jnp.dot(acc.astype(BF16), ones_mat, preferred_element_type=F32)

    def scan(step, init, *operands):
        outs = []
        for rows in row_blocks:
            ops = [o[rows] for o in operands]

            def body(t, acc, rows=rows, ops=ops):
                for g in range(n_groups):
                    acc = step(acc, s_ref[t, rows, g * LANES:(g + 1) * LANES], *ops)
                return acc

            outs.append(lax.fori_loop(0, n_tiles, body, jnp.full((rb, LANES), init, F32)))
        return outs[0] if len(outs) == 1 else jnp.concatenate(outs, axis=0)

    def count_ge(thr):
        return row_sum(scan(lambda acc, s, t: acc + jnp.where(s >= t, 1.0, 0.0), 0.0, thr))

    def max_below(bound):
        acc = scan(lambda acc, s, b: jnp.maximum(acc, jnp.where(s < b, s, -jnp.inf)), -jnp.inf, bound)
        return lanes_all(acc, jnp.max)

    assert topk <= 2 * LANES
    lane_best = hi_ref[...] if topk <= LANES else lo_ref[...]
    lowest = float(jnp.finfo(jnp.float32).min)
    rmin = jnp.maximum(lanes_all(lane_best, jnp.min), lowest)
    rmax = jnp.maximum(lanes_all(lane_best, jnp.max), lowest)

    c_max = count_ge(rmax)
    top_ties = c_max >= k_sel

    def bisect(_, c):
        lo, hi = c
        mid = 0.5 * lo + 0.5 * hi
        ge = count_ge(mid) >= k_sel
        return jnp.where(ge, mid, lo), jnp.where(ge, hi, mid)

    _, hi = lax.fori_loop(0, N_BISECT, bisect, (rmin, rmax))

    def walk_cond(st):
        it, _, _, _, active = st
        return jnp.logical_and(jnp.max(active) > 0.0, it < max_steps)

    def walk_body(st):
        it, cand, thr, c_thr, active = st
        c = count_ge(cand)
        ok = c >= k_sel
        act = active > 0.0
        hit = jnp.logical_and(act, ok)
        thr = jnp.where(hit, cand, thr)
        c_thr = jnp.where(hit, c, c_thr)
        active = jnp.where(jnp.logical_and(act, jnp.logical_not(ok)), 1.0, 0.0)
        return it + 1, max_below(cand), thr, c_thr, active

    thr0 = jnp.where(few, lowest, jnp.where(top_ties, rmax, lowest))
    c0 = jnp.where(few, k_sel, jnp.where(top_ties, c_max, k_sel))
    active0 = jnp.where(jnp.logical_or(few, top_ties), 0.0, 1.0)
    _, _, thr, c_thr, _ = lax.while_loop(
        walk_cond, walk_body, (jnp.int32(0), max_below(hi), thr0, c0, active0))

    has_excess = jnp.max(jnp.where(c_thr > k_sel, 1.0, 0.0)) > 0.0

    def ranked_bias():
        n_tie = k_sel - row_sum(scan(lambda acc, s, t: acc + jnp.where(s > t, 1.0, 0.0), 0.0, thr))
        for rows in row_blocks:
            thr_r = thr[rows]
            n_tie_r = n_tie[rows]

            def body(t, before, rows=rows, thr_r=thr_r, n_tie_r=n_tie_r):
                for g in range(n_groups):
                    lanes = slice(g * LANES, (g + 1) * LANES)
                    s = s_ref[t, rows, lanes]
                    tie = s == thr_r
                    pr = jnp.dot(jnp.where(tie, 1.0, 0.0).astype(BF16), prefix_mat,
                                 preferred_element_type=F32)
                    rank = before + pr[:, :LANES]
                    tie_bias = jnp.where(rank <= n_tie_r, 0.0, MASKED)
                    s_ref[t, rows, lanes] = jnp.where(s > thr_r, 0.0, jnp.where(tie, tie_bias, MASKED))
                    before = before + pr[:, LANES:]
                return before

            lax.fori_loop(0, n_tiles, body, jnp.zeros((rb, LANES), F32))
        return 0

    def plain_bias():
        for rows in row_blocks:
            thr_r = thr[rows]

            def body(t, c, rows=rows, thr_r=thr_r):
                for g in range(n_groups):
                    lanes = slice(g * LANES, (g + 1) * LANES)
                    s_ref[t, rows, lanes] = jnp.where(s_ref[t, rows, lanes] >= thr_r, 0.0, MASKED)
                return c

            lax.fori_loop(0, n_tiles, body, 0)
        return 0

    lax.cond(has_excess, ranked_bias, plain_bias)

    q_h = head_views(q_ref[...])
    for g in range(H_B // 2):
        lanes = slice(g * LANES, (g + 1) * LANES)
        mx_ref[...] = jnp.full(mx_ref.shape, -jnp.inf, F32)
        acc_ref[...] = jnp.zeros(acc_ref.shape, F32)

        def score_tile(t, c, g=g, lanes=lanes):
            start = pl.multiple_of(t * tk, tk)
            k = k_ref[pl.ds(start, tk), lanes]
            bias = s_ref[t]
            for i in range(2):
                s = _dot_nt(q_h[2 * g + i], k) + bias
                sc_ref[i, t] = s
                mx_ref[i] = _lane_max(mx_ref[i], s)
            return c

        _fori_pairs(n_tiles, score_tile)
        for i in range(2):
            mx_ref[i] = jnp.broadcast_to(jnp.max(mx_ref[i], axis=1, keepdims=True), (qb, LANES))

        def value_tile(t, c, lanes=lanes):
            start = pl.multiple_of(t * tk, tk)
            v = v_ref[pl.ds(start, tk), lanes]
            for i in range(2):
                acc_ref[i] = acc_ref[i] + _softmax_value_tile(sc_ref, (i, t), mx_ref[i], v)
            return c

        _fori_pairs(n_tiles, value_tile)
        even = acc_ref[0, :, :LANES] / acc_ref[0, :, LANES:]
        odd = acc_ref[1, :, :LANES] / acc_ref[1, :, LANES:]
        o_ref[:, lanes] = jnp.where(low_half, even, odd).astype(BF16)


def _dsa(qi, kw, q, ki, k, v, *, past, lk, tk):
    b, l, _ = q.shape
    lkp = k.shape[1]
    qb = min(l, Q_BLOCK)
    assert l % qb == 0 and lkp % tk == 0 and tk % MXU_DIM == 0 and qb % CHUNK == 0
    kern = functools.partial(_dsa_kernel, past=past, lk=lk, qb=qb, tk=tk,
                             topk=min(TOPK, lk // 4), max_steps=lkp)
    return pl.pallas_call(
        kern,
        grid=(b, l // qb),
        in_specs=[pl.BlockSpec((None, qb, H_IDX * D_IDX), lambda bi, j: (bi, j, 0)),
                  pl.BlockSpec((None, qb, LANES), lambda bi, j: (bi, j, 0)),
                  pl.BlockSpec((None, qb, W_B), lambda bi, j: (bi, j, 0)),
                  pl.BlockSpec((None, lkp, LANES), lambda bi, j: (bi, 0, 0)),
                  pl.BlockSpec((None, lkp, W_B), lambda bi, j: (bi, 0, 0)),
                  pl.BlockSpec((None, lkp, W_B), lambda bi, j: (bi, 0, 0))],
        out_specs=pl.BlockSpec((None, qb, W_B), lambda bi, j: (bi, j, 0)),
        out_shape=jax.ShapeDtypeStruct((b, l, W_B), BF16),
        scratch_shapes=[pltpu.VMEM((lkp // tk, qb, tk), F32),
                        pltpu.VMEM((2, lkp // tk, qb, tk), F32),
                        pltpu.VMEM((qb, LANES), F32), pltpu.VMEM((qb, LANES), F32),
                        pltpu.VMEM((2, qb, LANES), F32), pltpu.VMEM((2, qb, 2 * LANES), F32)],
        compiler_params=_cparams(("parallel", "parallel")),
        name="dsa",
    )(qi, kw, q, ki, k, v)


def _shift_rows(x, prev, k):
    rolled = pltpu.roll(x, k, 0)
    row = lax.broadcasted_iota(jnp.int32, (SUBLANES, 1), 0)
    top = jnp.where(row < k, pltpu.roll(prev, k, 0), rolled[0:SUBLANES])
    if x.shape[0] == SUBLANES:
        return top
    return jnp.concatenate([top, rolled[SUBLANES:]], axis=0)


def _rglru_kernel(xc_ref, gc_ref, cst_ref, h0_ref, cw_ref, cb_ref, wg_ref, bg_ref, lam_ref,
                  oc_ref, hl_ref, cn_ref, prev_ref, h_ref, a_ref, b_ref, hs_ref, *, past, tl):
    i = pl.program_id(1)

    @pl.when(i == 0)
    def _():
        prev_ref[...] = jnp.zeros(prev_ref.shape, F32)
        prev_ref[SUBLANES - (CONV_C - 1):SUBLANES, :] = cst_ref[...]
        h_ref[...] = h0_ref[...]

    x = xc_ref[...]
    prev = prev_ref[...]
    cw = cw_ref[...]
    xconv = cw[CONV_C - 1:CONV_C] * x + cb_ref[...]
    for k in range(1, CONV_C):
        xconv = xconv + cw[CONV_C - 1 - k:CONV_C - k] * _shift_rows(x, prev, k)
    prev_ref[...] = x[tl - SUBLANES:tl]

    pre = jnp.dot(xconv.astype(BF16), wg_ref[...], preferred_element_type=F32) + bg_ref[...]
    r = jax.nn.sigmoid(pre[:, :W_C])
    gate_i = jax.nn.sigmoid(pre[:, W_C:])
    neg_lam = -lam_ref[...]
    softplus = jnp.maximum(neg_lam, 0.0) + jnp.log1p(jnp.exp(-jnp.abs(neg_lam)))
    log_a = -RG_C * r * softplus
    pos = past + i * tl + lax.broadcasted_iota(jnp.int32, (tl, 1), 0)
    th = jnp.tanh(log_a)
    mult = jnp.where(pos == 0, 1.0, jnp.sqrt(-2.0 * th / (1.0 - th)))
    a_ref[...] = jnp.exp(log_a)
    b_ref[...] = mult * gate_i * xconv

    def step(t, h):
        h = a_ref[pl.ds(t, 1), :] * h + b_ref[pl.ds(t, 1), :]
        hs_ref[pl.ds(t, 1), :] = h
        return h

    h_last = lax.fori_loop(0, tl, step, h_ref[...], unroll=8)
    h_ref[...] = h_last

    gc = gc_ref[...]
    gelu = 0.5 * gc * (1.0 + jnp.tanh(math.sqrt(2.0 / math.pi) * (gc + 0.044715 * (gc * gc * gc))))
    oc_ref[...] = (hs_ref[...] * gelu).astype(BF16)

    @pl.when(i == pl.num_programs(1) - 1)
    def _():
        hl_ref[...] = h_last
        cn_ref[...] = x[tl - (CONV_C - 1):tl]


def _rglru(xc, gc, conv_state, h0, conv_w, conv_b, w_gate, b_gate, lam, *, past):
    b, l, _ = xc.shape
    tl = min(l, 512)
    assert l >= SUBLANES and l % tl == 0
    kern = functools.partial(_rglru_kernel, past=past, tl=tl)
    const = lambda bi, i: (0, 0)
    return pl.pallas_call(
        kern,
        grid=(b, l // tl),
        in_specs=[pl.BlockSpec((None, tl, W_C), lambda bi, i: (bi, i, 0)),
                  pl.BlockSpec((None, tl, W_C), lambda bi, i: (bi, i, 0)),
                  pl.BlockSpec((None, CONV_C - 1, W_C), lambda bi, i: (bi, 0, 0)),
                  pl.BlockSpec((None, 1, W_C), lambda bi, i: (bi, 0, 0)),
                  pl.BlockSpec((CONV_C, W_C), const),
                  pl.BlockSpec((1, W_C), const),
                  pl.BlockSpec((W_C, 2 * W_C), const),
                  pl.BlockSpec((1, 2 * W_C), const),
                  pl.BlockSpec((1, W_C), const)],
        out_specs=[pl.BlockSpec((None, tl, W_C), lambda bi, i: (bi, i, 0)),
                   pl.BlockSpec((None, 1, W_C), lambda bi, i: (bi, 0, 0)),
                   pl.BlockSpec((None, CONV_C - 1, W_C), lambda bi, i: (bi, 0, 0))],
        out_shape=[jax.ShapeDtypeStruct((b, l, W_C), BF16),
                   jax.ShapeDtypeStruct((b, 1, W_C), F32),
                   jax.ShapeDtypeStruct((b, CONV_C - 1, W_C), F32)],
        scratch_shapes=[pltpu.VMEM((SUBLANES, W_C), F32), pltpu.VMEM((1, W_C), F32),
                        pltpu.VMEM((tl, W_C), F32), pltpu.VMEM((tl, W_C), F32),
                        pltpu.VMEM((tl, W_C), F32)],
        compiler_params=_cparams(("parallel", "arbitrary")),
        name="rglru",
    )(xc, gc, conv_state, h0.reshape(b, 1, W_C), conv_w, conv_b.reshape(1, W_C), w_gate,
      b_gate, lam.reshape(1, W_C))


def _outproj_kernel(x_ref, oa_ref, ob_ref, oc_ref, w_ref, g_ref, x1_ref, hn_ref):
    mix = jnp.dot(oa_ref[...], w_ref[0:W_A, :], preferred_element_type=F32)
    mix = mix + jnp.dot(ob_ref[...], w_ref[W_A:W_A + W_B, :], preferred_element_type=F32)
    mix = mix + jnp.dot(oc_ref[...], w_ref[W_A + W_B:, :], preferred_element_type=F32)
    x1 = x_ref[...] + mix
    x1_ref[...] = x1
    hn_ref[...] = _rms(x1, g_ref[...]).astype(BF16)


def _outproj(x2d, oa, ob, oc, w_out, gain):
    t = x2d.shape[0]
    tm = min(512, t)
    row = lambda w: pl.BlockSpec((tm, w), lambda i: (i, 0))
    return pl.pallas_call(
        _outproj_kernel,
        grid=(t // tm,),
        in_specs=[row(D_MODEL), row(W_A), row(W_B), row(W_C),
                  pl.BlockSpec((D_MODEL, D_MODEL), lambda i: (0, 0)),
                  pl.BlockSpec((1, D_MODEL), lambda i: (0, 0))],
        out_specs=[row(D_MODEL), row(D_MODEL)],
        out_shape=[jax.ShapeDtypeStruct((t, D_MODEL), F32), jax.ShapeDtypeStruct((t, D_MODEL), BF16)],
        compiler_params=_cparams(("parallel",)),
        name="outproj",
    )(x2d, oa, ob, oc, w_out, gain.reshape(1, D_MODEL))


def _ffn_kernel(hn_ref, x1_ref, wu_ref, wg_ref, cwu_ref, cwg_ref, cbu_ref, cbg_ref, wd_ref,
                su_ref, sg_ref, gfin_ref, y_ref, fu_ref, fg_ref,
                acc_ref, au_ref, ag_ref, cu_ref, cg_ref, *, tm, final_norm):
    i = pl.program_id(1)
    s = pl.program_id(2)
    nf = pl.num_programs(2) - 1
    live = s > 0
    fb = jnp.maximum(s - 1, 0)
    row = jnp.where(live, fb, nf)

    @pl.when(s == 0)
    def _():
        au_ref[1] = jnp.zeros(au_ref.shape[1:], F32)
        ag_ref[1] = jnp.zeros(ag_ref.shape[1:], F32)
        cu_ref[nf] = jnp.zeros(cu_ref.shape[1:], F32)
        cg_ref[nf] = jnp.zeros(cg_ref.shape[1:], F32)
        acc_ref[...] = x1_ref[...]

    @pl.when(jnp.logical_and(i == 0, live))
    def _():
        for carry_ref, st_ref in ((cu_ref, su_ref), (cg_ref, sg_ref)):
            carry_ref[fb] = jnp.zeros(carry_ref.shape[1:], F32)
            carry_ref[fb, SUBLANES - (CONV_F - 1):SUBLANES, :] = st_ref[...]

    def step(wslot, rslot):
        au_ref[rslot, 0:SUBLANES, :] = cu_ref[row]
        ag_ref[rslot, 0:SUBLANES, :] = cg_ref[row]
        rc = tm // FFN_ROW_CHUNKS

        def conv(a_ref, cw_ref, cb_ref, r0):
            cw = cw_ref[...]
            y = cw[CONV_F - 1:CONV_F] * a_ref[rslot, SUBLANES + r0:SUBLANES + r0 + rc, :] + cb_ref[...]
            for k in range(1, CONV_F):
                y = y + (cw[CONV_F - 1 - k:CONV_F - k]
                         * a_ref[rslot, SUBLANES - k + r0:SUBLANES - k + r0 + rc, :])
            return y

        for c in range(FFN_ROW_CHUNKS):
            r0 = c * rc
            hn = hn_ref[r0:r0 + rc, :]
            au_ref[wslot, SUBLANES + r0:SUBLANES + r0 + rc, :] = jnp.dot(
                hn, wu_ref[...], preferred_element_type=F32)
            ag_ref[wslot, SUBLANES + r0:SUBLANES + r0 + rc, :] = jnp.dot(
                hn, wg_ref[...], preferred_element_type=F32)
            u = conv(au_ref, cwu_ref, cbu_ref, r0)
            g = conv(ag_ref, cwg_ref, cbg_ref, r0)
            mid = (g * jax.nn.sigmoid(g) * u).astype(BF16)
            contrib = jnp.dot(mid, wd_ref[...], preferred_element_type=F32)
            acc_ref[r0:r0 + rc, :] = acc_ref[r0:r0 + rc, :] + jnp.where(live, contrib, 0.0)

        for a_ref, carry_ref, tail_ref in ((au_ref, cu_ref, fu_ref), (ag_ref, cg_ref, fg_ref)):
            carry_ref[row] = a_ref[rslot, tm:tm + SUBLANES, :]
            tail_ref[row] = a_ref[rslot, tm + SUBLANES - (CONV_F - 1):tm + SUBLANES, :]

    parity = lax.rem(s, 2)

    @pl.when(parity == 0)
    def _():
        step(0, 1)

    @pl.when(parity == 1)
    def _():
        step(1, 0)

    @pl.when(s == nf)
    def _():
        y = acc_ref[...]
        if final_norm:
            y = _rms(y, gfin_ref[...])
        y_ref[...] = y


def _ffn(hn, x1, w_up, conv_w, conv_b, w_down, state, final_gain, *, final_norm):
    b, l, _ = hn.shape
    tm = min(l, 1024)
    tf = 512
    nf = D_FF // tf
    assert l % tm == 0 and tm >= SUBLANES
    kern = functools.partial(_ffn_kernel, tm=tm, final_norm=final_norm)
    conv_b = conv_b.reshape(1, 2 * D_FF)
    up = lambda s: jnp.minimum(s, nf - 1)
    fin = lambda s: jnp.maximum(s - 1, 0)
    tail_spec = pl.BlockSpec((None, nf + 1, CONV_F - 1, tf), lambda bi, i, s: (bi, 0, 0, 0))
    tail_shape = jax.ShapeDtypeStruct((b, nf + 1, CONV_F - 1, tf), F32)
    y, fu, fg = pl.pallas_call(
        kern,
        grid=(b, l // tm, nf + 1),
        in_specs=[pl.BlockSpec((None, tm, D_MODEL), lambda bi, i, s: (bi, i, 0)),
                  pl.BlockSpec((None, tm, D_MODEL), lambda bi, i, s: (bi, i, 0)),
                  pl.BlockSpec((D_MODEL, tf), lambda bi, i, s: (0, up(s))),
                  pl.BlockSpec((D_MODEL, tf), lambda bi, i, s: (0, nf + up(s))),
                  pl.BlockSpec((CONV_F, tf), lambda bi, i, s: (0, fin(s))),
                  pl.BlockSpec((CONV_F, tf), lambda bi, i, s: (0, nf + fin(s))),
                  pl.BlockSpec((1, tf), lambda bi, i, s: (0, fin(s))),
                  pl.BlockSpec((1, tf), lambda bi, i, s: (0, nf + fin(s))),
                  pl.BlockSpec((tf, D_MODEL), lambda bi, i, s: (fin(s), 0)),
                  pl.BlockSpec((None, CONV_F - 1, tf), lambda bi, i, s: (bi, 0, fin(s))),
                  pl.BlockSpec((None, CONV_F - 1, tf), lambda bi, i, s: (bi, 0, nf + fin(s))),
                  pl.BlockSpec((1, D_MODEL), lambda bi, i, s: (0, 0))],
        out_specs=[pl.BlockSpec((None, tm, D_MODEL), lambda bi, i, s: (bi, i, 0)),
                   tail_spec, tail_spec],
        out_shape=[jax.ShapeDtypeStruct((b, l, D_MODEL), F32), tail_shape, tail_shape],
        scratch_shapes=[pltpu.VMEM((tm, D_MODEL), F32),
                        pltpu.VMEM((2, tm + SUBLANES, tf), F32),
                        pltpu.VMEM((2, tm + SUBLANES, tf), F32),
                        pltpu.VMEM((nf + 1, SUBLANES, tf), F32),
                        pltpu.VMEM((nf + 1, SUBLANES, tf), F32)],
        compiler_params=_cparams(("parallel", "arbitrary", "arbitrary")),
        name="ffn",
    )(hn, x1, w_up, w_up, conv_w, conv_w, conv_b, conv_b, w_down, state, state,
      final_gain.reshape(1, D_MODEL))
    flat = lambda a: jnp.swapaxes(a[:, :nf], 1, 2).reshape(b, CONV_F - 1, D_FF)
    return y, jnp.concatenate([flat(fu), flat(fg)], axis=-1)


def _cast_kernel(x_ref, o_ref):
    o_ref[...] = x_ref[...].astype(o_ref.dtype)


def _layer_bf16(w, li):
    _, r, c = w.shape
    tr = 256 if r % 256 == 0 else r
    return pl.pallas_call(
        _cast_kernel,
        grid=(r // tr,),
        in_specs=[pl.BlockSpec((None, tr, c), lambda i: (li, i, 0))],
        out_specs=pl.BlockSpec((tr, c), lambda i: (i, 0)),
        out_shape=jax.ShapeDtypeStruct((r, c), BF16),
        compiler_params=_cparams(("parallel",)),
        name="cast",
    )(w)


def _prep_layer_weights(p, li):
    w_in = _layer_bf16(p["w_in"], li)
    zeros = lambda n: jnp.zeros((D_MODEL, n), w_in.dtype)
    k_idx = w_in[:, 2560:2624]
    w_pad = jnp.concatenate(
        [w_in[:, :2628], zeros(_C_KK - 2628), k_idx, k_idx, w_in[:, 2628:]], axis=1)
    assert w_pad.shape[1] == PROJ_W_PAD

    def block_diag(w):
        out = jnp.zeros((W_C, W_C), w.dtype)
        for n in range(N_GATE_BLOCKS):
            sl = slice(n * GATE_BLOCK, (n + 1) * GATE_BLOCK)
            out = out.at[sl, sl].set(w[n])
        return out

    w_gate = jnp.concatenate([block_diag(p["rg_w_r"][li]), block_diag(p["rg_w_i"][li])], axis=1)
    b_gate = jnp.concatenate([p["rg_b_r"][li].reshape(1, W_C), p["rg_b_i"][li].reshape(1, W_C)], axis=1)
    lam_init = 0.8 - 0.6 * math.exp(-0.3 * li)
    f32 = lambda a: a.astype(F32)
    lam = (jnp.exp(jnp.sum(f32(p["lam_q1"][li]) * f32(p["lam_k1"][li])))
           - jnp.exp(jnp.sum(f32(p["lam_q2"][li]) * f32(p["lam_k2"][li]))) + lam_init)
    return dict(
        norm_mix=p["norm_mix"][li], w_pad=w_pad, lam=jnp.full((1, LANES), lam, F32),
        lam_init=lam_init, diff_gain=p["diff_gain"][li].reshape(1, LANES),
        rg_conv_w=p["rg_conv_w"][li], rg_conv_b=p["rg_conv_b"][li],
        w_gate=w_gate.astype(BF16), b_gate=b_gate, rg_lambda=p["rg_lambda"][li],
        w_out=_layer_bf16(p["w_out"], li), norm_ffn=p["norm_ffn"][li],
        ffn_w_up=_layer_bf16(p["ffn_w_up"], li), ffn_conv_w=p["ffn_conv_w"][li],
        ffn_conv_b=p["ffn_conv_b"][li], ffn_w_down=_layer_bf16(p["ffn_w_down"], li))


def _layer(x, past, attn_cache, w, final_gain, final_norm):
    b, l, _ = x.shape
    _, _, b_k0, b_v0, b_ki0, c_h0, c_cv0, f_cv0 = past
    p_len = 0 if b_k0 is None else b_k0.shape[1]
    lk = p_len + l
    tk = PROMPT_KEY_TILE if p_len == 0 else _round_up(lk, MXU_DIM)
    lkp = _round_up(lk, tk)
    t = b * l

    (qa, qb, qi, ka, va, kb, vb, kw, xc, gc, kab, vab, kbb, vbb, kib) = _proj(
        x.reshape(t, D_MODEL), w["norm_mix"], w["w_pad"])

    def keys(cache, new, dup=False):
        new = new.reshape(b, l, -1)
        parts = []
        if cache is not None:
            c = cache.reshape(b, p_len, -1).astype(BF16)
            parts.append(jnp.concatenate([c, c], axis=-1) if dup else c)
        parts.append(new)
        if lkp > lk:
            parts.append(jnp.zeros((b, lkp - lk, new.shape[-1]), BF16))
        return parts[0] if len(parts) == 1 else jnp.concatenate(parts, axis=1)

    if attn_cache is None:
        k_a, v_a = keys(None, kab), keys(None, vab)
    else:
        k_a, v_a = kab.reshape(b, l, W_A), vab.reshape(b, l, W_A)
    o_a = _diff_attn(qa.reshape(b, l, W_A), k_a, v_a, w["lam"], w["diff_gain"], past=p_len, lk=lk,
                     tk=tk, out_scale=1.0 - w["lam_init"], cache=attn_cache)
    o_b = _dsa(qi.reshape(b, l, -1), kw.reshape(b, l, LANES), qb.reshape(b, l, W_B),
               keys(b_ki0, kib, dup=True), keys(b_k0, kbb), keys(b_v0, vbb),
               past=p_len, lk=lk, tk=tk)
    o_c, h_last, conv_new = _rglru(xc.reshape(b, l, W_C), gc.reshape(b, l, W_C), c_cv0, c_h0,
                                   w["rg_conv_w"], w["rg_conv_b"], w["w_gate"], w["b_gate"],
                                   w["rg_lambda"], past=p_len)
    x1, hn = _outproj(x.reshape(t, D_MODEL), o_a.reshape(t, W_A), o_b.reshape(t, W_B),
                      o_c.reshape(t, W_C), w["w_out"], w["norm_ffn"])
    y, f_buf = _ffn(hn.reshape(b, l, D_MODEL), x1.reshape(b, l, D_MODEL), w["ffn_w_up"],
                    w["ffn_conv_w"], w["ffn_conv_b"], w["ffn_w_down"], f_cv0, final_gain,
                    final_norm=final_norm)
    new = (ka.reshape(b, l, H_A, 2 * HEAD_DIM), va.reshape(b, l, H_A, 2 * HEAD_DIM),
           kb.reshape(b, l, H_B, HEAD_DIM), vb.reshape(b, l, H_B, HEAD_DIM),
           kw[:, :D_IDX].reshape(b, l, D_IDX), h_last.reshape(b, W_C), conv_new, f_buf)
    return y, new


def _trunk(x, past, weights, final_gain):
    states = []
    if past[0] is not None:
        flat_heads = lambda c: c.reshape(c.shape[:3] + (W_A,))
        cache_a = (flat_heads(past[0]), flat_heads(past[1]))
    for li in range(N_LAYERS):
        layer_past = tuple(None if c is None else c[li] for c in past)
        attn_cache = None if past[0] is None else cache_a + (li,)
        x, st = _layer(x, layer_past, attn_cache, weights[li], final_gain,
                       final_norm=(li == N_LAYERS - 1))
        states.append(st)
    return x, states


def _forward(x_prompt, x_sample, caches, params):
    weights = [_prep_layer_weights(params, li) for li in range(N_LAYERS)]
    bp = x_prompt.shape[0]
    dt = x_prompt.dtype
    past_prompt = (None, None, None, None, None,
                   jnp.zeros((N_LAYERS, bp, W_C), dt),
                   jnp.zeros((N_LAYERS, bp, CONV_C - 1, W_C), dt),
                   jnp.zeros((N_LAYERS, bp, CONV_F - 1, 2 * D_FF), dt))
    yp, sp = _trunk(x_prompt, past_prompt, weights, params["norm_final"])
    ys, ss = _trunk(x_sample, caches, weights, params["norm_final"])
    out = [yp, ys]
    for jdx in range(8):
        out.append(jnp.stack([st[jdx] for st in sp], axis=0))
        out.append(jnp.stack([st[jdx] for st in ss], axis=0))
    return tuple(out)


def kernel(x_prompt, x_sample, cache_a_k, cache_a_v, cache_b_k, cache_b_v, cache_b_kidx,
           state_c_h, state_c_conv, state_ffn_conv, norm_mix, w_in, lam_q1, lam_k1, lam_q2,
           lam_k2, diff_gain, rg_conv_w, rg_conv_b, rg_w_r, rg_b_r, rg_w_i, rg_b_i, rg_lambda,
           w_out, norm_ffn, ffn_w_up, ffn_conv_w, ffn_conv_b, ffn_w_down, norm_final):
    params = dict(norm_mix=norm_mix, w_in=w_in, lam_q1=lam_q1, lam_k1=lam_k1, lam_q2=lam_q2,
                  lam_k2=lam_k2, diff_gain=diff_gain, rg_conv_w=rg_conv_w, rg_conv_b=rg_conv_b,
                  rg_w_r=rg_w_r, rg_b_r=rg_b_r, rg_w_i=rg_w_i, rg_b_i=rg_b_i, rg_lambda=rg_lambda,
                  w_out=w_out, norm_ffn=norm_ffn, ffn_w_up=ffn_w_up, ffn_conv_w=ffn_conv_w,
                  ffn_conv_b=ffn_conv_b, ffn_w_down=ffn_w_down, norm_final=norm_final)
    caches = (cache_a_k, cache_a_v, cache_b_k, cache_b_v, cache_b_kidx,
              state_c_h, state_c_conv, state_ffn_conv)
    return _forward(x_prompt, x_sample, caches, params)
```

```python
import functools
import math

import jax
import jax.numpy as jnp
from jax import lax
from jax.experimental import pallas as pl
from jax.experimental.pallas import tpu as pltpu

F32 = jnp.float32
BF16 = jnp.bfloat16

D_MODEL = 1024
N_LAYERS = 2
CHUNK = 64
CHUNK_SHIFT = 6
HEAD_DIM = 64
H_A = 4
W_A = H_A * 2 * HEAD_DIM
H_B = 4
W_B = H_B * HEAD_DIM
H_IDX = 4
D_IDX = 64
TOPK = 256
W_C = 256
N_GATE_BLOCKS = 4
GATE_BLOCK = W_C // N_GATE_BLOCKS
RG_C = 8.0
CONV_C = 4
D_FF = 3072
CONV_F = 3
EPS = 1e-6

LANES = 128
SUBLANES = 8
MXU_DIM = 256
Q_BLOCK = 256
DIFF_Q_BLOCK = 512
SCAN_ROWS = 128
PROMPT_KEY_TILE = 1024
DIFF_KEY_TILE = 512
VMEM_LIMIT = 58 * 2**20
MASKED = -1e30
N_BISECT = 14
FFN_ROW_CHUNKS = 4
Q_SCALE = HEAD_DIM ** -0.5 * math.log2(math.e)

PROJ_W_PAD = 3328
_C_QA, _C_KA, _C_VA = 0, 512, 1024
_C_QB, _C_KB, _C_VB = 1536, 1792, 2048
_C_QI = 2304
_C_KW = 2560
_C_KK = 2688
_C_XC = 2816
_C_GC = 3072


def _cparams(sem):
    return pltpu.CompilerParams(dimension_semantics=sem, vmem_limit_bytes=VMEM_LIMIT)


def _rms(x, g):
    return x * lax.rsqrt(jnp.mean(x * x, axis=-1, keepdims=True) + EPS) * g


def _dot_nt(a, b):
    return lax.dot_general(a, b, (((1,), (1,)), ((), ())), preferred_element_type=F32)


def _round_up(n, m):
    return (n + m - 1) // m * m


def _proj_kernel(x_ref, g_ref, w_ref, qa_ref, qb_ref, qi_ref, ka_ref, va_ref, kb_ref, vb_ref,
                 kw_ref, xc_ref, gc_ref, kab_ref, vab_ref, kbb_ref, vbb_ref, kib_ref):
    h = _rms(x_ref[...], g_ref[...])
    z = jnp.dot(h.astype(BF16), w_ref[...], preferred_element_type=F32)
    qa_ref[...] = (z[:, _C_QA:_C_QA + W_A] * Q_SCALE).astype(BF16)
    qb_ref[...] = (z[:, _C_QB:_C_QB + W_B] * Q_SCALE).astype(BF16)
    qi_ref[...] = z[:, _C_QI:_C_QI + H_IDX * D_IDX].astype(BF16)
    ka = z[:, _C_KA:_C_KA + W_A]
    va = z[:, _C_VA:_C_VA + W_A]
    kb = z[:, _C_KB:_C_KB + W_B]
    vb = z[:, _C_VB:_C_VB + W_B]
    ka_ref[...] = ka
    va_ref[...] = va
    kb_ref[...] = kb
    vb_ref[...] = vb
    kab_ref[...] = ka.astype(BF16)
    vab_ref[...] = va.astype(BF16)
    kbb_ref[...] = kb.astype(BF16)
    vbb_ref[...] = vb.astype(BF16)
    kw_ref[...] = z[:, _C_KW:_C_KW + LANES]
    kib_ref[...] = z[:, _C_KK:_C_KK + LANES].astype(BF16)
    xc_ref[...] = z[:, _C_XC:_C_XC + W_C]
    gc_ref[...] = z[:, _C_GC:_C_GC + W_C]


def _proj(x2d, gain, w_pad):
    t = x2d.shape[0]
    tm = min(512, t)
    widths = [(W_A, BF16), (W_B, BF16), (H_IDX * D_IDX, BF16),
              (W_A, F32), (W_A, F32), (W_B, F32), (W_B, F32),
              (LANES, F32), (W_C, F32), (W_C, F32),
              (W_A, BF16), (W_A, BF16), (W_B, BF16), (W_B, BF16), (LANES, BF16)]
    return pl.pallas_call(
        _proj_kernel,
        grid=(t // tm,),
        in_specs=[pl.BlockSpec((tm, D_MODEL), lambda i: (i, 0)),
                  pl.BlockSpec((1, D_MODEL), lambda i: (0, 0)),
                  pl.BlockSpec((D_MODEL, PROJ_W_PAD), lambda i: (0, 0))],
        out_specs=[pl.BlockSpec((tm, w), lambda i: (i, 0)) for w, _ in widths],
        out_shape=[jax.ShapeDtypeStruct((t, w), d) for w, d in widths],
        compiler_params=_cparams(("parallel",)),
        name="proj",
    )(x2d, gain.reshape(1, D_MODEL), w_pad)


def _tile_bounds(q_start, qb, lk, tk):
    n_full = lax.div(jnp.minimum(q_start + CHUNK, lk), tk)
    n_tiles = lax.div(q_start + qb + tk - 1, tk)
    return n_full, n_tiles


def _admissible(start, tk, q_chunk, lk):
    col = start + lax.broadcasted_iota(jnp.int32, (1, tk), 1)
    return (lax.shift_right_logical(col, CHUNK_SHIFT) <= q_chunk) & (col < lk)


def _fori_pairs(n, body):
    pairs = lax.div(n, 2)

    def two(i, c):
        return body(2 * i + 1, body(2 * i, c))

    lax.fori_loop(0, pairs, two, 0)
    lax.fori_loop(2 * pairs, n, body, 0)


def _lane_max(acc, x):
    for g in range(x.shape[1] // LANES):
        acc = jnp.maximum(acc, x[:, g * LANES:(g + 1) * LANES])
    return acc


def _softmax_value_tile(s_ref, idx, m, v):
    parts = []
    for g in range(s_ref.shape[-1] // LANES):
        s = s_ref[idx + (slice(None), slice(g * LANES, (g + 1) * LANES))]
        parts.append(jnp.exp2((s - m).astype(BF16)))
    v_ones = jnp.concatenate([v, jnp.ones_like(v)], axis=1)
    return jnp.dot(jnp.concatenate(parts, axis=1), v_ones, preferred_element_type=F32)


def _diff_attn_kernel(*refs, past, lk, qb, tk, out_scale, cached):
    if cached:
        (lam_ref, gain_ref, q_ref, kn_ref, vn_ref, kc_ref, vc_ref, o_ref,
         s_ref, mx_ref, acc_ref, k_ref, v_ref) = refs
        n_cache, n_new = kc_ref.shape[0], kn_ref.shape[0]
        for cache_ref, new_ref, dst_ref in ((kc_ref, kn_ref, k_ref), (vc_ref, vn_ref, v_ref)):
            dst_ref[0:n_cache, :] = cache_ref[...].astype(BF16)
            dst_ref[n_cache:n_cache + n_new, :] = new_ref[...]
            if tk > n_cache + n_new:
                dst_ref[n_cache + n_new:, :] = jnp.zeros((tk - n_cache - n_new, LANES), BF16)
    else:
        lam_ref, gain_ref, q_ref, k_ref, v_ref, o_ref, s_ref, mx_ref, acc_ref = refs
    j = pl.program_id(2)
    q = q_ref[...]
    lane = lax.broadcasted_iota(jnp.int32, (1, LANES), 1)
    zero = jnp.zeros_like(q)
    q_half = (jnp.where(lane < HEAD_DIM, q, zero), jnp.where(lane >= HEAD_DIM, q, zero))
    q_start = past + j * qb
    row = lax.broadcasted_iota(jnp.int32, (qb, 1), 0)
    q_chunk = lax.shift_right_logical(q_start + row, CHUNK_SHIFT)
    n_full, n_tiles = _tile_bounds(q_start, qb, lk, tk)

    mx_ref[...] = jnp.full(mx_ref.shape, -jnp.inf, F32)
    acc_ref[...] = jnp.zeros(acc_ref.shape, F32)

    def score_tile(t, masked):
        start = pl.multiple_of(t * tk, tk)
        k = k_ref[pl.ds(start, tk), :]
        if masked:
            ok = _admissible(start, tk, q_chunk, lk)
        for i in range(2):
            s = _dot_nt(q_half[i], k)
            if masked:
                s = jnp.where(ok, s, -jnp.inf)
            s_ref[i, t] = s
            mx_ref[i] = _lane_max(mx_ref[i], s)

    def full_body(t, c):
        score_tile(t, False)
        return c

    def masked_body(t, c):
        score_tile(t, True)
        return c

    _fori_pairs(n_full, full_body)
    lax.fori_loop(n_full, n_tiles, masked_body, 0)

    for i in range(2):
        mx_ref[i] = jnp.broadcast_to(jnp.max(mx_ref[i], axis=1, keepdims=True), (qb, LANES))

    def value_tile(t, c):
        start = pl.multiple_of(t * tk, tk)
        v = v_ref[pl.ds(start, tk), :]
        for i in range(2):
            acc_ref[i] = acc_ref[i] + _softmax_value_tile(s_ref, (i, t), mx_ref[i], v)
        return c

    _fori_pairs(n_tiles, value_tile)

    o = (acc_ref[0, :, :LANES] / acc_ref[0, :, LANES:]
         - lam_ref[...] * (acc_ref[1, :, :LANES] / acc_ref[1, :, LANES:]))
    o_ref[...] = (_rms(o, gain_ref[...]) * out_scale).astype(BF16)


def _diff_attn(q, k, v, lam, gain, *, past, lk, tk, out_scale, cache=None):
    b, l, _ = q.shape
    qb = min(l, DIFF_Q_BLOCK)
    cached = cache is not None
    lkp = tk if cached else k.shape[1]
    assert l % qb == 0 and lkp % tk == 0 and qb % CHUNK == 0
    kern = functools.partial(_diff_attn_kernel, past=past, lk=lk, qb=qb, tk=tk,
                             out_scale=out_scale, cached=cached)
    in_specs = [pl.BlockSpec((1, LANES), lambda bi, h, j: (0, 0)),
                pl.BlockSpec((1, LANES), lambda bi, h, j: (0, 0)),
                pl.BlockSpec((None, qb, LANES), lambda bi, h, j: (bi, j, h)),
                pl.BlockSpec((None, k.shape[1], LANES), lambda bi, h, j: (bi, 0, h)),
                pl.BlockSpec((None, k.shape[1], LANES), lambda bi, h, j: (bi, 0, h))]
    scratch = [pltpu.VMEM((2, lkp // tk, qb, tk), F32),
               pltpu.VMEM((2, qb, LANES), F32), pltpu.VMEM((2, qb, 2 * LANES), F32)]
    operands = [lam, gain, q, k, v]
    if cached:
        cache_k, cache_v, li = cache
        assert l == qb and past + l <= tk and cache_k.shape[2:] == (past, W_A)
        head_rows = pl.BlockSpec((None, None, past, LANES), lambda bi, h, j: (li, bi, 0, h))
        in_specs += [head_rows, head_rows]
        scratch += [pltpu.VMEM((tk, LANES), BF16), pltpu.VMEM((tk, LANES), BF16)]
        operands += [cache_k, cache_v]
    return pl.pallas_call(
        kern,
        grid=(b, H_A, l // qb),
        in_specs=in_specs,
        out_specs=pl.BlockSpec((None, qb, LANES), lambda bi, h, j: (bi, j, h)),
        out_shape=jax.ShapeDtypeStruct((b, l, W_A), BF16),
        scratch_shapes=scratch,
        compiler_params=_cparams(("parallel", "parallel", "parallel")),
        name="diff_attn",
    )(*operands)


def _dsa_kernel(qi_ref, kw_ref, q_ref, ki_ref, k_ref, v_ref, o_ref,
                s_ref, sc_ref, lo_ref, hi_ref, mx_ref, acc_ref,
                *, past, lk, qb, tk, topk, max_steps):
    j = pl.program_id(1)
    q_start = past + j * qb
    n_full, n_tiles = _tile_bounds(q_start, qb, lk, tk)
    n_groups = tk // LANES
    row = lax.broadcasted_iota(jnp.int32, (qb, 1), 0)
    q_chunk = lax.shift_right_logical(q_start + row, CHUNK_SHIFT)
    lane = lax.broadcasted_iota(jnp.int32, (1, LANES), 1)
    low_half = lane < HEAD_DIM
    k_sel = float(topk)

    def wide(x):
        return jnp.broadcast_to(x, (qb, LANES))

    def head_views(x):
        views = []
        for h in range(4):
            pair = x[:, (h // 2) * LANES:(h // 2 + 1) * LANES]
            keep = low_half if h % 2 == 0 else jnp.logical_not(low_half)
            views.append(jnp.where(keep, pair, jnp.zeros_like(pair)))
        return views

    qi_h = head_views(qi_ref[...])
    kw = kw_ref[...]
    w_h = [wide(kw[:, D_IDX + h:D_IDX + h + 1]) for h in range(H_IDX)]
    lo_ref[...] = jnp.full(lo_ref.shape, -jnp.inf, F32)
    hi_ref[...] = jnp.full(hi_ref.shape, -jnp.inf, F32)

    def index_tile(t, masked):
        start = pl.multiple_of(t * tk, tk)
        if masked:
            adm = _admissible(start, tk, q_chunk, lk)
        top2 = lo_ref[...]
        top1 = hi_ref[...]
        for c in range(tk // MXU_DIM):
            ki = ki_ref[pl.ds(start + c * MXU_DIM, MXU_DIM), :]
            rel = [jnp.maximum(_dot_nt(qi_h[h], ki), 0.0) for h in range(H_IDX)]
            for g in range(MXU_DIM // LANES):
                sl = slice(g * LANES, (g + 1) * LANES)
                sc = w_h[0] * rel[0][:, sl]
                for h in range(1, H_IDX):
                    sc = sc + w_h[h] * rel[h][:, sl]
                csl = slice(c * MXU_DIM + g * LANES, c * MXU_DIM + (g + 1) * LANES)
                if masked:
                    sc = jnp.where(adm[:, csl], sc, -jnp.inf)
                s_ref[t, :, csl] = sc
                top2 = jnp.maximum(top2, jnp.minimum(top1, sc))
                top1 = jnp.maximum(top1, sc)
        lo_ref[...] = top2
        hi_ref[...] = top1

    def index_full(t, c):
        index_tile(t, False)
        return c

    def index_masked(t, c):
        index_tile(t, True)
        return c

    _fori_pairs(n_full, index_full)
    lax.fori_loop(n_full, n_tiles, index_masked, 0)
    rb = min(qb, SCAN_ROWS)
    ones_mat = jnp.ones((LANES, LANES), BF16)
    tri_i = lax.broadcasted_iota(jnp.int32, (LANES, 2 * LANES), 0)
    tri_j = lax.broadcasted_iota(jnp.int32, (LANES, 2 * LANES), 1)
    prefix_mat = jnp.where(jnp.logical_or(tri_i <= tri_j, tri_j >= LANES), 1.0, 0.0).astype(BF16)
    assert s_ref.shape[0] * n_groups <= 256

    row_blocks = [slice(r * rb, (r + 1) * rb) for r in range(qb // rb)]
    pos = q_start + lax.broadcasted_iota(jnp.int32, (qb, LANES), 0)
    n_adm = jnp.minimum((lax.shift_right_logical(pos, CHUNK_SHIFT) + 1) * CHUNK, lk)
    few = n_adm <= topk

    def lanes_all(x, reduce):
        return jnp.broadcast_to(reduce(x, axis=1, keepdims=True), (qb, LANES))

    def row_sum(acc):
        return jnp.dot(acc.astype(BF16), ones_mat, preferred_element_type=F32)

    def scan(step, init, *operands):
        outs = []
        for rows in row_blocks:
            ops = [o[rows] for o in operands]

            def body(t, acc, rows=rows, ops=ops):
                for g in range(n_groups):
                    acc = step(acc, s_ref[t, rows, g * LANES:(g + 1) * LANES], *ops)
                return acc

            outs.append(lax.fori_loop(0, n_tiles, body, jnp.full((rb, LANES), init, F32)))
        return outs[0] if len(outs) == 1 else jnp.concatenate(outs, axis=0)

    def count_ge(thr):
        return row_sum(scan(lambda acc, s, t: acc + jnp.where(s >= t, 1.0, 0.0), 0.0, thr))

    def max_below(bound):
        acc = scan(lambda acc, s, b: jnp.maximum(acc, jnp.where(s < b, s, -jnp.inf)), -jnp.inf, bound)
        return lanes_all(acc, jnp.max)

    assert topk <= 2 * LANES
    lane_best = hi_ref[...] if topk <= LANES else lo_ref[...]
    lowest = float(jnp.finfo(jnp.float32).min)
    rmin = jnp.maximum(lanes_all(lane_best, jnp.min), lowest)
    rmax = jnp.maximum(lanes_all(lane_best, jnp.max), lowest)

    c_max = count_ge(rmax)
    top_ties = c_max >= k_sel

    def bisect(_, c):
        lo, hi = c
        mid = 0.5 * lo + 0.5 * hi
        ge = count_ge(mid) >= k_sel
        return jnp.where(ge, mid, lo), jnp.where(ge, hi, mid)

    _, hi = lax.fori_loop(0, N_BISECT, bisect, (rmin, rmax))

    def walk_cond(st):
        it, _, _, _, active = st
        return jnp.logical_and(jnp.max(active) > 0.0, it < max_steps)

    def walk_body(st):
        it, cand, thr, c_thr, active = st
        c = count_ge(cand)
        ok = c >= k_sel
        act = active > 0.0
        hit = jnp.logical_and(act, ok)
        thr = jnp.where(hit, cand, thr)
        c_thr = jnp.where(hit, c, c_thr)
        active = jnp.where(jnp.logical_and(act, jnp.logical_not(ok)), 1.0, 0.0)
        return it + 1, max_below(cand), thr, c_thr, active

    thr0 = jnp.where(few, lowest, jnp.where(top_ties, rmax, lowest))
    c0 = jnp.where(few, k_sel, jnp.where(top_ties, c_max, k_sel))
    active0 = jnp.where(jnp.logical_or(few, top_ties), 0.0, 1.0)
    _, _, thr, c_thr, _ = lax.while_loop(
        walk_cond, walk_body, (jnp.int32(0), max_below(hi), thr0, c0, active0))

    has_excess = jnp.max(jnp.where(c_thr > k_sel, 1.0, 0.0)) > 0.0

    def ranked_bias():
        n_tie = k_sel - row_sum(scan(lambda acc, s, t: acc + jnp.where(s > t, 1.0, 0.0), 0.0, thr))
        for rows in row_blocks:
            thr_r = thr[rows]
            n_tie_r = n_tie[rows]

            def body(t, before, rows=rows, thr_r=thr_r, n_tie_r=n_tie_r):
                for g in range(n_groups):
                    lanes = slice(g * LANES, (g + 1) * LANES)
                    s = s_ref[t, rows, lanes]
                    tie = s == thr_r
                    pr = jnp.dot(jnp.where(tie, 1.0, 0.0).astype(BF16), prefix_mat,
                                 preferred_element_type=F32)
                    rank = before + pr[:, :LANES]
                    tie_bias = jnp.where(rank <= n_tie_r, 0.0, MASKED)
                    s_ref[t, rows, lanes] = jnp.where(s > thr_r, 0.0, jnp.where(tie, tie_bias, MASKED))
                    before = before + pr[:, LANES:]
                return before

            lax.fori_loop(0, n_tiles, body, jnp.zeros((rb, LANES), F32))
        return 0

    def plain_bias():
        for rows in row_blocks:
            thr_r = thr[rows]

            def body(t, c, rows=rows, thr_r=thr_r):
                for g in range(n_groups):
                    lanes = slice(g * LANES, (g + 1) * LANES)
                    s_ref[t, rows, lanes] = jnp.where(s_ref[t, rows, lanes] >= thr_r, 0.0, MASKED)
                return c

            lax.fori_loop(0, n_tiles, body, 0)
        return 0

    lax.cond(has_excess, ranked_bias, plain_bias)

    q_h = head_views(q_ref[...])
    for g in range(H_B // 2):
        lanes = slice(g * LANES, (g + 1) * LANES)
        mx_ref[...] = jnp.full(mx_ref.shape, -jnp.inf, F32)
        acc_ref[...] = jnp.zeros(acc_ref.shape, F32)

        def score_tile(t, c, g=g, lanes=lanes):
            start = pl.multiple_of(t * tk, tk)
            k = k_ref[pl.ds(start, tk), lanes]
            bias = s_ref[t]
            for i in range(2):
                s = _dot_nt(q_h[2 * g + i], k) + bias
                sc_ref[i, t] = s
                mx_ref[i] = _lane_max(mx_ref[i], s)
            return c

        _fori_pairs(n_tiles, score_tile)
        for i in range(2):
            mx_ref[i] = jnp.broadcast_to(jnp.max(mx_ref[i], axis=1, keepdims=True), (qb, LANES))

        def value_tile(t, c, lanes=lanes):
            start = pl.multiple_of(t * tk, tk)
            v = v_ref[pl.ds(start, tk), lanes]
            for i in range(2):
                acc_ref[i] = acc_ref[i] + _softmax_value_tile(sc_ref, (i, t), mx_ref[i], v)
            return c

        _fori_pairs(n_tiles, value_tile)
        even = acc_ref[0, :, :LANES] / acc_ref[0, :, LANES:]
        odd = acc_ref[1, :, :LANES] / acc_ref[1, :, LANES:]
        o_ref[:, lanes] = jnp.where(low_half, even, odd).astype(BF16)


def _dsa(qi, kw, q, ki, k, v, *, past, lk, tk):
    b, l, _ = q.shape
    lkp = k.shape[1]
    qb = min(l, Q_BLOCK)
    assert l % qb == 0 and lkp % tk == 0 and tk % MXU_DIM == 0 and qb % CHUNK == 0
    kern = functools.partial(_dsa_kernel, past=past, lk=lk, qb=qb, tk=tk,
                             topk=min(TOPK, lk // 4), max_steps=lkp)
    return pl.pallas_call(
        kern,
        grid=(b, l // qb),
        in_specs=[pl.BlockSpec((None, qb, H_IDX * D_IDX), lambda bi, j: (bi, j, 0)),
                  pl.BlockSpec((None, qb, LANES), lambda bi, j: (bi, j, 0)),
                  pl.BlockSpec((None, qb, W_B), lambda bi, j: (bi, j, 0)),
                  pl.BlockSpec((None, lkp, LANES), lambda bi, j: (bi, 0, 0)),
                  pl.BlockSpec((None, lkp, W_B), lambda bi, j: (bi, 0, 0)),
                  pl.BlockSpec((None, lkp, W_B), lambda bi, j: (bi, 0, 0))],
        out_specs=pl.BlockSpec((None, qb, W_B), lambda bi, j: (bi, j, 0)),
        out_shape=jax.ShapeDtypeStruct((b, l, W_B), BF16),
        scratch_shapes=[pltpu.VMEM((lkp // tk, qb, tk), F32),
                        pltpu.VMEM((2, lkp // tk, qb, tk), F32),
                        pltpu.VMEM((qb, LANES), F32), pltpu.VMEM((qb, LANES), F32),
                        pltpu.VMEM((2, qb, LANES), F32), pltpu.VMEM((2, qb, 2 * LANES), F32)],
        compiler_params=_cparams(("parallel", "parallel")),
        name="dsa",
    )(qi, kw, q, ki, k, v)


def _shift_rows(x, prev, k):
    rolled = pltpu.roll(x, k, 0)
    row = lax.broadcasted_iota(jnp.int32, (SUBLANES, 1), 0)
    top = jnp.where(row < k, pltpu.roll(prev, k, 0), rolled[0:SUBLANES])
    if x.shape[0] == SUBLANES:
        return top
    return jnp.concatenate([top, rolled[SUBLANES:]], axis=0)


def _rglru_kernel(xc_ref, gc_ref, cst_ref, h0_ref, cw_ref, cb_ref, wg_ref, bg_ref, lam_ref,
                  oc_ref, hl_ref, cn_ref, prev_ref, h_ref, a_ref, b_ref, hs_ref, *, past, tl):
    i = pl.program_id(1)

    @pl.when(i == 0)
    def _():
        prev_ref[...] = jnp.zeros(prev_ref.shape, F32)
        prev_ref[SUBLANES - (CONV_C - 1):SUBLANES, :] = cst_ref[...]
        h_ref[...] = h0_ref[...]

    x = xc_ref[...]
    prev = prev_ref[...]
    cw = cw_ref[...]
    xconv = cw[CONV_C - 1:CONV_C] * x + cb_ref[...]
    for k in range(1, CONV_C):
        xconv = xconv + cw[CONV_C - 1 - k:CONV_C - k] * _shift_rows(x, prev, k)
    prev_ref[...] = x[tl - SUBLANES:tl]

    pre = jnp.dot(xconv.astype(BF16), wg_ref[...], preferred_element_type=F32) + bg_ref[...]
    r = jax.nn.sigmoid(pre[:, :W_C])
    gate_i = jax.nn.sigmoid(pre[:, W_C:])
    neg_lam = -lam_ref[...]
    softplus = jnp.maximum(neg_lam, 0.0) + jnp.log1p(jnp.exp(-jnp.abs(neg_lam)))
    log_a = -RG_C * r * softplus
    pos = past + i * tl + lax.broadcasted_iota(jnp.int32, (tl, 1), 0)
    th = jnp.tanh(log_a)
    mult = jnp.where(pos == 0, 1.0, jnp.sqrt(-2.0 * th / (1.0 - th)))
    a_ref[...] = jnp.exp(log_a)
    b_ref[...] = mult * gate_i * xconv

    def step(t, h):
        h = a_ref[pl.ds(t, 1), :] * h + b_ref[pl.ds(t, 1), :]
        hs_ref[pl.ds(t, 1), :] = h
        return h

    h_last = lax.fori_loop(0, tl, step, h_ref[...], unroll=8)
    h_ref[...] = h_last

    gc = gc_ref[...]
    gelu = 0.5 * gc * (1.0 + jnp.tanh(math.sqrt(2.0 / math.pi) * (gc + 0.044715 * (gc * gc * gc))))
    oc_ref[...] = (hs_ref[...] * gelu).astype(BF16)

    @pl.when(i == pl.num_programs(1) - 1)
    def _():
        hl_ref[...] = h_last
        cn_ref[...] = x[tl - (CONV_C - 1):tl]


def _rglru(xc, gc, conv_state, h0, conv_w, conv_b, w_gate, b_gate, lam, *, past):
    b, l, _ = xc.shape
    tl = min(l, 512)
    assert l >= SUBLANES and l % tl == 0
    kern = functools.partial(_rglru_kernel, past=past, tl=tl)
    const = lambda bi, i: (0, 0)
    return pl.pallas_call(
        kern,
        grid=(b, l // tl),
        in_specs=[pl.BlockSpec((None, tl, W_C), lambda bi, i: (bi, i, 0)),
                  pl.BlockSpec((None, tl, W_C), lambda bi, i: (bi, i, 0)),
                  pl.BlockSpec((None, CONV_C - 1, W_C), lambda bi, i: (bi, 0, 0)),
                  pl.BlockSpec((None, 1, W_C), lambda bi, i: (bi, 0, 0)),
                  pl.BlockSpec((CONV_C, W_C), const),
                  pl.BlockSpec((1, W_C), const),
                  pl.BlockSpec((W_C, 2 * W_C), const),
                  pl.BlockSpec((1, 2 * W_C), const),
                  pl.BlockSpec((1, W_C), const)],
        out_specs=[pl.BlockSpec((None, tl, W_C), lambda bi, i: (bi, i, 0)),
                   pl.BlockSpec((None, 1, W_C), lambda bi, i: (bi, 0, 0)),
                   pl.BlockSpec((None, CONV_C - 1, W_C), lambda bi, i: (bi, 0, 0))],
        out_shape=[jax.ShapeDtypeStruct((b, l, W_C), BF16),
                   jax.ShapeDtypeStruct((b, 1, W_C), F32),
                   jax.ShapeDtypeStruct((b, CONV_C - 1, W_C), F32)],
        scratch_shapes=[pltpu.VMEM((SUBLANES, W_C), F32), pltpu.VMEM((1, W_C), F32),
                        pltpu.VMEM((tl, W_C), F32), pltpu.VMEM((tl, W_C), F32),
                        pltpu.VMEM((tl, W_C), F32)],
        compiler_params=_cparams(("parallel", "arbitrary")),
        name="rglru",
    )(xc, gc, conv_state, h0.reshape(b, 1, W_C), conv_w, conv_b.reshape(1, W_C), w_gate,
      b_gate, lam.reshape(1, W_C))


def _outproj_kernel(x_ref, oa_ref, ob_ref, oc_ref, w_ref, g_ref, x1_ref, hn_ref):
    mix = jnp.dot(oa_ref[...], w_ref[0:W_A, :], preferred_element_type=F32)
    mix = mix + jnp.dot(ob_ref[...], w_ref[W_A:W_A + W_B, :], preferred_element_type=F32)
    mix = mix + jnp.dot(oc_ref[...], w_ref[W_A + W_B:, :], preferred_element_type=F32)
    x1 = x_ref[...] + mix
    x1_ref[...] = x1
    hn_ref[...] = _rms(x1, g_ref[...]).astype(BF16)


def _outproj(x2d, oa, ob, oc, w_out, gain):
    t = x2d.shape[0]
    tm = min(512, t)
    row = lambda w: pl.BlockSpec((tm, w), lambda i: (i, 0))
    return pl.pallas_call(
        _outproj_kernel,
        grid=(t // tm,),
        in_specs=[row(D_MODEL), row(W_A), row(W_B), row(W_C),
                  pl.BlockSpec((D_MODEL, D_MODEL), lambda i: (0, 0)),
                  pl.BlockSpec((1, D_MODEL), lambda i: (0, 0))],
        out_specs=[row(D_MODEL), row(D_MODEL)],
        out_shape=[jax.ShapeDtypeStruct((t, D_MODEL), F32), jax.ShapeDtypeStruct((t, D_MODEL), BF16)],
        compiler_params=_cparams(("parallel",)),
        name="outproj",
    )(x2d, oa, ob, oc, w_out, gain.reshape(1, D_MODEL))


def _ffn_kernel(hn_ref, x1_ref, wu_ref, wg_ref, cwu_ref, cwg_ref, cbu_ref, cbg_ref, wd_ref,
                su_ref, sg_ref, gfin_ref, y_ref, fu_ref, fg_ref,
                acc_ref, au_ref, ag_ref, cu_ref, cg_ref, *, tm, final_norm):
    i = pl.program_id(1)
    s = pl.program_id(2)
    nf = pl.num_programs(2) - 1
    live = s > 0
    fb = jnp.maximum(s - 1, 0)
    row = jnp.where(live, fb, nf)

    @pl.when(s == 0)
    def _():
        au_ref[1] = jnp.zeros(au_ref.shape[1:], F32)
        ag_ref[1] = jnp.zeros(ag_ref.shape[1:], F32)
        cu_ref[nf] = jnp.zeros(cu_ref.shape[1:], F32)
        cg_ref[nf] = jnp.zeros(cg_ref.shape[1:], F32)
        acc_ref[...] = x1_ref[...]

    @pl.when(jnp.logical_and(i == 0, live))
    def _():
        for carry_ref, st_ref in ((cu_ref, su_ref), (cg_ref, sg_ref)):
            carry_ref[fb] = jnp.zeros(carry_ref.shape[1:], F32)
            carry_ref[fb, SUBLANES - (CONV_F - 1):SUBLANES, :] = st_ref[...]

    def step(wslot, rslot):
        au_ref[rslot, 0:SUBLANES, :] = cu_ref[row]
        ag_ref[rslot, 0:SUBLANES, :] = cg_ref[row]
        rc = tm // FFN_ROW_CHUNKS

        def conv(a_ref, cw_ref, cb_ref, r0):
            cw = cw_ref[...]
            y = cw[CONV_F - 1:CONV_F] * a_ref[rslot, SUBLANES + r0:SUBLANES + r0 + rc, :] + cb_ref[...]
            for k in range(1, CONV_F):
                y = y + (cw[CONV_F - 1 - k:CONV_F - k]
                         * a_ref[rslot, SUBLANES - k + r0:SUBLANES - k + r0 + rc, :])
            return y

        for c in range(FFN_ROW_CHUNKS):
            r0 = c * rc
            hn = hn_ref[r0:r0 + rc, :]
            au_ref[wslot, SUBLANES + r0:SUBLANES + r0 + rc, :] = jnp.dot(
                hn, wu_ref[...], preferred_element_type=F32)
            ag_ref[wslot, SUBLANES + r0:SUBLANES + r0 + rc, :] = jnp.dot(
                hn, wg_ref[...], preferred_element_type=F32)
            u = conv(au_ref, cwu_ref, cbu_ref, r0)
            g = conv(ag_ref, cwg_ref, cbg_ref, r0)
            mid = (g * jax.nn.sigmoid(g) * u).astype(BF16)
            contrib = jnp.dot(mid, wd_ref[...], preferred_element_type=F32)
            acc_ref[r0:r0 + rc, :] = acc_ref[r0:r0 + rc, :] + jnp.where(live, contrib, 0.0)

        for a_ref, carry_ref, tail_ref in ((au_ref, cu_ref, fu_ref), (ag_ref, cg_ref, fg_ref)):
            carry_ref[row] = a_ref[rslot, tm:tm + SUBLANES, :]
            tail_ref[row] = a_ref[rslot, tm + SUBLANES - (CONV_F - 1):tm + SUBLANES, :]

    parity = lax.rem(s, 2)

    @pl.when(parity == 0)
    def _():
        step(0, 1)

    @pl.when(parity == 1)
    def _():
        step(1, 0)

    @pl.when(s == nf)
    def _():
        y = acc_ref[...]
        if final_norm:
            y = _rms(y, gfin_ref[...])
        y_ref[...] = y


def _ffn(hn, x1, w_up, conv_w, conv_b, w_down, state, final_gain, *, final_norm):
    b, l, _ = hn.shape
    tm = min(l, 1024)
    tf = 512
    nf = D_FF // tf
    assert l % tm == 0 and tm >= SUBLANES
    kern = functools.partial(_ffn_kernel, tm=tm, final_norm=final_norm)
    conv_b = conv_b.reshape(1, 2 * D_FF)
    up = lambda s: jnp.minimum(s, nf - 1)
    fin = lambda s: jnp.maximum(s - 1, 0)
    tail_spec = pl.BlockSpec((None, nf + 1, CONV_F - 1, tf), lambda bi, i, s: (bi, 0, 0, 0))
    tail_shape = jax.ShapeDtypeStruct((b, nf + 1, CONV_F - 1, tf), F32)
    y, fu, fg = pl.pallas_call(
        kern,
        grid=(b, l // tm, nf + 1),
        in_specs=[pl.BlockSpec((None, tm, D_MODEL), lambda bi, i, s: (bi, i, 0)),
                  pl.BlockSpec((None, tm, D_MODEL), lambda bi, i, s: (bi, i, 0)),
                  pl.BlockSpec((D_MODEL, tf), lambda bi, i, s: (0, up(s))),
                  pl.BlockSpec((D_MODEL, tf), lambda bi, i, s: (0, nf + up(s))),
                  pl.BlockSpec((CONV_F, tf), lambda bi, i, s: (0, fin(s))),
                  pl.BlockSpec((CONV_F, tf), lambda bi, i, s: (0, nf + fin(s))),
                  pl.BlockSpec((1, tf), lambda bi, i, s: (0, fin(s))),
                  pl.BlockSpec((1, tf), lambda bi, i, s: (0, nf + fin(s))),
                  pl.BlockSpec((tf, D_MODEL), lambda bi, i, s: (fin(s), 0)),
                  pl.BlockSpec((None, CONV_F - 1, tf), lambda bi, i, s: (bi, 0, fin(s))),
                  pl.BlockSpec((None, CONV_F - 1, tf), lambda bi, i, s: (bi, 0, nf + fin(s))),
                  pl.BlockSpec((1, D_MODEL), lambda bi, i, s: (0, 0))],
        out_specs=[pl.BlockSpec((None, tm, D_MODEL), lambda bi, i, s: (bi, i, 0)),
                   tail_spec, tail_spec],
        out_shape=[jax.ShapeDtypeStruct((b, l, D_MODEL), F32), tail_shape, tail_shape],
        scratch_shapes=[pltpu.VMEM((tm, D_MODEL), F32),
                        pltpu.VMEM((2, tm + SUBLANES, tf), F32),
                        pltpu.VMEM((2, tm + SUBLANES, tf), F32),
                        pltpu.VMEM((nf + 1, SUBLANES, tf), F32),
                        pltpu.VMEM((nf + 1, SUBLANES, tf), F32)],
        compiler_params=_cparams(("parallel", "arbitrary", "arbitrary")),
        name="ffn",
    )(hn, x1, w_up, w_up, conv_w, conv_w, conv_b, conv_b, w_down, state, state,
      final_gain.reshape(1, D_MODEL))
    flat = lambda a: jnp.swapaxes(a[:, :nf], 1, 2).reshape(b, CONV_F - 1, D_FF)
    return y, jnp.concatenate([flat(fu), flat(fg)], axis=-1)


def _cast_kernel(x_ref, o_ref):
    o_ref[...] = x_ref[...].astype(o_ref.dtype)


def _layer_bf16(w, li):
    _, r, c = w.shape
    tr = 256 if r % 256 == 0 else r
    return pl.pallas_call(
        _cast_kernel,
        grid=(r // tr,),
        in_specs=[pl.BlockSpec((None, tr, c), lambda i: (li, i, 0))],
        out_specs=pl.BlockSpec((tr, c), lambda i: (i, 0)),
        out_shape=jax.ShapeDtypeStruct((r, c), BF16),
        compiler_params=_cparams(("parallel",)),
        name="cast",
    )(w)


def _prep_layer_weights(p, li):
    w_in = _layer_bf16(p["w_in"], li)
    zeros = lambda n: jnp.zeros((D_MODEL, n), w_in.dtype)
    k_idx = w_in[:, 2560:2624]
    w_pad = jnp.concatenate(
        [w_in[:, :2628], zeros(_C_KK - 2628), k_idx, k_idx, w_in[:, 2628:]], axis=1)
    assert w_pad.shape[1] == PROJ_W_PAD

    def block_diag(w):
        out = jnp.zeros((W_C, W_C), w.dtype)
        for n in range(N_GATE_BLOCKS):
            sl = slice(n * GATE_BLOCK, (n + 1) * GATE_BLOCK)
            out = out.at[sl, sl].set(w[n])
        return out

    w_gate = jnp.concatenate([block_diag(p["rg_w_r"][li]), block_diag(p["rg_w_i"][li])], axis=1)
    b_gate = jnp.concatenate([p["rg_b_r"][li].reshape(1, W_C), p["rg_b_i"][li].reshape(1, W_C)], axis=1)
    lam_init = 0.8 - 0.6 * math.exp(-0.3 * li)
    f32 = lambda a: a.astype(F32)
    lam = (jnp.exp(jnp.sum(f32(p["lam_q1"][li]) * f32(p["lam_k1"][li])))
           - jnp.exp(jnp.sum(f32(p["lam_q2"][li]) * f32(p["lam_k2"][li]))) + lam_init)
    return dict(
        norm_mix=p["norm_mix"][li], w_pad=w_pad, lam=jnp.full((1, LANES), lam, F32),
        lam_init=lam_init, diff_gain=p["diff_gain"][li].reshape(1, LANES),
        rg_conv_w=p["rg_conv_w"][li], rg_conv_b=p["rg_conv_b"][li],
        w_gate=w_gate.astype(BF16), b_gate=b_gate, rg_lambda=p["rg_lambda"][li],
        w_out=_layer_bf16(p["w_out"], li), norm_ffn=p["norm_ffn"][li],
        ffn_w_up=_layer_bf16(p["ffn_w_up"], li), ffn_conv_w=p["ffn_conv_w"][li],
        ffn_conv_b=p["ffn_conv_b"][li], ffn_w_down=_layer_bf16(p["ffn_w_down"], li))


def _layer(x, past, attn_cache, w, final_gain, final_norm):
    b, l, _ = x.shape
    _, _, b_k0, b_v0, b_ki0, c_h0, c_cv0, f_cv0 = past
    p_len = 0 if b_k0 is None else b_k0.shape[1]
    lk = p_len + l
    tk = PROMPT_KEY_TILE if p_len == 0 else _round_up(lk, MXU_DIM)
    lkp = _round_up(lk, tk)
    t = b * l

    (qa, qb, qi, ka, va, kb, vb, kw, xc, gc, kab, vab, kbb, vbb, kib) = _proj(
        x.reshape(t, D_MODEL), w["norm_mix"], w["w_pad"])

    def keys(cache, new, dup=False):
        new = new.reshape(b, l, -1)
        parts = []
        if cache is not None:
            c = cache.reshape(b, p_len, -1).astype(BF16)
            parts.append(jnp.concatenate([c, c], axis=-1) if dup else c)
        parts.append(new)
        if lkp > lk:
            parts.append(jnp.zeros((b, lkp - lk, new.shape[-1]), BF16))
        return parts[0] if len(parts) == 1 else jnp.concatenate(parts, axis=1)

    if attn_cache is None:
        k_a, v_a = keys(None, kab), keys(None, vab)
    else:
        k_a, v_a = kab.reshape(b, l, W_A), vab.reshape(b, l, W_A)
    o_a = _diff_attn(qa.reshape(b, l, W_A), k_a, v_a, w["lam"], w["diff_gain"], past=p_len, lk=lk,
                     tk=DIFF_KEY_TILE if p_len == 0 else tk, out_scale=1.0 - w["lam_init"],
                     cache=attn_cache)
    o_b = _dsa(qi.reshape(b, l, -1), kw.reshape(b, l, LANES), qb.reshape(b, l, W_B),
               keys(b_ki0, kib, dup=True), keys(b_k0, kbb), keys(b_v0, vbb),
               past=p_len, lk=lk, tk=tk)
    o_c, h_last, conv_new = _rglru(xc.reshape(b, l, W_C), gc.reshape(b, l, W_C), c_cv0, c_h0,
                                   w["rg_conv_w"], w["rg_conv_b"], w["w_gate"], w["b_gate"],
                                   w["rg_lambda"], past=p_len)
    x1, hn = _outproj(x.reshape(t, D_MODEL), o_a.reshape(t, W_A), o_b.reshape(t, W_B),
                      o_c.reshape(t, W_C), w["w_out"], w["norm_ffn"])
    y, f_buf = _ffn(hn.reshape(b, l, D_MODEL), x1.reshape(b, l, D_MODEL), w["ffn_w_up"],
                    w["ffn_conv_w"], w["ffn_conv_b"], w["ffn_w_down"], f_cv0, final_gain,
                    final_norm=final_norm)
    new = (ka.reshape(b, l, H_A, 2 * HEAD_DIM), va.reshape(b, l, H_A, 2 * HEAD_DIM),
           kb.reshape(b, l, H_B, HEAD_DIM), vb.reshape(b, l, H_B, HEAD_DIM),
           kw[:, :D_IDX].reshape(b, l, D_IDX), h_last.reshape(b, W_C), conv_new, f_buf)
    return y, new


def _trunk(x, past, weights, final_gain):
    states = []
    if past[0] is not None:
        flat_heads = lambda c: c.reshape(c.shape[:3] + (W_A,))
        cache_a = (flat_heads(past[0]), flat_heads(past[1]))
    for li in range(N_LAYERS):
        layer_past = tuple(None if c is None else c[li] for c in past)
        attn_cache = None if past[0] is None else cache_a + (li,)
        x, st = _layer(x, layer_past, attn_cache, weights[li], final_gain,
                       final_norm=(li == N_LAYERS - 1))
        states.append(st)
    return x, states


def _forward(x_prompt, x_sample, caches, params):
    weights = [_prep_layer_weights(params, li) for li in range(N_LAYERS)]
    bp = x_prompt.shape[0]
    dt = x_prompt.dtype
    past_prompt = (None, None, None, None, None,
                   jnp.zeros((N_LAYERS, bp, W_C), dt),
                   jnp.zeros((N_LAYERS, bp, CONV_C - 1, W_C), dt),
                   jnp.zeros((N_LAYERS, bp, CONV_F - 1, 2 * D_FF), dt))
    yp, sp = _trunk(x_prompt, past_prompt, weights, params["norm_final"])
    ys, ss = _trunk(x_sample, caches, weights, params["norm_final"])
    out = [yp, ys]
    for jdx in range(8):
        out.append(jnp.stack([st[jdx] for st in sp], axis=0))
        out.append(jnp.stack([st[jdx] for st in ss], axis=0))
    return tuple(out)


def kernel(x_prompt, x_sample, cache_a_k, cache_a_v, cache_b_k, cache_b_v, cache_b_kidx,
           state_c_h, state_c_conv, state_ffn_conv, norm_mix, w_in, lam_q1, lam_k1, lam_q2,
           lam_k2, diff_gain, rg_conv_w, rg_conv_b, rg_w_r, rg_b_r, rg_w_i, rg_b_i, rg_lambda,
           w_out, norm_ffn, ffn_w_up, ffn_conv_w, ffn_conv_b, ffn_w_down, norm_final):
    params = dict(norm_mix=norm_mix, w_in=w_in, lam_q1=lam_q1, lam_k1=lam_k1, lam_q2=lam_q2,
                  lam_k2=lam_k2, diff_gain=diff_gain, rg_conv_w=rg_conv_w, rg_conv_b=rg_conv_b,
                  rg_w_r=rg_w_r, rg_b_r=rg_b_r, rg_w_i=rg_w_i, rg_b_i=rg_b_i, rg_lambda=rg_lambda,
                  w_out=w_out, norm_ffn=norm_ffn, ffn_w_up=ffn_w_up, ffn_conv_w=ffn_conv_w,
                  ffn_conv_b=ffn_conv_b, ffn_w_down=ffn_w_down, norm_final=norm_final)
    caches = (cache_a_k, cache_a_v, cache_b_k, cache_b_v, cache_b_kidx,
              state_c_h, state_c_conv, state_ffn_conv)
    return _forward(x_prompt, x_sample, caches, params)
```

```python
import functools
import math

import jax
import jax.numpy as jnp
from jax import lax
from jax.experimental import pallas as pl
from jax.experimental.pallas import tpu as pltpu

F32 = jnp.float32
BF16 = jnp.bfloat16

D_MODEL = 1024
N_LAYERS = 2
CHUNK = 64
CHUNK_SHIFT = 6
HEAD_DIM = 64
H_A = 4
W_A = H_A * 2 * HEAD_DIM
H_B = 4
W_B = H_B * HEAD_DIM
H_IDX = 4
D_IDX = 64
TOPK = 256
W_C = 256
N_GATE_BLOCKS = 4
GATE_BLOCK = W_C // N_GATE_BLOCKS
RG_C = 8.0
CONV_C = 4
D_FF = 3072
CONV_F = 3
EPS = 1e-6

LANES = 128
SUBLANES = 8
MXU_DIM = 256
Q_BLOCK = 256
DIFF_Q_BLOCK = 512
SCAN_ROWS = 128
PROMPT_KEY_TILE = 1024
VMEM_LIMIT = 58 * 2**20
MASKED = -1e30
N_BISECT = 14
FFN_ROW_CHUNKS = 4
Q_SCALE = HEAD_DIM ** -0.5 * math.log2(math.e)

PROJ_W_PAD = 3328
_C_QA, _C_KA, _C_VA = 0, 512, 1024
_C_QB, _C_KB, _C_VB = 1536, 1792, 2048
_C_QI = 2304
_C_KW = 2560
_C_KK = 2688
_C_XC = 2816
_C_GC = 3072


def _cparams(sem):
    return pltpu.CompilerParams(dimension_semantics=sem, vmem_limit_bytes=VMEM_LIMIT)


def _rms(x, g):
    return x * lax.rsqrt(jnp.mean(x * x, axis=-1, keepdims=True) + EPS) * g


def _dot_nt(a, b):
    return lax.dot_general(a, b, (((1,), (1,)), ((), ())), preferred_element_type=F32)


def _round_up(n, m):
    return (n + m - 1) // m * m


def _proj_kernel(x_ref, g_ref, w_ref, qa_ref, qb_ref, qi_ref, ka_ref, va_ref, kb_ref, vb_ref,
                 kw_ref, xc_ref, gc_ref, kab_ref, vab_ref, kbb_ref, vbb_ref, kib_ref):
    h = _rms(x_ref[...], g_ref[...])
    z = jnp.dot(h.astype(BF16), w_ref[...], preferred_element_type=F32)
    qa_ref[...] = (z[:, _C_QA:_C_QA + W_A] * Q_SCALE).astype(BF16)
    qb_ref[...] = (z[:, _C_QB:_C_QB + W_B] * Q_SCALE).astype(BF16)
    qi_ref[...] = z[:, _C_QI:_C_QI + H_IDX * D_IDX].astype(BF16)
    ka = z[:, _C_KA:_C_KA + W_A]
    va = z[:, _C_VA:_C_VA + W_A]
    kb = z[:, _C_KB:_C_KB + W_B]
    vb = z[:, _C_VB:_C_VB + W_B]
    ka_ref[...] = ka
    va_ref[...] = va
    kb_ref[...] = kb
    vb_ref[...] = vb
    kab_ref[...] = ka.astype(BF16)
    vab_ref[...] = va.astype(BF16)
    kbb_ref[...] = kb.astype(BF16)
    vbb_ref[...] = vb.astype(BF16)
    kw_ref[...] = z[:, _C_KW:_C_KW + LANES]
    kib_ref[...] = z[:, _C_KK:_C_KK + LANES].astype(BF16)
    xc_ref[...] = z[:, _C_XC:_C_XC + W_C]
    gc_ref[...] = z[:, _C_GC:_C_GC + W_C]


def _proj(x2d, gain, w_pad):
    t = x2d.shape[0]
    tm = min(512, t)
    widths = [(W_A, BF16), (W_B, BF16), (H_IDX * D_IDX, BF16),
              (W_A, F32), (W_A, F32), (W_B, F32), (W_B, F32),
              (LANES, F32), (W_C, F32), (W_C, F32),
              (W_A, BF16), (W_A, BF16), (W_B, BF16), (W_B, BF16), (LANES, BF16)]
    return pl.pallas_call(
        _proj_kernel,
        grid=(t // tm,),
        in_specs=[pl.BlockSpec((tm, D_MODEL), lambda i: (i, 0)),
                  pl.BlockSpec((1, D_MODEL), lambda i: (0, 0)),
                  pl.BlockSpec((D_MODEL, PROJ_W_PAD), lambda i: (0, 0))],
        out_specs=[pl.BlockSpec((tm, w), lambda i: (i, 0)) for w, _ in widths],
        out_shape=[jax.ShapeDtypeStruct((t, w), d) for w, d in widths],
        compiler_params=_cparams(("parallel",)),
        name="proj",
    )(x2d, gain.reshape(1, D_MODEL), w_pad)


def _tile_bounds(q_start, qb, lk, tk):
    n_full = lax.div(jnp.minimum(q_start + CHUNK, lk), tk)
    n_tiles = lax.div(q_start + qb + tk - 1, tk)
    return n_full, n_tiles


def _admissible(start, tk, q_chunk, lk):
    col = start + lax.broadcasted_iota(jnp.int32, (1, tk), 1)
    return (lax.shift_right_logical(col, CHUNK_SHIFT) <= q_chunk) & (col < lk)


def _fori_pairs(n, body, init=0):
    pairs = lax.div(n, 2)

    def two(i, c):
        return body(2 * i + 1, body(2 * i, c))

    return lax.fori_loop(2 * pairs, n, body, lax.fori_loop(0, pairs, two, init))


def _lane_max(acc, x):
    for g in range(x.shape[1] // LANES):
        acc = jnp.maximum(acc, x[:, g * LANES:(g + 1) * LANES])
    return acc


def _softmax_value_tile(s_ref, idx, m, v):
    parts = []
    for g in range(s_ref.shape[-1] // LANES):
        s = s_ref[idx + (slice(None), slice(g * LANES, (g + 1) * LANES))]
        parts.append(jnp.exp2((s - m).astype(BF16)))
    v_ones = jnp.concatenate([v, jnp.ones_like(v)], axis=1)
    return jnp.dot(jnp.concatenate(parts, axis=1), v_ones, preferred_element_type=F32)


def _diff_attn_kernel(*refs, past, lk, qb, tk, out_scale, cached):
    if cached:
        (lam_ref, gain_ref, q_ref, kn_ref, vn_ref, kc_ref, vc_ref, o_ref,
         s_ref, mx_ref, acc_ref, k_ref, v_ref) = refs
        n_cache, n_new = kc_ref.shape[0], kn_ref.shape[0]
        for cache_ref, new_ref, dst_ref in ((kc_ref, kn_ref, k_ref), (vc_ref, vn_ref, v_ref)):
            dst_ref[0:n_cache, :] = cache_ref[...].astype(BF16)
            dst_ref[n_cache:n_cache + n_new, :] = new_ref[...]
            if tk > n_cache + n_new:
                dst_ref[n_cache + n_new:, :] = jnp.zeros((tk - n_cache - n_new, LANES), BF16)
    else:
        lam_ref, gain_ref, q_ref, k_ref, v_ref, o_ref, s_ref, mx_ref, acc_ref = refs
    j = pl.program_id(2)
    q = q_ref[...]
    lane = lax.broadcasted_iota(jnp.int32, (1, LANES), 1)
    zero = jnp.zeros_like(q)
    q_half = (jnp.where(lane < HEAD_DIM, q, zero), jnp.where(lane >= HEAD_DIM, q, zero))
    q_start = past + j * qb
    row = lax.broadcasted_iota(jnp.int32, (qb, 1), 0)
    q_chunk = lax.shift_right_logical(q_start + row, CHUNK_SHIFT)
    n_full, n_tiles = _tile_bounds(q_start, qb, lk, tk)

    mx_ref[...] = jnp.full(mx_ref.shape, -jnp.inf, F32)
    acc_ref[...] = jnp.zeros(acc_ref.shape, F32)

    def score_tile(t, masked):
        start = pl.multiple_of(t * tk, tk)
        k = k_ref[pl.ds(start, tk), :]
        if masked:
            ok = _admissible(start, tk, q_chunk, lk)
        for i in range(2):
            s = _dot_nt(q_half[i], k)
            if masked:
                s = jnp.where(ok, s, -jnp.inf)
            s_ref[i, t] = s
            mx_ref[i] = _lane_max(mx_ref[i], s)

    def full_body(t, c):
        score_tile(t, False)
        return c

    def masked_body(t, c):
        score_tile(t, True)
        return c

    _fori_pairs(n_full, full_body)
    lax.fori_loop(n_full, n_tiles, masked_body, 0)

    for i in range(2):
        mx_ref[i] = jnp.broadcast_to(jnp.max(mx_ref[i], axis=1, keepdims=True), (qb, LANES))

    def value_tile(t, c):
        start = pl.multiple_of(t * tk, tk)
        v = v_ref[pl.ds(start, tk), :]
        for i in range(2):
            acc_ref[i] = acc_ref[i] + _softmax_value_tile(s_ref, (i, t), mx_ref[i], v)
        return c

    _fori_pairs(n_tiles, value_tile)

    o = (acc_ref[0, :, :LANES] / acc_ref[0, :, LANES:]
         - lam_ref[...] * (acc_ref[1, :, :LANES] / acc_ref[1, :, LANES:]))
    o_ref[...] = (_rms(o, gain_ref[...]) * out_scale).astype(BF16)


def _diff_attn(q, k, v, lam, gain, *, past, lk, tk, out_scale, cache=None):
    b, l, _ = q.shape
    qb = min(l, DIFF_Q_BLOCK)
    cached = cache is not None
    lkp = tk if cached else k.shape[1]
    assert l % qb == 0 and lkp % tk == 0 and qb % CHUNK == 0
    kern = functools.partial(_diff_attn_kernel, past=past, lk=lk, qb=qb, tk=tk,
                             out_scale=out_scale, cached=cached)
    in_specs = [pl.BlockSpec((1, LANES), lambda bi, h, j: (0, 0)),
                pl.BlockSpec((1, LANES), lambda bi, h, j: (0, 0)),
                pl.BlockSpec((None, qb, LANES), lambda bi, h, j: (bi, j, h)),
                pl.BlockSpec((None, k.shape[1], LANES), lambda bi, h, j: (bi, 0, h)),
                pl.BlockSpec((None, k.shape[1], LANES), lambda bi, h, j: (bi, 0, h))]
    scratch = [pltpu.VMEM((2, lkp // tk, qb, tk), F32),
               pltpu.VMEM((2, qb, LANES), F32), pltpu.VMEM((2, qb, 2 * LANES), F32)]
    operands = [lam, gain, q, k, v]
    if cached:
        cache_k, cache_v, li = cache
        assert l == qb and past + l <= tk and cache_k.shape[2:] == (past, W_A)
        head_rows = pl.BlockSpec((None, None, past, LANES), lambda bi, h, j: (li, bi, 0, h))
        in_specs += [head_rows, head_rows]
        scratch += [pltpu.VMEM((tk, LANES), BF16), pltpu.VMEM((tk, LANES), BF16)]
        operands += [cache_k, cache_v]
    return pl.pallas_call(
        kern,
        grid=(b, H_A, l // qb),
        in_specs=in_specs,
        out_specs=pl.BlockSpec((None, qb, LANES), lambda bi, h, j: (bi, j, h)),
        out_shape=jax.ShapeDtypeStruct((b, l, W_A), BF16),
        scratch_shapes=scratch,
        compiler_params=_cparams(("parallel", "parallel", "parallel")),
        name="diff_attn",
    )(*operands)


def _dsa_kernel(qi_ref, kw_ref, q_ref, ki_ref, k_ref, v_ref, o_ref,
                s_ref, sc_ref, lo_ref, hi_ref, mx_ref, acc_ref,
                *, past, lk, qb, tk, topk, max_steps):
    j = pl.program_id(1)
    q_start = past + j * qb
    n_full, n_tiles = _tile_bounds(q_start, qb, lk, tk)
    n_groups = tk // LANES
    row = lax.broadcasted_iota(jnp.int32, (qb, 1), 0)
    q_chunk = lax.shift_right_logical(q_start + row, CHUNK_SHIFT)
    lane = lax.broadcasted_iota(jnp.int32, (1, LANES), 1)
    low_half = lane < HEAD_DIM
    k_sel = float(topk)

    def wide(x):
        return jnp.broadcast_to(x, (qb, LANES))

    def head_views(x):
        views = []
        for h in range(4):
            pair = x[:, (h // 2) * LANES:(h // 2 + 1) * LANES]
            keep = low_half if h % 2 == 0 else jnp.logical_not(low_half)
            views.append(jnp.where(keep, pair, jnp.zeros_like(pair)))
        return views

    qi_h = head_views(qi_ref[...])
    kw = kw_ref[...]
    w_h = [wide(kw[:, D_IDX + h:D_IDX + h + 1]) for h in range(H_IDX)]
    lo_ref[...] = jnp.full(lo_ref.shape, -jnp.inf, F32)
    hi_ref[...] = jnp.full(hi_ref.shape, -jnp.inf, F32)

    def index_tile(t, masked):
        start = pl.multiple_of(t * tk, tk)
        if masked:
            adm = _admissible(start, tk, q_chunk, lk)
        top2 = lo_ref[...]
        top1 = hi_ref[...]
        for c in range(tk // MXU_DIM):
            ki = ki_ref[pl.ds(start + c * MXU_DIM, MXU_DIM), :]
            rel = [jnp.maximum(_dot_nt(qi_h[h], ki), 0.0) for h in range(H_IDX)]
            for g in range(MXU_DIM // LANES):
                sl = slice(g * LANES, (g + 1) * LANES)
                sc = w_h[0] * rel[0][:, sl]
                for h in range(1, H_IDX):
                    sc = sc + w_h[h] * rel[h][:, sl]
                csl = slice(c * MXU_DIM + g * LANES, c * MXU_DIM + (g + 1) * LANES)
                if masked:
                    sc = jnp.where(adm[:, csl], sc, -jnp.inf)
                s_ref[t, :, csl] = sc
                top2 = jnp.maximum(top2, jnp.minimum(top1, sc))
                top1 = jnp.maximum(top1, sc)
        lo_ref[...] = top2
        hi_ref[...] = top1

    def index_full(t, c):
        index_tile(t, False)
        return c

    def index_masked(t, c):
        index_tile(t, True)
        return c

    _fori_pairs(n_full, index_full)
    lax.fori_loop(n_full, n_tiles, index_masked, 0)
    rb = min(qb, SCAN_ROWS)
    ones_mat = jnp.ones((LANES, LANES), BF16)
    tri_i = lax.broadcasted_iota(jnp.int32, (LANES, 2 * LANES), 0)
    tri_j = lax.broadcasted_iota(jnp.int32, (LANES, 2 * LANES), 1)
    prefix_mat = jnp.where(jnp.logical_or(tri_i <= tri_j, tri_j >= LANES), 1.0, 0.0).astype(BF16)
    assert s_ref.shape[0] * n_groups <= 256

    row_blocks = [slice(r * rb, (r + 1) * rb) for r in range(qb // rb)]
    pos = q_start + lax.broadcasted_iota(jnp.int32, (qb, LANES), 0)
    n_adm = jnp.minimum((lax.shift_right_logical(pos, CHUNK_SHIFT) + 1) * CHUNK, lk)
    few = n_adm <= topk

    def lanes_all(x, reduce):
        return jnp.broadcast_to(reduce(x, axis=1, keepdims=True), (qb, LANES))

    def row_sum(acc):
        return jnp.dot(acc.astype(BF16), ones_mat, preferred_element_type=F32)

    def scan(step, init, *operands):
        outs = []
        for rows in row_blocks:
            ops = [o[rows] for o in operands]

            def body(t, acc, rows=rows, ops=ops):
                for g in range(n_groups):
                    acc = step(acc, s_ref[t, rows, g * LANES:(g + 1) * LANES], *ops)
                return acc

            outs.append(lax.fori_loop(0, n_tiles, body, jnp.full((rb, LANES), init, F32)))
        return outs[0] if len(outs) == 1 else jnp.concatenate(outs, axis=0)

    def count_ge(thr):
        return row_sum(scan(lambda acc, s, t: acc + jnp.where(s >= t, 1.0, 0.0), 0.0, thr))

    def max_below(bound):
        acc = scan(lambda acc, s, b: jnp.maximum(acc, jnp.where(s < b, s, -jnp.inf)), -jnp.inf, bound)
        return lanes_all(acc, jnp.max)

    assert topk <= 2 * LANES
    lane_best = hi_ref[...] if topk <= LANES else lo_ref[...]
    lowest = float(jnp.finfo(jnp.float32).min)
    rmin = jnp.maximum(lanes_all(lane_best, jnp.min), lowest)
    rmax = jnp.maximum(lanes_all(lane_best, jnp.max), lowest)

    c_max = count_ge(rmax)
    top_ties = c_max >= k_sel

    def bisect(_, c):
        lo, hi = c
        mid = 0.5 * lo + 0.5 * hi
        ge = count_ge(mid) >= k_sel
        return jnp.where(ge, mid, lo), jnp.where(ge, hi, mid)

    _, hi = lax.fori_loop(0, N_BISECT, bisect, (rmin, rmax))

    def walk_cond(st):
        it, _, _, _, active = st
        return jnp.logical_and(jnp.max(active) > 0.0, it < max_steps)

    def walk_body(st):
        it, cand, thr, c_thr, active = st
        c = count_ge(cand)
        ok = c >= k_sel
        act = active > 0.0
        hit = jnp.logical_and(act, ok)
        thr = jnp.where(hit, cand, thr)
        c_thr = jnp.where(hit, c, c_thr)
        active = jnp.where(jnp.logical_and(act, jnp.logical_not(ok)), 1.0, 0.0)
        return it + 1, max_below(cand), thr, c_thr, active

    thr0 = jnp.where(few, lowest, jnp.where(top_ties, rmax, lowest))
    c0 = jnp.where(few, k_sel, jnp.where(top_ties, c_max, k_sel))
    active0 = jnp.where(jnp.logical_or(few, top_ties), 0.0, 1.0)
    _, _, thr, c_thr, _ = lax.while_loop(
        walk_cond, walk_body, (jnp.int32(0), max_below(hi), thr0, c0, active0))

    has_excess = jnp.max(jnp.where(c_thr > k_sel, 1.0, 0.0)) > 0.0

    def ranked_bias():
        n_tie = k_sel - row_sum(scan(lambda acc, s, t: acc + jnp.where(s > t, 1.0, 0.0), 0.0, thr))
        for rows in row_blocks:
            thr_r = thr[rows]
            n_tie_r = n_tie[rows]

            def body(t, before, rows=rows, thr_r=thr_r, n_tie_r=n_tie_r):
                for g in range(n_groups):
                    lanes = slice(g * LANES, (g + 1) * LANES)
                    s = s_ref[t, rows, lanes]
                    tie = s == thr_r
                    pr = jnp.dot(jnp.where(tie, 1.0, 0.0).astype(BF16), prefix_mat,
                                 preferred_element_type=F32)
                    rank = before + pr[:, :LANES]
                    tie_bias = jnp.where(rank <= n_tie_r, 0.0, MASKED)
                    s_ref[t, rows, lanes] = jnp.where(s > thr_r, 0.0, jnp.where(tie, tie_bias, MASKED))
                    before = before + pr[:, LANES:]
                return before

            _fori_pairs(n_tiles, body, jnp.zeros((rb, LANES), F32))
        return 0

    def plain_bias():
        for rows in row_blocks:
            thr_r = thr[rows]

            def body(t, c, rows=rows, thr_r=thr_r):
                for g in range(n_groups):
                    lanes = slice(g * LANES, (g + 1) * LANES)
                    s_ref[t, rows, lanes] = jnp.where(s_ref[t, rows, lanes] >= thr_r, 0.0, MASKED)
                return c

            lax.fori_loop(0, n_tiles, body, 0)
        return 0

    lax.cond(has_excess, ranked_bias, plain_bias)

    q_h = head_views(q_ref[...])
    for g in range(H_B // 2):
        lanes = slice(g * LANES, (g + 1) * LANES)
        mx_ref[...] = jnp.full(mx_ref.shape, -jnp.inf, F32)
        acc_ref[...] = jnp.zeros(acc_ref.shape, F32)

        def score_tile(t, c, g=g, lanes=lanes):
            start = pl.multiple_of(t * tk, tk)
            k = k_ref[pl.ds(start, tk), lanes]
            bias = s_ref[t]
            for i in range(2):
                s = _dot_nt(q_h[2 * g + i], k) + bias
                sc_ref[i, t] = s
                mx_ref[i] = _lane_max(mx_ref[i], s)
            return c

        _fori_pairs(n_tiles, score_tile)
        for i in range(2):
            mx_ref[i] = jnp.broadcast_to(jnp.max(mx_ref[i], axis=1, keepdims=True), (qb, LANES))

        def value_tile(t, c, lanes=lanes):
            start = pl.multiple_of(t * tk, tk)
            v = v_ref[pl.ds(start, tk), lanes]
            for i in range(2):
                acc_ref[i] = acc_ref[i] + _softmax_value_tile(sc_ref, (i, t), mx_ref[i], v)
            return c

        _fori_pairs(n_tiles, value_tile)
        even = acc_ref[0, :, :LANES] / acc_ref[0, :, LANES:]
        odd = acc_ref[1, :, :LANES] / acc_ref[1, :, LANES:]
        o_ref[:, lanes] = jnp.where(low_half, even, odd).astype(BF16)


def _dsa(qi, kw, q, ki, k, v, *, past, lk, tk):
    b, l, _ = q.shape
    lkp = k.shape[1]
    qb = min(l, Q_BLOCK)
    assert l % qb == 0 and lkp % tk == 0 and tk % MXU_DIM == 0 and qb % CHUNK == 0
    kern = functools.partial(_dsa_kernel, past=past, lk=lk, qb=qb, tk=tk,
                             topk=min(TOPK, lk // 4), max_steps=lkp)
    return pl.pallas_call(
        kern,
        grid=(b, l // qb),
        in_specs=[pl.BlockSpec((None, qb, H_IDX * D_IDX), lambda bi, j: (bi, j, 0)),
                  pl.BlockSpec((None, qb, LANES), lambda bi, j: (bi, j, 0)),
                  pl.BlockSpec((None, qb, W_B), lambda bi, j: (bi, j, 0)),
                  pl.BlockSpec((None, lkp, LANES), lambda bi, j: (bi, 0, 0)),
                  pl.BlockSpec((None, lkp, W_B), lambda bi, j: (bi, 0, 0)),
                  pl.BlockSpec((None, lkp, W_B), lambda bi, j: (bi, 0, 0))],
        out_specs=pl.BlockSpec((None, qb, W_B), lambda bi, j: (bi, j, 0)),
        out_shape=jax.ShapeDtypeStruct((b, l, W_B), BF16),
        scratch_shapes=[pltpu.VMEM((lkp // tk, qb, tk), F32),
                        pltpu.VMEM((2, lkp // tk, qb, tk), F32),
                        pltpu.VMEM((qb, LANES), F32), pltpu.VMEM((qb, LANES), F32),
                        pltpu.VMEM((2, qb, LANES), F32), pltpu.VMEM((2, qb, 2 * LANES), F32)],
        compiler_params=_cparams(("parallel", "parallel")),
        name="dsa",
    )(qi, kw, q, ki, k, v)


def _shift_rows(x, prev, k):
    rolled = pltpu.roll(x, k, 0)
    row = lax.broadcasted_iota(jnp.int32, (SUBLANES, 1), 0)
    top = jnp.where(row < k, pltpu.roll(prev, k, 0), rolled[0:SUBLANES])
    if x.shape[0] == SUBLANES:
        return top
    return jnp.concatenate([top, rolled[SUBLANES:]], axis=0)


def _rglru_kernel(xc_ref, gc_ref, cst_ref, h0_ref, cw_ref, cb_ref, wg_ref, bg_ref, lam_ref,
                  oc_ref, hl_ref, cn_ref, prev_ref, h_ref, a_ref, b_ref, hs_ref, *, past, tl):
    i = pl.program_id(1)

    @pl.when(i == 0)
    def _():
        prev_ref[...] = jnp.zeros(prev_ref.shape, F32)
        prev_ref[SUBLANES - (CONV_C - 1):SUBLANES, :] = cst_ref[...]
        h_ref[...] = h0_ref[...]

    x = xc_ref[...]
    prev = prev_ref[...]
    cw = cw_ref[...]
    xconv = cw[CONV_C - 1:CONV_C] * x + cb_ref[...]
    for k in range(1, CONV_C):
        xconv = xconv + cw[CONV_C - 1 - k:CONV_C - k] * _shift_rows(x, prev, k)
    prev_ref[...] = x[tl - SUBLANES:tl]

    pre = jnp.dot(xconv.astype(BF16), wg_ref[...], preferred_element_type=F32) + bg_ref[...]
    r = jax.nn.sigmoid(pre[:, :W_C])
    gate_i = jax.nn.sigmoid(pre[:, W_C:])
    neg_lam = -lam_ref[...]
    softplus = jnp.maximum(neg_lam, 0.0) + jnp.log1p(jnp.exp(-jnp.abs(neg_lam)))
    log_a = -RG_C * r * softplus
    pos = past + i * tl + lax.broadcasted_iota(jnp.int32, (tl, 1), 0)
    th = jnp.tanh(log_a)
    mult = jnp.where(pos == 0, 1.0, jnp.sqrt(-2.0 * th / (1.0 - th)))
    a_ref[...] = jnp.exp(log_a)
    b_ref[...] = mult * gate_i * xconv

    def step(t, h):
        h = a_ref[pl.ds(t, 1), :] * h + b_ref[pl.ds(t, 1), :]
        hs_ref[pl.ds(t, 1), :] = h
        return h

    h_last = lax.fori_loop(0, tl, step, h_ref[...], unroll=8)
    h_ref[...] = h_last

    gc = gc_ref[...]
    gelu = 0.5 * gc * (1.0 + jnp.tanh(math.sqrt(2.0 / math.pi) * (gc + 0.044715 * (gc * gc * gc))))
    oc_ref[...] = (hs_ref[...] * gelu).astype(BF16)

    @pl.when(i == pl.num_programs(1) - 1)
    def _():
        hl_ref[...] = h_last
        cn_ref[...] = x[tl - (CONV_C - 1):tl]


def _rglru(xc, gc, conv_state, h0, conv_w, conv_b, w_gate, b_gate, lam, *, past):
    b, l, _ = xc.shape
    tl = min(l, 512)
    assert l >= SUBLANES and l % tl == 0
    kern = functools.partial(_rglru_kernel, past=past, tl=tl)
    const = lambda bi, i: (0, 0)
    return pl.pallas_call(
        kern,
        grid=(b, l // tl),
        in_specs=[pl.BlockSpec((None, tl, W_C), lambda bi, i: (bi, i, 0)),
                  pl.BlockSpec((None, tl, W_C), lambda bi, i: (bi, i, 0)),
                  pl.BlockSpec((None, CONV_C - 1, W_C), lambda bi, i: (bi, 0, 0)),
                  pl.BlockSpec((None, 1, W_C), lambda bi, i: (bi, 0, 0)),
                  pl.BlockSpec((CONV_C, W_C), const),
                  pl.BlockSpec((1, W_C), const),
                  pl.BlockSpec((W_C, 2 * W_C), const),
                  pl.BlockSpec((1, 2 * W_C), const),
                  pl.BlockSpec((1, W_C), const)],
        out_specs=[pl.BlockSpec((None, tl, W_C), lambda bi, i: (bi, i, 0)),
                   pl.BlockSpec((None, 1, W_C), lambda bi, i: (bi, 0, 0)),
                   pl.BlockSpec((None, CONV_C - 1, W_C), lambda bi, i: (bi, 0, 0))],
        out_shape=[jax.ShapeDtypeStruct((b, l, W_C), BF16),
                   jax.ShapeDtypeStruct((b, 1, W_C), F32),
                   jax.ShapeDtypeStruct((b, CONV_C - 1, W_C), F32)],
        scratch_shapes=[pltpu.VMEM((SUBLANES, W_C), F32), pltpu.VMEM((1, W_C), F32),
                        pltpu.VMEM((tl, W_C), F32), pltpu.VMEM((tl, W_C), F32),
                        pltpu.VMEM((tl, W_C), F32)],
        compiler_params=_cparams(("parallel", "arbitrary")),
        name="rglru",
    )(xc, gc, conv_state, h0.reshape(b, 1, W_C), conv_w, conv_b.reshape(1, W_C), w_gate,
      b_gate, lam.reshape(1, W_C))


def _outproj_kernel(x_ref, oa_ref, ob_ref, oc_ref, w_ref, g_ref, x1_ref, hn_ref):
    mix = jnp.dot(oa_ref[...], w_ref[0:W_A, :], preferred_element_type=F32)
    mix = mix + jnp.dot(ob_ref[...], w_ref[W_A:W_A + W_B, :], preferred_element_type=F32)
    mix = mix + jnp.dot(oc_ref[...], w_ref[W_A + W_B:, :], preferred_element_type=F32)
    x1 = x_ref[...] + mix
    x1_ref[...] = x1
    hn_ref[...] = _rms(x1, g_ref[...]).astype(BF16)


def _outproj(x2d, oa, ob, oc, w_out, gain):
    t = x2d.shape[0]
    tm = min(512, t)
    row = lambda w: pl.BlockSpec((tm, w), lambda i: (i, 0))
    return pl.pallas_call(
        _outproj_kernel,
        grid=(t // tm,),
        in_specs=[row(D_MODEL), row(W_A), row(W_B), row(W_C),
                  pl.BlockSpec((D_MODEL, D_MODEL), lambda i: (0, 0)),
                  pl.BlockSpec((1, D_MODEL), lambda i: (0, 0))],
        out_specs=[row(D_MODEL), row(D_MODEL)],
        out_shape=[jax.ShapeDtypeStruct((t, D_MODEL), F32), jax.ShapeDtypeStruct((t, D_MODEL), BF16)],
        compiler_params=_cparams(("parallel",)),
        name="outproj",
    )(x2d, oa, ob, oc, w_out, gain.reshape(1, D_MODEL))


def _ffn_kernel(hn_ref, x1_ref, wu_ref, wg_ref, cwu_ref, cwg_ref, cbu_ref, cbg_ref, wd_ref,
                su_ref, sg_ref, gfin_ref, y_ref, fu_ref, fg_ref,
                acc_ref, au_ref, ag_ref, cu_ref, cg_ref, *, tm, final_norm):
    i = pl.program_id(1)
    s = pl.program_id(2)
    nf = pl.num_programs(2) - 1
    live = s > 0
    fb = jnp.maximum(s - 1, 0)
    row = jnp.where(live, fb, nf)

    @pl.when(s == 0)
    def _():
        au_ref[1] = jnp.zeros(au_ref.shape[1:], F32)
        ag_ref[1] = jnp.zeros(ag_ref.shape[1:], F32)
        cu_ref[nf] = jnp.zeros(cu_ref.shape[1:], F32)
        cg_ref[nf] = jnp.zeros(cg_ref.shape[1:], F32)
        acc_ref[...] = x1_ref[...]

    @pl.when(jnp.logical_and(i == 0, live))
    def _():
        for carry_ref, st_ref in ((cu_ref, su_ref), (cg_ref, sg_ref)):
            carry_ref[fb] = jnp.zeros(carry_ref.shape[1:], F32)
            carry_ref[fb, SUBLANES - (CONV_F - 1):SUBLANES, :] = st_ref[...]

    def step(wslot, rslot):
        au_ref[rslot, 0:SUBLANES, :] = cu_ref[row]
        ag_ref[rslot, 0:SUBLANES, :] = cg_ref[row]
        rc = tm // FFN_ROW_CHUNKS

        def conv(a_ref, cw_ref, cb_ref, r0):
            cw = cw_ref[...]
            y = cw[CONV_F - 1:CONV_F] * a_ref[rslot, SUBLANES + r0:SUBLANES + r0 + rc, :] + cb_ref[...]
            for k in range(1, CONV_F):
                y = y + (cw[CONV_F - 1 - k:CONV_F - k]
                         * a_ref[rslot, SUBLANES - k + r0:SUBLANES - k + r0 + rc, :])
            return y

        for c in range(FFN_ROW_CHUNKS):
            r0 = c * rc
            hn = hn_ref[r0:r0 + rc, :]
            au_ref[wslot, SUBLANES + r0:SUBLANES + r0 + rc, :] = jnp.dot(
                hn, wu_ref[...], preferred_element_type=F32)
            ag_ref[wslot, SUBLANES + r0:SUBLANES + r0 + rc, :] = jnp.dot(
                hn, wg_ref[...], preferred_element_type=F32)
            u = conv(au_ref, cwu_ref, cbu_ref, r0)
            g = conv(ag_ref, cwg_ref, cbg_ref, r0)
            mid = (g * jax.nn.sigmoid(g) * u).astype(BF16)
            contrib = jnp.dot(mid, wd_ref[...], preferred_element_type=F32)
            acc_ref[r0:r0 + rc, :] = acc_ref[r0:r0 + rc, :] + jnp.where(live, contrib, 0.0)

        for a_ref, carry_ref, tail_ref in ((au_ref, cu_ref, fu_ref), (ag_ref, cg_ref, fg_ref)):
            carry_ref[row] = a_ref[rslot, tm:tm + SUBLANES, :]
            tail_ref[row] = a_ref[rslot, tm + SUBLANES - (CONV_F - 1):tm + SUBLANES, :]

    parity = lax.rem(s, 2)

    @pl.when(parity == 0)
    def _():
        step(0, 1)

    @pl.when(parity == 1)
    def _():
        step(1, 0)

    @pl.when(s == nf)
    def _():
        y = acc_ref[...]
        if final_norm:
            y = _rms(y, gfin_ref[...])
        y_ref[...] = y


def _ffn(hn, x1, w_up, conv_w, conv_b, w_down, state, final_gain, *, final_norm):
    b, l, _ = hn.shape
    tm = min(l, 1024)
    tf = 512
    nf = D_FF // tf
    assert l % tm == 0 and tm >= SUBLANES
    kern = functools.partial(_ffn_kernel, tm=tm, final_norm=final_norm)
    conv_b = conv_b.reshape(1, 2 * D_FF)
    up = lambda s: jnp.minimum(s, nf - 1)
    fin = lambda s: jnp.maximum(s - 1, 0)
    tail_spec = pl.BlockSpec((None, nf + 1, CONV_F - 1, tf), lambda bi, i, s: (bi, 0, 0, 0))
    tail_shape = jax.ShapeDtypeStruct((b, nf + 1, CONV_F - 1, tf), F32)
    y, fu, fg = pl.pallas_call(
        kern,
        grid=(b, l // tm, nf + 1),
        in_specs=[pl.BlockSpec((None, tm, D_MODEL), lambda bi, i, s: (bi, i, 0)),
                  pl.BlockSpec((None, tm, D_MODEL), lambda bi, i, s: (bi, i, 0)),
                  pl.BlockSpec((D_MODEL, tf), lambda bi, i, s: (0, up(s))),
                  pl.BlockSpec((D_MODEL, tf), lambda bi, i, s: (0, nf + up(s))),
                  pl.BlockSpec((CONV_F, tf), lambda bi, i, s: (0, fin(s))),
                  pl.BlockSpec((CONV_F, tf), lambda bi, i, s: (0, nf + fin(s))),
                  pl.BlockSpec((1, tf), lambda bi, i, s: (0, fin(s))),
                  pl.BlockSpec((1, tf), lambda bi, i, s: (0, nf + fin(s))),
                  pl.BlockSpec((tf, D_MODEL), lambda bi, i, s: (fin(s), 0)),
                  pl.BlockSpec((None, CONV_F - 1, tf), lambda bi, i, s: (bi, 0, fin(s))),
                  pl.BlockSpec((None, CONV_F - 1, tf), lambda bi, i, s: (bi, 0, nf + fin(s))),
                  pl.BlockSpec((1, D_MODEL), lambda bi, i, s: (0, 0))],
        out_specs=[pl.BlockSpec((None, tm, D_MODEL), lambda bi, i, s: (bi, i, 0)),
                   tail_spec, tail_spec],
        out_shape=[jax.ShapeDtypeStruct((b, l, D_MODEL), F32), tail_shape, tail_shape],
        scratch_shapes=[pltpu.VMEM((tm, D_MODEL), F32),
                        pltpu.VMEM((2, tm + SUBLANES, tf), F32),
                        pltpu.VMEM((2, tm + SUBLANES, tf), F32),
                        pltpu.VMEM((nf + 1, SUBLANES, tf), F32),
                        pltpu.VMEM((nf + 1, SUBLANES, tf), F32)],
        compiler_params=_cparams(("parallel", "arbitrary", "arbitrary")),
        name="ffn",
    )(hn, x1, w_up, w_up, conv_w, conv_w, conv_b, conv_b, w_down, state, state,
      final_gain.reshape(1, D_MODEL))
    flat = lambda a: jnp.swapaxes(a[:, :nf], 1, 2).reshape(b, CONV_F - 1, D_FF)
    return y, jnp.concatenate([flat(fu), flat(fg)], axis=-1)


def _cast_kernel(x_ref, o_ref):
    o_ref[...] = x_ref[...].astype(o_ref.dtype)


def _layer_bf16(w, li):
    _, r, c = w.shape
    tr = 256 if r % 256 == 0 else r
    return pl.pallas_call(
        _cast_kernel,
        grid=(r // tr,),
        in_specs=[pl.BlockSpec((None, tr, c), lambda i: (li, i, 0))],
        out_specs=pl.BlockSpec((tr, c), lambda i: (i, 0)),
        out_shape=jax.ShapeDtypeStruct((r, c), BF16),
        compiler_params=_cparams(("parallel",)),
        name="cast",
    )(w)


def _prep_layer_weights(p, li):
    w_in = _layer_bf16(p["w_in"], li)
    zeros = lambda n: jnp.zeros((D_MODEL, n), w_in.dtype)
    k_idx = w_in[:, 2560:2624]
    w_pad = jnp.concatenate(
        [w_in[:, :2628], zeros(_C_KK - 2628), k_idx, k_idx, w_in[:, 2628:]], axis=1)
    assert w_pad.shape[1] == PROJ_W_PAD

    def block_diag(w):
        out = jnp.zeros((W_C, W_C), w.dtype)
        for n in range(N_GATE_BLOCKS):
            sl = slice(n * GATE_BLOCK, (n + 1) * GATE_BLOCK)
            out = out.at[sl, sl].set(w[n])
        return out

    w_gate = jnp.concatenate([block_diag(p["rg_w_r"][li]), block_diag(p["rg_w_i"][li])], axis=1)
    b_gate = jnp.concatenate([p["rg_b_r"][li].reshape(1, W_C), p["rg_b_i"][li].reshape(1, W_C)], axis=1)
    lam_init = 0.8 - 0.6 * math.exp(-0.3 * li)
    f32 = lambda a: a.astype(F32)
    lam = (jnp.exp(jnp.sum(f32(p["lam_q1"][li]) * f32(p["lam_k1"][li])))
           - jnp.exp(jnp.sum(f32(p["lam_q2"][li]) * f32(p["lam_k2"][li]))) + lam_init)
    return dict(
        norm_mix=p["norm_mix"][li], w_pad=w_pad, lam=jnp.full((1, LANES), lam, F32),
        lam_init=lam_init, diff_gain=p["diff_gain"][li].reshape(1, LANES),
        rg_conv_w=p["rg_conv_w"][li], rg_conv_b=p["rg_conv_b"][li],
        w_gate=w_gate.astype(BF16), b_gate=b_gate, rg_lambda=p["rg_lambda"][li],
        w_out=_layer_bf16(p["w_out"], li), norm_ffn=p["norm_ffn"][li],
        ffn_w_up=_layer_bf16(p["ffn_w_up"], li), ffn_conv_w=p["ffn_conv_w"][li],
        ffn_conv_b=p["ffn_conv_b"][li], ffn_w_down=_layer_bf16(p["ffn_w_down"], li))


def _layer(x, past, attn_cache, w, final_gain, final_norm):
    b, l, _ = x.shape
    _, _, b_k0, b_v0, b_ki0, c_h0, c_cv0, f_cv0 = past
    p_len = 0 if b_k0 is None else b_k0.shape[1]
    lk = p_len + l
    tk = PROMPT_KEY_TILE if p_len == 0 else _round_up(lk, MXU_DIM)
    lkp = _round_up(lk, tk)
    t = b * l

    (qa, qb, qi, ka, va, kb, vb, kw, xc, gc, kab, vab, kbb, vbb, kib) = _proj(
        x.reshape(t, D_MODEL), w["norm_mix"], w["w_pad"])

    def keys(cache, new, dup=False):
        new = new.reshape(b, l, -1)
        parts = []
        if cache is not None:
            c = cache.reshape(b, p_len, -1).astype(BF16)
            parts.append(jnp.concatenate([c, c], axis=-1) if dup else c)
        parts.append(new)
        if lkp > lk:
            parts.append(jnp.zeros((b, lkp - lk, new.shape[-1]), BF16))
        return parts[0] if len(parts) == 1 else jnp.concatenate(parts, axis=1)

    if attn_cache is None:
        k_a, v_a = keys(None, kab), keys(None, vab)
    else:
        k_a, v_a = kab.reshape(b, l, W_A), vab.reshape(b, l, W_A)
    o_a = _diff_attn(qa.reshape(b, l, W_A), k_a, v_a, w["lam"], w["diff_gain"], past=p_len, lk=lk,
                     tk=tk, out_scale=1.0 - w["lam_init"], cache=attn_cache)
    o_b = _dsa(qi.reshape(b, l, -1), kw.reshape(b, l, LANES), qb.reshape(b, l, W_B),
               keys(b_ki0, kib, dup=True), keys(b_k0, kbb), keys(b_v0, vbb),
               past=p_len, lk=lk, tk=tk)
    o_c, h_last, conv_new = _rglru(xc.reshape(b, l, W_C), gc.reshape(b, l, W_C), c_cv0, c_h0,
                                   w["rg_conv_w"], w["rg_conv_b"], w["w_gate"], w["b_gate"],
                                   w["rg_lambda"], past=p_len)
    x1, hn = _outproj(x.reshape(t, D_MODEL), o_a.reshape(t, W_A), o_b.reshape(t, W_B),
                      o_c.reshape(t, W_C), w["w_out"], w["norm_ffn"])
    y, f_buf = _ffn(hn.reshape(b, l, D_MODEL), x1.reshape(b, l, D_MODEL), w["ffn_w_up"],
                    w["ffn_conv_w"], w["ffn_conv_b"], w["ffn_w_down"], f_cv0, final_gain,
                    final_norm=final_norm)
    new = (ka.reshape(b, l, H_A, 2 * HEAD_DIM), va.reshape(b, l, H_A, 2 * HEAD_DIM),
           kb.reshape(b, l, H_B, HEAD_DIM), vb.reshape(b, l, H_B, HEAD_DIM),
           kw[:, :D_IDX].reshape(b, l, D_IDX), h_last.reshape(b, W_C), conv_new, f_buf)
    return y, new


def _trunk(x, past, weights, final_gain):
    states = []
    if past[0] is not None:
        flat_heads = lambda c: c.reshape(c.shape[:3] + (W_A,))
        cache_a = (flat_heads(past[0]), flat_heads(past[1]))
    for li in range(N_LAYERS):
        layer_past = tuple(None if c is None else c[li] for c in past)
        attn_cache = None if past[0] is None else cache_a + (li,)
        x, st = _layer(x, layer_past, attn_cache, weights[li], final_gain,
                       final_norm=(li == N_LAYERS - 1))
        states.append(st)
    return x, states


def _forward(x_prompt, x_sample, caches, params):
    weights = [_prep_layer_weights(params, li) for li in range(N_LAYERS)]
    bp = x_prompt.shape[0]
    dt = x_prompt.dtype
    past_prompt = (None, None, None, None, None,
                   jnp.zeros((N_LAYERS, bp, W_C), dt),
                   jnp.zeros((N_LAYERS, bp, CONV_C - 1, W_C), dt),
                   jnp.zeros((N_LAYERS, bp, CONV_F - 1, 2 * D_FF), dt))
    yp, sp = _trunk(x_prompt, past_prompt, weights, params["norm_final"])
    ys, ss = _trunk(x_sample, caches, weights, params["norm_final"])
    out = [yp, ys]
    for jdx in range(8):
        out.append(jnp.stack([st[jdx] for st in sp], axis=0))
        out.append(jnp.stack([st[jdx] for st in ss], axis=0))
    return tuple(out)


def kernel(x_prompt, x_sample, cache_a_k, cache_a_v, cache_b_k, cache_b_v, cache_b_kidx,
           state_c_h, state_c_conv, state_ffn_conv, norm_mix, w_in, lam_q1, lam_k1, lam_q2,
           lam_k2, diff_gain, rg_conv_w, rg_conv_b, rg_w_r, rg_b_r, rg_w_i, rg_b_i, rg_lambda,
           w_out, norm_ffn, ffn_w_up, ffn_conv_w, ffn_conv_b, ffn_w_down, norm_final):
    params = dict(norm_mix=norm_mix, w_in=w_in, lam_q1=lam_q1, lam_k1=lam_k1, lam_q2=lam_q2,
                  lam_k2=lam_k2, diff_gain=diff_gain, rg_conv_w=rg_conv_w, rg_conv_b=rg_conv_b,
                  rg_w_r=rg_w_r, rg_b_r=rg_b_r, rg_w_i=rg_w_i, rg_b_i=rg_b_i, rg_lambda=rg_lambda,
                  w_out=w_out, norm_ffn=norm_ffn, ffn_w_up=ffn_w_up, ffn_conv_w=ffn_conv_w,
                  ffn_conv_b=ffn_conv_b, ffn_w_down=ffn_w_down, norm_final=norm_final)
    caches = (cache_a_k, cache_a_v, cache_b_k, cache_b_v, cache_b_kidx,
              state_c_h, state_c_conv, state_ffn_conv)
    return _forward(x_prompt, x_sample, caches, params)
```

```python
import functools
import math

import jax
import jax.numpy as jnp
from jax import lax
from jax.experimental import pallas as pl
from jax.experimental.pallas import tpu as pltpu

F32 = jnp.float32
BF16 = jnp.bfloat16

D_MODEL = 1024
N_LAYERS = 2
CHUNK = 64
CHUNK_SHIFT = 6
HEAD_DIM = 64
H_A = 4
W_A = H_A * 2 * HEAD_DIM
H_B = 4
W_B = H_B * HEAD_DIM
H_IDX = 4
D_IDX = 64
TOPK = 256
W_C = 256
N_GATE_BLOCKS = 4
GATE_BLOCK = W_C // N_GATE_BLOCKS
RG_C = 8.0
CONV_C = 4
D_FF = 3072
CONV_F = 3
EPS = 1e-6

LANES = 128
SUBLANES = 8
MXU_DIM = 256
Q_BLOCK = 256
DIFF_Q_BLOCK = 512
SCAN_ROWS = 128
PROMPT_KEY_TILE = 1024
VMEM_LIMIT = 58 * 2**20
MASKED = -1e30
N_BISECT = 14
FFN_ROW_CHUNKS = 4
Q_SCALE = HEAD_DIM ** -0.5 * math.log2(math.e)

PROJ_W_PAD = 3328
_C_QA, _C_KA, _C_VA = 0, 512, 1024
_C_QB, _C_KB, _C_VB = 1536, 1792, 2048
_C_QI = 2304
_C_KW = 2560
_C_KK = 2688
_C_XC = 2816
_C_GC = 3072


def _cparams(sem):
    return pltpu.CompilerParams(dimension_semantics=sem, vmem_limit_bytes=VMEM_LIMIT)


def _rms(x, g):
    return x * lax.rsqrt(jnp.mean(x * x, axis=-1, keepdims=True) + EPS) * g


def _dot_nt(a, b):
    return lax.dot_general(a, b, (((1,), (1,)), ((), ())), preferred_element_type=F32)


def _round_up(n, m):
    return (n + m - 1) // m * m


def _proj_kernel(x_ref, g_ref, w_ref, qa_ref, qb_ref, qi_ref, ka_ref, va_ref, kb_ref, vb_ref,
                 kw_ref, xc_ref, gc_ref, kab_ref, vab_ref, kbb_ref, vbb_ref, kib_ref):
    h = _rms(x_ref[...], g_ref[...])
    z = jnp.dot(h.astype(BF16), w_ref[...], preferred_element_type=F32)
    qa_ref[...] = (z[:, _C_QA:_C_QA + W_A] * Q_SCALE).astype(BF16)
    qb_ref[...] = (z[:, _C_QB:_C_QB + W_B] * Q_SCALE).astype(BF16)
    qi_ref[...] = z[:, _C_QI:_C_QI + H_IDX * D_IDX].astype(BF16)
    ka = z[:, _C_KA:_C_KA + W_A]
    va = z[:, _C_VA:_C_VA + W_A]
    kb = z[:, _C_KB:_C_KB + W_B]
    vb = z[:, _C_VB:_C_VB + W_B]
    ka_ref[...] = ka
    va_ref[...] = va
    kb_ref[...] = kb
    vb_ref[...] = vb
    kab_ref[...] = ka.astype(BF16)
    vab_ref[...] = va.astype(BF16)
    kbb_ref[...] = kb.astype(BF16)
    vbb_ref[...] = vb.astype(BF16)
    kw_ref[...] = z[:, _C_KW:_C_KW + LANES]
    kib_ref[...] = z[:, _C_KK:_C_KK + LANES].astype(BF16)
    xc_ref[...] = z[:, _C_XC:_C_XC + W_C]
    gc_ref[...] = z[:, _C_GC:_C_GC + W_C]


def _proj(x2d, gain, w_pad):
    t = x2d.shape[0]
    tm = min(512, t)
    widths = [(W_A, BF16), (W_B, BF16), (H_IDX * D_IDX, BF16),
              (W_A, F32), (W_A, F32), (W_B, F32), (W_B, F32),
              (LANES, F32), (W_C, F32), (W_C, F32),
              (W_A, BF16), (W_A, BF16), (W_B, BF16), (W_B, BF16), (LANES, BF16)]
    return pl.pallas_call(
        _proj_kernel,
        grid=(t // tm,),
        in_specs=[pl.BlockSpec((tm, D_MODEL), lambda i: (i, 0)),
                  pl.BlockSpec((1, D_MODEL), lambda i: (0, 0)),
                  pl.BlockSpec((D_MODEL, PROJ_W_PAD), lambda i: (0, 0))],
        out_specs=[pl.BlockSpec((tm, w), lambda i: (i, 0)) for w, _ in widths],
        out_shape=[jax.ShapeDtypeStruct((t, w), d) for w, d in widths],
        compiler_params=_cparams(("parallel",)),
        name="proj",
    )(x2d, gain.reshape(1, D_MODEL), w_pad)


def _tile_bounds(q_start, qb, lk, tk):
    n_full = lax.div(jnp.minimum(q_start + CHUNK, lk), tk)
    n_tiles = lax.div(q_start + qb + tk - 1, tk)
    return n_full, n_tiles


def _admissible(start, tk, q_chunk, lk):
    col = start + lax.broadcasted_iota(jnp.int32, (1, tk), 1)
    return (lax.shift_right_logical(col, CHUNK_SHIFT) <= q_chunk) & (col < lk)


def _fori_pairs(n, body, init=0):
    pairs = lax.div(n, 2)

    def two(i, c):
        return body(2 * i + 1, body(2 * i, c))

    return lax.fori_loop(2 * pairs, n, body, lax.fori_loop(0, pairs, two, init))


def _lane_max(acc, x):
    for g in range(x.shape[1] // LANES):
        acc = jnp.maximum(acc, x[:, g * LANES:(g + 1) * LANES])
    return acc


def _softmax_value_tile(s_ref, idx, m, v):
    parts = []
    for g in range(s_ref.shape[-1] // LANES):
        s = s_ref[idx + (slice(None), slice(g * LANES, (g + 1) * LANES))]
        parts.append(jnp.exp2((s - m).astype(BF16)))
    v_ones = jnp.concatenate([v, jnp.ones_like(v)], axis=1)
    return jnp.dot(jnp.concatenate(parts, axis=1), v_ones, preferred_element_type=F32)


def _diff_attn_kernel(*refs, past, lk, qb, tk, out_scale, cached):
    if cached:
        (lam_ref, gain_ref, q_ref, kn_ref, vn_ref, kc_ref, vc_ref, o_ref,
         s_ref, mx_ref, acc_ref, k_ref, v_ref) = refs
        n_cache, n_new = kc_ref.shape[0], kn_ref.shape[0]
        for cache_ref, new_ref, dst_ref in ((kc_ref, kn_ref, k_ref), (vc_ref, vn_ref, v_ref)):
            dst_ref[0:n_cache, :] = cache_ref[...].astype(BF16)
            dst_ref[n_cache:n_cache + n_new, :] = new_ref[...]
            if tk > n_cache + n_new:
                dst_ref[n_cache + n_new:, :] = jnp.zeros((tk - n_cache - n_new, LANES), BF16)
    else:
        lam_ref, gain_ref, q_ref, k_ref, v_ref, o_ref, s_ref, mx_ref, acc_ref = refs
    j = pl.program_id(2)
    q = q_ref[...]
    lane = lax.broadcasted_iota(jnp.int32, (1, LANES), 1)
    zero = jnp.zeros_like(q)
    q_half = (jnp.where(lane < HEAD_DIM, q, zero), jnp.where(lane >= HEAD_DIM, q, zero))
    q_start = past + j * qb
    row = lax.broadcasted_iota(jnp.int32, (qb, 1), 0)
    q_chunk = lax.shift_right_logical(q_start + row, CHUNK_SHIFT)
    n_full, n_tiles = _tile_bounds(q_start, qb, lk, tk)

    mx_ref[...] = jnp.full(mx_ref.shape, -jnp.inf, F32)
    acc_ref[...] = jnp.zeros(acc_ref.shape, F32)

    def score_tile(t, masked):
        start = pl.multiple_of(t * tk, tk)
        k = k_ref[pl.ds(start, tk), :]
        if masked:
            ok = _admissible(start, tk, q_chunk, lk)
        for i in range(2):
            s = _dot_nt(q_half[i], k)
            if masked:
                s = jnp.where(ok, s, -jnp.inf)
            s_ref[i, t] = s
            mx_ref[i] = _lane_max(mx_ref[i], s)

    def full_body(t, c):
        score_tile(t, False)
        return c

    def masked_body(t, c):
        score_tile(t, True)
        return c

    _fori_pairs(n_full, full_body)
    lax.fori_loop(n_full, n_tiles, masked_body, 0)

    for i in range(2):
        mx_ref[i] = jnp.broadcast_to(jnp.max(mx_ref[i], axis=1, keepdims=True), (qb, LANES))

    def value_tile(t, c):
        start = pl.multiple_of(t * tk, tk)
        v = v_ref[pl.ds(start, tk), :]
        for i in range(2):
            acc_ref[i] = acc_ref[i] + _softmax_value_tile(s_ref, (i, t), mx_ref[i], v)
        return c

    _fori_pairs(n_tiles, value_tile)

    o = (acc_ref[0, :, :LANES] / acc_ref[0, :, LANES:]
         - lam_ref[...] * (acc_ref[1, :, :LANES] / acc_ref[1, :, LANES:]))
    o_ref[...] = (_rms(o, gain_ref[...]) * out_scale).astype(BF16)


def _diff_attn(q, k, v, lam, gain, *, past, lk, tk, out_scale, cache=None):
    b, l, _ = q.shape
    qb = min(l, DIFF_Q_BLOCK)
    cached = cache is not None
    lkp = tk if cached else k.shape[1]
    assert l % qb == 0 and lkp % tk == 0 and qb % CHUNK == 0
    kern = functools.partial(_diff_attn_kernel, past=past, lk=lk, qb=qb, tk=tk,
                             out_scale=out_scale, cached=cached)
    in_specs = [pl.BlockSpec((1, LANES), lambda bi, h, j: (0, 0)),
                pl.BlockSpec((1, LANES), lambda bi, h, j: (0, 0)),
                pl.BlockSpec((None, qb, LANES), lambda bi, h, j: (bi, j, h)),
                pl.BlockSpec((None, k.shape[1], LANES), lambda bi, h, j: (bi, 0, h)),
                pl.BlockSpec((None, k.shape[1], LANES), lambda bi, h, j: (bi, 0, h))]
    scratch = [pltpu.VMEM((2, lkp // tk, qb, tk), F32),
               pltpu.VMEM((2, qb, LANES), F32), pltpu.VMEM((2, qb, 2 * LANES), F32)]
    operands = [lam, gain, q, k, v]
    if cached:
        cache_k, cache_v, li = cache
        assert l == qb and past + l <= tk and cache_k.shape[2:] == (past, W_A)
        head_rows = pl.BlockSpec((None, None, past, LANES), lambda bi, h, j: (li, bi, 0, h))
        in_specs += [head_rows, head_rows]
        scratch += [pltpu.VMEM((tk, LANES), BF16), pltpu.VMEM((tk, LANES), BF16)]
        operands += [cache_k, cache_v]
    return pl.pallas_call(
        kern,
        grid=(b, H_A, l // qb),
        in_specs=in_specs,
        out_specs=pl.BlockSpec((None, qb, LANES), lambda bi, h, j: (bi, j, h)),
        out_shape=jax.ShapeDtypeStruct((b, l, W_A), BF16),
        scratch_shapes=scratch,
        compiler_params=_cparams(("parallel", "parallel", "parallel")),
        name="diff_attn",
    )(*operands)


def _dsa_kernel(qi_ref, kw_ref, q_ref, ki_ref, k_ref, v_ref, o_ref,
                s_ref, sc_ref, lo_ref, hi_ref, mx_ref, acc_ref,
                *, past, lk, qb, tk, topk, max_steps):
    j = pl.program_id(1)
    q_start = past + j * qb
    n_full, n_tiles = _tile_bounds(q_start, qb, lk, tk)
    n_groups = tk // LANES
    row = lax.broadcasted_iota(jnp.int32, (qb, 1), 0)
    q_chunk = lax.shift_right_logical(q_start + row, CHUNK_SHIFT)
    lane = lax.broadcasted_iota(jnp.int32, (1, LANES), 1)
    low_half = lane < HEAD_DIM
    k_sel = float(topk)

    def wide(x):
        return jnp.broadcast_to(x, (qb, LANES))

    def head_views(x):
        views = []
        for h in range(4):
            pair = x[:, (h // 2) * LANES:(h // 2 + 1) * LANES]
            keep = low_half if h % 2 == 0 else jnp.logical_not(low_half)
            views.append(jnp.where(keep, pair, jnp.zeros_like(pair)))
        return views

    qi_h = head_views(qi_ref[...])
    kw = kw_ref[...]
    w_h = [wide(kw[:, D_IDX + h:D_IDX + h + 1]) for h in range(H_IDX)]
    lo_ref[...] = jnp.full(lo_ref.shape, -jnp.inf, F32)
    hi_ref[...] = jnp.full(hi_ref.shape, -jnp.inf, F32)

    def index_tile(t, masked):
        start = pl.multiple_of(t * tk, tk)
        if masked:
            adm = _admissible(start, tk, q_chunk, lk)
        top2 = lo_ref[...]
        top1 = hi_ref[...]
        for c in range(tk // MXU_DIM):
            ki = ki_ref[pl.ds(start + c * MXU_DIM, MXU_DIM), :]
            rel = [jnp.maximum(_dot_nt(qi_h[h], ki), 0.0) for h in range(H_IDX)]
            for g in range(MXU_DIM // LANES):
                sl = slice(g * LANES, (g + 1) * LANES)
                sc = w_h[0] * rel[0][:, sl]
                for h in range(1, H_IDX):
                    sc = sc + w_h[h] * rel[h][:, sl]
                csl = slice(c * MXU_DIM + g * LANES, c * MXU_DIM + (g + 1) * LANES)
                if masked:
                    sc = jnp.where(adm[:, csl], sc, -jnp.inf)
                s_ref[t, :, csl] = sc
                top2 = jnp.maximum(top2, jnp.minimum(top1, sc))
                top1 = jnp.maximum(top1, sc)
        lo_ref[...] = top2
        hi_ref[...] = top1

    def index_full(t, c):
        index_tile(t, False)
        return c

    def index_masked(t, c):
        index_tile(t, True)
        return c

    _fori_pairs(n_full, index_full)
    lax.fori_loop(n_full, n_tiles, index_masked, 0)
    rb = min(qb, SCAN_ROWS)
    ones_mat = jnp.ones((LANES, LANES), BF16)
    tri_i = lax.broadcasted_iota(jnp.int32, (LANES, 2 * LANES), 0)
    tri_j = lax.broadcasted_iota(jnp.int32, (LANES, 2 * LANES), 1)
    prefix_mat = jnp.where(jnp.logical_or(tri_i <= tri_j, tri_j >= LANES), 1.0, 0.0).astype(BF16)
    assert s_ref.shape[0] * n_groups <= 256

    row_blocks = [slice(r * rb, (r + 1) * rb) for r in range(qb // rb)]
    pos = q_start + lax.broadcasted_iota(jnp.int32, (qb, LANES), 0)
    n_adm = jnp.minimum((lax.shift_right_logical(pos, CHUNK_SHIFT) + 1) * CHUNK, lk)
    few = n_adm <= topk

    def lanes_all(x, reduce):
        return jnp.broadcast_to(reduce(x, axis=1, keepdims=True), (qb, LANES))

    def row_sum(acc):
        return jnp.dot(acc.astype(BF16), ones_mat, preferred_element_type=F32)

    def scan(step, init, *operands):
        outs = []
        for rows in row_blocks:
            ops = [o[rows] for o in operands]

            def body(t, acc, rows=rows, ops=ops):
                for g in range(n_groups):
                    acc = step(acc, s_ref[t, rows, g * LANES:(g + 1) * LANES], *ops)
                return acc

            outs.append(lax.fori_loop(0, n_tiles, body, jnp.full((rb, LANES), init, F32)))
        return outs[0] if len(outs) == 1 else jnp.concatenate(outs, axis=0)

    def count_ge(thr):
        return row_sum(scan(lambda acc, s, t: acc + jnp.where(s >= t, 1.0, 0.0), 0.0, thr))

    def max_below(bound):
        acc = scan(lambda acc, s, b: jnp.maximum(acc, jnp.where(s < b, s, -jnp.inf)), -jnp.inf, bound)
        return lanes_all(acc, jnp.max)

    assert topk <= 2 * LANES
    lane_best = hi_ref[...] if topk <= LANES else lo_ref[...]
    lowest = float(jnp.finfo(jnp.float32).min)
    rmin = jnp.maximum(lanes_all(lane_best, jnp.min), lowest)
    rmax = jnp.maximum(lanes_all(lane_best, jnp.max), lowest)

    c_max = count_ge(rmax)
    top_ties = c_max >= k_sel

    def bisect(_, c):
        lo, hi = c
        mid = 0.5 * lo + 0.5 * hi
        ge = count_ge(mid) >= k_sel
        return jnp.where(ge, mid, lo), jnp.where(ge, hi, mid)

    _, hi = lax.fori_loop(0, N_BISECT, bisect, (rmin, rmax))

    def walk_cond(st):
        it, _, _, _, active = st
        return jnp.logical_and(jnp.max(active) > 0.0, it < max_steps)

    def walk_body(st):
        it, cand, thr, c_thr, active = st
        c = count_ge(cand)
        ok = c >= k_sel
        act = active > 0.0
        hit = jnp.logical_and(act, ok)
        thr = jnp.where(hit, cand, thr)
        c_thr = jnp.where(hit, c, c_thr)
        active = jnp.where(jnp.logical_and(act, jnp.logical_not(ok)), 1.0, 0.0)
        return it + 1, max_below(cand), thr, c_thr, active

    thr0 = jnp.where(few, lowest, jnp.where(top_ties, rmax, lowest))
    c0 = jnp.where(few, k_sel, jnp.where(top_ties, c_max, k_sel))
    active0 = jnp.where(jnp.logical_or(few, top_ties), 0.0, 1.0)
    _, _, thr, c_thr, _ = lax.while_loop(
        walk_cond, walk_body, (jnp.int32(0), max_below(hi), thr0, c0, active0))

    has_excess = jnp.max(jnp.where(c_thr > k_sel, 1.0, 0.0)) > 0.0

    def ranked_bias():
        n_tie = k_sel - row_sum(scan(lambda acc, s, t: acc + jnp.where(s > t, 1.0, 0.0), 0.0, thr))
        for rows in row_blocks:
            thr_r = thr[rows]
            n_tie_r = n_tie[rows]

            def body(t, before, rows=rows, thr_r=thr_r, n_tie_r=n_tie_r):
                for g in range(n_groups):
                    lanes = slice(g * LANES, (g + 1) * LANES)
                    s = s_ref[t, rows, lanes]
                    tie = s == thr_r
                    pr = jnp.dot(jnp.where(tie, 1.0, 0.0).astype(BF16), prefix_mat,
                                 preferred_element_type=F32)
                    rank = before + pr[:, :LANES]
                    tie_bias = jnp.where(rank <= n_tie_r, 0.0, MASKED)
                    s_ref[t, rows, lanes] = jnp.where(s > thr_r, 0.0, jnp.where(tie, tie_bias, MASKED))
                    before = before + pr[:, LANES:]
                return before

            _fori_pairs(n_tiles, body, jnp.zeros((rb, LANES), F32))
        return 0

    def plain_bias():
        for rows in row_blocks:
            thr_r = thr[rows]

            def body(t, c, rows=rows, thr_r=thr_r):
                for g in range(n_groups):
                    lanes = slice(g * LANES, (g + 1) * LANES)
                    s_ref[t, rows, lanes] = jnp.where(s_ref[t, rows, lanes] >= thr_r, 0.0, MASKED)
                return c

            lax.fori_loop(0, n_tiles, body, 0)
        return 0

    lax.cond(has_excess, ranked_bias, plain_bias)

    q_h = head_views(q_ref[...])
    for g in range(H_B // 2):
        lanes = slice(g * LANES, (g + 1) * LANES)
        mx_ref[...] = jnp.full(mx_ref.shape, -jnp.inf, F32)
        acc_ref[...] = jnp.zeros(acc_ref.shape, F32)

        def score_tile(t, c, g=g, lanes=lanes):
            start = pl.multiple_of(t * tk, tk)
            k = k_ref[pl.ds(start, tk), lanes]
            bias = s_ref[t]
            for i in range(2):
                s = _dot_nt(q_h[2 * g + i], k) + bias
                sc_ref[i, t] = s
                mx_ref[i] = _lane_max(mx_ref[i], s)
            return c

        _fori_pairs(n_tiles, score_tile)
        for i in range(2):
            mx_ref[i] = jnp.broadcast_to(jnp.max(mx_ref[i], axis=1, keepdims=True), (qb, LANES))

        def value_tile(t, c, lanes=lanes):
            start = pl.multiple_of(t * tk, tk)
            v = v_ref[pl.ds(start, tk), lanes]
            for i in range(2):
                acc_ref[i] = acc_ref[i] + _softmax_value_tile(sc_ref, (i, t), mx_ref[i], v)
            return c

        _fori_pairs(n_tiles, value_tile)
        even = acc_ref[0, :, :LANES] / acc_ref[0, :, LANES:]
        odd = acc_ref[1, :, :LANES] / acc_ref[1, :, LANES:]
        o_ref[:, lanes] = jnp.where(low_half, even, odd).astype(BF16)


def _dsa(qi, kw, q, ki, k, v, *, past, lk, tk):
    b, l, _ = q.shape
    lkp = k.shape[1]
    qb = min(l, Q_BLOCK)
    assert l % qb == 0 and lkp % tk == 0 and tk % MXU_DIM == 0 and qb % CHUNK == 0
    kern = functools.partial(_dsa_kernel, past=past, lk=lk, qb=qb, tk=tk,
                             topk=min(TOPK, lk // 4), max_steps=lkp)
    return pl.pallas_call(
        kern,
        grid=(b, l // qb),
        in_specs=[pl.BlockSpec((None, qb, H_IDX * D_IDX), lambda bi, j: (bi, j, 0)),
                  pl.BlockSpec((None, qb, LANES), lambda bi, j: (bi, j, 0)),
                  pl.BlockSpec((None, qb, W_B), lambda bi, j: (bi, j, 0)),
                  pl.BlockSpec((None, lkp, LANES), lambda bi, j: (bi, 0, 0)),
                  pl.BlockSpec((None, lkp, W_B), lambda bi, j: (bi, 0, 0)),
                  pl.BlockSpec((None, lkp, W_B), lambda bi, j: (bi, 0, 0))],
        out_specs=pl.BlockSpec((None, qb, W_B), lambda bi, j: (bi, j, 0)),
        out_shape=jax.ShapeDtypeStruct((b, l, W_B), BF16),
        scratch_shapes=[pltpu.VMEM((lkp // tk, qb, tk), F32),
                        pltpu.VMEM((2, lkp // tk, qb, tk), F32),
                        pltpu.VMEM((qb, LANES), F32), pltpu.VMEM((qb, LANES), F32),
                        pltpu.VMEM((2, qb, LANES), F32), pltpu.VMEM((2, qb, 2 * LANES), F32)],
        compiler_params=_cparams(("parallel", "parallel")),
        name="dsa",
    )(qi, kw, q, ki, k, v)


def _shift_rows(x, prev, k):
    rolled = pltpu.roll(x, k, 0)
    row = lax.broadcasted_iota(jnp.int32, (SUBLANES, 1), 0)
    top = jnp.where(row < k, pltpu.roll(prev, k, 0), rolled[0:SUBLANES])
    if x.shape[0] == SUBLANES:
        return top
    return jnp.concatenate([top, rolled[SUBLANES:]], axis=0)


def _rglru_kernel(xc_ref, gc_ref, cst_ref, h0_ref, cw_ref, cb_ref, wg_ref, bg_ref, lam_ref,
                  oc_ref, hl_ref, cn_ref, prev_ref, h_ref, a_ref, b_ref, hs_ref, *, past, tl):
    i = pl.program_id(1)

    @pl.when(i == 0)
    def _():
        prev_ref[...] = jnp.zeros(prev_ref.shape, F32)
        prev_ref[SUBLANES - (CONV_C - 1):SUBLANES, :] = cst_ref[...]
        h_ref[...] = h0_ref[...]

    x = xc_ref[...]
    prev = prev_ref[...]
    cw = cw_ref[...]
    xconv = cw[CONV_C - 1:CONV_C] * x + cb_ref[...]
    for k in range(1, CONV_C):
        xconv = xconv + cw[CONV_C - 1 - k:CONV_C - k] * _shift_rows(x, prev, k)
    prev_ref[...] = x[tl - SUBLANES:tl]

    pre = jnp.dot(xconv.astype(BF16), wg_ref[...], preferred_element_type=F32) + bg_ref[...]
    r = jax.nn.sigmoid(pre[:, :W_C])
    gate_i = jax.nn.sigmoid(pre[:, W_C:])
    neg_lam = -lam_ref[...]
    softplus = jnp.maximum(neg_lam, 0.0) + jnp.log1p(jnp.exp(-jnp.abs(neg_lam)))
    log_a = -RG_C * r * softplus
    pos = past + i * tl + lax.broadcasted_iota(jnp.int32, (tl, 1), 0)
    th = jnp.tanh(log_a)
    mult = jnp.where(pos == 0, 1.0, jnp.sqrt(-2.0 * th / (1.0 - th)))
    a_ref[...] = jnp.exp(log_a)
    b_ref[...] = mult * gate_i * xconv

    row8 = lax.broadcasted_iota(jnp.int32, (SUBLANES, 1), 0)

    def tile_step(i, h):
        r0 = pl.multiple_of(i * SUBLANES, SUBLANES)
        a = a_ref[pl.ds(r0, SUBLANES), :]
        b = b_ref[pl.ds(r0, SUBLANES), :]
        for sh in (1, 2, 4):
            a_prev = jnp.where(row8 >= sh, pltpu.roll(a, sh, 0), 1.0)
            b_prev = jnp.where(row8 >= sh, pltpu.roll(b, sh, 0), 0.0)
            b = a * b_prev + b
            a = a * a_prev
        hs = a * h + b
        hs_ref[pl.ds(r0, SUBLANES), :] = hs
        return hs[SUBLANES - 1:SUBLANES, :]

    h_last = lax.fori_loop(0, tl // SUBLANES, tile_step, h_ref[...], unroll=4)
    h_ref[...] = h_last

    gc = gc_ref[...]
    gelu = 0.5 * gc * (1.0 + jnp.tanh(math.sqrt(2.0 / math.pi) * (gc + 0.044715 * (gc * gc * gc))))
    oc_ref[...] = (hs_ref[...] * gelu).astype(BF16)

    @pl.when(i == pl.num_programs(1) - 1)
    def _():
        hl_ref[...] = h_last
        cn_ref[...] = x[tl - (CONV_C - 1):tl]


def _rglru(xc, gc, conv_state, h0, conv_w, conv_b, w_gate, b_gate, lam, *, past):
    b, l, _ = xc.shape
    tl = min(l, 512)
    assert l >= SUBLANES and l % tl == 0
    kern = functools.partial(_rglru_kernel, past=past, tl=tl)
    const = lambda bi, i: (0, 0)
    return pl.pallas_call(
        kern,
        grid=(b, l // tl),
        in_specs=[pl.BlockSpec((None, tl, W_C), lambda bi, i: (bi, i, 0)),
                  pl.BlockSpec((None, tl, W_C), lambda bi, i: (bi, i, 0)),
                  pl.BlockSpec((None, CONV_C - 1, W_C), lambda bi, i: (bi, 0, 0)),
                  pl.BlockSpec((None, 1, W_C), lambda bi, i: (bi, 0, 0)),
                  pl.BlockSpec((CONV_C, W_C), const),
                  pl.BlockSpec((1, W_C), const),
                  pl.BlockSpec((W_C, 2 * W_C), const),
                  pl.BlockSpec((1, 2 * W_C), const),
                  pl.BlockSpec((1, W_C), const)],
        out_specs=[pl.BlockSpec((None, tl, W_C), lambda bi, i: (bi, i, 0)),
                   pl.BlockSpec((None, 1, W_C), lambda bi, i: (bi, 0, 0)),
                   pl.BlockSpec((None, CONV_C - 1, W_C), lambda bi, i: (bi, 0, 0))],
        out_shape=[jax.ShapeDtypeStruct((b, l, W_C), BF16),
                   jax.ShapeDtypeStruct((b, 1, W_C), F32),
                   jax.ShapeDtypeStruct((b, CONV_C - 1, W_C), F32)],
        scratch_shapes=[pltpu.VMEM((SUBLANES, W_C), F32), pltpu.VMEM((1, W_C), F32),
                        pltpu.VMEM((tl, W_C), F32), pltpu.VMEM((tl, W_C), F32),
                        pltpu.VMEM((tl, W_C), F32)],
        compiler_params=_cparams(("parallel", "arbitrary")),
        name="rglru",
    )(xc, gc, conv_state, h0.reshape(b, 1, W_C), conv_w, conv_b.reshape(1, W_C), w_gate,
      b_gate, lam.reshape(1, W_C))


def _outproj_kernel(x_ref, oa_ref, ob_ref, oc_ref, w_ref, g_ref, x1_ref, hn_ref):
    mix = jnp.dot(oa_ref[...], w_ref[0:W_A, :], preferred_element_type=F32)
    mix = mix + jnp.dot(ob_ref[...], w_ref[W_A:W_A + W_B, :], preferred_element_type=F32)
    mix = mix + jnp.dot(oc_ref[...], w_ref[W_A + W_B:, :], preferred_element_type=F32)
    x1 = x_ref[...] + mix
    x1_ref[...] = x1
    hn_ref[...] = _rms(x1, g_ref[...]).astype(BF16)


def _outproj(x2d, oa, ob, oc, w_out, gain):
    t = x2d.shape[0]
    tm = min(512, t)
    row = lambda w: pl.BlockSpec((tm, w), lambda i: (i, 0))
    return pl.pallas_call(
        _outproj_kernel,
        grid=(t // tm,),
        in_specs=[row(D_MODEL), row(W_A), row(W_B), row(W_C),
                  pl.BlockSpec((D_MODEL, D_MODEL), lambda i: (0, 0)),
                  pl.BlockSpec((1, D_MODEL), lambda i: (0, 0))],
        out_specs=[row(D_MODEL), row(D_MODEL)],
        out_shape=[jax.ShapeDtypeStruct((t, D_MODEL), F32), jax.ShapeDtypeStruct((t, D_MODEL), BF16)],
        compiler_params=_cparams(("parallel",)),
        name="outproj",
    )(x2d, oa, ob, oc, w_out, gain.reshape(1, D_MODEL))


def _ffn_kernel(hn_ref, x1_ref, wu_ref, wg_ref, cwu_ref, cwg_ref, cbu_ref, cbg_ref, wd_ref,
                su_ref, sg_ref, gfin_ref, y_ref, fu_ref, fg_ref,
                acc_ref, au_ref, ag_ref, cu_ref, cg_ref, *, tm, final_norm):
    i = pl.program_id(1)
    s = pl.program_id(2)
    nf = pl.num_programs(2) - 1
    live = s > 0
    fb = jnp.maximum(s - 1, 0)
    row = jnp.where(live, fb, nf)

    @pl.when(s == 0)
    def _():
        au_ref[1] = jnp.zeros(au_ref.shape[1:], F32)
        ag_ref[1] = jnp.zeros(ag_ref.shape[1:], F32)
        cu_ref[nf] = jnp.zeros(cu_ref.shape[1:], F32)
        cg_ref[nf] = jnp.zeros(cg_ref.shape[1:], F32)
        acc_ref[...] = x1_ref[...]

    @pl.when(jnp.logical_and(i == 0, live))
    def _():
        for carry_ref, st_ref in ((cu_ref, su_ref), (cg_ref, sg_ref)):
            carry_ref[fb] = jnp.zeros(carry_ref.shape[1:], F32)
            carry_ref[fb, SUBLANES - (CONV_F - 1):SUBLANES, :] = st_ref[...]

    def step(wslot, rslot):
        au_ref[rslot, 0:SUBLANES, :] = cu_ref[row]
        ag_ref[rslot, 0:SUBLANES, :] = cg_ref[row]
        rc = tm // FFN_ROW_CHUNKS

        def conv(a_ref, cw_ref, cb_ref, r0):
            cw = cw_ref[...]
            y = cw[CONV_F - 1:CONV_F] * a_ref[rslot, SUBLANES + r0:SUBLANES + r0 + rc, :] + cb_ref[...]
            for k in range(1, CONV_F):
                y = y + (cw[CONV_F - 1 - k:CONV_F - k]
                         * a_ref[rslot, SUBLANES - k + r0:SUBLANES - k + r0 + rc, :])
            return y

        for c in range(FFN_ROW_CHUNKS):
            r0 = c * rc
            hn = hn_ref[r0:r0 + rc, :]
            au_ref[wslot, SUBLANES + r0:SUBLANES + r0 + rc, :] = jnp.dot(
                hn, wu_ref[...], preferred_element_type=F32)
            ag_ref[wslot, SUBLANES + r0:SUBLANES + r0 + rc, :] = jnp.dot(
                hn, wg_ref[...], preferred_element_type=F32)
            u = conv(au_ref, cwu_ref, cbu_ref, r0)
            g = conv(ag_ref, cwg_ref, cbg_ref, r0)
            mid = (g * jax.nn.sigmoid(g) * u).astype(BF16)
            contrib = jnp.dot(mid, wd_ref[...], preferred_element_type=F32)
            acc_ref[r0:r0 + rc, :] = acc_ref[r0:r0 + rc, :] + jnp.where(live, contrib, 0.0)

        for a_ref, carry_ref, tail_ref in ((au_ref, cu_ref, fu_ref), (ag_ref, cg_ref, fg_ref)):
            carry_ref[row] = a_ref[rslot, tm:tm + SUBLANES, :]
            tail_ref[row] = a_ref[rslot, tm + SUBLANES - (CONV_F - 1):tm + SUBLANES, :]

    parity = lax.rem(s, 2)

    @pl.when(parity == 0)
    def _():
        step(0, 1)

    @pl.when(parity == 1)
    def _():
        step(1, 0)

    @pl.when(s == nf)
    def _():
        y = acc_ref[...]
        if final_norm:
            y = _rms(y, gfin_ref[...])
        y_ref[...] = y


def _ffn(hn, x1, w_up, conv_w, conv_b, w_down, state, final_gain, *, final_norm):
    b, l, _ = hn.shape
    tm = min(l, 1024)
    tf = 512
    nf = D_FF // tf
    assert l % tm == 0 and tm >= SUBLANES
    kern = functools.partial(_ffn_kernel, tm=tm, final_norm=final_norm)
    conv_b = conv_b.reshape(1, 2 * D_FF)
    up = lambda s: jnp.minimum(s, nf - 1)
    fin = lambda s: jnp.maximum(s - 1, 0)
    tail_spec = pl.BlockSpec((None, nf + 1, CONV_F - 1, tf), lambda bi, i, s: (bi, 0, 0, 0))
    tail_shape = jax.ShapeDtypeStruct((b, nf + 1, CONV_F - 1, tf), F32)
    y, fu, fg = pl.pallas_call(
        kern,
        grid=(b, l // tm, nf + 1),
        in_specs=[pl.BlockSpec((None, tm, D_MODEL), lambda bi, i, s: (bi, i, 0)),
                  pl.BlockSpec((None, tm, D_MODEL), lambda bi, i, s: (bi, i, 0)),
                  pl.BlockSpec((D_MODEL, tf), lambda bi, i, s: (0, up(s))),
                  pl.BlockSpec((D_MODEL, tf), lambda bi, i, s: (0, nf + up(s))),
                  pl.BlockSpec((CONV_F, tf), lambda bi, i, s: (0, fin(s))),
                  pl.BlockSpec((CONV_F, tf), lambda bi, i, s: (0, nf + fin(s))),
                  pl.BlockSpec((1, tf), lambda bi, i, s: (0, fin(s))),
                  pl.BlockSpec((1, tf), lambda bi, i, s: (0, nf + fin(s))),
                  pl.BlockSpec((tf, D_MODEL), lambda bi, i, s: (fin(s), 0)),
                  pl.BlockSpec((None, CONV_F - 1, tf), lambda bi, i, s: (bi, 0, fin(s))),
                  pl.BlockSpec((None, CONV_F - 1, tf), lambda bi, i, s: (bi, 0, nf + fin(s))),
                  pl.BlockSpec((1, D_MODEL), lambda bi, i, s: (0, 0))],
        out_specs=[pl.BlockSpec((None, tm, D_MODEL), lambda bi, i, s: (bi, i, 0)),
                   tail_spec, tail_spec],
        out_shape=[jax.ShapeDtypeStruct((b, l, D_MODEL), F32), tail_shape, tail_shape],
        scratch_shapes=[pltpu.VMEM((tm, D_MODEL), F32),
                        pltpu.VMEM((2, tm + SUBLANES, tf), F32),
                        pltpu.VMEM((2, tm + SUBLANES, tf), F32),
                        pltpu.VMEM((nf + 1, SUBLANES, tf), F32),
                        pltpu.VMEM((nf + 1, SUBLANES, tf), F32)],
        compiler_params=_cparams(("parallel", "arbitrary", "arbitrary")),
        name="ffn",
    )(hn, x1, w_up, w_up, conv_w, conv_w, conv_b, conv_b, w_down, state, state,
      final_gain.reshape(1, D_MODEL))
    flat = lambda a: jnp.swapaxes(a[:, :nf], 1, 2).reshape(b, CONV_F - 1, D_FF)
    return y, jnp.concatenate([flat(fu), flat(fg)], axis=-1)


def _cast_kernel(x_ref, o_ref):
    o_ref[...] = x_ref[...].astype(o_ref.dtype)


def _layer_bf16(w, li):
    _, r, c = w.shape
    tr = 256 if r % 256 == 0 else r
    return pl.pallas_call(
        _cast_kernel,
        grid=(r // tr,),
        in_specs=[pl.BlockSpec((None, tr, c), lambda i: (li, i, 0))],
        out_specs=pl.BlockSpec((tr, c), lambda i: (i, 0)),
        out_shape=jax.ShapeDtypeStruct((r, c), BF16),
        compiler_params=_cparams(("parallel",)),
        name="cast",
    )(w)


def _prep_layer_weights(p, li):
    w_in = _layer_bf16(p["w_in"], li)
    zeros = lambda n: jnp.zeros((D_MODEL, n), w_in.dtype)
    k_idx = w_in[:, 2560:2624]
    w_pad = jnp.concatenate(
        [w_in[:, :2628], zeros(_C_KK - 2628), k_idx, k_idx, w_in[:, 2628:]], axis=1)
    assert w_pad.shape[1] == PROJ_W_PAD

    def block_diag(w):
        out = jnp.zeros((W_C, W_C), w.dtype)
        for n in range(N_GATE_BLOCKS):
            sl = slice(n * GATE_BLOCK, (n + 1) * GATE_BLOCK)
            out = out.at[sl, sl].set(w[n])
        return out

    w_gate = jnp.concatenate([block_diag(p["rg_w_r"][li]), block_diag(p["rg_w_i"][li])], axis=1)
    b_gate = jnp.concatenate([p["rg_b_r"][li].reshape(1, W_C), p["rg_b_i"][li].reshape(1, W_C)], axis=1)
    lam_init = 0.8 - 0.6 * math.exp(-0.3 * li)
    f32 = lambda a: a.astype(F32)
    lam = (jnp.exp(jnp.sum(f32(p["lam_q1"][li]) * f32(p["lam_k1"][li])))
           - jnp.exp(jnp.sum(f32(p["lam_q2"][li]) * f32(p["lam_k2"][li]))) + lam_init)
    return dict(
        norm_mix=p["norm_mix"][li], w_pad=w_pad, lam=jnp.full((1, LANES), lam, F32),
        lam_init=lam_init, diff_gain=p["diff_gain"][li].reshape(1, LANES),
        rg_conv_w=p["rg_conv_w"][li], rg_conv_b=p["rg_conv_b"][li],
        w_gate=w_gate.astype(BF16), b_gate=b_gate, rg_lambda=p["rg_lambda"][li],
        w_out=_layer_bf16(p["w_out"], li), norm_ffn=p["norm_ffn"][li],
        ffn_w_up=_layer_bf16(p["ffn_w_up"], li), ffn_conv_w=p["ffn_conv_w"][li],
        ffn_conv_b=p["ffn_conv_b"][li], ffn_w_down=_layer_bf16(p["ffn_w_down"], li))


def _layer(x, past, attn_cache, w, final_gain, final_norm):
    b, l, _ = x.shape
    _, _, b_k0, b_v0, b_ki0, c_h0, c_cv0, f_cv0 = past
    p_len = 0 if b_k0 is None else b_k0.shape[1]
    lk = p_len + l
    tk = PROMPT_KEY_TILE if p_len == 0 else _round_up(lk, MXU_DIM)
    lkp = _round_up(lk, tk)
    t = b * l

    (qa, qb, qi, ka, va, kb, vb, kw, xc, gc, kab, vab, kbb, vbb, kib) = _proj(
        x.reshape(t, D_MODEL), w["norm_mix"], w["w_pad"])

    def keys(cache, new, dup=False):
        new = new.reshape(b, l, -1)
        parts = []
        if cache is not None:
            c = cache.reshape(b, p_len, -1).astype(BF16)
            parts.append(jnp.concatenate([c, c], axis=-1) if dup else c)
        parts.append(new)
        if lkp > lk:
            parts.append(jnp.zeros((b, lkp - lk, new.shape[-1]), BF16))
        return parts[0] if len(parts) == 1 else jnp.concatenate(parts, axis=1)

    if attn_cache is None:
        k_a, v_a = keys(None, kab), keys(None, vab)
    else:
        k_a, v_a = kab.reshape(b, l, W_A), vab.reshape(b, l, W_A)
    o_a = _diff_attn(qa.reshape(b, l, W_A), k_a, v_a, w["lam"], w["diff_gain"], past=p_len, lk=lk,
                     tk=tk, out_scale=1.0 - w["lam_init"], cache=attn_cache)
    o_b = _dsa(qi.reshape(b, l, -1), kw.reshape(b, l, LANES), qb.reshape(b, l, W_B),
               keys(b_ki0, kib, dup=True), keys(b_k0, kbb), keys(b_v0, vbb),
               past=p_len, lk=lk, tk=tk)
    o_c, h_last, conv_new = _rglru(xc.reshape(b, l, W_C), gc.reshape(b, l, W_C), c_cv0, c_h0,
                                   w["rg_conv_w"], w["rg_conv_b"], w["w_gate"], w["b_gate"],
                                   w["rg_lambda"], past=p_len)
    x1, hn = _outproj(x.reshape(t, D_MODEL), o_a.reshape(t, W_A), o_b.reshape(t, W_B),
                      o_c.reshape(t, W_C), w["w_out"], w["norm_ffn"])
    y, f_buf = _ffn(hn.reshape(b, l, D_MODEL), x1.reshape(b, l, D_MODEL), w["ffn_w_up"],
                    w["ffn_conv_w"], w["ffn_conv_b"], w["ffn_w_down"], f_cv0, final_gain,
                    final_norm=final_norm)
    new = (ka.reshape(b, l, H_A, 2 * HEAD_DIM), va.reshape(b, l, H_A, 2 * HEAD_DIM),
           kb.reshape(b, l, H_B, HEAD_DIM), vb.reshape(b, l, H_B, HEAD_DIM),
           kw[:, :D_IDX].reshape(b, l, D_IDX), h_last.reshape(b, W_C), conv_new, f_buf)
    return y, new


def _trunk(x, past, weights, final_gain):
    states = []
    if past[0] is not None:
        flat_heads = lambda c: c.reshape(c.shape[:3] + (W_A,))
        cache_a = (flat_heads(past[0]), flat_heads(past[1]))
    for li in range(N_LAYERS):
        layer_past = tuple(None if c is None else c[li] for c in past)
        attn_cache = None if past[0] is None else cache_a + (li,)
        x, st = _layer(x, layer_past, attn_cache, weights[li], final_gain,
                       final_norm=(li == N_LAYERS - 1))
        states.append(st)
    return x, states


def _forward(x_prompt, x_sample, caches, params):
    weights = [_prep_layer_weights(params, li) for li in range(N_LAYERS)]
    bp = x_prompt.shape[0]
    dt = x_prompt.dtype
    past_prompt = (None, None, None, None, None,
                   jnp.zeros((N_LAYERS, bp, W_C), dt),
                   jnp.zeros((N_LAYERS, bp, CONV_C - 1, W_C), dt),
                   jnp.zeros((N_LAYERS, bp, CONV_F - 1, 2 * D_FF), dt))
    yp, sp = _trunk(x_prompt, past_prompt, weights, params["norm_final"])
    ys, ss = _trunk(x_sample, caches, weights, params["norm_final"])
    out = [yp, ys]
    for jdx in range(8):
        out.append(jnp.stack([st[jdx] for st in sp], axis=0))
        out.append(jnp.stack([st[jdx] for st in ss], axis=0))
    return tuple(out)


def kernel(x_prompt, x_sample, cache_a_k, cache_a_v, cache_b_k, cache_b_v, cache_b_kidx,
           state_c_h, state_c_conv, state_ffn_conv, norm_mix, w_in, lam_q1, lam_k1, lam_q2,
           lam_k2, diff_gain, rg_conv_w, rg_conv_b, rg_w_r, rg_b_r, rg_w_i, rg_b_i, rg_lambda,
           w_out, norm_ffn, ffn_w_up, ffn_conv_w, ffn_conv_b, ffn_w_down, norm_final):
    params = dict(norm_mix=norm_mix, w_in=w_in, lam_q1=lam_q1, lam_k1=lam_k1, lam_q2=lam_q2,
                  lam_k2=lam_k2, diff_gain=diff_gain, rg_conv_w=rg_conv_w, rg_conv_b=rg_conv_b,
                  rg_w_r=rg_w_r, rg_b_r=rg_b_r, rg_w_i=rg_w_i, rg_b_i=rg_b_i, rg_lambda=rg_lambda,
                  w_out=w_out, norm_ffn=norm_ffn, ffn_w_up=ffn_w_up, ffn_conv_w=ffn_conv_w,
                  ffn_conv_b=ffn_conv_b, ffn_w_down=ffn_w_down, norm_final=norm_final)
    caches = (cache_a_k, cache_a_v, cache_b_k, cache_b_v, cache_b_kidx,
              state_c_h, state_c_conv, state_ffn_conv)
    return _forward(x_prompt, x_sample, caches, params)
```

```python
import functools
import math

import jax
import jax.numpy as jnp
from jax import lax
from jax.experimental import pallas as pl
from jax.experimental.pallas import tpu as pltpu

F32 = jnp.float32
BF16 = jnp.bfloat16

D_MODEL = 1024
N_LAYERS = 2
CHUNK = 64
CHUNK_SHIFT = 6
HEAD_DIM = 64
H_A = 4
W_A = H_A * 2 * HEAD_DIM
H_B = 4
W_B = H_B * HEAD_DIM
H_IDX = 4
D_IDX = 64
TOPK = 256
W_C = 256
N_GATE_BLOCKS = 4
GATE_BLOCK = W_C // N_GATE_BLOCKS
RG_C = 8.0
CONV_C = 4
D_FF = 3072
CONV_F = 3
EPS = 1e-6

LANES = 128
SUBLANES = 8
MXU_DIM = 256
Q_BLOCK = 256
DIFF_Q_BLOCK = 512
SCAN_ROWS = 128
PROMPT_KEY_TILE = 1024
VMEM_LIMIT = 58 * 2**20
MASKED = -1e30
N_BISECT = 12
FFN_ROW_CHUNKS = 4
Q_SCALE = HEAD_DIM ** -0.5 * math.log2(math.e)

PROJ_W_PAD = 3328
_C_QA, _C_KA, _C_VA = 0, 512, 1024
_C_QB, _C_KB, _C_VB = 1536, 1792, 2048
_C_QI = 2304
_C_KW = 2560
_C_KK = 2688
_C_XC = 2816
_C_GC = 3072


def _cparams(sem):
    return pltpu.CompilerParams(dimension_semantics=sem, vmem_limit_bytes=VMEM_LIMIT)


def _rms(x, g):
    return x * lax.rsqrt(jnp.mean(x * x, axis=-1, keepdims=True) + EPS) * g


def _dot_nt(a, b):
    return lax.dot_general(a, b, (((1,), (1,)), ((), ())), preferred_element_type=F32)


def _round_up(n, m):
    return (n + m - 1) // m * m


def _proj_kernel(x_ref, g_ref, w_ref, qa_ref, qb_ref, qi_ref, ka_ref, va_ref, kb_ref, vb_ref,
                 kw_ref, xc_ref, gc_ref, kab_ref, vab_ref, kbb_ref, vbb_ref, kib_ref):
    h = _rms(x_ref[...], g_ref[...])
    z = jnp.dot(h.astype(BF16), w_ref[...], preferred_element_type=F32)
    qa_ref[...] = (z[:, _C_QA:_C_QA + W_A] * Q_SCALE).astype(BF16)
    qb_ref[...] = (z[:, _C_QB:_C_QB + W_B] * Q_SCALE).astype(BF16)
    qi_ref[...] = z[:, _C_QI:_C_QI + H_IDX * D_IDX].astype(BF16)
    ka = z[:, _C_KA:_C_KA + W_A]
    va = z[:, _C_VA:_C_VA + W_A]
    kb = z[:, _C_KB:_C_KB + W_B]
    vb = z[:, _C_VB:_C_VB + W_B]
    ka_ref[...] = ka
    va_ref[...] = va
    kb_ref[...] = kb
    vb_ref[...] = vb
    kab_ref[...] = ka.astype(BF16)
    vab_ref[...] = va.astype(BF16)
    kbb_ref[...] = kb.astype(BF16)
    vbb_ref[...] = vb.astype(BF16)
    kw_ref[...] = z[:, _C_KW:_C_KW + LANES]
    kib_ref[...] = z[:, _C_KK:_C_KK + LANES].astype(BF16)
    xc_ref[...] = z[:, _C_XC:_C_XC + W_C]
    gc_ref[...] = z[:, _C_GC:_C_GC + W_C]


def _proj(x2d, gain, w_pad):
    t = x2d.shape[0]
    tm = min(512, t)
    widths = [(W_A, BF16), (W_B, BF16), (H_IDX * D_IDX, BF16),
              (W_A, F32), (W_A, F32), (W_B, F32), (W_B, F32),
              (LANES, F32), (W_C, F32), (W_C, F32),
              (W_A, BF16), (W_A, BF16), (W_B, BF16), (W_B, BF16), (LANES, BF16)]
    return pl.pallas_call(
        _proj_kernel,
        grid=(t // tm,),
        in_specs=[pl.BlockSpec((tm, D_MODEL), lambda i: (i, 0)),
                  pl.BlockSpec((1, D_MODEL), lambda i: (0, 0)),
                  pl.BlockSpec((D_MODEL, PROJ_W_PAD), lambda i: (0, 0))],
        out_specs=[pl.BlockSpec((tm, w), lambda i: (i, 0)) for w, _ in widths],
        out_shape=[jax.ShapeDtypeStruct((t, w), d) for w, d in widths],
        compiler_params=_cparams(("parallel",)),
        name="proj",
    )(x2d, gain.reshape(1, D_MODEL), w_pad)


def _tile_bounds(q_start, qb, lk, tk):
    n_full = lax.div(jnp.minimum(q_start + CHUNK, lk), tk)
    n_tiles = lax.div(q_start + qb + tk - 1, tk)
    return n_full, n_tiles


def _admissible(start, tk, q_chunk, lk):
    col = start + lax.broadcasted_iota(jnp.int32, (1, tk), 1)
    return (lax.shift_right_logical(col, CHUNK_SHIFT) <= q_chunk) & (col < lk)


def _fori_pairs(n, body, init=0):
    pairs = lax.div(n, 2)

    def two(i, c):
        return body(2 * i + 1, body(2 * i, c))

    return lax.fori_loop(2 * pairs, n, body, lax.fori_loop(0, pairs, two, init))


def _lane_max(acc, x):
    for g in range(x.shape[1] // LANES):
        acc = jnp.maximum(acc, x[:, g * LANES:(g + 1) * LANES])
    return acc


def _softmax_value_tile(s_ref, idx, m, v):
    parts = []
    for g in range(s_ref.shape[-1] // LANES):
        s = s_ref[idx + (slice(None), slice(g * LANES, (g + 1) * LANES))]
        parts.append(jnp.exp2((s - m).astype(BF16)))
    v_ones = jnp.concatenate([v, jnp.ones_like(v)], axis=1)
    return jnp.dot(jnp.concatenate(parts, axis=1), v_ones, preferred_element_type=F32)


def _diff_attn_kernel(*refs, past, lk, qb, tk, out_scale, cached):
    if cached:
        (lam_ref, gain_ref, q_ref, kn_ref, vn_ref, kc_ref, vc_ref, o_ref,
         s_ref, mx_ref, acc_ref, k_ref, v_ref) = refs
        n_cache, n_new = kc_ref.shape[0], kn_ref.shape[0]
        for cache_ref, new_ref, dst_ref in ((kc_ref, kn_ref, k_ref), (vc_ref, vn_ref, v_ref)):
            dst_ref[0:n_cache, :] = cache_ref[...].astype(BF16)
            dst_ref[n_cache:n_cache + n_new, :] = new_ref[...]
            if tk > n_cache + n_new:
                dst_ref[n_cache + n_new:, :] = jnp.zeros((tk - n_cache - n_new, LANES), BF16)
    else:
        lam_ref, gain_ref, q_ref, k_ref, v_ref, o_ref, s_ref, mx_ref, acc_ref = refs
    j = pl.program_id(2)
    q = q_ref[...]
    lane = lax.broadcasted_iota(jnp.int32, (1, LANES), 1)
    zero = jnp.zeros_like(q)
    q_half = (jnp.where(lane < HEAD_DIM, q, zero), jnp.where(lane >= HEAD_DIM, q, zero))
    q_start = past + j * qb
    row = lax.broadcasted_iota(jnp.int32, (qb, 1), 0)
    q_chunk = lax.shift_right_logical(q_start + row, CHUNK_SHIFT)
    n_full, n_tiles = _tile_bounds(q_start, qb, lk, tk)

    mx_ref[...] = jnp.full(mx_ref.shape, -jnp.inf, F32)
    acc_ref[...] = jnp.zeros(acc_ref.shape, F32)

    def score_tile(t, masked):
        start = pl.multiple_of(t * tk, tk)
        k = k_ref[pl.ds(start, tk), :]
        if masked:
            ok = _admissible(start, tk, q_chunk, lk)
        for i in range(2):
            s = _dot_nt(q_half[i], k)
            if masked:
                s = jnp.where(ok, s, -jnp.inf)
            s_ref[i, t] = s
            mx_ref[i] = _lane_max(mx_ref[i], s)

    def full_body(t, c):
        score_tile(t, False)
        return c

    def masked_body(t, c):
        score_tile(t, True)
        return c

    _fori_pairs(n_full, full_body)
    lax.fori_loop(n_full, n_tiles, masked_body, 0)

    for i in range(2):
        mx_ref[i] = jnp.broadcast_to(jnp.max(mx_ref[i], axis=1, keepdims=True), (qb, LANES))

    def value_tile(t, c):
        start = pl.multiple_of(t * tk, tk)
        v = v_ref[pl.ds(start, tk), :]
        for i in range(2):
            acc_ref[i] = acc_ref[i] + _softmax_value_tile(s_ref, (i, t), mx_ref[i], v)
        return c

    _fori_pairs(n_tiles, value_tile)

    o = (acc_ref[0, :, :LANES] / acc_ref[0, :, LANES:]
         - lam_ref[...] * (acc_ref[1, :, :LANES] / acc_ref[1, :, LANES:]))
    o_ref[...] = (_rms(o, gain_ref[...]) * out_scale).astype(BF16)


def _diff_attn(q, k, v, lam, gain, *, past, lk, tk, out_scale, cache=None):
    b, l, _ = q.shape
    qb = min(l, DIFF_Q_BLOCK)
    cached = cache is not None
    lkp = tk if cached else k.shape[1]
    assert l % qb == 0 and lkp % tk == 0 and qb % CHUNK == 0
    kern = functools.partial(_diff_attn_kernel, past=past, lk=lk, qb=qb, tk=tk,
                             out_scale=out_scale, cached=cached)
    in_specs = [pl.BlockSpec((1, LANES), lambda bi, h, j: (0, 0)),
                pl.BlockSpec((1, LANES), lambda bi, h, j: (0, 0)),
                pl.BlockSpec((None, qb, LANES), lambda bi, h, j: (bi, j, h)),
                pl.BlockSpec((None, k.shape[1], LANES), lambda bi, h, j: (bi, 0, h)),
                pl.BlockSpec((None, k.shape[1], LANES), lambda bi, h, j: (bi, 0, h))]
    scratch = [pltpu.VMEM((2, lkp // tk, qb, tk), F32),
               pltpu.VMEM((2, qb, LANES), F32), pltpu.VMEM((2, qb, 2 * LANES), F32)]
    operands = [lam, gain, q, k, v]
    if cached:
        cache_k, cache_v, li = cache
        assert l == qb and past + l <= tk and cache_k.shape[2:] == (past, W_A)
        head_rows = pl.BlockSpec((None, None, past, LANES), lambda bi, h, j: (li, bi, 0, h))
        in_specs += [head_rows, head_rows]
        scratch += [pltpu.VMEM((tk, LANES), BF16), pltpu.VMEM((tk, LANES), BF16)]
        operands += [cache_k, cache_v]
    return pl.pallas_call(
        kern,
        grid=(b, H_A, l // qb),
        in_specs=in_specs,
        out_specs=pl.BlockSpec((None, qb, LANES), lambda bi, h, j: (bi, j, h)),
        out_shape=jax.ShapeDtypeStruct((b, l, W_A), BF16),
        scratch_shapes=scratch,
        compiler_params=_cparams(("parallel", "parallel", "parallel")),
        name="diff_attn",
    )(*operands)


def _dsa_kernel(qi_ref, kw_ref, q_ref, ki_ref, k_ref, v_ref, o_ref,
                s_ref, sc_ref, lo_ref, hi_ref, mx_ref, acc_ref,
                *, past, lk, qb, tk, topk, max_steps):
    j = pl.program_id(1)
    q_start = past + j * qb
    n_full, n_tiles = _tile_bounds(q_start, qb, lk, tk)
    n_groups = tk // LANES
    row = lax.broadcasted_iota(jnp.int32, (qb, 1), 0)
    q_chunk = lax.shift_right_logical(q_start + row, CHUNK_SHIFT)
    lane = lax.broadcasted_iota(jnp.int32, (1, LANES), 1)
    low_half = lane < HEAD_DIM
    k_sel = float(topk)

    def wide(x):
        return jnp.broadcast_to(x, (qb, LANES))

    def head_views(x):
        views = []
        for h in range(4):
            pair = x[:, (h // 2) * LANES:(h // 2 + 1) * LANES]
            keep = low_half if h % 2 == 0 else jnp.logical_not(low_half)
            views.append(jnp.where(keep, pair, jnp.zeros_like(pair)))
        return views

    qi_h = head_views(qi_ref[...])
    kw = kw_ref[...]
    w_h = [wide(kw[:, D_IDX + h:D_IDX + h + 1]) for h in range(H_IDX)]
    lo_ref[...] = jnp.full(lo_ref.shape, -jnp.inf, F32)
    hi_ref[...] = jnp.full(hi_ref.shape, -jnp.inf, F32)

    def index_tile(t, masked):
        start = pl.multiple_of(t * tk, tk)
        if masked:
            adm = _admissible(start, tk, q_chunk, lk)
        top2 = lo_ref[...]
        top1 = hi_ref[...]
        for c in range(tk // MXU_DIM):
            ki = ki_ref[pl.ds(start + c * MXU_DIM, MXU_DIM), :]
            rel = [jnp.maximum(_dot_nt(qi_h[h], ki), 0.0) for h in range(H_IDX)]
            for g in range(MXU_DIM // LANES):
                sl = slice(g * LANES, (g + 1) * LANES)
                sc = w_h[0] * rel[0][:, sl]
                for h in range(1, H_IDX):
                    sc = sc + w_h[h] * rel[h][:, sl]
                csl = slice(c * MXU_DIM + g * LANES, c * MXU_DIM + (g + 1) * LANES)
                if masked:
                    sc = jnp.where(adm[:, csl], sc, -jnp.inf)
                s_ref[t, :, csl] = sc
                top2 = jnp.maximum(top2, jnp.minimum(top1, sc))
                top1 = jnp.maximum(top1, sc)
        lo_ref[...] = top2
        hi_ref[...] = top1

    def index_full(t, c):
        index_tile(t, False)
        return c

    def index_masked(t, c):
        index_tile(t, True)
        return c

    _fori_pairs(n_full, index_full)
    lax.fori_loop(n_full, n_tiles, index_masked, 0)
    rb = min(qb, SCAN_ROWS)
    ones_mat = jnp.ones((LANES, LANES), BF16)
    tri_i = lax.broadcasted_iota(jnp.int32, (LANES, 2 * LANES), 0)
    tri_j = lax.broadcasted_iota(jnp.int32, (LANES, 2 * LANES), 1)
    prefix_mat = jnp.where(jnp.logical_or(tri_i <= tri_j, tri_j >= LANES), 1.0, 0.0).astype(BF16)
    assert s_ref.shape[0] * n_groups <= 256

    row_blocks = [slice(r * rb, (r + 1) * rb) for r in range(qb // rb)]
    pos = q_start + lax.broadcasted_iota(jnp.int32, (qb, LANES), 0)
    n_adm = jnp.minimum((lax.shift_right_logical(pos, CHUNK_SHIFT) + 1) * CHUNK, lk)
    few = n_adm <= topk

    def lanes_all(x, reduce):
        return jnp.broadcast_to(reduce(x, axis=1, keepdims=True), (qb, LANES))

    def row_sum(acc):
        return jnp.dot(acc.astype(BF16), ones_mat, preferred_element_type=F32)

    def scan(step, init, *operands):
        outs = []
        for rows in row_blocks:
            ops = [o[rows] for o in operands]

            def body(t, acc, rows=rows, ops=ops):
                for g in range(n_groups):
                    acc = step(acc, s_ref[t, rows, g * LANES:(g + 1) * LANES], *ops)
                return acc

            outs.append(lax.fori_loop(0, n_tiles, body, jnp.full((rb, LANES), init, F32)))
        return outs[0] if len(outs) == 1 else jnp.concatenate(outs, axis=0)

    def count_ge(thr):
        return row_sum(scan(lambda acc, s, t: acc + jnp.where(s >= t, 1.0, 0.0), 0.0, thr))

    def max_below(bound):
        acc = scan(lambda acc, s, b: jnp.maximum(acc, jnp.where(s < b, s, -jnp.inf)), -jnp.inf, bound)
        return lanes_all(acc, jnp.max)

    assert topk <= 2 * LANES
    lane_best = hi_ref[...] if topk <= LANES else lo_ref[...]
    lowest = float(jnp.finfo(jnp.float32).min)
    rmin = jnp.maximum(lanes_all(lane_best, jnp.min), lowest)
    rmax = jnp.maximum(lanes_all(lane_best, jnp.max), lowest)

    c_max = count_ge(rmax)
    top_ties = c_max >= k_sel

    def bisect(_, c):
        lo, hi = c
        mid = 0.5 * lo + 0.5 * hi
        ge = count_ge(mid) >= k_sel
        return jnp.where(ge, mid, lo), jnp.where(ge, hi, mid)

    _, hi = lax.fori_loop(0, N_BISECT, bisect, (rmin, rmax))

    def walk_cond(st):
        it, _, _, _, active = st
        return jnp.logical_and(jnp.max(active) > 0.0, it < max_steps)

    def walk_body(st):
        it, cand, thr, c_thr, active = st
        c = count_ge(cand)
        ok = c >= k_sel
        act = active > 0.0
        hit = jnp.logical_and(act, ok)
        thr = jnp.where(hit, cand, thr)
        c_thr = jnp.where(hit, c, c_thr)
        active = jnp.where(jnp.logical_and(act, jnp.logical_not(ok)), 1.0, 0.0)
        return it + 1, max_below(cand), thr, c_thr, active

    thr0 = jnp.where(few, lowest, jnp.where(top_ties, rmax, lowest))
    c0 = jnp.where(few, k_sel, jnp.where(top_ties, c_max, k_sel))
    active0 = jnp.where(jnp.logical_or(few, top_ties), 0.0, 1.0)
    _, _, thr, c_thr, _ = lax.while_loop(
        walk_cond, walk_body, (jnp.int32(0), max_below(hi), thr0, c0, active0))

    has_excess = jnp.max(jnp.where(c_thr > k_sel, 1.0, 0.0)) > 0.0

    def ranked_bias():
        n_tie = k_sel - row_sum(scan(lambda acc, s, t: acc + jnp.where(s > t, 1.0, 0.0), 0.0, thr))
        for rows in row_blocks:
            thr_r = thr[rows]
            n_tie_r = n_tie[rows]

            def body(t, before, rows=rows, thr_r=thr_r, n_tie_r=n_tie_r):
                for g in range(n_groups):
                    lanes = slice(g * LANES, (g + 1) * LANES)
                    s = s_ref[t, rows, lanes]
                    tie = s == thr_r
                    pr = jnp.dot(jnp.where(tie, 1.0, 0.0).astype(BF16), prefix_mat,
                                 preferred_element_type=F32)
                    rank = before + pr[:, :LANES]
                    tie_bias = jnp.where(rank <= n_tie_r, 0.0, MASKED)
                    s_ref[t, rows, lanes] = jnp.where(s > thr_r, 0.0, jnp.where(tie, tie_bias, MASKED))
                    before = before + pr[:, LANES:]
                return before

            _fori_pairs(n_tiles, body, jnp.zeros((rb, LANES), F32))
        return 0

    def plain_bias():
        for rows in row_blocks:
            thr_r = thr[rows]

            def body(t, c, rows=rows, thr_r=thr_r):
                for g in range(n_groups):
                    lanes = slice(g * LANES, (g + 1) * LANES)
                    s_ref[t, rows, lanes] = jnp.where(s_ref[t, rows, lanes] >= thr_r, 0.0, MASKED)
                return c

            lax.fori_loop(0, n_tiles, body, 0)
        return 0

    lax.cond(has_excess, ranked_bias, plain_bias)

    q_h = head_views(q_ref[...])
    for g in range(H_B // 2):
        lanes = slice(g * LANES, (g + 1) * LANES)
        mx_ref[...] = jnp.full(mx_ref.shape, -jnp.inf, F32)
        acc_ref[...] = jnp.zeros(acc_ref.shape, F32)

        def score_tile(t, c, g=g, lanes=lanes):
            start = pl.multiple_of(t * tk, tk)
            k = k_ref[pl.ds(start, tk), lanes]
            bias = s_ref[t]
            for i in range(2):
                s = _dot_nt(q_h[2 * g + i], k) + bias
                sc_ref[i, t] = s
                mx_ref[i] = _lane_max(mx_ref[i], s)
            return c

        _fori_pairs(n_tiles, score_tile)
        for i in range(2):
            mx_ref[i] = jnp.broadcast_to(jnp.max(mx_ref[i], axis=1, keepdims=True), (qb, LANES))

        def value_tile(t, c, lanes=lanes):
            start = pl.multiple_of(t * tk, tk)
            v = v_ref[pl.ds(start, tk), lanes]
            for i in range(2):
                acc_ref[i] = acc_ref[i] + _softmax_value_tile(sc_ref, (i, t), mx_ref[i], v)
            return c

        _fori_pairs(n_tiles, value_tile)
        even = acc_ref[0, :, :LANES] / acc_ref[0, :, LANES:]
        odd = acc_ref[1, :, :LANES] / acc_ref[1, :, LANES:]
        o_ref[:, lanes] = jnp.where(low_half, even, odd).astype(BF16)


def _dsa(qi, kw, q, ki, k, v, *, past, lk, tk):
    b, l, _ = q.shape
    lkp = k.shape[1]
    qb = min(l, Q_BLOCK)
    assert l % qb == 0 and lkp % tk == 0 and tk % MXU_DIM == 0 and qb % CHUNK == 0
    kern = functools.partial(_dsa_kernel, past=past, lk=lk, qb=qb, tk=tk,
                             topk=min(TOPK, lk // 4), max_steps=lkp)
    return pl.pallas_call(
        kern,
        grid=(b, l // qb),
        in_specs=[pl.BlockSpec((None, qb, H_IDX * D_IDX), lambda bi, j: (bi, j, 0)),
                  pl.BlockSpec((None, qb, LANES), lambda bi, j: (bi, j, 0)),
                  pl.BlockSpec((None, qb, W_B), lambda bi, j: (bi, j, 0)),
                  pl.BlockSpec((None, lkp, LANES), lambda bi, j: (bi, 0, 0)),
                  pl.BlockSpec((None, lkp, W_B), lambda bi, j: (bi, 0, 0)),
                  pl.BlockSpec((None, lkp, W_B), lambda bi, j: (bi, 0, 0))],
        out_specs=pl.BlockSpec((None, qb, W_B), lambda bi, j: (bi, j, 0)),
        out_shape=jax.ShapeDtypeStruct((b, l, W_B), BF16),
        scratch_shapes=[pltpu.VMEM((lkp // tk, qb, tk), F32),
                        pltpu.VMEM((2, lkp // tk, qb, tk), F32),
                        pltpu.VMEM((qb, LANES), F32), pltpu.VMEM((qb, LANES), F32),
                        pltpu.VMEM((2, qb, LANES), F32), pltpu.VMEM((2, qb, 2 * LANES), F32)],
        compiler_params=_cparams(("parallel", "parallel")),
        name="dsa",
    )(qi, kw, q, ki, k, v)


def _shift_rows(x, prev, k):
    rolled = pltpu.roll(x, k, 0)
    row = lax.broadcasted_iota(jnp.int32, (SUBLANES, 1), 0)
    top = jnp.where(row < k, pltpu.roll(prev, k, 0), rolled[0:SUBLANES])
    if x.shape[0] == SUBLANES:
        return top
    return jnp.concatenate([top, rolled[SUBLANES:]], axis=0)


def _rglru_kernel(xc_ref, gc_ref, cst_ref, h0_ref, cw_ref, cb_ref, wg_ref, bg_ref, lam_ref,
                  oc_ref, hl_ref, cn_ref, prev_ref, h_ref, a_ref, b_ref, hs_ref, *, past, tl):
    i = pl.program_id(1)

    @pl.when(i == 0)
    def _():
        prev_ref[...] = jnp.zeros(prev_ref.shape, F32)
        prev_ref[SUBLANES - (CONV_C - 1):SUBLANES, :] = cst_ref[...]
        h_ref[...] = h0_ref[...]

    x = xc_ref[...]
    prev = prev_ref[...]
    cw = cw_ref[...]
    xconv = cw[CONV_C - 1:CONV_C] * x + cb_ref[...]
    for k in range(1, CONV_C):
        xconv = xconv + cw[CONV_C - 1 - k:CONV_C - k] * _shift_rows(x, prev, k)
    prev_ref[...] = x[tl - SUBLANES:tl]

    pre = jnp.dot(xconv.astype(BF16), wg_ref[...], preferred_element_type=F32) + bg_ref[...]
    r = jax.nn.sigmoid(pre[:, :W_C])
    gate_i = jax.nn.sigmoid(pre[:, W_C:])
    neg_lam = -lam_ref[...]
    softplus = jnp.maximum(neg_lam, 0.0) + jnp.log1p(jnp.exp(-jnp.abs(neg_lam)))
    log_a = -RG_C * r * softplus
    pos = past + i * tl + lax.broadcasted_iota(jnp.int32, (tl, 1), 0)
    th = jnp.tanh(log_a)
    mult = jnp.where(pos == 0, 1.0, jnp.sqrt(-2.0 * th / (1.0 - th)))
    a_ref[...] = jnp.exp(log_a)
    b_ref[...] = mult * gate_i * xconv

    row8 = lax.broadcasted_iota(jnp.int32, (SUBLANES, 1), 0)

    def tile_step(i, h):
        r0 = pl.multiple_of(i * SUBLANES, SUBLANES)
        a = a_ref[pl.ds(r0, SUBLANES), :]
        b = b_ref[pl.ds(r0, SUBLANES), :]
        for sh in (1, 2, 4):
            a_prev = jnp.where(row8 >= sh, pltpu.roll(a, sh, 0), 1.0)
            b_prev = jnp.where(row8 >= sh, pltpu.roll(b, sh, 0), 0.0)
            b = a * b_prev + b
            a = a * a_prev
        hs = a * h + b
        hs_ref[pl.ds(r0, SUBLANES), :] = hs
        return hs[SUBLANES - 1:SUBLANES, :]

    h_last = lax.fori_loop(0, tl // SUBLANES, tile_step, h_ref[...], unroll=4)
    h_ref[...] = h_last

    gc = gc_ref[...]
    gelu = 0.5 * gc * (1.0 + jnp.tanh(math.sqrt(2.0 / math.pi) * (gc + 0.044715 * (gc * gc * gc))))
    oc_ref[...] = (hs_ref[...] * gelu).astype(BF16)

    @pl.when(i == pl.num_programs(1) - 1)
    def _():
        hl_ref[...] = h_last
        cn_ref[...] = x[tl - (CONV_C - 1):tl]


def _rglru(xc, gc, conv_state, h0, conv_w, conv_b, w_gate, b_gate, lam, *, past):
    b, l, _ = xc.shape
    tl = min(l, 512)
    assert l >= SUBLANES and l % tl == 0
    kern = functools.partial(_rglru_kernel, past=past, tl=tl)
    const = lambda bi, i: (0, 0)
    return pl.pallas_call(
        kern,
        grid=(b, l // tl),
        in_specs=[pl.BlockSpec((None, tl, W_C), lambda bi, i: (bi, i, 0)),
                  pl.BlockSpec((None, tl, W_C), lambda bi, i: (bi, i, 0)),
                  pl.BlockSpec((None, CONV_C - 1, W_C), lambda bi, i: (bi, 0, 0)),
                  pl.BlockSpec((None, 1, W_C), lambda bi, i: (bi, 0, 0)),
                  pl.BlockSpec((CONV_C, W_C), const),
                  pl.BlockSpec((1, W_C), const),
                  pl.BlockSpec((W_C, 2 * W_C), const),
                  pl.BlockSpec((1, 2 * W_C), const),
                  pl.BlockSpec((1, W_C), const)],
        out_specs=[pl.BlockSpec((None, tl, W_C), lambda bi, i: (bi, i, 0)),
                   pl.BlockSpec((None, 1, W_C), lambda bi, i: (bi, 0, 0)),
                   pl.BlockSpec((None, CONV_C - 1, W_C), lambda bi, i: (bi, 0, 0))],
        out_shape=[jax.ShapeDtypeStruct((b, l, W_C), BF16),
                   jax.ShapeDtypeStruct((b, 1, W_C), F32),
                   jax.ShapeDtypeStruct((b, CONV_C - 1, W_C), F32)],
        scratch_shapes=[pltpu.VMEM((SUBLANES, W_C), F32), pltpu.VMEM((1, W_C), F32),
                        pltpu.VMEM((tl, W_C), F32), pltpu.VMEM((tl, W_C), F32),
                        pltpu.VMEM((tl, W_C), F32)],
        compiler_params=_cparams(("parallel", "arbitrary")),
        name="rglru",
    )(xc, gc, conv_state, h0.reshape(b, 1, W_C), conv_w, conv_b.reshape(1, W_C), w_gate,
      b_gate, lam.reshape(1, W_C))


def _outproj_kernel(x_ref, oa_ref, ob_ref, oc_ref, w_ref, g_ref, x1_ref, hn_ref):
    mix = jnp.dot(oa_ref[...], w_ref[0:W_A, :], preferred_element_type=F32)
    mix = mix + jnp.dot(ob_ref[...], w_ref[W_A:W_A + W_B, :], preferred_element_type=F32)
    mix = mix + jnp.dot(oc_ref[...], w_ref[W_A + W_B:, :], preferred_element_type=F32)
    x1 = x_ref[...] + mix
    x1_ref[...] = x1
    hn_ref[...] = _rms(x1, g_ref[...]).astype(BF16)


def _outproj(x2d, oa, ob, oc, w_out, gain):
    t = x2d.shape[0]
    tm = min(512, t)
    row = lambda w: pl.BlockSpec((tm, w), lambda i: (i, 0))
    return pl.pallas_call(
        _outproj_kernel,
        grid=(t // tm,),
        in_specs=[row(D_MODEL), row(W_A), row(W_B), row(W_C),
                  pl.BlockSpec((D_MODEL, D_MODEL), lambda i: (0, 0)),
                  pl.BlockSpec((1, D_MODEL), lambda i: (0, 0))],
        out_specs=[row(D_MODEL), row(D_MODEL)],
        out_shape=[jax.ShapeDtypeStruct((t, D_MODEL), F32), jax.ShapeDtypeStruct((t, D_MODEL), BF16)],
        compiler_params=_cparams(("parallel",)),
        name="outproj",
    )(x2d, oa, ob, oc, w_out, gain.reshape(1, D_MODEL))


def _ffn_kernel(hn_ref, x1_ref, wu_ref, wg_ref, cwu_ref, cwg_ref, cbu_ref, cbg_ref, wd_ref,
                su_ref, sg_ref, gfin_ref, y_ref, fu_ref, fg_ref,
                acc_ref, au_ref, ag_ref, cu_ref, cg_ref, *, tm, final_norm):
    i = pl.program_id(1)
    s = pl.program_id(2)
    nf = pl.num_programs(2) - 1
    live = s > 0
    fb = jnp.maximum(s - 1, 0)
    row = jnp.where(live, fb, nf)

    @pl.when(s == 0)
    def _():
        au_ref[1] = jnp.zeros(au_ref.shape[1:], F32)
        ag_ref[1] = jnp.zeros(ag_ref.shape[1:], F32)
        cu_ref[nf] = jnp.zeros(cu_ref.shape[1:], F32)
        cg_ref[nf] = jnp.zeros(cg_ref.shape[1:], F32)
        acc_ref[...] = x1_ref[...]

    @pl.when(jnp.logical_and(i == 0, live))
    def _():
        for carry_ref, st_ref in ((cu_ref, su_ref), (cg_ref, sg_ref)):
            carry_ref[fb] = jnp.zeros(carry_ref.shape[1:], F32)
            carry_ref[fb, SUBLANES - (CONV_F - 1):SUBLANES, :] = st_ref[...]

    def step(wslot, rslot):
        au_ref[rslot, 0:SUBLANES, :] = cu_ref[row]
        ag_ref[rslot, 0:SUBLANES, :] = cg_ref[row]
        rc = tm // FFN_ROW_CHUNKS

        def conv(a_ref, cw_ref, cb_ref, r0):
            cw = cw_ref[...]
            y = cw[CONV_F - 1:CONV_F] * a_ref[rslot, SUBLANES + r0:SUBLANES + r0 + rc, :] + cb_ref[...]
            for k in range(1, CONV_F):
                y = y + (cw[CONV_F - 1 - k:CONV_F - k]
                         * a_ref[rslot, SUBLANES - k + r0:SUBLANES - k + r0 + rc, :])
            return y

        for c in range(FFN_ROW_CHUNKS):
            r0 = c * rc
            hn = hn_ref[r0:r0 + rc, :]
            au_ref[wslot, SUBLANES + r0:SUBLANES + r0 + rc, :] = jnp.dot(
                hn, wu_ref[...], preferred_element_type=F32)
            ag_ref[wslot, SUBLANES + r0:SUBLANES + r0 + rc, :] = jnp.dot(
                hn, wg_ref[...], preferred_element_type=F32)
            u = conv(au_ref, cwu_ref, cbu_ref, r0)
            g = conv(ag_ref, cwg_ref, cbg_ref, r0)
            mid = (g * jax.nn.sigmoid(g) * u).astype(BF16)
            contrib = jnp.dot(mid, wd_ref[...], preferred_element_type=F32)
            acc_ref[r0:r0 + rc, :] = acc_ref[r0:r0 + rc, :] + jnp.where(live, contrib, 0.0)

        for a_ref, carry_ref, tail_ref in ((au_ref, cu_ref, fu_ref), (ag_ref, cg_ref, fg_ref)):
            carry_ref[row] = a_ref[rslot, tm:tm + SUBLANES, :]
            tail_ref[row] = a_ref[rslot, tm + SUBLANES - (CONV_F - 1):tm + SUBLANES, :]

    parity = lax.rem(s, 2)

    @pl.when(parity == 0)
    def _():
        step(0, 1)

    @pl.when(parity == 1)
    def _():
        step(1, 0)

    @pl.when(s == nf)
    def _():
        y = acc_ref[...]
        if final_norm:
            y = _rms(y, gfin_ref[...])
        y_ref[...] = y


def _ffn(hn, x1, w_up, conv_w, conv_b, w_down, state, final_gain, *, final_norm):
    b, l, _ = hn.shape
    tm = min(l, 1024)
    tf = 512
    nf = D_FF // tf
    assert l % tm == 0 and tm >= SUBLANES
    kern = functools.partial(_ffn_kernel, tm=tm, final_norm=final_norm)
    conv_b = conv_b.reshape(1, 2 * D_FF)
    up = lambda s: jnp.minimum(s, nf - 1)
    fin = lambda s: jnp.maximum(s - 1, 0)
    tail_spec = pl.BlockSpec((None, nf + 1, CONV_F - 1, tf), lambda bi, i, s: (bi, 0, 0, 0))
    tail_shape = jax.ShapeDtypeStruct((b, nf + 1, CONV_F - 1, tf), F32)
    y, fu, fg = pl.pallas_call(
        kern,
        grid=(b, l // tm, nf + 1),
        in_specs=[pl.BlockSpec((None, tm, D_MODEL), lambda bi, i, s: (bi, i, 0)),
                  pl.BlockSpec((None, tm, D_MODEL), lambda bi, i, s: (bi, i, 0)),
                  pl.BlockSpec((D_MODEL, tf), lambda bi, i, s: (0, up(s))),
                  pl.BlockSpec((D_MODEL, tf), lambda bi, i, s: (0, nf + up(s))),
                  pl.BlockSpec((CONV_F, tf), lambda bi, i, s: (0, fin(s))),
                  pl.BlockSpec((CONV_F, tf), lambda bi, i, s: (0, nf + fin(s))),
                  pl.BlockSpec((1, tf), lambda bi, i, s: (0, fin(s))),
                  pl.BlockSpec((1, tf), lambda bi, i, s: (0, nf + fin(s))),
                  pl.BlockSpec((tf, D_MODEL), lambda bi, i, s: (fin(s), 0)),
                  pl.BlockSpec((None, CONV_F - 1, tf), lambda bi, i, s: (bi, 0, fin(s))),
                  pl.BlockSpec((None, CONV_F - 1, tf), lambda bi, i, s: (bi, 0, nf + fin(s))),
                  pl.BlockSpec((1, D_MODEL), lambda bi, i, s: (0, 0))],
        out_specs=[pl.BlockSpec((None, tm, D_MODEL), lambda bi, i, s: (bi, i, 0)),
                   tail_spec, tail_spec],
        out_shape=[jax.ShapeDtypeStruct((b, l, D_MODEL), F32), tail_shape, tail_shape],
        scratch_shapes=[pltpu.VMEM((tm, D_MODEL), F32),
                        pltpu.VMEM((2, tm + SUBLANES, tf), F32),
                        pltpu.VMEM((2, tm + SUBLANES, tf), F32),
                        pltpu.VMEM((nf + 1, SUBLANES, tf), F32),
                        pltpu.VMEM((nf + 1, SUBLANES, tf), F32)],
        compiler_params=_cparams(("parallel", "arbitrary", "arbitrary")),
        name="ffn",
    )(hn, x1, w_up, w_up, conv_w, conv_w, conv_b, conv_b, w_down, state, state,
      final_gain.reshape(1, D_MODEL))
    flat = lambda a: jnp.swapaxes(a[:, :nf], 1, 2).reshape(b, CONV_F - 1, D_FF)
    return y, jnp.concatenate([flat(fu), flat(fg)], axis=-1)


def _cast_kernel(x_ref, o_ref):
    o_ref[...] = x_ref[...].astype(o_ref.dtype)


def _layer_bf16(w, li):
    _, r, c = w.shape
    tr = 256 if r % 256 == 0 else r
    return pl.pallas_call(
        _cast_kernel,
        grid=(r // tr,),
        in_specs=[pl.BlockSpec((None, tr, c), lambda i: (li, i, 0))],
        out_specs=pl.BlockSpec((tr, c), lambda i: (i, 0)),
        out_shape=jax.ShapeDtypeStruct((r, c), BF16),
        compiler_params=_cparams(("parallel",)),
        name="cast",
    )(w)


def _prep_layer_weights(p, li):
    w_in = _layer_bf16(p["w_in"], li)
    zeros = lambda n: jnp.zeros((D_MODEL, n), w_in.dtype)
    k_idx = w_in[:, 2560:2624]
    w_pad = jnp.concatenate(
        [w_in[:, :2628], zeros(_C_KK - 2628), k_idx, k_idx, w_in[:, 2628:]], axis=1)
    assert w_pad.shape[1] == PROJ_W_PAD

    def block_diag(w):
        out = jnp.zeros((W_C, W_C), w.dtype)
        for n in range(N_GATE_BLOCKS):
            sl = slice(n * GATE_BLOCK, (n + 1) * GATE_BLOCK)
            out = out.at[sl, sl].set(w[n])
        return out

    w_gate = jnp.concatenate([block_diag(p["rg_w_r"][li]), block_diag(p["rg_w_i"][li])], axis=1)
    b_gate = jnp.concatenate([p["rg_b_r"][li].reshape(1, W_C), p["rg_b_i"][li].reshape(1, W_C)], axis=1)
    lam_init = 0.8 - 0.6 * math.exp(-0.3 * li)
    f32 = lambda a: a.astype(F32)
    lam = (jnp.exp(jnp.sum(f32(p["lam_q1"][li]) * f32(p["lam_k1"][li])))
           - jnp.exp(jnp.sum(f32(p["lam_q2"][li]) * f32(p["lam_k2"][li]))) + lam_init)
    return dict(
        norm_mix=p["norm_mix"][li], w_pad=w_pad, lam=jnp.full((1, LANES), lam, F32),
        lam_init=lam_init, diff_gain=p["diff_gain"][li].reshape(1, LANES),
        rg_conv_w=p["rg_conv_w"][li], rg_conv_b=p["rg_conv_b"][li],
        w_gate=w_gate.astype(BF16), b_gate=b_gate, rg_lambda=p["rg_lambda"][li],
        w_out=_layer_bf16(p["w_out"], li), norm_ffn=p["norm_ffn"][li],
        ffn_w_up=_layer_bf16(p["ffn_w_up"], li), ffn_conv_w=p["ffn_conv_w"][li],
        ffn_conv_b=p["ffn_conv_b"][li], ffn_w_down=_layer_bf16(p["ffn_w_down"], li))


def _layer(x, past, attn_cache, w, final_gain, final_norm):
    b, l, _ = x.shape
    _, _, b_k0, b_v0, b_ki0, c_h0, c_cv0, f_cv0 = past
    p_len = 0 if b_k0 is None else b_k0.shape[1]
    lk = p_len + l
    tk = PROMPT_KEY_TILE if p_len == 0 else _round_up(lk, MXU_DIM)
    lkp = _round_up(lk, tk)
    t = b * l

    (qa, qb, qi, ka, va, kb, vb, kw, xc, gc, kab, vab, kbb, vbb, kib) = _proj(
        x.reshape(t, D_MODEL), w["norm_mix"], w["w_pad"])

    def keys(cache, new, dup=False):
        new = new.reshape(b, l, -1)
        parts = []
        if cache is not None:
            c = cache.reshape(b, p_len, -1).astype(BF16)
            parts.append(jnp.concatenate([c, c], axis=-1) if dup else c)
        parts.append(new)
        if lkp > lk:
            parts.append(jnp.zeros((b, lkp - lk, new.shape[-1]), BF16))
        return parts[0] if len(parts) == 1 else jnp.concatenate(parts, axis=1)

    if attn_cache is None:
        k_a, v_a = keys(None, kab), keys(None, vab)
    else:
        k_a, v_a = kab.reshape(b, l, W_A), vab.reshape(b, l, W_A)
    o_a = _diff_attn(qa.reshape(b, l, W_A), k_a, v_a, w["lam"], w["diff_gain"], past=p_len, lk=lk,
                     tk=tk, out_scale=1.0 - w["lam_init"], cache=attn_cache)
    o_b = _dsa(qi.reshape(b, l, -1), kw.reshape(b, l, LANES), qb.reshape(b, l, W_B),
               keys(b_ki0, kib, dup=True), keys(b_k0, kbb), keys(b_v0, vbb),
               past=p_len, lk=lk, tk=tk)
    o_c, h_last, conv_new = _rglru(xc.reshape(b, l, W_C), gc.reshape(b, l, W_C), c_cv0, c_h0,
                                   w["rg_conv_w"], w["rg_conv_b"], w["w_gate"], w["b_gate"],
                                   w["rg_lambda"], past=p_len)
    x1, hn = _outproj(x.reshape(t, D_MODEL), o_a.reshape(t, W_A), o_b.reshape(t, W_B),
                      o_c.reshape(t, W_C), w["w_out"], w["norm_ffn"])
    y, f_buf = _ffn(hn.reshape(b, l, D_MODEL), x1.reshape(b, l, D_MODEL), w["ffn_w_up"],
                    w["ffn_conv_w"], w["ffn_conv_b"], w["ffn_w_down"], f_cv0, final_gain,
                    final_norm=final_norm)
    new = (ka.reshape(b, l, H_A, 2 * HEAD_DIM), va.reshape(b, l, H_A, 2 * HEAD_DIM),
           kb.reshape(b, l, H_B, HEAD_DIM), vb.reshape(b, l, H_B, HEAD_DIM),
           kw[:, :D_IDX].reshape(b, l, D_IDX), h_last.reshape(b, W_C), conv_new, f_buf)
    return y, new


def _trunk(x, past, weights, final_gain):
    states = []
    if past[0] is not None:
        flat_heads = lambda c: c.reshape(c.shape[:3] + (W_A,))
        cache_a = (flat_heads(past[0]), flat_heads(past[1]))
    for li in range(N_LAYERS):
        layer_past = tuple(None if c is None else c[li] for c in past)
        attn_cache = None if past[0] is None else cache_a + (li,)
        x, st = _layer(x, layer_past, attn_cache, weights[li], final_gain,
                       final_norm=(li == N_LAYERS - 1))
        states.append(st)
    return x, states


def _forward(x_prompt, x_sample, caches, params):
    weights = [_prep_layer_weights(params, li) for li in range(N_LAYERS)]
    bp = x_prompt.shape[0]
    dt = x_prompt.dtype
    past_prompt = (None, None, None, None, None,
                   jnp.zeros((N_LAYERS, bp, W_C), dt),
                   jnp.zeros((N_LAYERS, bp, CONV_C - 1, W_C), dt),
                   jnp.zeros((N_LAYERS, bp, CONV_F - 1, 2 * D_FF), dt))
    yp, sp = _trunk(x_prompt, past_prompt, weights, params["norm_final"])
    ys, ss = _trunk(x_sample, caches, weights, params["norm_final"])
    out = [yp, ys]
    for jdx in range(8):
        out.append(jnp.stack([st[jdx] for st in sp], axis=0))
        out.append(jnp.stack([st[jdx] for st in ss], axis=0))
    return tuple(out)


def kernel(x_prompt, x_sample, cache_a_k, cache_a_v, cache_b_k, cache_b_v, cache_b_kidx,
           state_c_h, state_c_conv, state_ffn_conv, norm_mix, w_in, lam_q1, lam_k1, lam_q2,
           lam_k2, diff_gain, rg_conv_w, rg_conv_b, rg_w_r, rg_b_r, rg_w_i, rg_b_i, rg_lambda,
           w_out, norm_ffn, ffn_w_up, ffn_conv_w, ffn_conv_b, ffn_w_down, norm_final):
    params = dict(norm_mix=norm_mix, w_in=w_in, lam_q1=lam_q1, lam_k1=lam_k1, lam_q2=lam_q2,
                  lam_k2=lam_k2, diff_gain=diff_gain, rg_conv_w=rg_conv_w, rg_conv_b=rg_conv_b,
                  rg_w_r=rg_w_r, rg_b_r=rg_b_r, rg_w_i=rg_w_i, rg_b_i=rg_b_i, rg_lambda=rg_lambda,
                  w_out=w_out, norm_ffn=norm_ffn, ffn_w_up=ffn_w_up, ffn_conv_w=ffn_conv_w,
                  ffn_conv_b=ffn_conv_b, ffn_w_down=ffn_w_down, norm_final=norm_final)
    caches = (cache_a_k, cache_a_v, cache_b_k, cache_b_v, cache_b_kidx,
              state_c_h, state_c_conv, state_ffn_conv)
    return _forward(x_prompt, x_sample, caches, params)
```

```python
import functools
import math

import jax
import jax.numpy as jnp
from jax import lax
from jax.experimental import pallas as pl
from jax.experimental.pallas import tpu as pltpu

F32 = jnp.float32
BF16 = jnp.bfloat16

D_MODEL = 1024
N_LAYERS = 2
CHUNK = 64
CHUNK_SHIFT = 6
HEAD_DIM = 64
H_A = 4
W_A = H_A * 2 * HEAD_DIM
H_B = 4
W_B = H_B * HEAD_DIM
H_IDX = 4
D_IDX = 64
TOPK = 256
W_C = 256
N_GATE_BLOCKS = 4
GATE_BLOCK = W_C // N_GATE_BLOCKS
RG_C = 8.0
CONV_C = 4
D_FF = 3072
CONV_F = 3
EPS = 1e-6

LANES = 128
SUBLANES = 8
MXU_DIM = 256
Q_BLOCK = 256
DIFF_Q_BLOCK = 512
SCAN_ROWS = 128
PROMPT_KEY_TILE = 1024
VMEM_LIMIT = 58 * 2**20
MASKED = -1e30
N_BISECT = 14
FFN_ROW_CHUNKS = 2
Q_SCALE = HEAD_DIM ** -0.5 * math.log2(math.e)

PROJ_W_PAD = 3328
_C_QA, _C_KA, _C_VA = 0, 512, 1024
_C_QB, _C_KB, _C_VB = 1536, 1792, 2048
_C_QI = 2304
_C_KW = 2560
_C_KK = 2688
_C_XC = 2816
_C_GC = 3072


def _cparams(sem):
    return pltpu.CompilerParams(dimension_semantics=sem, vmem_limit_bytes=VMEM_LIMIT)


def _rms(x, g):
    return x * lax.rsqrt(jnp.mean(x * x, axis=-1, keepdims=True) + EPS) * g


def _dot_nt(a, b):
    return lax.dot_general(a, b, (((1,), (1,)), ((), ())), preferred_element_type=F32)


def _round_up(n, m):
    return (n + m - 1) // m * m


def _proj_kernel(x_ref, g_ref, w_ref, qa_ref, qb_ref, qi_ref, ka_ref, va_ref, kb_ref, vb_ref,
                 kw_ref, xc_ref, gc_ref, kab_ref, vab_ref, kbb_ref, vbb_ref, kib_ref):
    h = _rms(x_ref[...], g_ref[...])
    z = jnp.dot(h.astype(BF16), w_ref[...], preferred_element_type=F32)
    qa_ref[...] = (z[:, _C_QA:_C_QA + W_A] * Q_SCALE).astype(BF16)
    qb_ref[...] = (z[:, _C_QB:_C_QB + W_B] * Q_SCALE).astype(BF16)
    qi_ref[...] = z[:, _C_QI:_C_QI + H_IDX * D_IDX].astype(BF16)
    ka = z[:, _C_KA:_C_KA + W_A]
    va = z[:, _C_VA:_C_VA + W_A]
    kb = z[:, _C_KB:_C_KB + W_B]
    vb = z[:, _C_VB:_C_VB + W_B]
    ka_ref[...] = ka
    va_ref[...] = va
    kb_ref[...] = kb
    vb_ref[...] = vb
    kab_ref[...] = ka.astype(BF16)
    vab_ref[...] = va.astype(BF16)
    kbb_ref[...] = kb.astype(BF16)
    vbb_ref[...] = vb.astype(BF16)
    kw_ref[...] = z[:, _C_KW:_C_KW + LANES]
    kib_ref[...] = z[:, _C_KK:_C_KK + LANES].astype(BF16)
    xc_ref[...] = z[:, _C_XC:_C_XC + W_C]
    gc_ref[...] = z[:, _C_GC:_C_GC + W_C]


def _proj(x2d, gain, w_pad):
    t = x2d.shape[0]
    tm = min(512, t)
    widths = [(W_A, BF16), (W_B, BF16), (H_IDX * D_IDX, BF16),
              (W_A, F32), (W_A, F32), (W_B, F32), (W_B, F32),
              (LANES, F32), (W_C, F32), (W_C, F32),
              (W_A, BF16), (W_A, BF16), (W_B, BF16), (W_B, BF16), (LANES, BF16)]
    return pl.pallas_call(
        _proj_kernel,
        grid=(t // tm,),
        in_specs=[pl.BlockSpec((tm, D_MODEL), lambda i: (i, 0)),
                  pl.BlockSpec((1, D_MODEL), lambda i: (0, 0)),
                  pl.BlockSpec((D_MODEL, PROJ_W_PAD), lambda i: (0, 0))],
        out_specs=[pl.BlockSpec((tm, w), lambda i: (i, 0)) for w, _ in widths],
        out_shape=[jax.ShapeDtypeStruct((t, w), d) for w, d in widths],
        compiler_params=_cparams(("parallel",)),
        name="proj",
    )(x2d, gain.reshape(1, D_MODEL), w_pad)


def _tile_bounds(q_start, qb, lk, tk):
    n_full = lax.div(jnp.minimum(q_start + CHUNK, lk), tk)
    n_tiles = lax.div(q_start + qb + tk - 1, tk)
    return n_full, n_tiles


def _admissible(start, tk, q_chunk, lk):
    col = start + lax.broadcasted_iota(jnp.int32, (1, tk), 1)
    return (lax.shift_right_logical(col, CHUNK_SHIFT) <= q_chunk) & (col < lk)


def _fori_pairs(n, body, init=0):
    pairs = lax.div(n, 2)

    def two(i, c):
        return body(2 * i + 1, body(2 * i, c))

    return lax.fori_loop(2 * pairs, n, body, lax.fori_loop(0, pairs, two, init))


def _lane_max(acc, x):
    for g in range(x.shape[1] // LANES):
        acc = jnp.maximum(acc, x[:, g * LANES:(g + 1) * LANES])
    return acc


def _softmax_value_tile(s_ref, idx, m, v):
    parts = []
    for g in range(s_ref.shape[-1] // LANES):
        s = s_ref[idx + (slice(None), slice(g * LANES, (g + 1) * LANES))]
        parts.append(jnp.exp2((s - m).astype(BF16)))
    v_ones = jnp.concatenate([v, jnp.ones_like(v)], axis=1)
    return jnp.dot(jnp.concatenate(parts, axis=1), v_ones, preferred_element_type=F32)


def _diff_attn_kernel(*refs, past, lk, qb, tk, out_scale, cached):
    if cached:
        (lam_ref, gain_ref, q_ref, kn_ref, vn_ref, kc_ref, vc_ref, o_ref,
         s_ref, mx_ref, acc_ref, k_ref, v_ref) = refs
        n_cache, n_new = kc_ref.shape[0], kn_ref.shape[0]
        for cache_ref, new_ref, dst_ref in ((kc_ref, kn_ref, k_ref), (vc_ref, vn_ref, v_ref)):
            dst_ref[0:n_cache, :] = cache_ref[...].astype(BF16)
            dst_ref[n_cache:n_cache + n_new, :] = new_ref[...]
            if tk > n_cache + n_new:
                dst_ref[n_cache + n_new:, :] = jnp.zeros((tk - n_cache - n_new, LANES), BF16)
    else:
        lam_ref, gain_ref, q_ref, k_ref, v_ref, o_ref, s_ref, mx_ref, acc_ref = refs
    j = pl.program_id(2)
    q = q_ref[...]
    lane = lax.broadcasted_iota(jnp.int32, (1, LANES), 1)
    zero = jnp.zeros_like(q)
    q_half = (jnp.where(lane < HEAD_DIM, q, zero), jnp.where(lane >= HEAD_DIM, q, zero))
    q_start = past + j * qb
    row = lax.broadcasted_iota(jnp.int32, (qb, 1), 0)
    q_chunk = lax.shift_right_logical(q_start + row, CHUNK_SHIFT)
    n_full, n_tiles = _tile_bounds(q_start, qb, lk, tk)

    mx_ref[...] = jnp.full(mx_ref.shape, -jnp.inf, F32)
    acc_ref[...] = jnp.zeros(acc_ref.shape, F32)

    def score_tile(t, masked):
        start = pl.multiple_of(t * tk, tk)
        k = k_ref[pl.ds(start, tk), :]
        if masked:
            ok = _admissible(start, tk, q_chunk, lk)
        for i in range(2):
            s = _dot_nt(q_half[i], k)
            if masked:
                s = jnp.where(ok, s, -jnp.inf)
            s_ref[i, t] = s
            mx_ref[i] = _lane_max(mx_ref[i], s)

    def full_body(t, c):
        score_tile(t, False)
        return c

    def masked_body(t, c):
        score_tile(t, True)
        return c

    _fori_pairs(n_full, full_body)
    lax.fori_loop(n_full, n_tiles, masked_body, 0)

    for i in range(2):
        mx_ref[i] = jnp.broadcast_to(jnp.max(mx_ref[i], axis=1, keepdims=True), (qb, LANES))

    def value_tile(t, c):
        start = pl.multiple_of(t * tk, tk)
        v = v_ref[pl.ds(start, tk), :]
        for i in range(2):
            acc_ref[i] = acc_ref[i] + _softmax_value_tile(s_ref, (i, t), mx_ref[i], v)
        return c

    _fori_pairs(n_tiles, value_tile)

    o = (acc_ref[0, :, :LANES] / acc_ref[0, :, LANES:]
         - lam_ref[...] * (acc_ref[1, :, :LANES] / acc_ref[1, :, LANES:]))
    o_ref[...] = (_rms(o, gain_ref[...]) * out_scale).astype(BF16)


def _diff_attn(q, k, v, lam, gain, *, past, lk, tk, out_scale, cache=None):
    b, l, _ = q.shape
    qb = min(l, DIFF_Q_BLOCK)
    cached = cache is not None
    lkp = tk if cached else k.shape[1]
    assert l % qb == 0 and lkp % tk == 0 and qb % CHUNK == 0
    kern = functools.partial(_diff_attn_kernel, past=past, lk=lk, qb=qb, tk=tk,
                             out_scale=out_scale, cached=cached)
    in_specs = [pl.BlockSpec((1, LANES), lambda bi, h, j: (0, 0)),
                pl.BlockSpec((1, LANES), lambda bi, h, j: (0, 0)),
                pl.BlockSpec((None, qb, LANES), lambda bi, h, j: (bi, j, h)),
                pl.BlockSpec((None, k.shape[1], LANES), lambda bi, h, j: (bi, 0, h)),
                pl.BlockSpec((None, k.shape[1], LANES), lambda bi, h, j: (bi, 0, h))]
    scratch = [pltpu.VMEM((2, lkp // tk, qb, tk), F32),
               pltpu.VMEM((2, qb, LANES), F32), pltpu.VMEM((2, qb, 2 * LANES), F32)]
    operands = [lam, gain, q, k, v]
    if cached:
        cache_k, cache_v, li = cache
        assert l == qb and past + l <= tk and cache_k.shape[2:] == (past, W_A)
        head_rows = pl.BlockSpec((None, None, past, LANES), lambda bi, h, j: (li, bi, 0, h))
        in_specs += [head_rows, head_rows]
        scratch += [pltpu.VMEM((tk, LANES), BF16), pltpu.VMEM((tk, LANES), BF16)]
        operands += [cache_k, cache_v]
    return pl.pallas_call(
        kern,
        grid=(b, H_A, l // qb),
        in_specs=in_specs,
        out_specs=pl.BlockSpec((None, qb, LANES), lambda bi, h, j: (bi, j, h)),
        out_shape=jax.ShapeDtypeStruct((b, l, W_A), BF16),
        scratch_shapes=scratch,
        compiler_params=_cparams(("parallel", "parallel", "parallel")),
        name="diff_attn",
    )(*operands)


def _dsa_kernel(qi_ref, kw_ref, q_ref, ki_ref, k_ref, v_ref, o_ref,
                s_ref, sc_ref, lo_ref, hi_ref, mx_ref, acc_ref,
                *, past, lk, qb, tk, topk, max_steps):
    j = pl.program_id(1)
    q_start = past + j * qb
    n_full, n_tiles = _tile_bounds(q_start, qb, lk, tk)
    n_groups = tk // LANES
    row = lax.broadcasted_iota(jnp.int32, (qb, 1), 0)
    q_chunk = lax.shift_right_logical(q_start + row, CHUNK_SHIFT)
    lane = lax.broadcasted_iota(jnp.int32, (1, LANES), 1)
    low_half = lane < HEAD_DIM
    k_sel = float(topk)

    def wide(x):
        return jnp.broadcast_to(x, (qb, LANES))

    def head_views(x):
        views = []
        for h in range(4):
            pair = x[:, (h // 2) * LANES:(h // 2 + 1) * LANES]
            keep = low_half if h % 2 == 0 else jnp.logical_not(low_half)
            views.append(jnp.where(keep, pair, jnp.zeros_like(pair)))
        return views

    qi_h = head_views(qi_ref[...])
    kw = kw_ref[...]
    w_h = [wide(kw[:, D_IDX + h:D_IDX + h + 1]) for h in range(H_IDX)]
    lo_ref[...] = jnp.full(lo_ref.shape, -jnp.inf, F32)
    hi_ref[...] = jnp.full(hi_ref.shape, -jnp.inf, F32)

    def index_tile(t, masked):
        start = pl.multiple_of(t * tk, tk)
        if masked:
            adm = _admissible(start, tk, q_chunk, lk)
        top2 = lo_ref[...]
        top1 = hi_ref[...]
        for c in range(tk // MXU_DIM):
            ki = ki_ref[pl.ds(start + c * MXU_DIM, MXU_DIM), :]
            rel = [jnp.maximum(_dot_nt(qi_h[h], ki), 0.0) for h in range(H_IDX)]
            for g in range(MXU_DIM // LANES):
                sl = slice(g * LANES, (g + 1) * LANES)
                sc = w_h[0] * rel[0][:, sl]
                for h in range(1, H_IDX):
                    sc = sc + w_h[h] * rel[h][:, sl]
                csl = slice(c * MXU_DIM + g * LANES, c * MXU_DIM + (g + 1) * LANES)
                if masked:
                    sc = jnp.where(adm[:, csl], sc, -jnp.inf)
                s_ref[t, :, csl] = sc
                top2 = jnp.maximum(top2, jnp.minimum(top1, sc))
                top1 = jnp.maximum(top1, sc)
        lo_ref[...] = top2
        hi_ref[...] = top1

    def index_full(t, c):
        index_tile(t, False)
        return c

    def index_masked(t, c):
        index_tile(t, True)
        return c

    _fori_pairs(n_full, index_full)
    lax.fori_loop(n_full, n_tiles, index_masked, 0)
    rb = min(qb, SCAN_ROWS)
    ones_mat = jnp.ones((LANES, LANES), BF16)
    tri_i = lax.broadcasted_iota(jnp.int32, (LANES, 2 * LANES), 0)
    tri_j = lax.broadcasted_iota(jnp.int32, (LANES, 2 * LANES), 1)
    prefix_mat = jnp.where(jnp.logical_or(tri_i <= tri_j, tri_j >= LANES), 1.0, 0.0).astype(BF16)
    assert s_ref.shape[0] * n_groups <= 256

    row_blocks = [slice(r * rb, (r + 1) * rb) for r in range(qb // rb)]
    pos = q_start + lax.broadcasted_iota(jnp.int32, (qb, LANES), 0)
    n_adm = jnp.minimum((lax.shift_right_logical(pos, CHUNK_SHIFT) + 1) * CHUNK, lk)
    few = n_adm <= topk

    def lanes_all(x, reduce):
        return jnp.broadcast_to(reduce(x, axis=1, keepdims=True), (qb, LANES))

    def row_sum(acc):
        return jnp.dot(acc.astype(BF16), ones_mat, preferred_element_type=F32)

    def scan(step, init, *operands):
        outs = []
        for rows in row_blocks:
            ops = [o[rows] for o in operands]

            def body(t, acc, rows=rows, ops=ops):
                for g in range(n_groups):
                    acc = step(acc, s_ref[t, rows, g * LANES:(g + 1) * LANES], *ops)
                return acc

            outs.append(lax.fori_loop(0, n_tiles, body, jnp.full((rb, LANES), init, F32)))
        return outs[0] if len(outs) == 1 else jnp.concatenate(outs, axis=0)

    def count_ge(thr):
        return row_sum(scan(lambda acc, s, t: acc + jnp.where(s >= t, 1.0, 0.0), 0.0, thr))

    def max_below(bound):
        acc = scan(lambda acc, s, b: jnp.maximum(acc, jnp.where(s < b, s, -jnp.inf)), -jnp.inf, bound)
        return lanes_all(acc, jnp.max)

    assert topk <= 2 * LANES
    lane_best = hi_ref[...] if topk <= LANES else lo_ref[...]
    lowest = float(jnp.finfo(jnp.float32).min)
    rmin = jnp.maximum(lanes_all(lane_best, jnp.min), lowest)
    rmax = jnp.maximum(lanes_all(lane_best, jnp.max), lowest)

    c_max = count_ge(rmax)
    top_ties = c_max >= k_sel

    def bisect(_, c):
        lo, hi = c
        mid = 0.5 * lo + 0.5 * hi
        ge = count_ge(mid) >= k_sel
        return jnp.where(ge, mid, lo), jnp.where(ge, hi, mid)

    _, hi = lax.fori_loop(0, N_BISECT, bisect, (rmin, rmax))

    def walk_cond(st):
        it, _, _, _, active = st
        return jnp.logical_and(jnp.max(active) > 0.0, it < max_steps)

    def walk_body(st):
        it, cand, thr, c_thr, active = st
        c = count_ge(cand)
        ok = c >= k_sel
        act = active > 0.0
        hit = jnp.logical_and(act, ok)
        thr = jnp.where(hit, cand, thr)
        c_thr = jnp.where(hit, c, c_thr)
        active = jnp.where(jnp.logical_and(act, jnp.logical_not(ok)), 1.0, 0.0)
        return it + 1, max_below(cand), thr, c_thr, active

    thr0 = jnp.where(few, lowest, jnp.where(top_ties, rmax, lowest))
    c0 = jnp.where(few, k_sel, jnp.where(top_ties, c_max, k_sel))
    active0 = jnp.where(jnp.logical_or(few, top_ties), 0.0, 1.0)
    _, _, thr, c_thr, _ = lax.while_loop(
        walk_cond, walk_body, (jnp.int32(0), max_below(hi), thr0, c0, active0))

    has_excess = jnp.max(jnp.where(c_thr > k_sel, 1.0, 0.0)) > 0.0

    def ranked_bias():
        n_tie = k_sel - row_sum(scan(lambda acc, s, t: acc + jnp.where(s > t, 1.0, 0.0), 0.0, thr))
        for rows in row_blocks:
            thr_r = thr[rows]
            n_tie_r = n_tie[rows]

            def body(t, before, rows=rows, thr_r=thr_r, n_tie_r=n_tie_r):
                for g in range(n_groups):
                    lanes = slice(g * LANES, (g + 1) * LANES)
                    s = s_ref[t, rows, lanes]
                    tie = s == thr_r
                    pr = jnp.dot(jnp.where(tie, 1.0, 0.0).astype(BF16), prefix_mat,
                                 preferred_element_type=F32)
                    rank = before + pr[:, :LANES]
                    tie_bias = jnp.where(rank <= n_tie_r, 0.0, MASKED)
                    s_ref[t, rows, lanes] = jnp.where(s > thr_r, 0.0, jnp.where(tie, tie_bias, MASKED))
                    before = before + pr[:, LANES:]
                return before

            _fori_pairs(n_tiles, body, jnp.zeros((rb, LANES), F32))
        return 0

    def plain_bias():
        for rows in row_blocks:
            thr_r = thr[rows]

            def body(t, c, rows=rows, thr_r=thr_r):
                for g in range(n_groups):
                    lanes = slice(g * LANES, (g + 1) * LANES)
                    s_ref[t, rows, lanes] = jnp.where(s_ref[t, rows, lanes] >= thr_r, 0.0, MASKED)
                return c

            lax.fori_loop(0, n_tiles, body, 0)
        return 0

    lax.cond(has_excess, ranked_bias, plain_bias)

    q_h = head_views(q_ref[...])
    for g in range(H_B // 2):
        lanes = slice(g * LANES, (g + 1) * LANES)
        mx_ref[...] = jnp.full(mx_ref.shape, -jnp.inf, F32)
        acc_ref[...] = jnp.zeros(acc_ref.shape, F32)

        def score_tile(t, c, g=g, lanes=lanes):
            start = pl.multiple_of(t * tk, tk)
            k = k_ref[pl.ds(start, tk), lanes]
            bias = s_ref[t]
            for i in range(2):
                s = _dot_nt(q_h[2 * g + i], k) + bias
                sc_ref[i, t] = s
                mx_ref[i] = _lane_max(mx_ref[i], s)
            return c

        _fori_pairs(n_tiles, score_tile)
        for i in range(2):
            mx_ref[i] = jnp.broadcast_to(jnp.max(mx_ref[i], axis=1, keepdims=True), (qb, LANES))

        def value_tile(t, c, lanes=lanes):
            start = pl.multiple_of(t * tk, tk)
            v = v_ref[pl.ds(start, tk), lanes]
            for i in range(2):
                acc_ref[i] = acc_ref[i] + _softmax_value_tile(sc_ref, (i, t), mx_ref[i], v)
            return c

        _fori_pairs(n_tiles, value_tile)
        even = acc_ref[0, :, :LANES] / acc_ref[0, :, LANES:]
        odd = acc_ref[1, :, :LANES] / acc_ref[1, :, LANES:]
        o_ref[:, lanes] = jnp.where(low_half, even, odd).astype(BF16)


def _dsa(qi, kw, q, ki, k, v, *, past, lk, tk):
    b, l, _ = q.shape
    lkp = k.shape[1]
    qb = min(l, Q_BLOCK)
    assert l % qb == 0 and lkp % tk == 0 and tk % MXU_DIM == 0 and qb % CHUNK == 0
    kern = functools.partial(_dsa_kernel, past=past, lk=lk, qb=qb, tk=tk,
                             topk=min(TOPK, lk // 4), max_steps=lkp)
    return pl.pallas_call(
        kern,
        grid=(b, l // qb),
        in_specs=[pl.BlockSpec((None, qb, H_IDX * D_IDX), lambda bi, j: (bi, j, 0)),
                  pl.BlockSpec((None, qb, LANES), lambda bi, j: (bi, j, 0)),
                  pl.BlockSpec((None, qb, W_B), lambda bi, j: (bi, j, 0)),
                  pl.BlockSpec((None, lkp, LANES), lambda bi, j: (bi, 0, 0)),
                  pl.BlockSpec((None, lkp, W_B), lambda bi, j: (bi, 0, 0)),
                  pl.BlockSpec((None, lkp, W_B), lambda bi, j: (bi, 0, 0))],
        out_specs=pl.BlockSpec((None, qb, W_B), lambda bi, j: (bi, j, 0)),
        out_shape=jax.ShapeDtypeStruct((b, l, W_B), BF16),
        scratch_shapes=[pltpu.VMEM((lkp // tk, qb, tk), F32),
                        pltpu.VMEM((2, lkp // tk, qb, tk), F32),
                        pltpu.VMEM((qb, LANES), F32), pltpu.VMEM((qb, LANES), F32),
                        pltpu.VMEM((2, qb, LANES), F32), pltpu.VMEM((2, qb, 2 * LANES), F32)],
        compiler_params=_cparams(("parallel", "parallel")),
        name="dsa",
    )(qi, kw, q, ki, k, v)


def _shift_rows(x, prev, k):
    rolled = pltpu.roll(x, k, 0)
    row = lax.broadcasted_iota(jnp.int32, (SUBLANES, 1), 0)
    top = jnp.where(row < k, pltpu.roll(prev, k, 0), rolled[0:SUBLANES])
    if x.shape[0] == SUBLANES:
        return top
    return jnp.concatenate([top, rolled[SUBLANES:]], axis=0)


def _rglru_kernel(xc_ref, gc_ref, cst_ref, h0_ref, cw_ref, cb_ref, wg_ref, bg_ref, lam_ref,
                  oc_ref, hl_ref, cn_ref, prev_ref, h_ref, a_ref, b_ref, hs_ref, *, past, tl):
    i = pl.program_id(1)

    @pl.when(i == 0)
    def _():
        prev_ref[...] = jnp.zeros(prev_ref.shape, F32)
        prev_ref[SUBLANES - (CONV_C - 1):SUBLANES, :] = cst_ref[...]
        h_ref[...] = h0_ref[...]

    x = xc_ref[...]
    prev = prev_ref[...]
    cw = cw_ref[...]
    xconv = cw[CONV_C - 1:CONV_C] * x + cb_ref[...]
    for k in range(1, CONV_C):
        xconv = xconv + cw[CONV_C - 1 - k:CONV_C - k] * _shift_rows(x, prev, k)
    prev_ref[...] = x[tl - SUBLANES:tl]

    pre = jnp.dot(xconv.astype(BF16), wg_ref[...], preferred_element_type=F32) + bg_ref[...]
    r = jax.nn.sigmoid(pre[:, :W_C])
    gate_i = jax.nn.sigmoid(pre[:, W_C:])
    neg_lam = -lam_ref[...]
    softplus = jnp.maximum(neg_lam, 0.0) + jnp.log1p(jnp.exp(-jnp.abs(neg_lam)))
    log_a = -RG_C * r * softplus
    pos = past + i * tl + lax.broadcasted_iota(jnp.int32, (tl, 1), 0)
    th = jnp.tanh(log_a)
    mult = jnp.where(pos == 0, 1.0, jnp.sqrt(-2.0 * th / (1.0 - th)))
    a_ref[...] = jnp.exp(log_a)
    b_ref[...] = mult * gate_i * xconv

    row8 = lax.broadcasted_iota(jnp.int32, (SUBLANES, 1), 0)

    def tile_step(i, h):
        r0 = pl.multiple_of(i * SUBLANES, SUBLANES)
        a = a_ref[pl.ds(r0, SUBLANES), :]
        b = b_ref[pl.ds(r0, SUBLANES), :]
        for sh in (1, 2, 4):
            a_prev = jnp.where(row8 >= sh, pltpu.roll(a, sh, 0), 1.0)
            b_prev = jnp.where(row8 >= sh, pltpu.roll(b, sh, 0), 0.0)
            b = a * b_prev + b
            a = a * a_prev
        hs = a * h + b
        hs_ref[pl.ds(r0, SUBLANES), :] = hs
        return hs[SUBLANES - 1:SUBLANES, :]

    h_last = lax.fori_loop(0, tl // SUBLANES, tile_step, h_ref[...], unroll=4)
    h_ref[...] = h_last

    gc = gc_ref[...]
    gelu = 0.5 * gc * (1.0 + jnp.tanh(math.sqrt(2.0 / math.pi) * (gc + 0.044715 * (gc * gc * gc))))
    oc_ref[...] = (hs_ref[...] * gelu).astype(BF16)

    @pl.when(i == pl.num_programs(1) - 1)
    def _():
        hl_ref[...] = h_last
        cn_ref[...] = x[tl - (CONV_C - 1):tl]


def _rglru(xc, gc, conv_state, h0, conv_w, conv_b, w_gate, b_gate, lam, *, past):
    b, l, _ = xc.shape
    tl = min(l, 512)
    assert l >= SUBLANES and l % tl == 0
    kern = functools.partial(_rglru_kernel, past=past, tl=tl)
    const = lambda bi, i: (0, 0)
    return pl.pallas_call(
        kern,
        grid=(b, l // tl),
        in_specs=[pl.BlockSpec((None, tl, W_C), lambda bi, i: (bi, i, 0)),
                  pl.BlockSpec((None, tl, W_C), lambda bi, i: (bi, i, 0)),
                  pl.BlockSpec((None, CONV_C - 1, W_C), lambda bi, i: (bi, 0, 0)),
                  pl.BlockSpec((None, 1, W_C), lambda bi, i: (bi, 0, 0)),
                  pl.BlockSpec((CONV_C, W_C), const),
                  pl.BlockSpec((1, W_C), const),
                  pl.BlockSpec((W_C, 2 * W_C), const),
                  pl.BlockSpec((1, 2 * W_C), const),
                  pl.BlockSpec((1, W_C), const)],
        out_specs=[pl.BlockSpec((None, tl, W_C), lambda bi, i: (bi, i, 0)),
                   pl.BlockSpec((None, 1, W_C), lambda bi, i: (bi, 0, 0)),
                   pl.BlockSpec((None, CONV_C - 1, W_C), lambda bi, i: (bi, 0, 0))],
        out_shape=[jax.ShapeDtypeStruct((b, l, W_C), BF16),
                   jax.ShapeDtypeStruct((b, 1, W_C), F32),
                   jax.ShapeDtypeStruct((b, CONV_C - 1, W_C), F32)],
        scratch_shapes=[pltpu.VMEM((SUBLANES, W_C), F32), pltpu.VMEM((1, W_C), F32),
                        pltpu.VMEM((tl, W_C), F32), pltpu.VMEM((tl, W_C), F32),
                        pltpu.VMEM((tl, W_C), F32)],
        compiler_params=_cparams(("parallel", "arbitrary")),
        name="rglru",
    )(xc, gc, conv_state, h0.reshape(b, 1, W_C), conv_w, conv_b.reshape(1, W_C), w_gate,
      b_gate, lam.reshape(1, W_C))


def _outproj_kernel(x_ref, oa_ref, ob_ref, oc_ref, w_ref, g_ref, x1_ref, hn_ref):
    mix = jnp.dot(oa_ref[...], w_ref[0:W_A, :], preferred_element_type=F32)
    mix = mix + jnp.dot(ob_ref[...], w_ref[W_A:W_A + W_B, :], preferred_element_type=F32)
    mix = mix + jnp.dot(oc_ref[...], w_ref[W_A + W_B:, :], preferred_element_type=F32)
    x1 = x_ref[...] + mix
    x1_ref[...] = x1
    hn_ref[...] = _rms(x1, g_ref[...]).astype(BF16)


def _outproj(x2d, oa, ob, oc, w_out, gain):
    t = x2d.shape[0]
    tm = min(512, t)
    row = lambda w: pl.BlockSpec((tm, w), lambda i: (i, 0))
    return pl.pallas_call(
        _outproj_kernel,
        grid=(t // tm,),
        in_specs=[row(D_MODEL), row(W_A), row(W_B), row(W_C),
                  pl.BlockSpec((D_MODEL, D_MODEL), lambda i: (0, 0)),
                  pl.BlockSpec((1, D_MODEL), lambda i: (0, 0))],
        out_specs=[row(D_MODEL), row(D_MODEL)],
        out_shape=[jax.ShapeDtypeStruct((t, D_MODEL), F32), jax.ShapeDtypeStruct((t, D_MODEL), BF16)],
        compiler_params=_cparams(("parallel",)),
        name="outproj",
    )(x2d, oa, ob, oc, w_out, gain.reshape(1, D_MODEL))


def _ffn_kernel(hn_ref, x1_ref, wu_ref, wg_ref, cwu_ref, cwg_ref, cbu_ref, cbg_ref, wd_ref,
                su_ref, sg_ref, gfin_ref, y_ref, fu_ref, fg_ref,
                acc_ref, au_ref, ag_ref, cu_ref, cg_ref, *, tm, final_norm):
    i = pl.program_id(1)
    s = pl.program_id(2)
    nf = pl.num_programs(2) - 1
    live = s > 0
    fb = jnp.maximum(s - 1, 0)
    row = jnp.where(live, fb, nf)

    @pl.when(s == 0)
    def _():
        au_ref[1] = jnp.zeros(au_ref.shape[1:], F32)
        ag_ref[1] = jnp.zeros(ag_ref.shape[1:], F32)
        cu_ref[nf] = jnp.zeros(cu_ref.shape[1:], F32)
        cg_ref[nf] = jnp.zeros(cg_ref.shape[1:], F32)
        acc_ref[...] = x1_ref[...]

    @pl.when(jnp.logical_and(i == 0, live))
    def _():
        for carry_ref, st_ref in ((cu_ref, su_ref), (cg_ref, sg_ref)):
            carry_ref[fb] = jnp.zeros(carry_ref.shape[1:], F32)
            carry_ref[fb, SUBLANES - (CONV_F - 1):SUBLANES, :] = st_ref[...]

    def step(wslot, rslot):
        au_ref[rslot, 0:SUBLANES, :] = cu_ref[row]
        ag_ref[rslot, 0:SUBLANES, :] = cg_ref[row]
        rc = tm // FFN_ROW_CHUNKS

        def conv(a_ref, cw_ref, cb_ref, r0):
            cw = cw_ref[...]
            y = cw[CONV_F - 1:CONV_F] * a_ref[rslot, SUBLANES + r0:SUBLANES + r0 + rc, :] + cb_ref[...]
            for k in range(1, CONV_F):
                y = y + (cw[CONV_F - 1 - k:CONV_F - k]
                         * a_ref[rslot, SUBLANES - k + r0:SUBLANES - k + r0 + rc, :])
            return y

        for c in range(FFN_ROW_CHUNKS):
            r0 = c * rc
            hn = hn_ref[r0:r0 + rc, :]
            au_ref[wslot, SUBLANES + r0:SUBLANES + r0 + rc, :] = jnp.dot(
                hn, wu_ref[...], preferred_element_type=F32)
            ag_ref[wslot, SUBLANES + r0:SUBLANES + r0 + rc, :] = jnp.dot(
                hn, wg_ref[...], preferred_element_type=F32)
            u = conv(au_ref, cwu_ref, cbu_ref, r0)
            g = conv(ag_ref, cwg_ref, cbg_ref, r0)
            mid = (g * jax.nn.sigmoid(g) * u).astype(BF16)
            contrib = jnp.dot(mid, wd_ref[...], preferred_element_type=F32)
            acc_ref[r0:r0 + rc, :] = acc_ref[r0:r0 + rc, :] + jnp.where(live, contrib, 0.0)

        for a_ref, carry_ref, tail_ref in ((au_ref, cu_ref, fu_ref), (ag_ref, cg_ref, fg_ref)):
            carry_ref[row] = a_ref[rslot, tm:tm + SUBLANES, :]
            tail_ref[row] = a_ref[rslot, tm + SUBLANES - (CONV_F - 1):tm + SUBLANES, :]

    parity = lax.rem(s, 2)

    @pl.when(parity == 0)
    def _():
        step(0, 1)

    @pl.when(parity == 1)
    def _():
        step(1, 0)

    @pl.when(s == nf)
    def _():
        y = acc_ref[...]
        if final_norm:
            y = _rms(y, gfin_ref[...])
        y_ref[...] = y


def _ffn(hn, x1, w_up, conv_w, conv_b, w_down, state, final_gain, *, final_norm):
    b, l, _ = hn.shape
    tm = min(l, 1024)
    tf = 512
    nf = D_FF // tf
    assert l % tm == 0 and tm >= SUBLANES
    kern = functools.partial(_ffn_kernel, tm=tm, final_norm=final_norm)
    conv_b = conv_b.reshape(1, 2 * D_FF)
    up = lambda s: jnp.minimum(s, nf - 1)
    fin = lambda s: jnp.maximum(s - 1, 0)
    tail_spec = pl.BlockSpec((None, nf + 1, CONV_F - 1, tf), lambda bi, i, s: (bi, 0, 0, 0))
    tail_shape = jax.ShapeDtypeStruct((b, nf + 1, CONV_F - 1, tf), F32)
    y, fu, fg = pl.pallas_call(
        kern,
        grid=(b, l // tm, nf + 1),
        in_specs=[pl.BlockSpec((None, tm, D_MODEL), lambda bi, i, s: (bi, i, 0)),
                  pl.BlockSpec((None, tm, D_MODEL), lambda bi, i, s: (bi, i, 0)),
                  pl.BlockSpec((D_MODEL, tf), lambda bi, i, s: (0, up(s))),
                  pl.BlockSpec((D_MODEL, tf), lambda bi, i, s: (0, nf + up(s))),
                  pl.BlockSpec((CONV_F, tf), lambda bi, i, s: (0, fin(s))),
                  pl.BlockSpec((CONV_F, tf), lambda bi, i, s: (0, nf + fin(s))),
                  pl.BlockSpec((1, tf), lambda bi, i, s: (0, fin(s))),
                  pl.BlockSpec((1, tf), lambda bi, i, s: (0, nf + fin(s))),
                  pl.BlockSpec((tf, D_MODEL), lambda bi, i, s: (fin(s), 0)),
                  pl.BlockSpec((None, CONV_F - 1, tf), lambda bi, i, s: (bi, 0, fin(s))),
                  pl.BlockSpec((None, CONV_F - 1, tf), lambda bi, i, s: (bi, 0, nf + fin(s))),
                  pl.BlockSpec((1, D_MODEL), lambda bi, i, s: (0, 0))],
        out_specs=[pl.BlockSpec((None, tm, D_MODEL), lambda bi, i, s: (bi, i, 0)),
                   tail_spec, tail_spec],
        out_shape=[jax.ShapeDtypeStruct((b, l, D_MODEL), F32), tail_shape, tail_shape],
        scratch_shapes=[pltpu.VMEM((tm, D_MODEL), F32),
                        pltpu.VMEM((2, tm + SUBLANES, tf), F32),
                        pltpu.VMEM((2, tm + SUBLANES, tf), F32),
                        pltpu.VMEM((nf + 1, SUBLANES, tf), F32),
                        pltpu.VMEM((nf + 1, SUBLANES, tf), F32)],
        compiler_params=_cparams(("parallel", "arbitrary", "arbitrary")),
        name="ffn",
    )(hn, x1, w_up, w_up, conv_w, conv_w, conv_b, conv_b, w_down, state, state,
      final_gain.reshape(1, D_MODEL))
    flat = lambda a: jnp.swapaxes(a[:, :nf], 1, 2).reshape(b, CONV_F - 1, D_FF)
    return y, jnp.concatenate([flat(fu), flat(fg)], axis=-1)


def _cast_kernel(x_ref, o_ref):
    o_ref[...] = x_ref[...].astype(o_ref.dtype)


def _layer_bf16(w, li):
    _, r, c = w.shape
    tr = 256 if r % 256 == 0 else r
    return pl.pallas_call(
        _cast_kernel,
        grid=(r // tr,),
        in_specs=[pl.BlockSpec((None, tr, c), lambda i: (li, i, 0))],
        out_specs=pl.BlockSpec((tr, c), lambda i: (i, 0)),
        out_shape=jax.ShapeDtypeStruct((r, c), BF16),
        compiler_params=_cparams(("parallel",)),
        name="cast",
    )(w)


def _prep_layer_weights(p, li):
    w_in = _layer_bf16(p["w_in"], li)
    zeros = lambda n: jnp.zeros((D_MODEL, n), w_in.dtype)
    k_idx = w_in[:, 2560:2624]
    w_pad = jnp.concatenate(
        [w_in[:, :2628], zeros(_C_KK - 2628), k_idx, k_idx, w_in[:, 2628:]], axis=1)
    assert w_pad.shape[1] == PROJ_W_PAD

    def block_diag(w):
        out = jnp.zeros((W_C, W_C), w.dtype)
        for n in range(N_GATE_BLOCKS):
            sl = slice(n * GATE_BLOCK, (n + 1) * GATE_BLOCK)
            out = out.at[sl, sl].set(w[n])
        return out

    w_gate = jnp.concatenate([block_diag(p["rg_w_r"][li]), block_diag(p["rg_w_i"][li])], axis=1)
    b_gate = jnp.concatenate([p["rg_b_r"][li].reshape(1, W_C), p["rg_b_i"][li].reshape(1, W_C)], axis=1)
    lam_init = 0.8 - 0.6 * math.exp(-0.3 * li)
    f32 = lambda a: a.astype(F32)
    lam = (jnp.exp(jnp.sum(f32(p["lam_q1"][li]) * f32(p["lam_k1"][li])))
           - jnp.exp(jnp.sum(f32(p["lam_q2"][li]) * f32(p["lam_k2"][li]))) + lam_init)
    return dict(
        norm_mix=p["norm_mix"][li], w_pad=w_pad, lam=jnp.full((1, LANES), lam, F32),
        lam_init=lam_init, diff_gain=p["diff_gain"][li].reshape(1, LANES),
        rg_conv_w=p["rg_conv_w"][li], rg_conv_b=p["rg_conv_b"][li],
        w_gate=w_gate.astype(BF16), b_gate=b_gate, rg_lambda=p["rg_lambda"][li],
        w_out=_layer_bf16(p["w_out"], li), norm_ffn=p["norm_ffn"][li],
        ffn_w_up=_layer_bf16(p["ffn_w_up"], li), ffn_conv_w=p["ffn_conv_w"][li],
        ffn_conv_b=p["ffn_conv_b"][li], ffn_w_down=_layer_bf16(p["ffn_w_down"], li))


def _layer(x, past, attn_cache, w, final_gain, final_norm):
    b, l, _ = x.shape
    _, _, b_k0, b_v0, b_ki0, c_h0, c_cv0, f_cv0 = past
    p_len = 0 if b_k0 is None else b_k0.shape[1]
    lk = p_len + l
    tk = PROMPT_KEY_TILE if p_len == 0 else _round_up(lk, MXU_DIM)
    lkp = _round_up(lk, tk)
    t = b * l

    (qa, qb, qi, ka, va, kb, vb, kw, xc, gc, kab, vab, kbb, vbb, kib) = _proj(
        x.reshape(t, D_MODEL), w["norm_mix"], w["w_pad"])

    def keys(cache, new, dup=False):
        new = new.reshape(b, l, -1)
        parts = []
        if cache is not None:
            c = cache.reshape(b, p_len, -1).astype(BF16)
            parts.append(jnp.concatenate([c, c], axis=-1) if dup else c)
        parts.append(new)
        if lkp > lk:
            parts.append(jnp.zeros((b, lkp - lk, new.shape[-1]), BF16))
        return parts[0] if len(parts) == 1 else jnp.concatenate(parts, axis=1)

    if attn_cache is None:
        k_a, v_a = keys(None, kab), keys(None, vab)
    else:
        k_a, v_a = kab.reshape(b, l, W_A), vab.reshape(b, l, W_A)
    o_a = _diff_attn(qa.reshape(b, l, W_A), k_a, v_a, w["lam"], w["diff_gain"], past=p_len, lk=lk,
                     tk=tk, out_scale=1.0 - w["lam_init"], cache=attn_cache)
    o_b = _dsa(qi.reshape(b, l, -1), kw.reshape(b, l, LANES), qb.reshape(b, l, W_B),
               keys(b_ki0, kib, dup=True), keys(b_k0, kbb), keys(b_v0, vbb),
               past=p_len, lk=lk, tk=tk)
    o_c, h_last, conv_new = _rglru(xc.reshape(b, l, W_C), gc.reshape(b, l, W_C), c_cv0, c_h0,
                                   w["rg_conv_w"], w["rg_conv_b"], w["w_gate"], w["b_gate"],
                                   w["rg_lambda"], past=p_len)
    x1, hn = _outproj(x.reshape(t, D_MODEL), o_a.reshape(t, W_A), o_b.reshape(t, W_B),
                      o_c.reshape(t, W_C), w["w_out"], w["norm_ffn"])
    y, f_buf = _ffn(hn.reshape(b, l, D_MODEL), x1.reshape(b, l, D_MODEL), w["ffn_w_up"],
                    w["ffn_conv_w"], w["ffn_conv_b"], w["ffn_w_down"], f_cv0, final_gain,
                    final_norm=final_norm)
    new = (ka.reshape(b, l, H_A, 2 * HEAD_DIM), va.reshape(b, l, H_A, 2 * HEAD_DIM),
           kb.reshape(b, l, H_B, HEAD_DIM), vb.reshape(b, l, H_B, HEAD_DIM),
           kw[:, :D_IDX].reshape(b, l, D_IDX), h_last.reshape(b, W_C), conv_new, f_buf)
    return y, new


def _trunk(x, past, weights, final_gain):
    states = []
    if past[0] is not None:
        flat_heads = lambda c: c.reshape(c.shape[:3] + (W_A,))
        cache_a = (flat_heads(past[0]), flat_heads(past[1]))
    for li in range(N_LAYERS):
        layer_past = tuple(None if c is None else c[li] for c in past)
        attn_cache = None if past[0] is None else cache_a + (li,)
        x, st = _layer(x, layer_past, attn_cache, weights[li], final_gain,
                       final_norm=(li == N_LAYERS - 1))
        states.append(st)
    return x, states


def _forward(x_prompt, x_sample, caches, params):
    weights = [_prep_layer_weights(params, li) for li in range(N_LAYERS)]
    bp = x_prompt.shape[0]
    dt = x_prompt.dtype
    past_prompt = (None, None, None, None, None,
                   jnp.zeros((N_LAYERS, bp, W_C), dt),
                   jnp.zeros((N_LAYERS, bp, CONV_C - 1, W_C), dt),
                   jnp.zeros((N_LAYERS, bp, CONV_F - 1, 2 * D_FF), dt))
    yp, sp = _trunk(x_prompt, past_prompt, weights, params["norm_final"])
    ys, ss = _trunk(x_sample, caches, weights, params["norm_final"])
    out = [yp, ys]
    for jdx in range(8):
        out.append(jnp.stack([st[jdx] for st in sp], axis=0))
        out.append(jnp.stack([st[jdx] for st in ss], axis=0))
    return tuple(out)


def kernel(x_prompt, x_sample, cache_a_k, cache_a_v, cache_b_k, cache_b_v, cache_b_kidx,
           state_c_h, state_c_conv, state_ffn_conv, norm_mix, w_in, lam_q1, lam_k1, lam_q2,
           lam_k2, diff_gain, rg_conv_w, rg_conv_b, rg_w_r, rg_b_r, rg_w_i, rg_b_i, rg_lambda,
           w_out, norm_ffn, ffn_w_up, ffn_conv_w, ffn_conv_b, ffn_w_down, norm_final):
    params = dict(norm_mix=norm_mix, w_in=w_in, lam_q1=lam_q1, lam_k1=lam_k1, lam_q2=lam_q2,
                  lam_k2=lam_k2, diff_gain=diff_gain, rg_conv_w=rg_conv_w, rg_conv_b=rg_conv_b,
                  rg_w_r=rg_w_r, rg_b_r=rg_b_r, rg_w_i=rg_w_i, rg_b_i=rg_b_i, rg_lambda=rg_lambda,
                  w_out=w_out, norm_ffn=norm_ffn, ffn_w_up=ffn_w_up, ffn_conv_w=ffn_conv_w,
                  ffn_conv_b=ffn_conv_b, ffn_w_down=ffn_w_down, norm_final=norm_final)
    caches = (cache_a_k, cache_a_v, cache_b_k, cache_b_v, cache_b_kidx,
              state_c_h, state_c_conv, state_ffn_conv)
    return _forward(x_prompt, x_sample, caches, params)
```

```python
import functools
import math

import jax
import jax.numpy as jnp
from jax import lax
from jax.experimental import pallas as pl
from jax.experimental.pallas import tpu as pltpu

F32 = jnp.float32
BF16 = jnp.bfloat16

D_MODEL = 1024
N_LAYERS = 2
CHUNK = 64
CHUNK_SHIFT = 6
HEAD_DIM = 64
H_A = 4
W_A = H_A * 2 * HEAD_DIM
H_B = 4
W_B = H_B * HEAD_DIM
H_IDX = 4
D_IDX = 64
TOPK = 256
W_C = 256
N_GATE_BLOCKS = 4
GATE_BLOCK = W_C // N_GATE_BLOCKS
RG_C = 8.0
CONV_C = 4
D_FF = 3072
CONV_F = 3
EPS = 1e-6

LANES = 128
SUBLANES = 8
MXU_DIM = 256
Q_BLOCK = 256
DIFF_Q_BLOCK = 512
SCAN_ROWS = 128
PROMPT_KEY_TILE = 1024
VMEM_LIMIT = 58 * 2**20
MASKED = -1e30
N_BISECT = 14
FFN_ROW_CHUNKS = 1
Q_SCALE = HEAD_DIM ** -0.5 * math.log2(math.e)

PROJ_W_PAD = 3328
_C_QA, _C_KA, _C_VA = 0, 512, 1024
_C_QB, _C_KB, _C_VB = 1536, 1792, 2048
_C_QI = 2304
_C_KW = 2560
_C_KK = 2688
_C_XC = 2816
_C_GC = 3072


def _cparams(sem):
    return pltpu.CompilerParams(dimension_semantics=sem, vmem_limit_bytes=VMEM_LIMIT)


def _rms(x, g):
    return x * lax.rsqrt(jnp.mean(x * x, axis=-1, keepdims=True) + EPS) * g


def _dot_nt(a, b):
    return lax.dot_general(a, b, (((1,), (1,)), ((), ())), preferred_element_type=F32)


def _round_up(n, m):
    return (n + m - 1) // m * m


def _proj_kernel(x_ref, g_ref, w_ref, qa_ref, qb_ref, qi_ref, ka_ref, va_ref, kb_ref, vb_ref,
                 kw_ref, xc_ref, gc_ref, kab_ref, vab_ref, kbb_ref, vbb_ref, kib_ref):
    h = _rms(x_ref[...], g_ref[...])
    z = jnp.dot(h.astype(BF16), w_ref[...], preferred_element_type=F32)
    qa_ref[...] = (z[:, _C_QA:_C_QA + W_A] * Q_SCALE).astype(BF16)
    qb_ref[...] = (z[:, _C_QB:_C_QB + W_B] * Q_SCALE).astype(BF16)
    qi_ref[...] = z[:, _C_QI:_C_QI + H_IDX * D_IDX].astype(BF16)
    ka = z[:, _C_KA:_C_KA + W_A]
    va = z[:, _C_VA:_C_VA + W_A]
    kb = z[:, _C_KB:_C_KB + W_B]
    vb = z[:, _C_VB:_C_VB + W_B]
    ka_ref[...] = ka
    va_ref[...] = va
    kb_ref[...] = kb
    vb_ref[...] = vb
    kab_ref[...] = ka.astype(BF16)
    vab_ref[...] = va.astype(BF16)
    kbb_ref[...] = kb.astype(BF16)
    vbb_ref[...] = vb.astype(BF16)
    kw_ref[...] = z[:, _C_KW:_C_KW + LANES]
    kib_ref[...] = z[:, _C_KK:_C_KK + LANES].astype(BF16)
    xc_ref[...] = z[:, _C_XC:_C_XC + W_C]
    gc_ref[...] = z[:, _C_GC:_C_GC + W_C]


def _proj(x2d, gain, w_pad):
    t = x2d.shape[0]
    tm = min(512, t)
    widths = [(W_A, BF16), (W_B, BF16), (H_IDX * D_IDX, BF16),
              (W_A, F32), (W_A, F32), (W_B, F32), (W_B, F32),
              (LANES, F32), (W_C, F32), (W_C, F32),
              (W_A, BF16), (W_A, BF16), (W_B, BF16), (W_B, BF16), (LANES, BF16)]
    return pl.pallas_call(
        _proj_kernel,
        grid=(t // tm,),
        in_specs=[pl.BlockSpec((tm, D_MODEL), lambda i: (i, 0)),
                  pl.BlockSpec((1, D_MODEL), lambda i: (0, 0)),
                  pl.BlockSpec((D_MODEL, PROJ_W_PAD), lambda i: (0, 0))],
        out_specs=[pl.BlockSpec((tm, w), lambda i: (i, 0)) for w, _ in widths],
        out_shape=[jax.ShapeDtypeStruct((t, w), d) for w, d in widths],
        compiler_params=_cparams(("parallel",)),
        name="proj",
    )(x2d, gain.reshape(1, D_MODEL), w_pad)


def _tile_bounds(q_start, qb, lk, tk):
    n_full = lax.div(jnp.minimum(q_start + CHUNK, lk), tk)
    n_tiles = lax.div(q_start + qb + tk - 1, tk)
    return n_full, n_tiles


def _admissible(start, tk, q_chunk, lk):
    col = start + lax.broadcasted_iota(jnp.int32, (1, tk), 1)
    return (lax.shift_right_logical(col, CHUNK_SHIFT) <= q_chunk) & (col < lk)


def _fori_pairs(n, body, init=0):
    pairs = lax.div(n, 2)

    def two(i, c):
        return body(2 * i + 1, body(2 * i, c))

    return lax.fori_loop(2 * pairs, n, body, lax.fori_loop(0, pairs, two, init))


def _lane_max(acc, x):
    for g in range(x.shape[1] // LANES):
        acc = jnp.maximum(acc, x[:, g * LANES:(g + 1) * LANES])
    return acc


def _softmax_value_tile(s_ref, idx, m, v):
    parts = []
    for g in range(s_ref.shape[-1] // LANES):
        s = s_ref[idx + (slice(None), slice(g * LANES, (g + 1) * LANES))]
        parts.append(jnp.exp2((s - m).astype(BF16)))
    v_ones = jnp.concatenate([v, jnp.ones_like(v)], axis=1)
    return jnp.dot(jnp.concatenate(parts, axis=1), v_ones, preferred_element_type=F32)


def _diff_attn_kernel(*refs, past, lk, qb, tk, out_scale, cached):
    if cached:
        (lam_ref, gain_ref, q_ref, kn_ref, vn_ref, kc_ref, vc_ref, o_ref,
         s_ref, mx_ref, acc_ref, k_ref, v_ref) = refs
        n_cache, n_new = kc_ref.shape[0], kn_ref.shape[0]
        for cache_ref, new_ref, dst_ref in ((kc_ref, kn_ref, k_ref), (vc_ref, vn_ref, v_ref)):
            dst_ref[0:n_cache, :] = cache_ref[...].astype(BF16)
            dst_ref[n_cache:n_cache + n_new, :] = new_ref[...]
            if tk > n_cache + n_new:
                dst_ref[n_cache + n_new:, :] = jnp.zeros((tk - n_cache - n_new, LANES), BF16)
    else:
        lam_ref, gain_ref, q_ref, k_ref, v_ref, o_ref, s_ref, mx_ref, acc_ref = refs
    j = pl.program_id(2)
    q = q_ref[...]
    lane = lax.broadcasted_iota(jnp.int32, (1, LANES), 1)
    zero = jnp.zeros_like(q)
    q_half = (jnp.where(lane < HEAD_DIM, q, zero), jnp.where(lane >= HEAD_DIM, q, zero))
    q_start = past + j * qb
    row = lax.broadcasted_iota(jnp.int32, (qb, 1), 0)
    q_chunk = lax.shift_right_logical(q_start + row, CHUNK_SHIFT)
    n_full, n_tiles = _tile_bounds(q_start, qb, lk, tk)

    mx_ref[...] = jnp.full(mx_ref.shape, -jnp.inf, F32)
    acc_ref[...] = jnp.zeros(acc_ref.shape, F32)

    def score_tile(t, masked):
        start = pl.multiple_of(t * tk, tk)
        k = k_ref[pl.ds(start, tk), :]
        if masked:
            ok = _admissible(start, tk, q_chunk, lk)
        for i in range(2):
            s = _dot_nt(q_half[i], k)
            if masked:
                s = jnp.where(ok, s, -jnp.inf)
            s_ref[i, t] = s
            mx_ref[i] = _lane_max(mx_ref[i], s)

    def full_body(t, c):
        score_tile(t, False)
        return c

    def masked_body(t, c):
        score_tile(t, True)
        return c

    _fori_pairs(n_full, full_body)
    lax.fori_loop(n_full, n_tiles, masked_body, 0)

    for i in range(2):
        mx_ref[i] = jnp.broadcast_to(jnp.max(mx_ref[i], axis=1, keepdims=True), (qb, LANES))

    def value_tile(t, c):
        start = pl.multiple_of(t * tk, tk)
        v = v_ref[pl.ds(start, tk), :]
        for i in range(2):
            acc_ref[i] = acc_ref[i] + _softmax_value_tile(s_ref, (i, t), mx_ref[i], v)
        return c

    _fori_pairs(n_tiles, value_tile)

    o = (acc_ref[0, :, :LANES] / acc_ref[0, :, LANES:]
         - lam_ref[...] * (acc_ref[1, :, :LANES] / acc_ref[1, :, LANES:]))
    o_ref[...] = (_rms(o, gain_ref[...]) * out_scale).astype(BF16)


def _diff_attn(q, k, v, lam, gain, *, past, lk, tk, out_scale, cache=None):
    b, l, _ = q.shape
    qb = min(l, DIFF_Q_BLOCK)
    cached = cache is not None
    lkp = tk if cached else k.shape[1]
    assert l % qb == 0 and lkp % tk == 0 and qb % CHUNK == 0
    kern = functools.partial(_diff_attn_kernel, past=past, lk=lk, qb=qb, tk=tk,
                             out_scale=out_scale, cached=cached)
    in_specs = [pl.BlockSpec((1, LANES), lambda bi, h, j: (0, 0)),
                pl.BlockSpec((1, LANES), lambda bi, h, j: (0, 0)),
                pl.BlockSpec((None, qb, LANES), lambda bi, h, j: (bi, j, h)),
                pl.BlockSpec((None, k.shape[1], LANES), lambda bi, h, j: (bi, 0, h)),
                pl.BlockSpec((None, k.shape[1], LANES), lambda bi, h, j: (bi, 0, h))]
    scratch = [pltpu.VMEM((2, lkp // tk, qb, tk), F32),
               pltpu.VMEM((2, qb, LANES), F32), pltpu.VMEM((2, qb, 2 * LANES), F32)]
    operands = [lam, gain, q, k, v]
    if cached:
        cache_k, cache_v, li = cache
        assert l == qb and past + l <= tk and cache_k.shape[2:] == (past, W_A)
        head_rows = pl.BlockSpec((None, None, past, LANES), lambda bi, h, j: (li, bi, 0, h))
        in_specs += [head_rows, head_rows]
        scratch += [pltpu.VMEM((tk, LANES), BF16), pltpu.VMEM((tk, LANES), BF16)]
        operands += [cache_k, cache_v]
    return pl.pallas_call(
        kern,
        grid=(b, H_A, l // qb),
        in_specs=in_specs,
        out_specs=pl.BlockSpec((None, qb, LANES), lambda bi, h, j: (bi, j, h)),
        out_shape=jax.ShapeDtypeStruct((b, l, W_A), BF16),
        scratch_shapes=scratch,
        compiler_params=_cparams(("parallel", "parallel", "parallel")),
        name="diff_attn",
    )(*operands)


def _dsa_kernel(qi_ref, kw_ref, q_ref, ki_ref, k_ref, v_ref, o_ref,
                s_ref, sc_ref, lo_ref, hi_ref, mx_ref, acc_ref,
                *, past, lk, qb, tk, topk, max_steps):
    j = pl.program_id(1)
    q_start = past + j * qb
    n_full, n_tiles = _tile_bounds(q_start, qb, lk, tk)
    n_groups = tk // LANES
    row = lax.broadcasted_iota(jnp.int32, (qb, 1), 0)
    q_chunk = lax.shift_right_logical(q_start + row, CHUNK_SHIFT)
    lane = lax.broadcasted_iota(jnp.int32, (1, LANES), 1)
    low_half = lane < HEAD_DIM
    k_sel = float(topk)

    def wide(x):
        return jnp.broadcast_to(x, (qb, LANES))

    def head_views(x):
        views = []
        for h in range(4):
            pair = x[:, (h // 2) * LANES:(h // 2 + 1) * LANES]
            keep = low_half if h % 2 == 0 else jnp.logical_not(low_half)
            views.append(jnp.where(keep, pair, jnp.zeros_like(pair)))
        return views

    qi_h = head_views(qi_ref[...])
    kw = kw_ref[...]
    w_h = [wide(kw[:, D_IDX + h:D_IDX + h + 1]) for h in range(H_IDX)]
    lo_ref[...] = jnp.full(lo_ref.shape, -jnp.inf, F32)
    hi_ref[...] = jnp.full(hi_ref.shape, -jnp.inf, F32)

    def index_tile(t, masked):
        start = pl.multiple_of(t * tk, tk)
        if masked:
            adm = _admissible(start, tk, q_chunk, lk)
        top2 = lo_ref[...]
        top1 = hi_ref[...]
        for c in range(tk // MXU_DIM):
            ki = ki_ref[pl.ds(start + c * MXU_DIM, MXU_DIM), :]
            rel = [jnp.maximum(_dot_nt(qi_h[h], ki), 0.0) for h in range(H_IDX)]
            for g in range(MXU_DIM // LANES):
                sl = slice(g * LANES, (g + 1) * LANES)
                sc = w_h[0] * rel[0][:, sl]
                for h in range(1, H_IDX):
                    sc = sc + w_h[h] * rel[h][:, sl]
                csl = slice(c * MXU_DIM + g * LANES, c * MXU_DIM + (g + 1) * LANES)
                if masked:
                    sc = jnp.where(adm[:, csl], sc, -jnp.inf)
                s_ref[t, :, csl] = sc
                top2 = jnp.maximum(top2, jnp.minimum(top1, sc))
                top1 = jnp.maximum(top1, sc)
        lo_ref[...] = top2
        hi_ref[...] = top1

    def index_full(t, c):
        index_tile(t, False)
        return c

    def index_masked(t, c):
        index_tile(t, True)
        return c

    _fori_pairs(n_full, index_full)
    lax.fori_loop(n_full, n_tiles, index_masked, 0)
    rb = min(qb, SCAN_ROWS)
    ones_mat = jnp.ones((LANES, LANES), BF16)
    tri_i = lax.broadcasted_iota(jnp.int32, (LANES, 2 * LANES), 0)
    tri_j = lax.broadcasted_iota(jnp.int32, (LANES, 2 * LANES), 1)
    prefix_mat = jnp.where(jnp.logical_or(tri_i <= tri_j, tri_j >= LANES), 1.0, 0.0).astype(BF16)
    assert s_ref.shape[0] * n_groups <= 256

    row_blocks = [slice(r * rb, (r + 1) * rb) for r in range(qb // rb)]
    pos = q_start + lax.broadcasted_iota(jnp.int32, (qb, LANES), 0)
    n_adm = jnp.minimum((lax.shift_right_logical(pos, CHUNK_SHIFT) + 1) * CHUNK, lk)
    few = n_adm <= topk

    def lanes_all(x, reduce):
        return jnp.broadcast_to(reduce(x, axis=1, keepdims=True), (qb, LANES))

    def row_sum(acc):
        return jnp.dot(acc.astype(BF16), ones_mat, preferred_element_type=F32)

    def scan(step, init, *operands):
        outs = []
        for rows in row_blocks:
            ops = [o[rows] for o in operands]

            def body(t, acc, rows=rows, ops=ops):
                for g in range(n_groups):
                    acc = step(acc, s_ref[t, rows, g * LANES:(g + 1) * LANES], *ops)
                return acc

            outs.append(lax.fori_loop(0, n_tiles, body, jnp.full((rb, LANES), init, F32)))
        return outs[0] if len(outs) == 1 else jnp.concatenate(outs, axis=0)

    def count_ge(thr):
        return row_sum(scan(lambda acc, s, t: acc + jnp.where(s >= t, 1.0, 0.0), 0.0, thr))

    def max_below(bound):
        acc = scan(lambda acc, s, b: jnp.maximum(acc, jnp.where(s < b, s, -jnp.inf)), -jnp.inf, bound)
        return lanes_all(acc, jnp.max)

    assert topk <= 2 * LANES
    lane_best = hi_ref[...] if topk <= LANES else lo_ref[...]
    lowest = float(jnp.finfo(jnp.float32).min)
    rmin = jnp.maximum(lanes_all(lane_best, jnp.min), lowest)
    rmax = jnp.maximum(lanes_all(lane_best, jnp.max), lowest)

    c_max = count_ge(rmax)
    top_ties = c_max >= k_sel

    def bisect(_, c):
        lo, hi = c
        mid = 0.5 * lo + 0.5 * hi
        ge = count_ge(mid) >= k_sel
        return jnp.where(ge, mid, lo), jnp.where(ge, hi, mid)

    _, hi = lax.fori_loop(0, N_BISECT, bisect, (rmin, rmax))

    def walk_cond(st):
        it, _, _, _, active = st
        return jnp.logical_and(jnp.max(active) > 0.0, it < max_steps)

    def walk_body(st):
        it, cand, thr, c_thr, active = st
        c = count_ge(cand)
        ok = c >= k_sel
        act = active > 0.0
        hit = jnp.logical_and(act, ok)
        thr = jnp.where(hit, cand, thr)
        c_thr = jnp.where(hit, c, c_thr)
        active = jnp.where(jnp.logical_and(act, jnp.logical_not(ok)), 1.0, 0.0)
        return it + 1, max_below(cand), thr, c_thr, active

    thr0 = jnp.where(few, lowest, jnp.where(top_ties, rmax, lowest))
    c0 = jnp.where(few, k_sel, jnp.where(top_ties, c_max, k_sel))
    active0 = jnp.where(jnp.logical_or(few, top_ties), 0.0, 1.0)
    _, _, thr, c_thr, _ = lax.while_loop(
        walk_cond, walk_body, (jnp.int32(0), max_below(hi), thr0, c0, active0))

    has_excess = jnp.max(jnp.where(c_thr > k_sel, 1.0, 0.0)) > 0.0

    def ranked_bias():
        n_tie = k_sel - row_sum(scan(lambda acc, s, t: acc + jnp.where(s > t, 1.0, 0.0), 0.0, thr))
        for rows in row_blocks:
            thr_r = thr[rows]
            n_tie_r = n_tie[rows]

            def body(t, before, rows=rows, thr_r=thr_r, n_tie_r=n_tie_r):
                for g in range(n_groups):
                    lanes = slice(g * LANES, (g + 1) * LANES)
                    s = s_ref[t, rows, lanes]
                    tie = s == thr_r
                    pr = jnp.dot(jnp.where(tie, 1.0, 0.0).astype(BF16), prefix_mat,
                                 preferred_element_type=F32)
                    rank = before + pr[:, :LANES]
                    tie_bias = jnp.where(rank <= n_tie_r, 0.0, MASKED)
                    s_ref[t, rows, lanes] = jnp.where(s > thr_r, 0.0, jnp.where(tie, tie_bias, MASKED))
                    before = before + pr[:, LANES:]
                return before

            _fori_pairs(n_tiles, body, jnp.zeros((rb, LANES), F32))
        return 0

    def plain_bias():
        for rows in row_blocks:
            thr_r = thr[rows]

            def body(t, c, rows=rows, thr_r=thr_r):
                for g in range(n_groups):
                    lanes = slice(g * LANES, (g + 1) * LANES)
                    s_ref[t, rows, lanes] = jnp.where(s_ref[t, rows, lanes] >= thr_r, 0.0, MASKED)
                return c

            lax.fori_loop(0, n_tiles, body, 0)
        return 0

    lax.cond(has_excess, ranked_bias, plain_bias)

    q_h = head_views(q_ref[...])
    for g in range(H_B // 2):
        lanes = slice(g * LANES, (g + 1) * LANES)
        mx_ref[...] = jnp.full(mx_ref.shape, -jnp.inf, F32)
        acc_ref[...] = jnp.zeros(acc_ref.shape, F32)

        def score_tile(t, c, g=g, lanes=lanes):
            start = pl.multiple_of(t * tk, tk)
            k = k_ref[pl.ds(start, tk), lanes]
            bias = s_ref[t]
            for i in range(2):
                s = _dot_nt(q_h[2 * g + i], k) + bias
                sc_ref[i, t] = s
                mx_ref[i] = _lane_max(mx_ref[i], s)
            return c

        _fori_pairs(n_tiles, score_tile)
        for i in range(2):
            mx_ref[i] = jnp.broadcast_to(jnp.max(mx_ref[i], axis=1, keepdims=True), (qb, LANES))

        def value_tile(t, c, lanes=lanes):
            start = pl.multiple_of(t * tk, tk)
            v = v_ref[pl.ds(start, tk), lanes]
            for i in range(2):
                acc_ref[i] = acc_ref[i] + _softmax_value_tile(sc_ref, (i, t), mx_ref[i], v)
            return c

        _fori_pairs(n_tiles, value_tile)
        even = acc_ref[0, :, :LANES] / acc_ref[0, :, LANES:]
        odd = acc_ref[1, :, :LANES] / acc_ref[1, :, LANES:]
        o_ref[:, lanes] = jnp.where(low_half, even, odd).astype(BF16)


def _dsa(qi, kw, q, ki, k, v, *, past, lk, tk):
    b, l, _ = q.shape
    lkp = k.shape[1]
    qb = min(l, Q_BLOCK)
    assert l % qb == 0 and lkp % tk == 0 and tk % MXU_DIM == 0 and qb % CHUNK == 0
    kern = functools.partial(_dsa_kernel, past=past, lk=lk, qb=qb, tk=tk,
                             topk=min(TOPK, lk // 4), max_steps=lkp)
    return pl.pallas_call(
        kern,
        grid=(b, l // qb),
        in_specs=[pl.BlockSpec((None, qb, H_IDX * D_IDX), lambda bi, j: (bi, j, 0)),
                  pl.BlockSpec((None, qb, LANES), lambda bi, j: (bi, j, 0)),
                  pl.BlockSpec((None, qb, W_B), lambda bi, j: (bi, j, 0)),
                  pl.BlockSpec((None, lkp, LANES), lambda bi, j: (bi, 0, 0)),
                  pl.BlockSpec((None, lkp, W_B), lambda bi, j: (bi, 0, 0)),
                  pl.BlockSpec((None, lkp, W_B), lambda bi, j: (bi, 0, 0))],
        out_specs=pl.BlockSpec((None, qb, W_B), lambda bi, j: (bi, j, 0)),
        out_shape=jax.ShapeDtypeStruct((b, l, W_B), BF16),
        scratch_shapes=[pltpu.VMEM((lkp // tk, qb, tk), F32),
                        pltpu.VMEM((2, lkp // tk, qb, tk), F32),
                        pltpu.VMEM((qb, LANES), F32), pltpu.VMEM((qb, LANES), F32),
                        pltpu.VMEM((2, qb, LANES), F32), pltpu.VMEM((2, qb, 2 * LANES), F32)],
        compiler_params=_cparams(("parallel", "parallel")),
        name="dsa",
    )(qi, kw, q, ki, k, v)


def _shift_rows(x, prev, k):
    rolled = pltpu.roll(x, k, 0)
    row = lax.broadcasted_iota(jnp.int32, (SUBLANES, 1), 0)
    top = jnp.where(row < k, pltpu.roll(prev, k, 0), rolled[0:SUBLANES])
    if x.shape[0] == SUBLANES:
        return top
    return jnp.concatenate([top, rolled[SUBLANES:]], axis=0)


def _rglru_kernel(xc_ref, gc_ref, cst_ref, h0_ref, cw_ref, cb_ref, wg_ref, bg_ref, lam_ref,
                  oc_ref, hl_ref, cn_ref, prev_ref, h_ref, a_ref, b_ref, hs_ref, *, past, tl):
    i = pl.program_id(1)

    @pl.when(i == 0)
    def _():
        prev_ref[...] = jnp.zeros(prev_ref.shape, F32)
        prev_ref[SUBLANES - (CONV_C - 1):SUBLANES, :] = cst_ref[...]
        h_ref[...] = h0_ref[...]

    x = xc_ref[...]
    prev = prev_ref[...]
    cw = cw_ref[...]
    xconv = cw[CONV_C - 1:CONV_C] * x + cb_ref[...]
    for k in range(1, CONV_C):
        xconv = xconv + cw[CONV_C - 1 - k:CONV_C - k] * _shift_rows(x, prev, k)
    prev_ref[...] = x[tl - SUBLANES:tl]

    pre = jnp.dot(xconv.astype(BF16), wg_ref[...], preferred_element_type=F32) + bg_ref[...]
    r = jax.nn.sigmoid(pre[:, :W_C])
    gate_i = jax.nn.sigmoid(pre[:, W_C:])
    neg_lam = -lam_ref[...]
    softplus = jnp.maximum(neg_lam, 0.0) + jnp.log1p(jnp.exp(-jnp.abs(neg_lam)))
    log_a = -RG_C * r * softplus
    pos = past + i * tl + lax.broadcasted_iota(jnp.int32, (tl, 1), 0)
    th = jnp.tanh(log_a)
    mult = jnp.where(pos == 0, 1.0, jnp.sqrt(-2.0 * th / (1.0 - th)))
    a_ref[...] = jnp.exp(log_a)
    b_ref[...] = mult * gate_i * xconv

    row8 = lax.broadcasted_iota(jnp.int32, (SUBLANES, 1), 0)

    def tile_step(i, h):
        r0 = pl.multiple_of(i * SUBLANES, SUBLANES)
        a = a_ref[pl.ds(r0, SUBLANES), :]
        b = b_ref[pl.ds(r0, SUBLANES), :]
        for sh in (1, 2, 4):
            a_prev = jnp.where(row8 >= sh, pltpu.roll(a, sh, 0), 1.0)
            b_prev = jnp.where(row8 >= sh, pltpu.roll(b, sh, 0), 0.0)
            b = a * b_prev + b
            a = a * a_prev
        hs = a * h + b
        hs_ref[pl.ds(r0, SUBLANES), :] = hs
        return hs[SUBLANES - 1:SUBLANES, :]

    h_last = lax.fori_loop(0, tl // SUBLANES, tile_step, h_ref[...], unroll=4)
    h_ref[...] = h_last

    gc = gc_ref[...]
    gelu = 0.5 * gc * (1.0 + jnp.tanh(math.sqrt(2.0 / math.pi) * (gc + 0.044715 * (gc * gc * gc))))
    oc_ref[...] = (hs_ref[...] * gelu).astype(BF16)

    @pl.when(i == pl.num_programs(1) - 1)
    def _():
        hl_ref[...] = h_last
        cn_ref[...] = x[tl - (CONV_C - 1):tl]


def _rglru(xc, gc, conv_state, h0, conv_w, conv_b, w_gate, b_gate, lam, *, past):
    b, l, _ = xc.shape
    tl = min(l, 512)
    assert l >= SUBLANES and l % tl == 0
    kern = functools.partial(_rglru_kernel, past=past, tl=tl)
    const = lambda bi, i: (0, 0)
    return pl.pallas_call(
        kern,
        grid=(b, l // tl),
        in_specs=[pl.BlockSpec((None, tl, W_C), lambda bi, i: (bi, i, 0)),
                  pl.BlockSpec((None, tl, W_C), lambda bi, i: (bi, i, 0)),
                  pl.BlockSpec((None, CONV_C - 1, W_C), lambda bi, i: (bi, 0, 0)),
                  pl.BlockSpec((None, 1, W_C), lambda bi, i: (bi, 0, 0)),
                  pl.BlockSpec((CONV_C, W_C), const),
                  pl.BlockSpec((1, W_C), const),
                  pl.BlockSpec((W_C, 2 * W_C), const),
                  pl.BlockSpec((1, 2 * W_C), const),
                  pl.BlockSpec((1, W_C), const)],
        out_specs=[pl.BlockSpec((None, tl, W_C), lambda bi, i: (bi, i, 0)),
                   pl.BlockSpec((None, 1, W_C), lambda bi, i: (bi, 0, 0)),
                   pl.BlockSpec((None, CONV_C - 1, W_C), lambda bi, i: (bi, 0, 0))],
        out_shape=[jax.ShapeDtypeStruct((b, l, W_C), BF16),
                   jax.ShapeDtypeStruct((b, 1, W_C), F32),
                   jax.ShapeDtypeStruct((b, CONV_C - 1, W_C), F32)],
        scratch_shapes=[pltpu.VMEM((SUBLANES, W_C), F32), pltpu.VMEM((1, W_C), F32),
                        pltpu.VMEM((tl, W_C), F32), pltpu.VMEM((tl, W_C), F32),
                        pltpu.VMEM((tl, W_C), F32)],
        compiler_params=_cparams(("parallel", "arbitrary")),
        name="rglru",
    )(xc, gc, conv_state, h0.reshape(b, 1, W_C), conv_w, conv_b.reshape(1, W_C), w_gate,
      b_gate, lam.reshape(1, W_C))


def _outproj_kernel(x_ref, oa_ref, ob_ref, oc_ref, w_ref, g_ref, x1_ref, hn_ref):
    mix = jnp.dot(oa_ref[...], w_ref[0:W_A, :], preferred_element_type=F32)
    mix = mix + jnp.dot(ob_ref[...], w_ref[W_A:W_A + W_B, :], preferred_element_type=F32)
    mix = mix + jnp.dot(oc_ref[...], w_ref[W_A + W_B:, :], preferred_element_type=F32)
    x1 = x_ref[...] + mix
    x1_ref[...] = x1
    hn_ref[...] = _rms(x1, g_ref[...]).astype(BF16)


def _outproj(x2d, oa, ob, oc, w_out, gain):
    t = x2d.shape[0]
    tm = min(512, t)
    row = lambda w: pl.BlockSpec((tm, w), lambda i: (i, 0))
    return pl.pallas_call(
        _outproj_kernel,
        grid=(t // tm,),
        in_specs=[row(D_MODEL), row(W_A), row(W_B), row(W_C),
                  pl.BlockSpec((D_MODEL, D_MODEL), lambda i: (0, 0)),
                  pl.BlockSpec((1, D_MODEL), lambda i: (0, 0))],
        out_specs=[row(D_MODEL), row(D_MODEL)],
        out_shape=[jax.ShapeDtypeStruct((t, D_MODEL), F32), jax.ShapeDtypeStruct((t, D_MODEL), BF16)],
        compiler_params=_cparams(("parallel",)),
        name="outproj",
    )(x2d, oa, ob, oc, w_out, gain.reshape(1, D_MODEL))


def _ffn_kernel(hn_ref, x1_ref, wu_ref, wg_ref, cwu_ref, cwg_ref, cbu_ref, cbg_ref, wd_ref,
                su_ref, sg_ref, gfin_ref, y_ref, fu_ref, fg_ref,
                acc_ref, au_ref, ag_ref, cu_ref, cg_ref, *, tm, final_norm):
    i = pl.program_id(1)
    s = pl.program_id(2)
    nf = pl.num_programs(2) - 1
    live = s > 0
    fb = jnp.maximum(s - 1, 0)
    row = jnp.where(live, fb, nf)

    @pl.when(s == 0)
    def _():
        au_ref[1] = jnp.zeros(au_ref.shape[1:], F32)
        ag_ref[1] = jnp.zeros(ag_ref.shape[1:], F32)
        cu_ref[nf] = jnp.zeros(cu_ref.shape[1:], F32)
        cg_ref[nf] = jnp.zeros(cg_ref.shape[1:], F32)
        acc_ref[...] = x1_ref[...]

    @pl.when(jnp.logical_and(i == 0, live))
    def _():
        for carry_ref, st_ref in ((cu_ref, su_ref), (cg_ref, sg_ref)):
            carry_ref[fb] = jnp.zeros(carry_ref.shape[1:], F32)
            carry_ref[fb, SUBLANES - (CONV_F - 1):SUBLANES, :] = st_ref[...]

    def step(wslot, rslot):
        au_ref[rslot, 0:SUBLANES, :] = cu_ref[row]
        ag_ref[rslot, 0:SUBLANES, :] = cg_ref[row]
        rc = tm // FFN_ROW_CHUNKS

        def conv(a_ref, cw_ref, cb_ref, r0):
            cw = cw_ref[...]
            y = cw[CONV_F - 1:CONV_F] * a_ref[rslot, SUBLANES + r0:SUBLANES + r0 + rc, :] + cb_ref[...]
            for k in range(1, CONV_F):
                y = y + (cw[CONV_F - 1 - k:CONV_F - k]
                         * a_ref[rslot, SUBLANES - k + r0:SUBLANES - k + r0 + rc, :])
            return y

        for c in range(FFN_ROW_CHUNKS):
            r0 = c * rc
            hn = hn_ref[r0:r0 + rc, :]
            au_ref[wslot, SUBLANES + r0:SUBLANES + r0 + rc, :] = jnp.dot(
                hn, wu_ref[...], preferred_element_type=F32)
            ag_ref[wslot, SUBLANES + r0:SUBLANES + r0 + rc, :] = jnp.dot(
                hn, wg_ref[...], preferred_element_type=F32)
            u = conv(au_ref, cwu_ref, cbu_ref, r0)
            g = conv(ag_ref, cwg_ref, cbg_ref, r0)
            mid = (g * jax.nn.sigmoid(g) * u).astype(BF16)
            contrib = jnp.dot(mid, wd_ref[...], preferred_element_type=F32)
            acc_ref[r0:r0 + rc, :] = acc_ref[r0:r0 + rc, :] + jnp.where(live, contrib, 0.0)

        for a_ref, carry_ref, tail_ref in ((au_ref, cu_ref, fu_ref), (ag_ref, cg_ref, fg_ref)):
            carry_ref[row] = a_ref[rslot, tm:tm + SUBLANES, :]
            tail_ref[row] = a_ref[rslot, tm + SUBLANES - (CONV_F - 1):tm + SUBLANES, :]

    parity = lax.rem(s, 2)

    @pl.when(parity == 0)
    def _():
        step(0, 1)

    @pl.when(parity == 1)
    def _():
        step(1, 0)

    @pl.when(s == nf)
    def _():
        y = acc_ref[...]
        if final_norm:
            y = _rms(y, gfin_ref[...])
        y_ref[...] = y


def _ffn(hn, x1, w_up, conv_w, conv_b, w_down, state, final_gain, *, final_norm):
    b, l, _ = hn.shape
    tm = min(l, 1024)
    tf = 512
    nf = D_FF // tf
    assert l % tm == 0 and tm >= SUBLANES
    kern = functools.partial(_ffn_kernel, tm=tm, final_norm=final_norm)
    conv_b = conv_b.reshape(1, 2 * D_FF)
    up = lambda s: jnp.minimum(s, nf - 1)
    fin = lambda s: jnp.maximum(s - 1, 0)
    tail_spec = pl.BlockSpec((None, nf + 1, CONV_F - 1, tf), lambda bi, i, s: (bi, 0, 0, 0))
    tail_shape = jax.ShapeDtypeStruct((b, nf + 1, CONV_F - 1, tf), F32)
    y, fu, fg = pl.pallas_call(
        kern,
        grid=(b, l // tm, nf + 1),
        in_specs=[pl.BlockSpec((None, tm, D_MODEL), lambda bi, i, s: (bi, i, 0)),
                  pl.BlockSpec((None, tm, D_MODEL), lambda bi, i, s: (bi, i, 0)),
                  pl.BlockSpec((D_MODEL, tf), lambda bi, i, s: (0, up(s))),
                  pl.BlockSpec((D_MODEL, tf), lambda bi, i, s: (0, nf + up(s))),
                  pl.BlockSpec((CONV_F, tf), lambda bi, i, s: (0, fin(s))),
                  pl.BlockSpec((CONV_F, tf), lambda bi, i, s: (0, nf + fin(s))),
                  pl.BlockSpec((1, tf), lambda bi, i, s: (0, fin(s))),
                  pl.BlockSpec((1, tf), lambda bi, i, s: (0, nf + fin(s))),
                  pl.BlockSpec((tf, D_MODEL), lambda bi, i, s: (fin(s), 0)),
                  pl.BlockSpec((None, CONV_F - 1, tf), lambda bi, i, s: (bi, 0, fin(s))),
                  pl.BlockSpec((None, CONV_F - 1, tf), lambda bi, i, s: (bi, 0, nf + fin(s))),
                  pl.BlockSpec((1, D_MODEL), lambda bi, i, s: (0, 0))],
        out_specs=[pl.BlockSpec((None, tm, D_MODEL), lambda bi, i, s: (bi, i, 0)),
                   tail_spec, tail_spec],
        out_shape=[jax.ShapeDtypeStruct((b, l, D_MODEL), F32), tail_shape, tail_shape],
        scratch_shapes=[pltpu.VMEM((tm, D_MODEL), F32),
                        pltpu.VMEM((2, tm + SUBLANES, tf), F32),
                        pltpu.VMEM((2, tm + SUBLANES, tf), F32),
                        pltpu.VMEM((nf + 1, SUBLANES, tf), F32),
                        pltpu.VMEM((nf + 1, SUBLANES, tf), F32)],
        compiler_params=_cparams(("parallel", "arbitrary", "arbitrary")),
        name="ffn",
    )(hn, x1, w_up, w_up, conv_w, conv_w, conv_b, conv_b, w_down, state, state,
      final_gain.reshape(1, D_MODEL))
    flat = lambda a: jnp.swapaxes(a[:, :nf], 1, 2).reshape(b, CONV_F - 1, D_FF)
    return y, jnp.concatenate([flat(fu), flat(fg)], axis=-1)


def _cast_kernel(x_ref, o_ref):
    o_ref[...] = x_ref[...].astype(o_ref.dtype)


def _layer_bf16(w, li):
    _, r, c = w.shape
    tr = 256 if r % 256 == 0 else r
    return pl.pallas_call(
        _cast_kernel,
        grid=(r // tr,),
        in_specs=[pl.BlockSpec((None, tr, c), lambda i: (li, i, 0))],
        out_specs=pl.BlockSpec((tr, c), lambda i: (i, 0)),
        out_shape=jax.ShapeDtypeStruct((r, c), BF16),
        compiler_params=_cparams(("parallel",)),
        name="cast",
    )(w)


def _prep_layer_weights(p, li):
    w_in = _layer_bf16(p["w_in"], li)
    zeros = lambda n: jnp.zeros((D_MODEL, n), w_in.dtype)
    k_idx = w_in[:, 2560:2624]
    w_pad = jnp.concatenate(
        [w_in[:, :2628], zeros(_C_KK - 2628), k_idx, k_idx, w_in[:, 2628:]], axis=1)
    assert w_pad.shape[1] == PROJ_W_PAD

    def block_diag(w):
        out = jnp.zeros((W_C, W_C), w.dtype)
        for n in range(N_GATE_BLOCKS):
            sl = slice(n * GATE_BLOCK, (n + 1) * GATE_BLOCK)
            out = out.at[sl, sl].set(w[n])
        return out

    w_gate = jnp.concatenate([block_diag(p["rg_w_r"][li]), block_diag(p["rg_w_i"][li])], axis=1)
    b_gate = jnp.concatenate([p["rg_b_r"][li].reshape(1, W_C), p["rg_b_i"][li].reshape(1, W_C)], axis=1)
    lam_init = 0.8 - 0.6 * math.exp(-0.3 * li)
    f32 = lambda a: a.astype(F32)
    lam = (jnp.exp(jnp.sum(f32(p["lam_q1"][li]) * f32(p["lam_k1"][li])))
           - jnp.exp(jnp.sum(f32(p["lam_q2"][li]) * f32(p["lam_k2"][li]))) + lam_init)
    return dict(
        norm_mix=p["norm_mix"][li], w_pad=w_pad, lam=jnp.full((1, LANES), lam, F32),
        lam_init=lam_init, diff_gain=p["diff_gain"][li].reshape(1, LANES),
        rg_conv_w=p["rg_conv_w"][li], rg_conv_b=p["rg_conv_b"][li],
        w_gate=w_gate.astype(BF16), b_gate=b_gate, rg_lambda=p["rg_lambda"][li],
        w_out=_layer_bf16(p["w_out"], li), norm_ffn=p["norm_ffn"][li],
        ffn_w_up=_layer_bf16(p["ffn_w_up"], li), ffn_conv_w=p["ffn_conv_w"][li],
        ffn_conv_b=p["ffn_conv_b"][li], ffn_w_down=_layer_bf16(p["ffn_w_down"], li))


def _layer(x, past, attn_cache, w, final_gain, final_norm):
    b, l, _ = x.shape
    _, _, b_k0, b_v0, b_ki0, c_h0, c_cv0, f_cv0 = past
    p_len = 0 if b_k0 is None else b_k0.shape[1]
    lk = p_len + l
    tk = PROMPT_KEY_TILE if p_len == 0 else _round_up(lk, MXU_DIM)
    lkp = _round_up(lk, tk)
    t = b * l

    (qa, qb, qi, ka, va, kb, vb, kw, xc, gc, kab, vab, kbb, vbb, kib) = _proj(
        x.reshape(t, D_MODEL), w["norm_mix"], w["w_pad"])

    def keys(cache, new, dup=False):
        new = new.reshape(b, l, -1)
        parts = []
        if cache is not None:
            c = cache.reshape(b, p_len, -1).astype(BF16)
            parts.append(jnp.concatenate([c, c], axis=-1) if dup else c)
        parts.append(new)
        if lkp > lk:
            parts.append(jnp.zeros((b, lkp - lk, new.shape[-1]), BF16))
        return parts[0] if len(parts) == 1 else jnp.concatenate(parts, axis=1)

    if attn_cache is None:
        k_a, v_a = keys(None, kab), keys(None, vab)
    else:
        k_a, v_a = kab.reshape(b, l, W_A), vab.reshape(b, l, W_A)
    o_a = _diff_attn(qa.reshape(b, l, W_A), k_a, v_a, w["lam"], w["diff_gain"], past=p_len, lk=lk,
                     tk=tk, out_scale=1.0 - w["lam_init"], cache=attn_cache)
    o_b = _dsa(qi.reshape(b, l, -1), kw.reshape(b, l, LANES), qb.reshape(b, l, W_B),
               keys(b_ki0, kib, dup=True), keys(b_k0, kbb), keys(b_v0, vbb),
               past=p_len, lk=lk, tk=tk)
    o_c, h_last, conv_new = _rglru(xc.reshape(b, l, W_C), gc.reshape(b, l, W_C), c_cv0, c_h0,
                                   w["rg_conv_w"], w["rg_conv_b"], w["w_gate"], w["b_gate"],
                                   w["rg_lambda"], past=p_len)
    x1, hn = _outproj(x.reshape(t, D_MODEL), o_a.reshape(t, W_A), o_b.reshape(t, W_B),
                      o_c.reshape(t, W_C), w["w_out"], w["norm_ffn"])
    y, f_buf = _ffn(hn.reshape(b, l, D_MODEL), x1.reshape(b, l, D_MODEL), w["ffn_w_up"],
                    w["ffn_conv_w"], w["ffn_conv_b"], w["ffn_w_down"], f_cv0, final_gain,
                    final_norm=final_norm)
    new = (ka.reshape(b, l, H_A, 2 * HEAD_DIM), va.reshape(b, l, H_A, 2 * HEAD_DIM),
           kb.reshape(b, l, H_B, HEAD_DIM), vb.reshape(b, l, H_B, HEAD_DIM),
           kw[:, :D_IDX].reshape(b, l, D_IDX), h_last.reshape(b, W_C), conv_new, f_buf)
    return y, new


def _trunk(x, past, weights, final_gain):
    states = []
    if past[0] is not None:
        flat_heads = lambda c: c.reshape(c.shape[:3] + (W_A,))
        cache_a = (flat_heads(past[0]), flat_heads(past[1]))
    for li in range(N_LAYERS):
        layer_past = tuple(None if c is None else c[li] for c in past)
        attn_cache = None if past[0] is None else cache_a + (li,)
        x, st = _layer(x, layer_past, attn_cache, weights[li], final_gain,
                       final_norm=(li == N_LAYERS - 1))
        states.append(st)
    return x, states


def _forward(x_prompt, x_sample, caches, params):
    weights = [_prep_layer_weights(params, li) for li in range(N_LAYERS)]
    bp = x_prompt.shape[0]
    dt = x_prompt.dtype
    past_prompt = (None, None, None, None, None,
                   jnp.zeros((N_LAYERS, bp, W_C), dt),
                   jnp.zeros((N_LAYERS, bp, CONV_C - 1, W_C), dt),
                   jnp.zeros((N_LAYERS, bp, CONV_F - 1, 2 * D_FF), dt))
    yp, sp = _trunk(x_prompt, past_prompt, weights, params["norm_final"])
    ys, ss = _trunk(x_sample, caches, weights, params["norm_final"])
    out = [yp, ys]
    for jdx in range(8):
        out.append(jnp.stack([st[jdx] for st in sp], axis=0))
        out.append(jnp.stack([st[jdx] for st in ss], axis=0))
    return tuple(out)


def kernel(x_prompt, x_sample, cache_a_k, cache_a_v, cache_b_k, cache_b_v, cache_b_kidx,
           state_c_h, state_c_conv, state_ffn_conv, norm_mix, w_in, lam_q1, lam_k1, lam_q2,
           lam_k2, diff_gain, rg_conv_w, rg_conv_b, rg_w_r, rg_b_r, rg_w_i, rg_b_i, rg_lambda,
           w_out, norm_ffn, ffn_w_up, ffn_conv_w, ffn_conv_b, ffn_w_down, norm_final):
    params = dict(norm_mix=norm_mix, w_in=w_in, lam_q1=lam_q1, lam_k1=lam_k1, lam_q2=lam_q2,
                  lam_k2=lam_k2, diff_gain=diff_gain, rg_conv_w=rg_conv_w, rg_conv_b=rg_conv_b,
                  rg_w_r=rg_w_r, rg_b_r=rg_b_r, rg_w_i=rg_w_i, rg_b_i=rg_b_i, rg_lambda=rg_lambda,
                  w_out=w_out, norm_ffn=norm_ffn, ffn_w_up=ffn_w_up, ffn_conv_w=ffn_conv_w,
                  ffn_conv_b=ffn_conv_b, ffn_w_down=ffn_w_down, norm_final=norm_final)
    caches = (cache_a_k, cache_a_v, cache_b_k, cache_b_v, cache_b_kidx,
              state_c_h, state_c_conv, state_ffn_conv)
    return _forward(x_prompt, x_sample, caches, params)
```
